```python
import math
import jax, jax.numpy as jnp
from jax import lax
import numpy as np

D_MODEL = 1024
BATCH = 8
SEQ = 2048
DEPTH = 1
DEC_BATCH = 128
DEC_SEQ = 8
PAST_LEN = 16384
PAGE_SIZE = 128

MIX_WIDTH = D_MODEL
RET_WIDTH = MIX_WIDTH // 2
RET_HEADS = 4
RET_HD = RET_WIDTH // RET_HEADS
CONV_WIDTH = MIX_WIDTH - RET_WIDTH
CONV_GROUPS = 4
CONV_K = 3
RET_CHUNK = 128
ROPE_BASE = 10000.0
D_FF = ((8 * D_MODEL + 3 * 256 - 1) // (3 * 256)) * 256
IN_COLS = 4 * RET_WIDTH + 3 * CONV_WIDTH
NORM_EPS = 1e-6
GN_EPS = 1e-5

kernel_name = "retnet_shortconv_hybrid_step"


def rmsnorm(x, g):
    x32 = x.astype(jnp.float32)
    y = x32 * lax.rsqrt(jnp.mean(x32 * x32, axis=-1, keepdims=True) + NORM_EPS)
    return (y * g.astype(jnp.float32)).astype(x.dtype)


def rotary(t, pos):
    half = t.shape[-1] // 2
    inv = ROPE_BASE ** (-jnp.arange(half, dtype=jnp.float32) / half)
    ang = pos[:, None] * inv[None, :]
    cos = jnp.cos(ang)[None, :, None, :]
    sin = jnp.sin(ang)[None, :, None, :]
    t32 = t.astype(jnp.float32)
    t1, t2 = t32[..., :half], t32[..., half:]
    return jnp.concatenate([t1 * cos - t2 * sin, t1 * sin + t2 * cos], axis=-1)


def retention_chunkwise(q, k, v, state0):
    B, L, H, D = q.shape
    C = math.gcd(L, RET_CHUNK)
    n = L // C
    log_g = jnp.log1p(-jnp.exp2(-5.0 - jnp.arange(H, dtype=jnp.float32)))
    idx = jnp.arange(C, dtype=jnp.float32)
    diff = idx[:, None] - idx[None, :]
    causal = diff >= 0
    decay = jnp.where(causal[None], jnp.exp(jnp.where(causal, diff, 0.0)[None] * log_g[:, None, None]), 0.0)
    xi = jnp.exp((idx[None, :] + 1.0) * log_g[:, None])
    zeta = jnp.exp((C - 1.0 - idx[None, :]) * log_g[:, None])
    g_chunk = jnp.exp(C * log_g)

    def to_chunks(t):
        return t.reshape(B, n, C, H, D).transpose(1, 0, 3, 2, 4)

    def step(R, xs):
        qc, kc, vc = xs
        scores = jnp.einsum('bhid,bhjd->bhij', qc, kc) * decay[None]
        o = jnp.einsum('bhij,bhjv->bhiv', scores, vc)
        o = o + jnp.einsum('bhid,bhdv->bhiv', qc, R) * xi[None, :, :, None]
        R_new = R * g_chunk[None, :, None, None] + jnp.einsum(
            'bhjd,bhjv->bhdv', kc * zeta[None, :, :, None], vc)
        return R_new, o

    R_fin, o = lax.scan(step, state0, (to_chunks(q), to_chunks(k), to_chunks(v)))
    o = o.transpose(1, 0, 3, 2, 4).reshape(B, L, H, D)
    return o, R_fin


def layer(x, conv_state, ret_state, pos0, norm1_g, w_in, conv_w, ret_gn_g, w_out,
          norm2_g, w_gate, w_up, w_down):
    B, L, _ = x.shape
    h = rmsnorm(x, norm1_g)
    proj = h @ w_in
    splits = np.cumsum([RET_WIDTH] * 4 + [CONV_WIDTH] * 2).tolist()
    q, k, v, g, bg, cg, xt = jnp.split(proj, splits, axis=-1)

    pos = pos0 + jnp.arange(L, dtype=jnp.float32)
    qh = rotary(q.reshape(B, L, RET_HEADS, RET_HD), pos)
    kh = rotary(k.reshape(B, L, RET_HEADS, RET_HD), pos) * (RET_HD ** -0.5)
    vh = v.reshape(B, L, RET_HEADS, RET_HD).astype(jnp.float32)
    o, R_new = retention_chunkwise(qh, kh, vh, ret_state.astype(jnp.float32))
    mu = jnp.mean(o, axis=-1, keepdims=True)
    var = jnp.mean(jnp.square(o - mu), axis=-1, keepdims=True)
    o = ((o - mu) * lax.rsqrt(var + GN_EPS)).reshape(B, L, RET_WIDTH) * ret_gn_g.astype(jnp.float32)
    o_ret = (jax.nn.silu(g.astype(jnp.float32)) * o).astype(x.dtype)

    u = cg * xt
    ext = jnp.concatenate([conv_state.astype(u.dtype), u], axis=1)
    conv_y = sum(conv_w[j] * ext[:, j:j + L] for j in range(CONV_K))
    conv_new = ext[:, L:]
    o_conv = bg * conv_y

    x = x + jnp.concatenate([o_ret, o_conv.astype(x.dtype)], axis=-1) @ w_out

    h2 = rmsnorm(x, norm2_g)
    x = x + (jax.nn.silu(h2 @ w_gate) * (h2 @ w_up)) @ w_down
    return x, conv_new, R_new


def setup_inputs(seed: int = 0) -> dict:
    key = jax.random.key(seed)
    ks = jax.random.split(key, 16)
    f32 = jnp.float32
    nrm = lambda k, s, sc: jax.random.normal(k, s, f32) * sc
    return {
        "x_prompt": nrm(ks[0], (BATCH, SEQ, D_MODEL), 1.0),
        "x_sample": nrm(ks[1], (DEC_BATCH, DEC_SEQ, D_MODEL), 1.0),
        "state_conv": nrm(ks[2], (DEPTH, DEC_BATCH, CONV_K - 1, CONV_WIDTH), 1.0),
        "state_ret": nrm(ks[3], (DEPTH, DEC_BATCH, RET_HEADS, RET_HD, RET_HD), 0.5),
        "norm1_g": 1.0 + nrm(ks[4], (DEPTH, D_MODEL), 0.01),
        "w_in": nrm(ks[5], (DEPTH, D_MODEL, IN_COLS), D_MODEL ** -0.5),
        "conv_w": nrm(ks[6], (DEPTH, CONV_K, CONV_WIDTH), 0.5),
        "ret_gn_g": 1.0 + nrm(ks[7], (DEPTH, RET_WIDTH), 0.01),
        "w_out": nrm(ks[8], (DEPTH, MIX_WIDTH, D_MODEL), MIX_WIDTH ** -0.5),
        "norm2_g": 1.0 + nrm(ks[9], (DEPTH, D_MODEL), 0.01),
        "w_gate": nrm(ks[10], (DEPTH, D_MODEL, D_FF), D_MODEL ** -0.5),
        "w_up": nrm(ks[11], (DEPTH, D_MODEL, D_FF), D_MODEL ** -0.5),
        "w_down": nrm(ks[12], (DEPTH, D_FF, D_MODEL), D_FF ** -0.5),
        "norm_f_g": 1.0 + nrm(ks[13], (D_MODEL,), 0.01),
    }


def reference(x_prompt, x_sample, state_conv, state_ret, norm1_g, w_in, conv_w, ret_gn_g,
              w_out, norm2_g, w_gate, w_up, w_down, norm_f_g):
    hp, hs = x_prompt, x_sample
    cp_list, rp_list, cs_list, rs_list = [], [], [], []
    for l in range(DEPTH):
        params = (norm1_g[l], w_in[l], conv_w[l], ret_gn_g[l], w_out[l],
                  norm2_g[l], w_gate[l], w_up[l], w_down[l])
        conv0 = jnp.zeros((BATCH, CONV_K - 1, CONV_WIDTH), x_prompt.dtype)
        ret0 = jnp.zeros((BATCH, RET_HEADS, RET_HD, RET_HD), jnp.float32)
        hp, cp, rp = layer(hp, conv0, ret0, 0.0, *params)
        hs, cs, rs = layer(hs, state_conv[l], state_ret[l], float(PAST_LEN), *params)
        cp_list.append(cp); rp_list.append(rp); cs_list.append(cs); rs_list.append(rs)
    y_prompt = rmsnorm(hp, norm_f_g)
    y_sample = rmsnorm(hs, norm_f_g)
    new_conv_prompt = jnp.stack(cp_list)
    new_ret_prompt = jnp.stack(rp_list)
    new_conv_sample = jnp.stack(cs_list)
    new_ret_sample = jnp.stack(rs_list)
    return (y_prompt, y_sample, new_conv_prompt, new_ret_prompt, new_conv_sample, new_ret_sample)
```

```python
import functools
import math

import numpy as np
import jax
import jax.numpy as jnp
from jax import lax
from jax.experimental import pallas as pl
from jax.experimental.pallas import tpu as pltpu

F32 = jnp.float32
BF16 = jnp.bfloat16

RET_HEADS = 4
RET_HD = 128
RET_WIDTH = RET_HEADS * RET_HD
CONV_K = 3
RET_CHUNK = 128
ROPE_BASE = 10000.0
NORM_EPS = 1e-6
GN_EPS = 1e-5
PAST_LEN = 16384

LOG_G = tuple(math.log1p(-(2.0 ** (-5.0 - h))) for h in range(RET_HEADS))

SUBLANES = 8
PROMPT_TILE = 256
SAMPLE_SEQS = 16
VMEM_LIMIT_BYTES = 56 * 1024 * 1024


def _ff_chunks(d_ff, width=1536):
    edges = list(range(0, d_ff, width)) + [d_ff]
    return tuple(zip(edges[:-1], edges[1:]))


def _rope_tables(pos):
    half = RET_HD // 2
    inv = ROPE_BASE ** (-np.arange(half, dtype=np.float64) / half)
    ang = np.asarray(pos, np.float64)[:, None] * inv[None, :]
    cos, sin = np.cos(ang), np.sin(ang)
    cs = np.concatenate([cos, cos], axis=1).astype(np.float32)
    sn = np.concatenate([-sin, sin], axis=1).astype(np.float32)
    return jnp.asarray(cs), jnp.asarray(sn)


def _rmsnorm(x, g):
    ms = jnp.mean(x * x, axis=-1, keepdims=True)
    return x * lax.rsqrt(ms + NORM_EPS) * g


def _silu(x):
    return x * (1.0 / (1.0 + jnp.exp(-x)))


def _rotary(t, cs, sn):
    return t * cs + pltpu.roll(t, RET_HD // 2, axis=1) * sn


def _dot(a, b):
    return jnp.dot(a, b, preferred_element_type=F32)


def _dot_nt(a, b):
    return lax.dot_general(a, b, (((1,), (1,)), ((), ())), preferred_element_type=F32)


def _dot_tn(a, b):
    return lax.dot_general(a, b, (((0,), (0,)), ((), ())), preferred_element_type=F32)


def _groupnorm_gate(o, g, gn_g):
    mu = jnp.mean(o, axis=-1, keepdims=True)
    d = o - mu
    var = jnp.mean(d * d, axis=-1, keepdims=True)
    return _silu(g) * (d * lax.rsqrt(var + GN_EPS) * gn_g)


def _in_proj(x, g1_ref, win_ref, h_scr, proj_scr):
    h_scr[...] = _rmsnorm(x, g1_ref[...]).astype(BF16)
    proj_scr[...] = _dot(h_scr[...], win_ref[...])


def _dense_tail(x, mix_scr, wout_ref, g2_ref, wg_ref, wu_ref, wd_ref, gf_ref, y_ref, x1_scr, h_scr):
    x1_scr[...] = x + _dot(mix_scr[...], wout_ref[...])
    h_scr[...] = _rmsnorm(x1_scr[...], g2_ref[...]).astype(BF16)
    for c0, c1 in _ff_chunks(wg_ref.shape[1]):
        gate = _dot(h_scr[...], wg_ref[:, c0:c1])
        up = _dot(h_scr[...], wu_ref[:, c0:c1])
        act = (_silu(gate) * up).astype(BF16)
        x1_scr[...] += _dot(act, wd_ref[c0:c1, :])
    y_ref[...] = _rmsnorm(x1_scr[...], gf_ref[...])


def _prompt_kernel(x_ref, cs_ref, sn_ref, g1_ref, win_ref, convw_ref, gng_ref, wout_ref, g2_ref,
                   wg_ref, wu_ref, wd_ref, gf_ref,
                   y_ref, cnew_ref, ret_ref,
                   proj_scr, uext_scr, mix_scr, x1_scr, h_scr, dec_scr, xi_scr, zeta_scr):
    tile = x_ref.shape[1]
    t = pl.program_id(1)
    chunk = RET_CHUNK

    @pl.when((pl.program_id(0) == 0) & (t == 0))
    def _():
        i = lax.broadcasted_iota(jnp.int32, (chunk, chunk), 0)
        j = lax.broadcasted_iota(jnp.int32, (chunk, chunk), 1)
        causal = i >= j
        diff = jnp.where(causal, i - j, 0).astype(F32)
        i_f = i.astype(F32)
        for hh in range(RET_HEADS):
            dec_scr[hh] = jnp.where(causal, jnp.exp(diff * LOG_G[hh]), 0.0)
            xi_scr[hh] = jnp.exp((i_f + 1.0) * LOG_G[hh])
            zeta_scr[hh] = jnp.exp((chunk - 1.0 - i_f) * LOG_G[hh])

    @pl.when(t == 0)
    def _():
        ret_ref[...] = jnp.zeros_like(ret_ref)
        uext_scr[0:SUBLANES, :] = jnp.zeros((SUBLANES, uext_scr.shape[1]), F32)

    x = x_ref[0]
    _in_proj(x, g1_ref, win_ref, h_scr, proj_scr)

    w = RET_WIDTH
    for c in range(tile // chunk):
        r0 = c * chunk
        cs = cs_ref[r0:r0 + chunk, :]
        sn = sn_ref[r0:r0 + chunk, :]
        for hh in range(RET_HEADS):
            c0 = hh * RET_HD
            q = _rotary(proj_scr[r0:r0 + chunk, c0:c0 + RET_HD], cs, sn)
            k = _rotary(proj_scr[r0:r0 + chunk, w + c0:w + c0 + RET_HD], cs, sn) * (RET_HD ** -0.5)
            v = proj_scr[r0:r0 + chunk, 2 * w + c0:2 * w + c0 + RET_HD]
            g = proj_scr[r0:r0 + chunk, 3 * w + c0:3 * w + c0 + RET_HD]
            qb, kb, vb = q.astype(BF16), k.astype(BF16), v.astype(BF16)
            state = ret_ref[0, hh]
            s = _dot_nt(qb, kb) * dec_scr[hh]
            o = _dot(s.astype(BF16), vb) + _dot(qb, state.astype(BF16)) * xi_scr[hh]
            kz = (k * zeta_scr[hh]).astype(BF16)
            ret_ref[0, hh] = state * math.exp(chunk * LOG_G[hh]) + _dot_tn(kz, vb)
            mix_scr[r0:r0 + chunk, c0:c0 + RET_HD] = _groupnorm_gate(
                o, g, gng_ref[:, c0:c0 + RET_HD]).astype(BF16)

    cw = w
    u = proj_scr[:, 5 * w:5 * w + cw] * proj_scr[:, 5 * w + cw:5 * w + 2 * cw]
    uext_scr[SUBLANES:SUBLANES + tile, :] = u
    um1 = uext_scr[SUBLANES - 1:SUBLANES - 1 + tile, :]
    um2 = uext_scr[SUBLANES - 2:SUBLANES - 2 + tile, :]
    conv_y = convw_ref[0:1, :] * um2 + convw_ref[1:2, :] * um1 + convw_ref[2:3, :] * u
    mix_scr[:, w:w + cw] = (proj_scr[:, 4 * w:4 * w + cw] * conv_y).astype(BF16)
    tail = uext_scr[tile + SUBLANES - (CONV_K - 1):tile + SUBLANES, :]
    cnew_ref[0] = tail
    uext_scr[SUBLANES - (CONV_K - 1):SUBLANES, :] = tail

    _dense_tail(x, mix_scr, wout_ref, g2_ref, wg_ref, wu_ref, wd_ref, gf_ref, y_ref.at[0], x1_scr, h_scr)


def _sample_kernel(x_ref, cs_ref, sn_ref, cst_ref, sret_ref, g1_ref, win_ref, convw_ref, gng_ref,
                   wout_ref, g2_ref, wg_ref, wu_ref, wd_ref, gf_ref,
                   y_ref, cnew_ref, retnew_ref,
                   proj_scr, mix_scr, x1_scr, h_scr, mask_scr, xi_scr, zeta_scr):
    tile = x_ref.shape[0]
    nseq = sret_ref.shape[0]
    seqlen = tile // nseq
    assert seqlen == SUBLANES, "decode sequences must fill exactly one f32 sublane tile"

    @pl.when(pl.program_id(0) == 0)
    def _():
        i = lax.broadcasted_iota(jnp.int32, (tile, tile), 0)
        j = lax.broadcasted_iota(jnp.int32, (tile, tile), 1)
        keep = (i >= j) & ((i // seqlen) == (j // seqlen))
        diff = jnp.where(keep, i - j, 0).astype(F32)
        ii = lax.broadcasted_iota(jnp.int32, (tile, RET_HD), 0)
        pos = (ii % seqlen).astype(F32)
        for hh in range(RET_HEADS):
            mask_scr[hh] = jnp.where(keep, jnp.exp(diff * LOG_G[hh]), 0.0)
            xi_scr[hh] = jnp.exp((pos + 1.0) * LOG_G[hh])
            zeta_scr[hh] = jnp.exp((seqlen - 1.0 - pos) * LOG_G[hh])

    x = x_ref[...]
    _in_proj(x, g1_ref, win_ref, h_scr, proj_scr)

    w = RET_WIDTH
    cs = cs_ref[...]
    sn = sn_ref[...]
    pair_rows = 2 * seqlen
    rowblk = lax.broadcasted_iota(jnp.int32, (tile, 2 * RET_HD), 0) // seqlen
    colhalf = lax.broadcasted_iota(jnp.int32, (tile, 2 * RET_HD), 1) // RET_HD
    for hh in range(RET_HEADS):
        c0 = hh * RET_HD
        q = _rotary(proj_scr[:, c0:c0 + RET_HD], cs, sn)
        k = _rotary(proj_scr[:, w + c0:w + c0 + RET_HD], cs, sn) * (RET_HD ** -0.5)
        v = proj_scr[:, 2 * w + c0:2 * w + c0 + RET_HD]
        g = proj_scr[:, 3 * w + c0:3 * w + c0 + RET_HD]
        qb, kb, vb = q.astype(BF16), k.astype(BF16), v.astype(BF16)
        s = _dot_nt(qb, kb) * mask_scr[hh]
        o_intra = _dot(s.astype(BF16), vb)
        kzt = (k * zeta_scr[hh]).T.astype(BF16)
        vv = jnp.concatenate([v, v], axis=1)
        g_chunk = math.exp(seqlen * LOG_G[hh])
        cross = []
        for p in range(nseq // 2):
            sa, sb = 2 * p, 2 * p + 1
            ra = sret_ref[sa, hh]
            rb = sret_ref[sb, hh]
            rcat = jnp.concatenate([ra, rb], axis=1).astype(BF16)
            pr = _dot(qb[p * pair_rows:(p + 1) * pair_rows, :], rcat)
            cross.append(pr[0:seqlen, 0:RET_HD])
            cross.append(pr[seqlen:pair_rows, RET_HD:2 * RET_HD])
            vpair = jnp.where(rowblk == sa + colhalf, vv, 0.0).astype(BF16)
            upd = _dot(kzt, vpair)
            retnew_ref[sa, hh] = ra * g_chunk + upd[:, 0:RET_HD]
            retnew_ref[sb, hh] = rb * g_chunk + upd[:, RET_HD:2 * RET_HD]
        o = o_intra + jnp.concatenate(cross, axis=0) * xi_scr[hh]
        mix_scr[:, c0:c0 + RET_HD] = _groupnorm_gate(o, g, gng_ref[:, c0:c0 + RET_HD]).astype(BF16)

    cw = w
    u = proj_scr[:, 5 * w:5 * w + cw] * proj_scr[:, 5 * w + cw:5 * w + 2 * cw]
    cst = cst_ref[...]
    r8 = lax.broadcasted_iota(jnp.int32, (tile, cw), 0) % seqlen
    um1 = jnp.where(r8 == 0, pltpu.roll(cst, tile - 1, axis=0), pltpu.roll(u, 1, axis=0))
    um2 = jnp.where(r8 < 2, cst, pltpu.roll(u, 2, axis=0))
    conv_y = convw_ref[0:1, :] * um2 + convw_ref[1:2, :] * um1 + convw_ref[2:3, :] * u
    mix_scr[:, w:w + cw] = (proj_scr[:, 4 * w:4 * w + cw] * conv_y).astype(BF16)
    cnew_ref[...] = u.reshape(nseq, seqlen, cw)[:, seqlen - (CONV_K - 1):, :]

    _dense_tail(x, mix_scr, wout_ref, g2_ref, wg_ref, wu_ref, wd_ref, gf_ref, y_ref, x1_scr, h_scr)


def _resident(shape):
    nd = len(shape)
    return pl.BlockSpec(shape, lambda *_: (0,) * nd, pipeline_mode=pl.Buffered(1))


def _weight_specs(ws):
    return [_resident(wt.shape) for wt in ws]


def _prompt_call(x, ws):
    batch, seq, d = x.shape
    tile = PROMPT_TILE
    cs, sn = _rope_tables(np.arange(seq))
    in_cols = ws[1].shape[1]
    cw = ws[2].shape[1]
    return pl.pallas_call(
        _prompt_kernel,
        grid=(batch, seq // tile),
        in_specs=[pl.BlockSpec((1, tile, d), lambda b, t: (b, t, 0)),
                  pl.BlockSpec((tile, RET_HD), lambda b, t: (t, 0)),
                  pl.BlockSpec((tile, RET_HD), lambda b, t: (t, 0))] + _weight_specs(ws),
        out_specs=[pl.BlockSpec((1, tile, d), lambda b, t: (b, t, 0)),
                   pl.BlockSpec((1, CONV_K - 1, cw), lambda b, t: (b, 0, 0)),
                   pl.BlockSpec((1, RET_HEADS, RET_HD, RET_HD), lambda b, t: (b, 0, 0, 0))],
        out_shape=[jax.ShapeDtypeStruct((batch, seq, d), F32),
                   jax.ShapeDtypeStruct((batch, CONV_K - 1, cw), F32),
                   jax.ShapeDtypeStruct((batch, RET_HEADS, RET_HD, RET_HD), F32)],
        scratch_shapes=[pltpu.VMEM((tile, in_cols), F32),
                        pltpu.VMEM((tile + SUBLANES, cw), F32),
                        pltpu.VMEM((tile, d), BF16),
                        pltpu.VMEM((tile, d), F32),
                        pltpu.VMEM((tile, d), BF16),
                        pltpu.VMEM((RET_HEADS, RET_CHUNK, RET_CHUNK), F32),
                        pltpu.VMEM((RET_HEADS, RET_CHUNK, RET_HD), F32),
                        pltpu.VMEM((RET_HEADS, RET_CHUNK, RET_HD), F32)],
        compiler_params=pltpu.CompilerParams(dimension_semantics=("arbitrary", "arbitrary"),
                                             vmem_limit_bytes=VMEM_LIMIT_BYTES),
        name="prompt_layer",
    )(x, cs, sn, *ws)


def _sample_call(x, state_conv, state_ret, ws):
    nseq_all, seqlen, d = x.shape
    nseq = SAMPLE_SEQS
    tile = nseq * seqlen
    cs8, sn8 = _rope_tables(PAST_LEN + np.arange(seqlen))
    cs, sn = jnp.tile(cs8, (nseq, 1)), jnp.tile(sn8, (nseq, 1))
    in_cols = ws[1].shape[1]
    cw = ws[2].shape[1]
    cst = jnp.pad(state_conv, ((0, 0), (0, seqlen - (CONV_K - 1)), (0, 0))).reshape(nseq_all * seqlen, cw)
    y, cnew, rnew = pl.pallas_call(
        _sample_kernel,
        grid=(nseq_all // nseq,),
        in_specs=[pl.BlockSpec((tile, d), lambda i: (i, 0)),
                  _resident((tile, RET_HD)), _resident((tile, RET_HD)),
                  pl.BlockSpec((tile, cw), lambda i: (i, 0)),
                  pl.BlockSpec((nseq, RET_HEADS, RET_HD, RET_HD), lambda i: (i, 0, 0, 0))] + _weight_specs(ws),
        out_specs=[pl.BlockSpec((tile, d), lambda i: (i, 0)),
                   pl.BlockSpec((nseq, CONV_K - 1, cw), lambda i: (i, 0, 0)),
                   pl.BlockSpec((nseq, RET_HEADS, RET_HD, RET_HD), lambda i: (i, 0, 0, 0))],
        out_shape=[jax.ShapeDtypeStruct((nseq_all * seqlen, d), F32),
                   jax.ShapeDtypeStruct((nseq_all, CONV_K - 1, cw), F32),
                   jax.ShapeDtypeStruct((nseq_all, RET_HEADS, RET_HD, RET_HD), F32)],
        scratch_shapes=[pltpu.VMEM((tile, in_cols), F32),
                        pltpu.VMEM((tile, d), BF16),
                        pltpu.VMEM((tile, d), F32),
                        pltpu.VMEM((tile, d), BF16),
                        pltpu.VMEM((RET_HEADS, tile, tile), F32),
                        pltpu.VMEM((RET_HEADS, tile, RET_HD), F32),
                        pltpu.VMEM((RET_HEADS, tile, RET_HD), F32)],
        compiler_params=pltpu.CompilerParams(dimension_semantics=("arbitrary",),
                                             vmem_limit_bytes=VMEM_LIMIT_BYTES),
        name="sample_layer",
    )(x.reshape(nseq_all * seqlen, d), cs, sn, cst, state_ret, *ws)
    return y.reshape(nseq_all, seqlen, d), cnew, rnew


def kernel(x_prompt, x_sample, state_conv, state_ret, norm1_g, w_in, conv_w, ret_gn_g, w_out, norm2_g,
           w_gate, w_up, w_down, norm_f_g):
    depth = w_in.shape[0]
    assert depth == 1, "the fused layer kernels take a single layer"
    row = lambda g: g.reshape(1, -1)
    ws = (row(norm1_g[0]), w_in[0].astype(BF16), conv_w[0], row(ret_gn_g[0]), w_out[0].astype(BF16),
          row(norm2_g[0]), w_gate[0].astype(BF16), w_up[0].astype(BF16), w_down[0].astype(BF16),
          row(norm_f_g))
    y_p, cnew_p, ret_p = _prompt_call(x_prompt, ws)
    y_s, cnew_s, ret_s = _sample_call(x_sample, state_conv[0], state_ret[0], ws)
    return (y_p, y_s, cnew_p[None], ret_p[None], cnew_s[None], ret_s[None])
```

```python
import functools
import math

import numpy as np
import jax
import jax.numpy as jnp
from jax import lax
from jax.experimental import pallas as pl
from jax.experimental.pallas import tpu as pltpu

F32 = jnp.float32
BF16 = jnp.bfloat16

RET_HEADS = 4
RET_HD = 128
RET_WIDTH = RET_HEADS * RET_HD
CONV_K = 3
RET_CHUNK = 128
ROPE_BASE = 10000.0
NORM_EPS = 1e-6
GN_EPS = 1e-5
PAST_LEN = 16384

LOG_G = tuple(math.log1p(-(2.0 ** (-5.0 - h))) for h in range(RET_HEADS))

SUBLANES = 8
MXU_N = 256
PROMPT_TILE = 256
SAMPLE_SEQS = 16
VMEM_LIMIT_BYTES = 56 * 1024 * 1024


def _ff_chunks(d_ff, width=1536):
    edges = list(range(0, d_ff, width)) + [d_ff]
    return tuple(zip(edges[:-1], edges[1:]))


def _rope_tables(pos):
    half = RET_HD // 2
    inv = ROPE_BASE ** (-np.arange(half, dtype=np.float64) / half)
    ang = np.asarray(pos, np.float64)[:, None] * inv[None, :]
    cos, sin = np.cos(ang), np.sin(ang)
    cs = np.concatenate([cos, cos], axis=1).astype(np.float32)
    sn = np.concatenate([-sin, sin], axis=1).astype(np.float32)
    return jnp.asarray(cs), jnp.asarray(sn)


def _rmsnorm(x, g):
    ms = jnp.mean(x * x, axis=-1, keepdims=True)
    return x * lax.rsqrt(ms + NORM_EPS) * g


def _silu(x):
    return x * (1.0 / (1.0 + jnp.exp(-x)))


def _rotary(t, cs, sn):
    return t * cs + pltpu.roll(t, RET_HD // 2, axis=1) * sn


def _dot(a, b):
    return jnp.dot(a, b, preferred_element_type=F32)


def _dot_nt(a, b):
    return lax.dot_general(a, b, (((1,), (1,)), ((), ())), preferred_element_type=F32)


def _dot_tn(a, b):
    return lax.dot_general(a, b, (((0,), (0,)), ((), ())), preferred_element_type=F32)


def _groupnorm_gate(o, g, gn_g):
    mu = jnp.mean(o, axis=-1, keepdims=True)
    d = o - mu
    var = jnp.mean(d * d, axis=-1, keepdims=True)
    return _silu(g) * (d * lax.rsqrt(var + GN_EPS) * gn_g)


def _in_proj(x, g1_ref, win_ref, h_scr, proj_scr):
    h_scr[...] = _rmsnorm(x, g1_ref[...]).astype(BF16)
    proj_scr[...] = _dot(h_scr[...], win_ref[...])


def _out_proj(x, mix_scr, wout_ref, g2_ref, x1_scr, h_scr):
    x1_scr[...] = x + _dot(mix_scr[...], wout_ref[...])
    h_scr[...] = _rmsnorm(x1_scr[...], g2_ref[...]).astype(BF16)


def _ffn(wg_ref, wu_ref, wd_ref, gf_ref, y_ref, x1_scr, h_scr):
    for c0, c1 in _ff_chunks(wg_ref.shape[1]):
        gate = _dot(h_scr[...], wg_ref[:, c0:c1])
        up = _dot(h_scr[...], wu_ref[:, c0:c1])
        act = (_silu(gate) * up).astype(BF16)
        x1_scr[...] += _dot(act, wd_ref[c0:c1, :])
    y_ref[...] = _rmsnorm(x1_scr[...], gf_ref[...])


def _prompt_kernel(x_ref, cs_ref, sn_ref, g1_ref, win_ref, convw_ref, gng_ref, wout_ref, g2_ref,
                   wg_ref, wu_ref, wd_ref, gf_ref,
                   y_ref, cnew_ref, ret_ref,
                   proj_scr, uext_scr, convy_scr, mix_scr, xkeep_scr, x1_scr, h_scr, h2_scr, act_scr, r_scr,
                   dec_scr, xi_scr, zeta_scr, *, n_tiles, tiles_per_seq):
    tile = x_ref.shape[0]
    i = pl.program_id(0)
    chunk = RET_CHUNK
    valid = i < n_tiles
    t = jnp.minimum(i, n_tiles - 1) % tiles_per_seq
    seq_start = t == 0
    seq_end = t == tiles_per_seq - 1

    @pl.when(i == 0)
    def _():
        ii = lax.broadcasted_iota(jnp.int32, (chunk, chunk), 0)
        jj = lax.broadcasted_iota(jnp.int32, (chunk, chunk), 1)
        causal = ii >= jj
        diff = jnp.where(causal, ii - jj, 0).astype(F32)
        i_f = ii.astype(F32)
        for hh in range(RET_HEADS):
            dec_scr[hh] = jnp.where(causal, jnp.exp(diff * LOG_G[hh]), 0.0)
            xi_scr[hh] = jnp.exp((i_f + 1.0) * LOG_G[hh])
            zeta_scr[hh] = jnp.exp((chunk - 1.0 - i_f) * LOG_G[hh])
        r_scr[...] = jnp.zeros_like(r_scr)
        uext_scr[0:SUBLANES, :] = jnp.zeros((SUBLANES, uext_scr.shape[1]), F32)
        mix_scr[...] = jnp.zeros_like(mix_scr)
        xkeep_scr[...] = jnp.zeros_like(xkeep_scr)

    w = RET_WIDTH
    cw = w
    d_ff = wg_ref.shape[1]

    x1_scr[...] = xkeep_scr[...] + _dot(mix_scr[...], wout_ref[...])
    x = x_ref[...]
    h_scr[...] = _rmsnorm(x, g1_ref[...]).astype(BF16)
    proj_scr[:, 0:2 * w] = _dot(h_scr[...], win_ref[:, 0:2 * w])
    h2_scr[...] = _rmsnorm(x1_scr[...], g2_ref[...]).astype(BF16)
    proj_scr[:, 2 * w:4 * w] = _dot(h_scr[...], win_ref[:, 2 * w:4 * w])

    def ffn_piece(c0):
        def emit():
            gate = _dot(h2_scr[...], wg_ref[:, c0:c0 + MXU_N])
            up = _dot(h2_scr[...], wu_ref[:, c0:c0 + MXU_N])
            act_scr[:, c0:c0 + MXU_N] = (_silu(gate) * up).astype(BF16)
        return emit

    def proj_piece(c0):
        def emit():
            proj_scr[:, c0:c0 + cw] = _dot(h_scr[...], win_ref[:, c0:c0 + cw])
        return emit

    fillers = ([proj_piece(5 * w), proj_piece(6 * w)]
               + [ffn_piece(c0) for c0 in range(0, d_ff, MXU_N)])

    def fill(n):
        for _ in range(min(n, len(fillers))):
            fillers.pop(0)()

    n_chunks = tile // chunk
    units = [(c, hh) for c in range(n_chunks) for hh in range(RET_HEADS)]
    qbs, vbs, scores, updates = {}, {}, {}, {}
    for c, hh in units:
        r0, c0 = c * chunk, hh * RET_HD
        cs = cs_ref[r0:r0 + chunk, :]
        sn = sn_ref[r0:r0 + chunk, :]
        q = _rotary(proj_scr[r0:r0 + chunk, c0:c0 + RET_HD], cs, sn)
        k = _rotary(proj_scr[r0:r0 + chunk, w + c0:w + c0 + RET_HD], cs, sn) * (RET_HD ** -0.5)
        qb, kb = q.astype(BF16), k.astype(BF16)
        vb = proj_scr[r0:r0 + chunk, 2 * w + c0:2 * w + c0 + RET_HD].astype(BF16)
        kz = (k * zeta_scr[hh]).astype(BF16)
        qbs[c, hh], vbs[c, hh] = qb, vb
        scores[c, hh] = _dot_nt(qb, kb)
        updates[c, hh] = _dot_tn(kz, vb)
    fill(2)

    u = proj_scr[:, 5 * w:5 * w + cw] * proj_scr[:, 5 * w + cw:5 * w + 2 * cw]
    uext_scr[SUBLANES:SUBLANES + tile, :] = u
    um1 = uext_scr[SUBLANES - 1:SUBLANES - 1 + tile, :]
    um2 = uext_scr[SUBLANES - 2:SUBLANES - 2 + tile, :]
    convy_scr[...] = convw_ref[0:1, :] * um2 + convw_ref[1:2, :] * um1 + convw_ref[2:3, :] * u
    tail = uext_scr[tile + SUBLANES - (CONV_K - 1):tile + SUBLANES, :]
    cnew_ref[0] = tail
    uext_scr[SUBLANES - (CONV_K - 1):SUBLANES, :] = jnp.where(seq_end, 0.0, tail)
    fill(len(fillers) - len(units) + 1)

    outs = {}
    for hh in range(RET_HEADS):
        prev = r_scr[hh]
        state = jnp.where(seq_start, 0.0, prev)
        for c in range(n_chunks):
            s = (scores[c, hh] * dec_scr[hh]).astype(BF16)
            outs[c, hh] = _dot(s, vbs[c, hh]) + _dot(qbs[c, hh], state.astype(BF16)) * xi_scr[hh]
            state = state * math.exp(chunk * LOG_G[hh]) + updates[c, hh]
        state = jnp.where(valid, state, prev)
        r_scr[hh] = state
        ret_ref[0, hh] = state
    for c, hh in units:
        fill(1)
        r0, c0 = c * chunk, hh * RET_HD
        g = proj_scr[r0:r0 + chunk, 3 * w + c0:3 * w + c0 + RET_HD]
        mix_scr[r0:r0 + chunk, c0:c0 + RET_HD] = _groupnorm_gate(
            outs[c, hh], g, gng_ref[:, c0:c0 + RET_HD]).astype(BF16)
    assert not fillers

    down = _dot(act_scr[...], wd_ref[...])
    proj_piece(4 * w)()
    y_ref[...] = _rmsnorm(x1_scr[...] + down, gf_ref[...])
    mix_scr[:, w:w + cw] = (proj_scr[:, 4 * w:4 * w + cw] * convy_scr[...]).astype(BF16)
    xkeep_scr[...] = x


def _sample_kernel(x_ref, cs_ref, sn_ref, cst_ref, sret_ref, g1_ref, win_ref, convw_ref, gng_ref,
                   wout_ref, g2_ref, wg_ref, wu_ref, wd_ref, gf_ref,
                   y_ref, cnew_ref, retnew_ref,
                   proj_scr, mix_scr, x1_scr, h_scr, mask_scr, xi_scr, zeta_scr):
    tile = x_ref.shape[0]
    nseq = sret_ref.shape[0]
    seqlen = tile // nseq
    assert seqlen == SUBLANES, "decode sequences must fill exactly one f32 sublane tile"

    @pl.when(pl.program_id(0) == 0)
    def _():
        i = lax.broadcasted_iota(jnp.int32, (tile, tile), 0)
        j = lax.broadcasted_iota(jnp.int32, (tile, tile), 1)
        keep = (i >= j) & ((i // seqlen) == (j // seqlen))
        diff = jnp.where(keep, i - j, 0).astype(F32)
        ii = lax.broadcasted_iota(jnp.int32, (tile, RET_HD), 0)
        pos = (ii % seqlen).astype(F32)
        for hh in range(RET_HEADS):
            mask_scr[hh] = jnp.where(keep, jnp.exp(diff * LOG_G[hh]), 0.0)
            xi_scr[hh] = jnp.exp((pos + 1.0) * LOG_G[hh])
            zeta_scr[hh] = jnp.exp((seqlen - 1.0 - pos) * LOG_G[hh])

    x = x_ref[...]
    _in_proj(x, g1_ref, win_ref, h_scr, proj_scr)

    w = RET_WIDTH
    cs = cs_ref[...]
    sn = sn_ref[...]
    pair_rows = 2 * seqlen
    rowblk = lax.broadcasted_iota(jnp.int32, (tile, 2 * RET_HD), 0) // seqlen
    colhalf = lax.broadcasted_iota(jnp.int32, (tile, 2 * RET_HD), 1) // RET_HD
    for hh in range(RET_HEADS):
        c0 = hh * RET_HD
        q = _rotary(proj_scr[:, c0:c0 + RET_HD], cs, sn)
        k = _rotary(proj_scr[:, w + c0:w + c0 + RET_HD], cs, sn) * (RET_HD ** -0.5)
        v = proj_scr[:, 2 * w + c0:2 * w + c0 + RET_HD]
        g = proj_scr[:, 3 * w + c0:3 * w + c0 + RET_HD]
        qb, kb, vb = q.astype(BF16), k.astype(BF16), v.astype(BF16)
        s = _dot_nt(qb, kb) * mask_scr[hh]
        o_intra = _dot(s.astype(BF16), vb)
        kzt = (k * zeta_scr[hh]).T.astype(BF16)
        vv = jnp.concatenate([v, v], axis=1)
        g_chunk = math.exp(seqlen * LOG_G[hh])
        cross = []
        for p in range(nseq // 2):
            sa, sb = 2 * p, 2 * p + 1
            ra = sret_ref[sa, hh]
            rb = sret_ref[sb, hh]
            rcat = jnp.concatenate([ra, rb], axis=1).astype(BF16)
            pr = _dot(qb[p * pair_rows:(p + 1) * pair_rows, :], rcat)
            cross.append(pr[0:seqlen, 0:RET_HD])
            cross.append(pr[seqlen:pair_rows, RET_HD:2 * RET_HD])
            vpair = jnp.where(rowblk == sa + colhalf, vv, 0.0).astype(BF16)
            upd = _dot(kzt, vpair)
            retnew_ref[sa, hh] = ra * g_chunk + upd[:, 0:RET_HD]
            retnew_ref[sb, hh] = rb * g_chunk + upd[:, RET_HD:2 * RET_HD]
        o = o_intra + jnp.concatenate(cross, axis=0) * xi_scr[hh]
        mix_scr[:, c0:c0 + RET_HD] = _groupnorm_gate(o, g, gng_ref[:, c0:c0 + RET_HD]).astype(BF16)

    cw = w
    u = proj_scr[:, 5 * w:5 * w + cw] * proj_scr[:, 5 * w + cw:5 * w + 2 * cw]
    cst = cst_ref[...]
    r8 = lax.broadcasted_iota(jnp.int32, (tile, cw), 0) % seqlen
    um1 = jnp.where(r8 == 0, pltpu.roll(cst, tile - 1, axis=0), pltpu.roll(u, 1, axis=0))
    um2 = jnp.where(r8 < 2, cst, pltpu.roll(u, 2, axis=0))
    conv_y = convw_ref[0:1, :] * um2 + convw_ref[1:2, :] * um1 + convw_ref[2:3, :] * u
    mix_scr[:, w:w + cw] = (proj_scr[:, 4 * w:4 * w + cw] * conv_y).astype(BF16)
    cnew_ref[...] = u.reshape(nseq, seqlen, cw)[:, seqlen - (CONV_K - 1):, :]

    _out_proj(x, mix_scr, wout_ref, g2_ref, x1_scr, h_scr)
    _ffn(wg_ref, wu_ref, wd_ref, gf_ref, y_ref, x1_scr, h_scr)


def _resident(shape):
    nd = len(shape)
    return pl.BlockSpec(shape, lambda *_: (0,) * nd, pipeline_mode=pl.Buffered(1))


def _weight_specs(ws):
    return [_resident(wt.shape) for wt in ws]


def _prompt_call(x, ws):
    batch, seq, d = x.shape
    tile = PROMPT_TILE
    tiles_per_seq = seq // tile
    n_tiles = batch * tiles_per_seq
    cs, sn = _rope_tables(np.arange(seq))
    in_cols = ws[1].shape[1]
    cw = ws[2].shape[1]
    mixer_tile = lambda i: jnp.minimum(i, n_tiles - 1)
    dense_tile = lambda i: jnp.maximum(i - 1, 0)
    y, cnew, ret = pl.pallas_call(
        functools.partial(_prompt_kernel, n_tiles=n_tiles, tiles_per_seq=tiles_per_seq),
        grid=(n_tiles + 1,),
        in_specs=[pl.BlockSpec((tile, d), lambda i: (mixer_tile(i), 0)),
                  pl.BlockSpec((tile, RET_HD), lambda i: (mixer_tile(i) % tiles_per_seq, 0)),
                  pl.BlockSpec((tile, RET_HD), lambda i: (mixer_tile(i) % tiles_per_seq, 0))] + _weight_specs(ws),
        out_specs=[pl.BlockSpec((tile, d), lambda i: (dense_tile(i), 0)),
                   pl.BlockSpec((1, CONV_K - 1, cw), lambda i: (mixer_tile(i) // tiles_per_seq, 0, 0)),
                   pl.BlockSpec((1, RET_HEADS, RET_HD, RET_HD),
                                lambda i: (mixer_tile(i) // tiles_per_seq, 0, 0, 0))],
        out_shape=[jax.ShapeDtypeStruct((batch * seq, d), F32),
                   jax.ShapeDtypeStruct((batch, CONV_K - 1, cw), F32),
                   jax.ShapeDtypeStruct((batch, RET_HEADS, RET_HD, RET_HD), F32)],
        scratch_shapes=[pltpu.VMEM((tile, in_cols), F32),
                        pltpu.VMEM((tile + SUBLANES, cw), F32),
                        pltpu.VMEM((tile, cw), F32),
                        pltpu.VMEM((tile, d), BF16),
                        pltpu.VMEM((tile, d), F32),
                        pltpu.VMEM((tile, d), F32),
                        pltpu.VMEM((tile, d), BF16),
                        pltpu.VMEM((tile, d), BF16),
                        pltpu.VMEM((tile, ws[6].shape[1]), BF16),
                        pltpu.VMEM((RET_HEADS, RET_HD, RET_HD), F32),
                        pltpu.VMEM((RET_HEADS, RET_CHUNK, RET_CHUNK), F32),
                        pltpu.VMEM((RET_HEADS, RET_CHUNK, RET_HD), F32),
                        pltpu.VMEM((RET_HEADS, RET_CHUNK, RET_HD), F32)],
        compiler_params=pltpu.CompilerParams(dimension_semantics=("arbitrary",),
                                             vmem_limit_bytes=VMEM_LIMIT_BYTES),
        name="prompt_layer",
    )(x.reshape(batch * seq, d), cs, sn, *ws)
    return y.reshape(batch, seq, d), cnew, ret


def _sample_call(x, state_conv, state_ret, ws):
    nseq_all, seqlen, d = x.shape
    nseq = SAMPLE_SEQS
    tile = nseq * seqlen
    cs8, sn8 = _rope_tables(PAST_LEN + np.arange(seqlen))
    cs, sn = jnp.tile(cs8, (nseq, 1)), jnp.tile(sn8, (nseq, 1))
    in_cols = ws[1].shape[1]
    cw = ws[2].shape[1]
    cst = jnp.pad(state_conv, ((0, 0), (0, seqlen - (CONV_K - 1)), (0, 0))).reshape(nseq_all * seqlen, cw)
    y, cnew, rnew = pl.pallas_call(
        _sample_kernel,
        grid=(nseq_all // nseq,),
        in_specs=[pl.BlockSpec((tile, d), lambda i: (i, 0)),
                  _resident((tile, RET_HD)), _resident((tile, RET_HD)),
                  pl.BlockSpec((tile, cw), lambda i: (i, 0)),
                  pl.BlockSpec((nseq, RET_HEADS, RET_HD, RET_HD), lambda i: (i, 0, 0, 0))] + _weight_specs(ws),
        out_specs=[pl.BlockSpec((tile, d), lambda i: (i, 0)),
                   pl.BlockSpec((nseq, CONV_K - 1, cw), lambda i: (i, 0, 0)),
                   pl.BlockSpec((nseq, RET_HEADS, RET_HD, RET_HD), lambda i: (i, 0, 0, 0))],
        out_shape=[jax.ShapeDtypeStruct((nseq_all * seqlen, d), F32),
                   jax.ShapeDtypeStruct((nseq_all, CONV_K - 1, cw), F32),
                   jax.ShapeDtypeStruct((nseq_all, RET_HEADS, RET_HD, RET_HD), F32)],
        scratch_shapes=[pltpu.VMEM((tile, in_cols), F32),
                        pltpu.VMEM((tile, d), BF16),
                        pltpu.VMEM((tile, d), F32),
                        pltpu.VMEM((tile, d), BF16),
                        pltpu.VMEM((RET_HEADS, tile, tile), F32),
                        pltpu.VMEM((RET_HEADS, tile, RET_HD), F32),
                        pltpu.VMEM((RET_HEADS, tile, RET_HD), F32)],
        compiler_params=pltpu.CompilerParams(dimension_semantics=("arbitrary",),
                                             vmem_limit_bytes=VMEM_LIMIT_BYTES),
        name="sample_layer",
    )(x.reshape(nseq_all * seqlen, d), cs, sn, cst, state_ret, *ws)
    return y.reshape(nseq_all, seqlen, d), cnew, rnew


def kernel(x_prompt, x_sample, state_conv, state_ret, norm1_g, w_in, conv_w, ret_gn_g, w_out, norm2_g,
           w_gate, w_up, w_down, norm_f_g):
    depth = w_in.shape[0]
    assert depth == 1, "the fused layer kernels take a single layer"
    row = lambda g: g.reshape(1, -1)
    ws = (row(norm1_g[0]), w_in[0].astype(BF16), conv_w[0], row(ret_gn_g[0]), w_out[0].astype(BF16),
          row(norm2_g[0]), w_gate[0].astype(BF16), w_up[0].astype(BF16), w_down[0].astype(BF16),
          row(norm_f_g))
    y_p, cnew_p, ret_p = _prompt_call(x_prompt, ws)
    y_s, cnew_s, ret_s = _sample_call(x_sample, state_conv[0], state_ret[0], ws)
    return (y_p, y_s, cnew_p[None], ret_p[None], cnew_s[None], ret_s[None])
```

```python
import functools
import math

import numpy as np
import jax
import jax.numpy as jnp
from jax import lax
from jax.experimental import pallas as pl
from jax.experimental.pallas import tpu as pltpu

F32 = jnp.float32
BF16 = jnp.bfloat16

RET_HEADS = 4
RET_HD = 128
RET_WIDTH = RET_HEADS * RET_HD
CONV_K = 3
RET_CHUNK = 128
ROPE_BASE = 10000.0
NORM_EPS = 1e-6
GN_EPS = 1e-5
PAST_LEN = 16384

LOG_G = tuple(math.log1p(-(2.0 ** (-5.0 - h))) for h in range(RET_HEADS))

SUBLANES = 8
MXU_N = 256
MATRIX_SLOTS = (1, 4, 6, 7, 8)
STAGE_ROWS_WIDE = 128
STAGE_ROWS_NARROW = 256
PROMPT_TILE = 256
SAMPLE_SEQS = 16
VMEM_LIMIT_BYTES = 56 * 1024 * 1024


def _ff_chunks(d_ff, width=1536):
    edges = list(range(0, d_ff, width)) + [d_ff]
    return tuple(zip(edges[:-1], edges[1:]))


def _rope_tables(pos):
    half = RET_HD // 2
    inv = ROPE_BASE ** (-np.arange(half, dtype=np.float64) / half)
    ang = np.asarray(pos, np.float64)[:, None] * inv[None, :]
    cos, sin = np.cos(ang), np.sin(ang)
    cs = np.concatenate([cos, cos], axis=1).astype(np.float32)
    sn = np.concatenate([-sin, sin], axis=1).astype(np.float32)
    return jnp.asarray(cs), jnp.asarray(sn)


def _rmsnorm(x, g):
    ms = jnp.mean(x * x, axis=-1, keepdims=True)
    return x * lax.rsqrt(ms + NORM_EPS) * g


def _silu(x):
    return x * (1.0 / (1.0 + jnp.exp(-x)))


def _rotary(t, cs, sn):
    return t * cs + pltpu.roll(t, RET_HD // 2, axis=1) * sn


def _dot(a, b):
    return jnp.dot(a, b, preferred_element_type=F32)


def _dot_nt(a, b):
    return lax.dot_general(a, b, (((1,), (1,)), ((), ())), preferred_element_type=F32)


def _dot_tn(a, b):
    return lax.dot_general(a, b, (((0,), (0,)), ((), ())), preferred_element_type=F32)


def _groupnorm_gate(o, g, gn_g):
    mu = jnp.mean(o, axis=-1, keepdims=True)
    d = o - mu
    var = jnp.mean(d * d, axis=-1, keepdims=True)
    return _silu(g) * (d * lax.rsqrt(var + GN_EPS) * gn_g)


def _in_proj(x, g1_ref, win_ref, h_scr, proj_scr):
    h_scr[...] = _rmsnorm(x, g1_ref[...]).astype(BF16)
    proj_scr[...] = _dot(h_scr[...], win_ref[...])


def _out_proj(x, mix_scr, wout_ref, g2_ref, x1_scr, h_scr):
    x1_scr[...] = x + _dot(mix_scr[...], wout_ref[...])
    h_scr[...] = _rmsnorm(x1_scr[...], g2_ref[...]).astype(BF16)


def _ffn(wg_ref, wu_ref, wd_ref, gf_ref, y_ref, x1_scr, h_scr):
    for c0, c1 in _ff_chunks(wg_ref.shape[1]):
        gate = _dot(h_scr[...], wg_ref[:, c0:c1])
        up = _dot(h_scr[...], wu_ref[:, c0:c1])
        act = (_silu(gate) * up).astype(BF16)
        x1_scr[...] += _dot(act, wd_ref[c0:c1, :])
    y_ref[...] = _rmsnorm(x1_scr[...], gf_ref[...])


def _load_weight_bf16(w_hbm, w_bf, stage, sem):
    rb = stage.shape[1]
    n = w_hbm.shape[0] // rb
    assert n * rb == w_hbm.shape[0] and stage.shape[2] == w_hbm.shape[1]

    def chunk_copy(c, slot):
        return pltpu.make_async_copy(w_hbm.at[pl.ds(c * rb, rb), :], stage.at[slot], sem.at[slot])

    chunk_copy(0, 0).start()

    def body(c, carry):
        slot = c % 2

        @pl.when(c + 1 < n)
        def _():
            chunk_copy(c + 1, 1 - slot).start()

        chunk_copy(c, slot).wait()
        w_bf[pl.ds(pl.multiple_of(c * rb, rb), rb), :] = stage[slot].astype(BF16)
        return carry

    lax.fori_loop(0, n, body, 0)


def _prompt_kernel(x_ref, cs_ref, sn_ref, g1_ref, win_hbm, convw_ref, gng_ref, wout_hbm, g2_ref,
                   wg_hbm, wu_hbm, wd_hbm, gf_ref,
                   y_ref, cnew_ref, ret_ref, win_out, wout_out, wg_out, wu_out, wd_out,
                   win_ref, wout_ref, wg_ref, wu_ref, wd_ref, stage_in, stage_sq, stage_ff, load_sem, store_sem,
                   proj_scr, uext_scr, convy_scr, mix_scr, xkeep_scr, x1_scr, h_scr, h2_scr, act_scr, r_scr,
                   dec_scr, xi_scr, zeta_scr, *, n_tiles, tiles_per_seq):
    tile = x_ref.shape[0]
    i = pl.program_id(0)
    chunk = RET_CHUNK
    valid = i < n_tiles
    t = jnp.minimum(i, n_tiles - 1) % tiles_per_seq
    seq_start = t == 0
    seq_end = t == tiles_per_seq - 1
    weight_stores = [pltpu.make_async_copy(src, dst, store_sem.at[n]) for n, (src, dst) in enumerate(
        ((win_ref, win_out), (wout_ref, wout_out), (wg_ref, wg_out), (wu_ref, wu_out), (wd_ref, wd_out)))]

    @pl.when(i == 0)
    def _():
        _load_weight_bf16(win_hbm, win_ref, stage_in, load_sem)
        _load_weight_bf16(wout_hbm, wout_ref, stage_sq, load_sem)
        _load_weight_bf16(wg_hbm, wg_ref, stage_ff, load_sem)
        _load_weight_bf16(wu_hbm, wu_ref, stage_ff, load_sem)
        _load_weight_bf16(wd_hbm, wd_ref, stage_sq, load_sem)
        for cp in weight_stores:
            cp.start()
        ii = lax.broadcasted_iota(jnp.int32, (chunk, chunk), 0)
        jj = lax.broadcasted_iota(jnp.int32, (chunk, chunk), 1)
        causal = ii >= jj
        diff = jnp.where(causal, ii - jj, 0).astype(F32)
        i_f = ii.astype(F32)
        for hh in range(RET_HEADS):
            dec_scr[hh] = jnp.where(causal, jnp.exp(diff * LOG_G[hh]), 0.0)
            xi_scr[hh] = jnp.exp((i_f + 1.0) * LOG_G[hh])
            zeta_scr[hh] = jnp.exp((chunk - 1.0 - i_f) * LOG_G[hh])
        r_scr[...] = jnp.zeros_like(r_scr)
        uext_scr[0:SUBLANES, :] = jnp.zeros((SUBLANES, uext_scr.shape[1]), F32)
        mix_scr[...] = jnp.zeros_like(mix_scr)
        xkeep_scr[...] = jnp.zeros_like(xkeep_scr)

    w = RET_WIDTH
    cw = w
    d_ff = wg_ref.shape[1]

    x1_scr[...] = xkeep_scr[...] + _dot(mix_scr[...], wout_ref[...])
    x = x_ref[...]
    h_scr[...] = _rmsnorm(x, g1_ref[...]).astype(BF16)
    proj_scr[:, 0:2 * w] = _dot(h_scr[...], win_ref[:, 0:2 * w])
    h2_scr[...] = _rmsnorm(x1_scr[...], g2_ref[...]).astype(BF16)
    proj_scr[:, 2 * w:4 * w] = _dot(h_scr[...], win_ref[:, 2 * w:4 * w])

    def ffn_piece(c0):
        def emit():
            gate = _dot(h2_scr[...], wg_ref[:, c0:c0 + MXU_N])
            up = _dot(h2_scr[...], wu_ref[:, c0:c0 + MXU_N])
            act_scr[:, c0:c0 + MXU_N] = (_silu(gate) * up).astype(BF16)
        return emit

    def proj_piece(c0):
        def emit():
            proj_scr[:, c0:c0 + cw] = _dot(h_scr[...], win_ref[:, c0:c0 + cw])
        return emit

    fillers = ([proj_piece(5 * w), proj_piece(6 * w)]
               + [ffn_piece(c0) for c0 in range(0, d_ff, MXU_N)])

    def fill(n):
        for _ in range(min(n, len(fillers))):
            fillers.pop(0)()

    n_chunks = tile // chunk
    units = [(c, hh) for c in range(n_chunks) for hh in range(RET_HEADS)]
    qbs, vbs, scores, updates = {}, {}, {}, {}
    for c, hh in units:
        r0, c0 = c * chunk, hh * RET_HD
        cs = cs_ref[r0:r0 + chunk, :]
        sn = sn_ref[r0:r0 + chunk, :]
        q = _rotary(proj_scr[r0:r0 + chunk, c0:c0 + RET_HD], cs, sn)
        k = _rotary(proj_scr[r0:r0 + chunk, w + c0:w + c0 + RET_HD], cs, sn) * (RET_HD ** -0.5)
        qb, kb = q.astype(BF16), k.astype(BF16)
        vb = proj_scr[r0:r0 + chunk, 2 * w + c0:2 * w + c0 + RET_HD].astype(BF16)
        kz = (k * zeta_scr[hh]).astype(BF16)
        qbs[c, hh], vbs[c, hh] = qb, vb
        scores[c, hh] = _dot_nt(qb, kb)
        updates[c, hh] = _dot_tn(kz, vb)
    fill(2)

    u = proj_scr[:, 5 * w:5 * w + cw] * proj_scr[:, 5 * w + cw:5 * w + 2 * cw]
    uext_scr[SUBLANES:SUBLANES + tile, :] = u
    um1 = uext_scr[SUBLANES - 1:SUBLANES - 1 + tile, :]
    um2 = uext_scr[SUBLANES - 2:SUBLANES - 2 + tile, :]
    convy_scr[...] = convw_ref[0:1, :] * um2 + convw_ref[1:2, :] * um1 + convw_ref[2:3, :] * u
    tail = uext_scr[tile + SUBLANES - (CONV_K - 1):tile + SUBLANES, :]
    cnew_ref[0] = tail
    uext_scr[SUBLANES - (CONV_K - 1):SUBLANES, :] = jnp.where(seq_end, 0.0, tail)
    fill(len(fillers) - len(units) + 1)

    outs = {}
    for hh in range(RET_HEADS):
        prev = r_scr[hh]
        state = jnp.where(seq_start, 0.0, prev)
        for c in range(n_chunks):
            s = (scores[c, hh] * dec_scr[hh]).astype(BF16)
            outs[c, hh] = _dot(s, vbs[c, hh]) + _dot(qbs[c, hh], state.astype(BF16)) * xi_scr[hh]
            state = state * math.exp(chunk * LOG_G[hh]) + updates[c, hh]
        state = jnp.where(valid, state, prev)
        r_scr[hh] = state
        ret_ref[0, hh] = state
    for c, hh in units:
        fill(1)
        r0, c0 = c * chunk, hh * RET_HD
        g = proj_scr[r0:r0 + chunk, 3 * w + c0:3 * w + c0 + RET_HD]
        mix_scr[r0:r0 + chunk, c0:c0 + RET_HD] = _groupnorm_gate(
            outs[c, hh], g, gng_ref[:, c0:c0 + RET_HD]).astype(BF16)
    assert not fillers

    down = _dot(act_scr[...], wd_ref[...])
    proj_piece(4 * w)()
    y_ref[...] = _rmsnorm(x1_scr[...] + down, gf_ref[...])
    mix_scr[:, w:w + cw] = (proj_scr[:, 4 * w:4 * w + cw] * convy_scr[...]).astype(BF16)
    xkeep_scr[...] = x

    @pl.when(i == n_tiles)
    def _():
        for cp in weight_stores:
            cp.wait()


def _sample_kernel(x_ref, cs_ref, sn_ref, cst_ref, sret_ref, g1_ref, win_ref, convw_ref, gng_ref,
                   wout_ref, g2_ref, wg_ref, wu_ref, wd_ref, gf_ref,
                   y_ref, cnew_ref, retnew_ref,
                   proj_scr, mix_scr, x1_scr, h_scr, mask_scr, xi_scr, zeta_scr):
    tile = x_ref.shape[0]
    nseq = sret_ref.shape[0]
    seqlen = tile // nseq
    assert seqlen == SUBLANES, "decode sequences must fill exactly one f32 sublane tile"

    @pl.when(pl.program_id(0) == 0)
    def _():
        i = lax.broadcasted_iota(jnp.int32, (tile, tile), 0)
        j = lax.broadcasted_iota(jnp.int32, (tile, tile), 1)
        keep = (i >= j) & ((i // seqlen) == (j // seqlen))
        diff = jnp.where(keep, i - j, 0).astype(F32)
        ii = lax.broadcasted_iota(jnp.int32, (tile, RET_HD), 0)
        pos = (ii % seqlen).astype(F32)
        for hh in range(RET_HEADS):
            mask_scr[hh] = jnp.where(keep, jnp.exp(diff * LOG_G[hh]), 0.0)
            xi_scr[hh] = jnp.exp((pos + 1.0) * LOG_G[hh])
            zeta_scr[hh] = jnp.exp((seqlen - 1.0 - pos) * LOG_G[hh])

    x = x_ref[...]
    _in_proj(x, g1_ref, win_ref, h_scr, proj_scr)

    w = RET_WIDTH
    cs = cs_ref[...]
    sn = sn_ref[...]
    pair_rows = 2 * seqlen
    rowblk = lax.broadcasted_iota(jnp.int32, (tile, 2 * RET_HD), 0) // seqlen
    colhalf = lax.broadcasted_iota(jnp.int32, (tile, 2 * RET_HD), 1) // RET_HD
    for hh in range(RET_HEADS):
        c0 = hh * RET_HD
        q = _rotary(proj_scr[:, c0:c0 + RET_HD], cs, sn)
        k = _rotary(proj_scr[:, w + c0:w + c0 + RET_HD], cs, sn) * (RET_HD ** -0.5)
        v = proj_scr[:, 2 * w + c0:2 * w + c0 + RET_HD]
        g = proj_scr[:, 3 * w + c0:3 * w + c0 + RET_HD]
        qb, kb, vb = q.astype(BF16), k.astype(BF16), v.astype(BF16)
        s = _dot_nt(qb, kb) * mask_scr[hh]
        o_intra = _dot(s.astype(BF16), vb)
        kzt = (k * zeta_scr[hh]).T.astype(BF16)
        vv = jnp.concatenate([v, v], axis=1)
        g_chunk = math.exp(seqlen * LOG_G[hh])
        cross = []
        for p in range(nseq // 2):
            sa, sb = 2 * p, 2 * p + 1
            ra = sret_ref[sa, hh]
            rb = sret_ref[sb, hh]
            rcat = jnp.concatenate([ra, rb], axis=1).astype(BF16)
            pr = _dot(qb[p * pair_rows:(p + 1) * pair_rows, :], rcat)
            cross.append(pr[0:seqlen, 0:RET_HD])
            cross.append(pr[seqlen:pair_rows, RET_HD:2 * RET_HD])
            vpair = jnp.where(rowblk == sa + colhalf, vv, 0.0).astype(BF16)
            upd = _dot(kzt, vpair)
            retnew_ref[sa, hh] = ra * g_chunk + upd[:, 0:RET_HD]
            retnew_ref[sb, hh] = rb * g_chunk + upd[:, RET_HD:2 * RET_HD]
        o = o_intra + jnp.concatenate(cross, axis=0) * xi_scr[hh]
        mix_scr[:, c0:c0 + RET_HD] = _groupnorm_gate(o, g, gng_ref[:, c0:c0 + RET_HD]).astype(BF16)

    cw = w
    u = proj_scr[:, 5 * w:5 * w + cw] * proj_scr[:, 5 * w + cw:5 * w + 2 * cw]
    cst = cst_ref[...]
    r8 = lax.broadcasted_iota(jnp.int32, (tile, cw), 0) % seqlen
    um1 = jnp.where(r8 == 0, pltpu.roll(cst, tile - 1, axis=0), pltpu.roll(u, 1, axis=0))
    um2 = jnp.where(r8 < 2, cst, pltpu.roll(u, 2, axis=0))
    conv_y = convw_ref[0:1, :] * um2 + convw_ref[1:2, :] * um1 + convw_ref[2:3, :] * u
    mix_scr[:, w:w + cw] = (proj_scr[:, 4 * w:4 * w + cw] * conv_y).astype(BF16)
    cnew_ref[...] = u.reshape(nseq, seqlen, cw)[:, seqlen - (CONV_K - 1):, :]

    _out_proj(x, mix_scr, wout_ref, g2_ref, x1_scr, h_scr)
    _ffn(wg_ref, wu_ref, wd_ref, gf_ref, y_ref, x1_scr, h_scr)


def _resident(shape):
    nd = len(shape)
    return pl.BlockSpec(shape, lambda *_: (0,) * nd, pipeline_mode=pl.Buffered(1))


def _weight_specs(ws):
    return [_resident(wt.shape) for wt in ws]


def _prompt_call(x, ws):
    batch, seq, d = x.shape
    tile = PROMPT_TILE
    tiles_per_seq = seq // tile
    n_tiles = batch * tiles_per_seq
    cs, sn = _rope_tables(np.arange(seq))
    in_cols = ws[1].shape[1]
    cw = ws[2].shape[1]
    mixer_tile = lambda i: jnp.minimum(i, n_tiles - 1)
    dense_tile = lambda i: jnp.maximum(i - 1, 0)
    mats = [ws[m] for m in MATRIX_SLOTS]
    w_in, w_out, w_gate, w_up, w_down = mats
    assert w_gate.shape == w_up.shape and w_out.shape[1] == w_down.shape[1]
    weight_specs = [pl.BlockSpec(memory_space=pl.ANY) if m in MATRIX_SLOTS else _resident(wt.shape)
                    for m, wt in enumerate(ws)]
    outs = pl.pallas_call(
        functools.partial(_prompt_kernel, n_tiles=n_tiles, tiles_per_seq=tiles_per_seq),
        grid=(n_tiles + 1,),
        in_specs=[pl.BlockSpec((tile, d), lambda i: (mixer_tile(i), 0)),
                  pl.BlockSpec((tile, RET_HD), lambda i: (mixer_tile(i) % tiles_per_seq, 0)),
                  pl.BlockSpec((tile, RET_HD), lambda i: (mixer_tile(i) % tiles_per_seq, 0))] + weight_specs,
        out_specs=[pl.BlockSpec((tile, d), lambda i: (dense_tile(i), 0)),
                   pl.BlockSpec((1, CONV_K - 1, cw), lambda i: (mixer_tile(i) // tiles_per_seq, 0, 0)),
                   pl.BlockSpec((1, RET_HEADS, RET_HD, RET_HD),
                                lambda i: (mixer_tile(i) // tiles_per_seq, 0, 0, 0))]
                  + [pl.BlockSpec(memory_space=pl.ANY)] * len(mats),
        out_shape=[jax.ShapeDtypeStruct((batch * seq, d), F32),
                   jax.ShapeDtypeStruct((batch, CONV_K - 1, cw), F32),
                   jax.ShapeDtypeStruct((batch, RET_HEADS, RET_HD, RET_HD), F32)]
                  + [jax.ShapeDtypeStruct(m.shape, BF16) for m in mats],
        scratch_shapes=[pltpu.VMEM(m.shape, BF16) for m in mats] + [
                        pltpu.VMEM((2, STAGE_ROWS_WIDE, w_in.shape[1]), F32),
                        pltpu.VMEM((2, STAGE_ROWS_NARROW, w_out.shape[1]), F32),
                        pltpu.VMEM((2, STAGE_ROWS_WIDE, w_gate.shape[1]), F32),
                        pltpu.SemaphoreType.DMA((2,)),
                        pltpu.SemaphoreType.DMA((len(mats),)),
                        pltpu.VMEM((tile, in_cols), F32),
                        pltpu.VMEM((tile + SUBLANES, cw), F32),
                        pltpu.VMEM((tile, cw), F32),
                        pltpu.VMEM((tile, d), BF16),
                        pltpu.VMEM((tile, d), F32),
                        pltpu.VMEM((tile, d), F32),
                        pltpu.VMEM((tile, d), BF16),
                        pltpu.VMEM((tile, d), BF16),
                        pltpu.VMEM((tile, ws[6].shape[1]), BF16),
                        pltpu.VMEM((RET_HEADS, RET_HD, RET_HD), F32),
                        pltpu.VMEM((RET_HEADS, RET_CHUNK, RET_CHUNK), F32),
                        pltpu.VMEM((RET_HEADS, RET_CHUNK, RET_HD), F32),
                        pltpu.VMEM((RET_HEADS, RET_CHUNK, RET_HD), F32)],
        compiler_params=pltpu.CompilerParams(dimension_semantics=("arbitrary",),
                                             vmem_limit_bytes=VMEM_LIMIT_BYTES),
        name="prompt_layer",
    )(x.reshape(batch * seq, d), cs, sn, *ws)
    y, cnew, ret = outs[:3]
    ws_bf16 = list(ws)
    for m, wt in zip(MATRIX_SLOTS, outs[3:]):
        ws_bf16[m] = wt
    return y.reshape(batch, seq, d), cnew, ret, tuple(ws_bf16)


def _sample_call(x, state_conv, state_ret, ws):
    nseq_all, seqlen, d = x.shape
    nseq = SAMPLE_SEQS
    tile = nseq * seqlen
    cs8, sn8 = _rope_tables(PAST_LEN + np.arange(seqlen))
    cs, sn = jnp.tile(cs8, (nseq, 1)), jnp.tile(sn8, (nseq, 1))
    in_cols = ws[1].shape[1]
    cw = ws[2].shape[1]
    cst = jnp.pad(state_conv, ((0, 0), (0, seqlen - (CONV_K - 1)), (0, 0))).reshape(nseq_all * seqlen, cw)
    y, cnew, rnew = pl.pallas_call(
        _sample_kernel,
        grid=(nseq_all // nseq,),
        in_specs=[pl.BlockSpec((tile, d), lambda i: (i, 0)),
                  _resident((tile, RET_HD)), _resident((tile, RET_HD)),
                  pl.BlockSpec((tile, cw), lambda i: (i, 0)),
                  pl.BlockSpec((nseq, RET_HEADS, RET_HD, RET_HD), lambda i: (i, 0, 0, 0))] + _weight_specs(ws),
        out_specs=[pl.BlockSpec((tile, d), lambda i: (i, 0)),
                   pl.BlockSpec((nseq, CONV_K - 1, cw), lambda i: (i, 0, 0)),
                   pl.BlockSpec((nseq, RET_HEADS, RET_HD, RET_HD), lambda i: (i, 0, 0, 0))],
        out_shape=[jax.ShapeDtypeStruct((nseq_all * seqlen, d), F32),
                   jax.ShapeDtypeStruct((nseq_all, CONV_K - 1, cw), F32),
                   jax.ShapeDtypeStruct((nseq_all, RET_HEADS, RET_HD, RET_HD), F32)],
        scratch_shapes=[pltpu.VMEM((tile, in_cols), F32),
                        pltpu.VMEM((tile, d), BF16),
                        pltpu.VMEM((tile, d), F32),
                        pltpu.VMEM((tile, d), BF16),
                        pltpu.VMEM((RET_HEADS, tile, tile), F32),
                        pltpu.VMEM((RET_HEADS, tile, RET_HD), F32),
                        pltpu.VMEM((RET_HEADS, tile, RET_HD), F32)],
        compiler_params=pltpu.CompilerParams(dimension_semantics=("arbitrary",),
                                             vmem_limit_bytes=VMEM_LIMIT_BYTES),
        name="sample_layer",
    )(x.reshape(nseq_all * seqlen, d), cs, sn, cst, state_ret, *ws)
    return y.reshape(nseq_all, seqlen, d), cnew, rnew


def kernel(x_prompt, x_sample, state_conv, state_ret, norm1_g, w_in, conv_w, ret_gn_g, w_out, norm2_g,
           w_gate, w_up, w_down, norm_f_g):
    depth = w_in.shape[0]
    assert depth == 1, "the fused layer kernels take a single layer"
    row = lambda g: g.reshape(1, -1)
    ws = (row(norm1_g[0]), w_in[0], conv_w[0], row(ret_gn_g[0]), w_out[0],
          row(norm2_g[0]), w_gate[0], w_up[0], w_down[0], row(norm_f_g))
    y_p, cnew_p, ret_p, ws_bf16 = _prompt_call(x_prompt, ws)
    y_s, cnew_s, ret_s = _sample_call(x_sample, state_conv[0], state_ret[0], ws_bf16)
    return (y_p, y_s, cnew_p[None], ret_p[None], cnew_s[None], ret_s[None])
```

```python
import functools
import math

import numpy as np
import jax
import jax.numpy as jnp
from jax import lax
from jax.experimental import pallas as pl
from jax.experimental.pallas import tpu as pltpu

F32 = jnp.float32
BF16 = jnp.bfloat16

RET_HEADS = 4
RET_HD = 128
RET_WIDTH = RET_HEADS * RET_HD
CONV_K = 3
RET_CHUNK = 128
ROPE_BASE = 10000.0
NORM_EPS = 1e-6
GN_EPS = 1e-5
PAST_LEN = 16384

LOG_G = tuple(math.log1p(-(2.0 ** (-5.0 - h))) for h in range(RET_HEADS))

SUBLANES = 8
MXU_N = 256
MATRIX_SLOTS = (1, 4, 6, 7, 8)
STAGE_ROWS = 128
STAGE_SLOTS = 6
PROMPT_TILE = 256
SAMPLE_SEQS = 16
VMEM_LIMIT_BYTES = 56 * 1024 * 1024


def _ff_chunks(d_ff, width=1536):
    edges = list(range(0, d_ff, width)) + [d_ff]
    return tuple(zip(edges[:-1], edges[1:]))


def _rope_tables(pos):
    half = RET_HD // 2
    inv = ROPE_BASE ** (-np.arange(half, dtype=np.float64) / half)
    ang = np.asarray(pos, np.float64)[:, None] * inv[None, :]
    cos, sin = np.cos(ang), np.sin(ang)
    cs = np.concatenate([cos, cos], axis=1).astype(np.float32)
    sn = np.concatenate([-sin, sin], axis=1).astype(np.float32)
    return jnp.asarray(cs), jnp.asarray(sn)


def _rmsnorm(x, g):
    ms = jnp.mean(x * x, axis=-1, keepdims=True)
    return x * lax.rsqrt(ms + NORM_EPS) * g


def _silu(x):
    return x * (1.0 / (1.0 + jnp.exp(-x)))


def _rotary(t, cs, sn):
    return t * cs + pltpu.roll(t, RET_HD // 2, axis=1) * sn


def _dot(a, b):
    return jnp.dot(a, b, preferred_element_type=F32)


def _dot_nt(a, b):
    return lax.dot_general(a, b, (((1,), (1,)), ((), ())), preferred_element_type=F32)


def _dot_tn(a, b):
    return lax.dot_general(a, b, (((0,), (0,)), ((), ())), preferred_element_type=F32)


def _groupnorm_gate(o, g, gn_g):
    mu = jnp.mean(o, axis=-1, keepdims=True)
    d = o - mu
    var = jnp.mean(d * d, axis=-1, keepdims=True)
    return _silu(g) * (d * lax.rsqrt(var + GN_EPS) * gn_g)


def _in_proj(x, g1_ref, win_ref, h_scr, proj_scr):
    h_scr[...] = _rmsnorm(x, g1_ref[...]).astype(BF16)
    proj_scr[...] = _dot(h_scr[...], win_ref[...])


def _out_proj(x, mix_scr, wout_ref, g2_ref, x1_scr, h_scr):
    x1_scr[...] = x + _dot(mix_scr[...], wout_ref[...])
    h_scr[...] = _rmsnorm(x1_scr[...], g2_ref[...]).astype(BF16)


def _ffn(wg_ref, wu_ref, wd_ref, gf_ref, y_ref, x1_scr, h_scr):
    for c0, c1 in _ff_chunks(wg_ref.shape[1]):
        gate = _dot(h_scr[...], wg_ref[:, c0:c1])
        up = _dot(h_scr[...], wu_ref[:, c0:c1])
        act = (_silu(gate) * up).astype(BF16)
        x1_scr[...] += _dot(act, wd_ref[c0:c1, :])
    y_ref[...] = _rmsnorm(x1_scr[...], gf_ref[...])


def _load_weights_bf16(pairs, stage, sem):
    slots, rb, width = stage.shape
    chunks = []
    for w_hbm, w_bf in pairs:
        rows, cols = w_hbm.shape
        assert rows % rb == 0 and cols <= width
        chunks += [(w_hbm, w_bf, r0, cols) for r0 in range(0, rows, rb)]

    def chunk_copy(n):
        w_hbm, _, r0, cols = chunks[n]
        slot = n % slots
        return pltpu.make_async_copy(w_hbm.at[r0:r0 + rb, :], stage.at[slot, :, 0:cols], sem.at[slot])

    for n in range(min(slots - 1, len(chunks))):
        chunk_copy(n).start()
    for n, (_, w_bf, r0, cols) in enumerate(chunks):
        chunk_copy(n).wait()
        w_bf[r0:r0 + rb, :] = stage[n % slots, :, 0:cols].astype(BF16)
        if n + slots - 1 < len(chunks):
            chunk_copy(n + slots - 1).start()


def _prompt_kernel(x_ref, cs_ref, sn_ref, g1_ref, win_hbm, convw_ref, gng_ref, wout_hbm, g2_ref,
                   wg_hbm, wu_hbm, wd_hbm, gf_ref,
                   y_ref, cnew_ref, ret_ref, win_out, wout_out, wg_out, wu_out, wd_out,
                   win_ref, wout_ref, wg_ref, wu_ref, wd_ref, stage_scr, load_sem, store_sem,
                   proj_scr, uext_scr, convy_scr, mix_scr, xkeep_scr, x1_scr, h_scr, h2_scr, act_scr, r_scr,
                   dec_scr, xi_scr, zeta_scr, *, n_tiles, tiles_per_seq):
    tile = x_ref.shape[0]
    i = pl.program_id(0)
    chunk = RET_CHUNK
    valid = i < n_tiles
    t = jnp.minimum(i, n_tiles - 1) % tiles_per_seq
    seq_start = t == 0
    seq_end = t == tiles_per_seq - 1
    weight_stores = [pltpu.make_async_copy(src, dst, store_sem.at[n]) for n, (src, dst) in enumerate(
        ((win_ref, win_out), (wout_ref, wout_out), (wg_ref, wg_out), (wu_ref, wu_out), (wd_ref, wd_out)))]

    @pl.when(i == 0)
    def _():
        _load_weights_bf16(((win_hbm, win_ref), (wout_hbm, wout_ref), (wg_hbm, wg_ref), (wu_hbm, wu_ref),
                            (wd_hbm, wd_ref)), stage_scr, load_sem)
        for cp in weight_stores:
            cp.start()
        ii = lax.broadcasted_iota(jnp.int32, (chunk, chunk), 0)
        jj = lax.broadcasted_iota(jnp.int32, (chunk, chunk), 1)
        causal = ii >= jj
        diff = jnp.where(causal, ii - jj, 0).astype(F32)
        i_f = ii.astype(F32)
        for hh in range(RET_HEADS):
            dec_scr[hh] = jnp.where(causal, jnp.exp(diff * LOG_G[hh]), 0.0)
            xi_scr[hh] = jnp.exp((i_f + 1.0) * LOG_G[hh])
            zeta_scr[hh] = jnp.exp((chunk - 1.0 - i_f) * LOG_G[hh])
        r_scr[...] = jnp.zeros_like(r_scr)
        uext_scr[0:SUBLANES, :] = jnp.zeros((SUBLANES, uext_scr.shape[1]), F32)
        mix_scr[...] = jnp.zeros_like(mix_scr)
        xkeep_scr[...] = jnp.zeros_like(xkeep_scr)

    w = RET_WIDTH
    cw = w
    d_ff = wg_ref.shape[1]

    x1_scr[...] = xkeep_scr[...] + _dot(mix_scr[...], wout_ref[...])
    x = x_ref[...]
    h_scr[...] = _rmsnorm(x, g1_ref[...]).astype(BF16)
    proj_scr[:, 0:2 * w] = _dot(h_scr[...], win_ref[:, 0:2 * w])
    h2_scr[...] = _rmsnorm(x1_scr[...], g2_ref[...]).astype(BF16)
    proj_scr[:, 2 * w:4 * w] = _dot(h_scr[...], win_ref[:, 2 * w:4 * w])

    def ffn_piece(c0):
        def emit():
            gate = _dot(h2_scr[...], wg_ref[:, c0:c0 + MXU_N])
            up = _dot(h2_scr[...], wu_ref[:, c0:c0 + MXU_N])
            act_scr[:, c0:c0 + MXU_N] = (_silu(gate) * up).astype(BF16)
        return emit

    def proj_piece(c0):
        def emit():
            proj_scr[:, c0:c0 + cw] = _dot(h_scr[...], win_ref[:, c0:c0 + cw])
        return emit

    fillers = ([proj_piece(5 * w), proj_piece(6 * w)]
               + [ffn_piece(c0) for c0 in range(0, d_ff, MXU_N)])

    def fill(n):
        for _ in range(min(n, len(fillers))):
            fillers.pop(0)()

    n_chunks = tile // chunk
    units = [(c, hh) for c in range(n_chunks) for hh in range(RET_HEADS)]
    qbs, vbs, scores, updates = {}, {}, {}, {}
    for c, hh in units:
        r0, c0 = c * chunk, hh * RET_HD
        cs = cs_ref[r0:r0 + chunk, :]
        sn = sn_ref[r0:r0 + chunk, :]
        q = _rotary(proj_scr[r0:r0 + chunk, c0:c0 + RET_HD], cs, sn)
        k = _rotary(proj_scr[r0:r0 + chunk, w + c0:w + c0 + RET_HD], cs, sn) * (RET_HD ** -0.5)
        qb, kb = q.astype(BF16), k.astype(BF16)
        vb = proj_scr[r0:r0 + chunk, 2 * w + c0:2 * w + c0 + RET_HD].astype(BF16)
        kz = (k * zeta_scr[hh]).astype(BF16)
        qbs[c, hh], vbs[c, hh] = qb, vb
        scores[c, hh] = _dot_nt(qb, kb)
        updates[c, hh] = _dot_tn(kz, vb)
    fill(2)

    u = proj_scr[:, 5 * w:5 * w + cw] * proj_scr[:, 5 * w + cw:5 * w + 2 * cw]
    uext_scr[SUBLANES:SUBLANES + tile, :] = u
    um1 = uext_scr[SUBLANES - 1:SUBLANES - 1 + tile, :]
    um2 = uext_scr[SUBLANES - 2:SUBLANES - 2 + tile, :]
    convy_scr[...] = convw_ref[0:1, :] * um2 + convw_ref[1:2, :] * um1 + convw_ref[2:3, :] * u
    tail = uext_scr[tile + SUBLANES - (CONV_K - 1):tile + SUBLANES, :]
    cnew_ref[0] = tail
    uext_scr[SUBLANES - (CONV_K - 1):SUBLANES, :] = jnp.where(seq_end, 0.0, tail)
    fill(len(fillers) - len(units) + 1)

    outs = {}
    for hh in range(RET_HEADS):
        prev = r_scr[hh]
        state = jnp.where(seq_start, 0.0, prev)
        for c in range(n_chunks):
            s = (scores[c, hh] * dec_scr[hh]).astype(BF16)
            outs[c, hh] = _dot(s, vbs[c, hh]) + _dot(qbs[c, hh], state.astype(BF16)) * xi_scr[hh]
            state = state * math.exp(chunk * LOG_G[hh]) + updates[c, hh]
        state = jnp.where(valid, state, prev)
        r_scr[hh] = state
        ret_ref[0, hh] = state
    for c, hh in units:
        fill(1)
        r0, c0 = c * chunk, hh * RET_HD
        g = proj_scr[r0:r0 + chunk, 3 * w + c0:3 * w + c0 + RET_HD]
        mix_scr[r0:r0 + chunk, c0:c0 + RET_HD] = _groupnorm_gate(
            outs[c, hh], g, gng_ref[:, c0:c0 + RET_HD]).astype(BF16)
    assert not fillers

    down = _dot(act_scr[...], wd_ref[...])
    proj_piece(4 * w)()
    y_ref[...] = _rmsnorm(x1_scr[...] + down, gf_ref[...])
    mix_scr[:, w:w + cw] = (proj_scr[:, 4 * w:4 * w + cw] * convy_scr[...]).astype(BF16)
    xkeep_scr[...] = x

    @pl.when(i == n_tiles)
    def _():
        for cp in weight_stores:
            cp.wait()


def _sample_kernel(x_ref, cs_ref, sn_ref, cst_ref, sret_ref, g1_ref, win_ref, convw_ref, gng_ref,
                   wout_ref, g2_ref, wg_ref, wu_ref, wd_ref, gf_ref,
                   y_ref, cnew_ref, retnew_ref,
                   proj_scr, mix_scr, x1_scr, h_scr, mask_scr, xi_scr, zeta_scr):
    tile = x_ref.shape[0]
    nseq = sret_ref.shape[0]
    seqlen = tile // nseq
    assert seqlen == SUBLANES, "decode sequences must fill exactly one f32 sublane tile"

    @pl.when(pl.program_id(0) == 0)
    def _():
        i = lax.broadcasted_iota(jnp.int32, (tile, tile), 0)
        j = lax.broadcasted_iota(jnp.int32, (tile, tile), 1)
        keep = (i >= j) & ((i // seqlen) == (j // seqlen))
        diff = jnp.where(keep, i - j, 0).astype(F32)
        ii = lax.broadcasted_iota(jnp.int32, (tile, RET_HD), 0)
        pos = (ii % seqlen).astype(F32)
        for hh in range(RET_HEADS):
            mask_scr[hh] = jnp.where(keep, jnp.exp(diff * LOG_G[hh]), 0.0)
            xi_scr[hh] = jnp.exp((pos + 1.0) * LOG_G[hh])
            zeta_scr[hh] = jnp.exp((seqlen - 1.0 - pos) * LOG_G[hh])

    x = x_ref[...]
    _in_proj(x, g1_ref, win_ref, h_scr, proj_scr)

    w = RET_WIDTH
    cs = cs_ref[...]
    sn = sn_ref[...]
    pair_rows = 2 * seqlen
    rowblk = lax.broadcasted_iota(jnp.int32, (tile, 2 * RET_HD), 0) // seqlen
    colhalf = lax.broadcasted_iota(jnp.int32, (tile, 2 * RET_HD), 1) // RET_HD
    for hh in range(RET_HEADS):
        c0 = hh * RET_HD
        q = _rotary(proj_scr[:, c0:c0 + RET_HD], cs, sn)
        k = _rotary(proj_scr[:, w + c0:w + c0 + RET_HD], cs, sn) * (RET_HD ** -0.5)
        v = proj_scr[:, 2 * w + c0:2 * w + c0 + RET_HD]
        g = proj_scr[:, 3 * w + c0:3 * w + c0 + RET_HD]
        qb, kb, vb = q.astype(BF16), k.astype(BF16), v.astype(BF16)
        s = _dot_nt(qb, kb) * mask_scr[hh]
        o_intra = _dot(s.astype(BF16), vb)
        kzt = (k * zeta_scr[hh]).T.astype(BF16)
        vv = jnp.concatenate([v, v], axis=1)
        g_chunk = math.exp(seqlen * LOG_G[hh])
        cross = []
        for p in range(nseq // 2):
            sa, sb = 2 * p, 2 * p + 1
            ra = sret_ref[sa, hh]
            rb = sret_ref[sb, hh]
            rcat = jnp.concatenate([ra, rb], axis=1).astype(BF16)
            pr = _dot(qb[p * pair_rows:(p + 1) * pair_rows, :], rcat)
            cross.append(pr[0:seqlen, 0:RET_HD])
            cross.append(pr[seqlen:pair_rows, RET_HD:2 * RET_HD])
            vpair = jnp.where(rowblk == sa + colhalf, vv, 0.0).astype(BF16)
            upd = _dot(kzt, vpair)
            retnew_ref[sa, hh] = ra * g_chunk + upd[:, 0:RET_HD]
            retnew_ref[sb, hh] = rb * g_chunk + upd[:, RET_HD:2 * RET_HD]
        o = o_intra + jnp.concatenate(cross, axis=0) * xi_scr[hh]
        mix_scr[:, c0:c0 + RET_HD] = _groupnorm_gate(o, g, gng_ref[:, c0:c0 + RET_HD]).astype(BF16)

    cw = w
    u = proj_scr[:, 5 * w:5 * w + cw] * proj_scr[:, 5 * w + cw:5 * w + 2 * cw]
    cst = cst_ref[...]
    r8 = lax.broadcasted_iota(jnp.int32, (tile, cw), 0) % seqlen
    um1 = jnp.where(r8 == 0, pltpu.roll(cst, tile - 1, axis=0), pltpu.roll(u, 1, axis=0))
    um2 = jnp.where(r8 < 2, cst, pltpu.roll(u, 2, axis=0))
    conv_y = convw_ref[0:1, :] * um2 + convw_ref[1:2, :] * um1 + convw_ref[2:3, :] * u
    mix_scr[:, w:w + cw] = (proj_scr[:, 4 * w:4 * w + cw] * conv_y).astype(BF16)
    cnew_ref[...] = u.reshape(nseq, seqlen, cw)[:, seqlen - (CONV_K - 1):, :]

    _out_proj(x, mix_scr, wout_ref, g2_ref, x1_scr, h_scr)
    _ffn(wg_ref, wu_ref, wd_ref, gf_ref, y_ref, x1_scr, h_scr)


def _resident(shape):
    nd = len(shape)
    return pl.BlockSpec(shape, lambda *_: (0,) * nd, pipeline_mode=pl.Buffered(1))


def _weight_specs(ws):
    return [_resident(wt.shape) for wt in ws]


def _prompt_call(x, ws):
    batch, seq, d = x.shape
    tile = PROMPT_TILE
    tiles_per_seq = seq // tile
    n_tiles = batch * tiles_per_seq
    cs, sn = _rope_tables(np.arange(seq))
    in_cols = ws[1].shape[1]
    cw = ws[2].shape[1]
    mixer_tile = lambda i: jnp.minimum(i, n_tiles - 1)
    dense_tile = lambda i: jnp.maximum(i - 1, 0)
    mats = [ws[m] for m in MATRIX_SLOTS]
    w_in, w_out, w_gate, w_up, w_down = mats
    assert w_gate.shape == w_up.shape and w_out.shape[1] == w_down.shape[1]
    weight_specs = [pl.BlockSpec(memory_space=pl.ANY) if m in MATRIX_SLOTS else _resident(wt.shape)
                    for m, wt in enumerate(ws)]
    outs = pl.pallas_call(
        functools.partial(_prompt_kernel, n_tiles=n_tiles, tiles_per_seq=tiles_per_seq),
        grid=(n_tiles + 1,),
        in_specs=[pl.BlockSpec((tile, d), lambda i: (mixer_tile(i), 0)),
                  pl.BlockSpec((tile, RET_HD), lambda i: (mixer_tile(i) % tiles_per_seq, 0)),
                  pl.BlockSpec((tile, RET_HD), lambda i: (mixer_tile(i) % tiles_per_seq, 0))] + weight_specs,
        out_specs=[pl.BlockSpec((tile, d), lambda i: (dense_tile(i), 0)),
                   pl.BlockSpec((1, CONV_K - 1, cw), lambda i: (mixer_tile(i) // tiles_per_seq, 0, 0)),
                   pl.BlockSpec((1, RET_HEADS, RET_HD, RET_HD),
                                lambda i: (mixer_tile(i) // tiles_per_seq, 0, 0, 0))]
                  + [pl.BlockSpec(memory_space=pl.ANY)] * len(mats),
        out_shape=[jax.ShapeDtypeStruct((batch * seq, d), F32),
                   jax.ShapeDtypeStruct((batch, CONV_K - 1, cw), F32),
                   jax.ShapeDtypeStruct((batch, RET_HEADS, RET_HD, RET_HD), F32)]
                  + [jax.ShapeDtypeStruct(m.shape, BF16) for m in mats],
        scratch_shapes=[pltpu.VMEM(m.shape, BF16) for m in mats] + [
                        pltpu.VMEM((STAGE_SLOTS, STAGE_ROWS, max(m.shape[1] for m in mats)), F32),
                        pltpu.SemaphoreType.DMA((STAGE_SLOTS,)),
                        pltpu.SemaphoreType.DMA((len(mats),)),
                        pltpu.VMEM((tile, in_cols), F32),
                        pltpu.VMEM((tile + SUBLANES, cw), F32),
                        pltpu.VMEM((tile, cw), F32),
                        pltpu.VMEM((tile, d), BF16),
                        pltpu.VMEM((tile, d), F32),
                        pltpu.VMEM((tile, d), F32),
                        pltpu.VMEM((tile, d), BF16),
                        pltpu.VMEM((tile, d), BF16),
                        pltpu.VMEM((tile, ws[6].shape[1]), BF16),
                        pltpu.VMEM((RET_HEADS, RET_HD, RET_HD), F32),
                        pltpu.VMEM((RET_HEADS, RET_CHUNK, RET_CHUNK), F32),
                        pltpu.VMEM((RET_HEADS, RET_CHUNK, RET_HD), F32),
                        pltpu.VMEM((RET_HEADS, RET_CHUNK, RET_HD), F32)],
        compiler_params=pltpu.CompilerParams(dimension_semantics=("arbitrary",),
                                             vmem_limit_bytes=VMEM_LIMIT_BYTES),
        name="prompt_layer",
    )(x.reshape(batch * seq, d), cs, sn, *ws)
    y, cnew, ret = outs[:3]
    ws_bf16 = list(ws)
    for m, wt in zip(MATRIX_SLOTS, outs[3:]):
        ws_bf16[m] = wt
    return y.reshape(batch, seq, d), cnew, ret, tuple(ws_bf16)


def _sample_call(x, state_conv, state_ret, ws):
    nseq_all, seqlen, d = x.shape
    nseq = SAMPLE_SEQS
    tile = nseq * seqlen
    cs8, sn8 = _rope_tables(PAST_LEN + np.arange(seqlen))
    cs, sn = jnp.tile(cs8, (nseq, 1)), jnp.tile(sn8, (nseq, 1))
    in_cols = ws[1].shape[1]
    cw = ws[2].shape[1]
    cst = jnp.pad(state_conv, ((0, 0), (0, seqlen - (CONV_K - 1)), (0, 0))).reshape(nseq_all * seqlen, cw)
    y, cnew, rnew = pl.pallas_call(
        _sample_kernel,
        grid=(nseq_all // nseq,),
        in_specs=[pl.BlockSpec((tile, d), lambda i: (i, 0)),
                  _resident((tile, RET_HD)), _resident((tile, RET_HD)),
                  pl.BlockSpec((tile, cw), lambda i: (i, 0)),
                  pl.BlockSpec((nseq, RET_HEADS, RET_HD, RET_HD), lambda i: (i, 0, 0, 0))] + _weight_specs(ws),
        out_specs=[pl.BlockSpec((tile, d), lambda i: (i, 0)),
                   pl.BlockSpec((nseq, CONV_K - 1, cw), lambda i: (i, 0, 0)),
                   pl.BlockSpec((nseq, RET_HEADS, RET_HD, RET_HD), lambda i: (i, 0, 0, 0))],
        out_shape=[jax.ShapeDtypeStruct((nseq_all * seqlen, d), F32),
                   jax.ShapeDtypeStruct((nseq_all, CONV_K - 1, cw), F32),
                   jax.ShapeDtypeStruct((nseq_all, RET_HEADS, RET_HD, RET_HD), F32)],
        scratch_shapes=[pltpu.VMEM((tile, in_cols), F32),
                        pltpu.VMEM((tile, d), BF16),
                        pltpu.VMEM((tile, d), F32),
                        pltpu.VMEM((tile, d), BF16),
                        pltpu.VMEM((RET_HEADS, tile, tile), F32),
                        pltpu.VMEM((RET_HEADS, tile, RET_HD), F32),
                        pltpu.VMEM((RET_HEADS, tile, RET_HD), F32)],
        compiler_params=pltpu.CompilerParams(dimension_semantics=("arbitrary",),
                                             vmem_limit_bytes=VMEM_LIMIT_BYTES),
        name="sample_layer",
    )(x.reshape(nseq_all * seqlen, d), cs, sn, cst, state_ret, *ws)
    return y.reshape(nseq_all, seqlen, d), cnew, rnew


def kernel(x_prompt, x_sample, state_conv, state_ret, norm1_g, w_in, conv_w, ret_gn_g, w_out, norm2_g,
           w_gate, w_up, w_down, norm_f_g):
    depth = w_in.shape[0]
    assert depth == 1, "the fused layer kernels take a single layer"
    row = lambda g: g.reshape(1, -1)
    ws = (row(norm1_g[0]), w_in[0], conv_w[0], row(ret_gn_g[0]), w_out[0],
          row(norm2_g[0]), w_gate[0], w_up[0], w_down[0], row(norm_f_g))
    y_p, cnew_p, ret_p, ws_bf16 = _prompt_call(x_prompt, ws)
    y_s, cnew_s, ret_s = _sample_call(x_sample, state_conv[0], state_ret[0], ws_bf16)
    return (y_p, y_s, cnew_p[None], ret_p[None], cnew_s[None], ret_s[None])
```

```python
import functools
import math

import numpy as np
import jax
import jax.numpy as jnp
from jax import lax
from jax.experimental import pallas as pl
from jax.experimental.pallas import tpu as pltpu

F32 = jnp.float32
BF16 = jnp.bfloat16

RET_HEADS = 4
RET_HD = 128
RET_WIDTH = RET_HEADS * RET_HD
CONV_K = 3
RET_CHUNK = 128
ROPE_BASE = 10000.0
NORM_EPS = 1e-6
GN_EPS = 1e-5
PAST_LEN = 16384

LOG_G = tuple(math.log1p(-(2.0 ** (-5.0 - h))) for h in range(RET_HEADS))

SUBLANES = 8
MXU_N = 256
MATRIX_SLOTS = (1, 4, 6, 7, 8)
STAGE_ROWS = 128
PROMPT_TILE = 512
SAMPLE_SEQS = 16
VMEM_LIMIT_BYTES = 56 * 1024 * 1024


def _ff_chunks(d_ff, width=1536):
    edges = list(range(0, d_ff, width)) + [d_ff]
    return tuple(zip(edges[:-1], edges[1:]))


def _rope_tables(pos):
    half = RET_HD // 2
    inv = ROPE_BASE ** (-np.arange(half, dtype=np.float64) / half)
    ang = np.asarray(pos, np.float64)[:, None] * inv[None, :]
    cos, sin = np.cos(ang), np.sin(ang)
    cs = np.concatenate([cos, cos], axis=1).astype(np.float32)
    sn = np.concatenate([-sin, sin], axis=1).astype(np.float32)
    return jnp.asarray(cs), jnp.asarray(sn)


def _rmsnorm(x, g):
    ms = jnp.mean(x * x, axis=-1, keepdims=True)
    return x * lax.rsqrt(ms + NORM_EPS) * g


def _silu(x):
    return x * (1.0 / (1.0 + jnp.exp(-x)))


def _rotary(t, cs, sn):
    return t * cs + pltpu.roll(t, RET_HD // 2, axis=1) * sn


def _dot(a, b):
    return jnp.dot(a, b, preferred_element_type=F32)


def _dot_nt(a, b):
    return lax.dot_general(a, b, (((1,), (1,)), ((), ())), preferred_element_type=F32)


def _dot_tn(a, b):
    return lax.dot_general(a, b, (((0,), (0,)), ((), ())), preferred_element_type=F32)


def _groupnorm_gate(o, g, gn_g):
    mu = jnp.mean(o, axis=-1, keepdims=True)
    d = o - mu
    var = jnp.mean(d * d, axis=-1, keepdims=True)
    return _silu(g) * (d * lax.rsqrt(var + GN_EPS) * gn_g)


def _in_proj(x, g1_ref, win_ref, h_scr, proj_scr):
    h_scr[...] = _rmsnorm(x, g1_ref[...]).astype(BF16)
    proj_scr[...] = _dot(h_scr[...], win_ref[...])


def _out_proj(x, mix_scr, wout_ref, g2_ref, x1_scr, h_scr):
    x1_scr[...] = x + _dot(mix_scr[...], wout_ref[...])
    h_scr[...] = _rmsnorm(x1_scr[...], g2_ref[...]).astype(BF16)


def _ffn(wg_ref, wu_ref, wd_ref, gf_ref, y_ref, x1_scr, h_scr):
    for c0, c1 in _ff_chunks(wg_ref.shape[1]):
        gate = _dot(h_scr[...], wg_ref[:, c0:c1])
        up = _dot(h_scr[...], wu_ref[:, c0:c1])
        act = (_silu(gate) * up).astype(BF16)
        x1_scr[...] += _dot(act, wd_ref[c0:c1, :])
    y_ref[...] = _rmsnorm(x1_scr[...], gf_ref[...])


def _load_weights_bf16(pairs, stage, sem):
    rb = STAGE_ROWS
    slots, width = stage.shape[0] // rb, stage.shape[1]
    assert slots >= 2 and slots <= sem.shape[0]
    chunks = []
    for w_hbm, w_bf in pairs:
        rows, cols = w_hbm.shape
        assert rows % rb == 0 and cols <= width
        chunks += [(w_hbm, w_bf, r0, cols) for r0 in range(0, rows, rb)]

    def slot_view(n):
        s0 = (n % slots) * rb
        return stage.at[s0:s0 + rb, 0:chunks[n][3]]

    def chunk_copy(n):
        w_hbm, _, r0, _ = chunks[n]
        return pltpu.make_async_copy(w_hbm.at[r0:r0 + rb, :], slot_view(n), sem.at[n % slots])

    for n in range(min(slots - 1, len(chunks))):
        chunk_copy(n).start()
    for n, (_, w_bf, r0, cols) in enumerate(chunks):
        chunk_copy(n).wait()
        w_bf[r0:r0 + rb, :] = slot_view(n)[...].astype(BF16)
        if n + slots - 1 < len(chunks):
            chunk_copy(n + slots - 1).start()


def _prompt_kernel(x_ref, cs_ref, sn_ref, g1_ref, win_hbm, convw_ref, gng_ref, wout_hbm, g2_ref,
                   wg_hbm, wu_hbm, wd_hbm, gf_ref,
                   y_ref, cnew_ref, ret_ref, win_out, wout_out, wg_out, wu_out, wd_out,
                   win_ref, wout_ref, wg_ref, wu_ref, wd_ref, load_sem, store_sem,
                   proj_scr, uext_scr, convy_scr, mix_scr, xkeep_scr, x1_scr, h_scr, h2_scr, act_scr, r_scr,
                   dec_scr, xi_scr, zeta_scr, *, n_tiles, tiles_per_seq):
    tile = x_ref.shape[0]
    i = pl.program_id(0)
    chunk = RET_CHUNK
    valid = i < n_tiles
    t = jnp.minimum(i, n_tiles - 1) % tiles_per_seq
    seq_start = t == 0
    seq_end = t == tiles_per_seq - 1
    weight_stores = [pltpu.make_async_copy(src, dst, store_sem.at[n]) for n, (src, dst) in enumerate(
        ((win_ref, win_out), (wout_ref, wout_out), (wg_ref, wg_out), (wu_ref, wu_out), (wd_ref, wd_out)))]

    @pl.when(i == 0)
    def _():
        _load_weights_bf16(((win_hbm, win_ref), (wout_hbm, wout_ref), (wg_hbm, wg_ref), (wu_hbm, wu_ref),
                            (wd_hbm, wd_ref)), proj_scr, load_sem)
        for cp in weight_stores:
            cp.start()
        ii = lax.broadcasted_iota(jnp.int32, (chunk, chunk), 0)
        jj = lax.broadcasted_iota(jnp.int32, (chunk, chunk), 1)
        causal = ii >= jj
        diff = jnp.where(causal, ii - jj, 0).astype(F32)
        i_f = ii.astype(F32)
        for hh in range(RET_HEADS):
            dec_scr[hh] = jnp.where(causal, jnp.exp(diff * LOG_G[hh]), 0.0)
            xi_scr[hh] = jnp.exp((i_f + 1.0) * LOG_G[hh])
            zeta_scr[hh] = jnp.exp((chunk - 1.0 - i_f) * LOG_G[hh])
        r_scr[...] = jnp.zeros_like(r_scr)
        uext_scr[0:SUBLANES, :] = jnp.zeros((SUBLANES, uext_scr.shape[1]), F32)
        mix_scr[...] = jnp.zeros_like(mix_scr)
        xkeep_scr[...] = jnp.zeros_like(xkeep_scr)

    w = RET_WIDTH
    cw = w
    d_ff = wg_ref.shape[1]

    x1_scr[...] = xkeep_scr[...] + _dot(mix_scr[...], wout_ref[...])
    x = x_ref[...]
    h_scr[...] = _rmsnorm(x, g1_ref[...]).astype(BF16)
    proj_scr[:, 0:2 * w] = _dot(h_scr[...], win_ref[:, 0:2 * w])
    h2_scr[...] = _rmsnorm(x1_scr[...], g2_ref[...]).astype(BF16)
    proj_scr[:, 2 * w:4 * w] = _dot(h_scr[...], win_ref[:, 2 * w:4 * w])

    def ffn_piece(c0):
        def emit():
            gate = _dot(h2_scr[...], wg_ref[:, c0:c0 + MXU_N])
            up = _dot(h2_scr[...], wu_ref[:, c0:c0 + MXU_N])
            act_scr[:, c0:c0 + MXU_N] = (_silu(gate) * up).astype(BF16)
        return emit

    def proj_piece(c0):
        def emit():
            proj_scr[:, c0:c0 + cw] = _dot(h_scr[...], win_ref[:, c0:c0 + cw])
        return emit

    fillers = ([proj_piece(5 * w), proj_piece(6 * w)]
               + [ffn_piece(c0) for c0 in range(0, d_ff, MXU_N)])

    def fill(n):
        for _ in range(min(n, len(fillers))):
            fillers.pop(0)()

    n_chunks = tile // chunk
    units = [(c, hh) for c in range(n_chunks) for hh in range(RET_HEADS)]
    qbs, vbs, scores, updates = {}, {}, {}, {}
    for c, hh in units:
        r0, c0 = c * chunk, hh * RET_HD
        cs = cs_ref[r0:r0 + chunk, :]
        sn = sn_ref[r0:r0 + chunk, :]
        q = _rotary(proj_scr[r0:r0 + chunk, c0:c0 + RET_HD], cs, sn)
        k = _rotary(proj_scr[r0:r0 + chunk, w + c0:w + c0 + RET_HD], cs, sn) * (RET_HD ** -0.5)
        qb, kb = q.astype(BF16), k.astype(BF16)
        vb = proj_scr[r0:r0 + chunk, 2 * w + c0:2 * w + c0 + RET_HD].astype(BF16)
        kz = (k * zeta_scr[hh]).astype(BF16)
        qbs[c, hh], vbs[c, hh] = qb, vb
        scores[c, hh] = _dot_nt(qb, kb)
        updates[c, hh] = _dot_tn(kz, vb)
    fill(2)

    u = proj_scr[:, 5 * w:5 * w + cw] * proj_scr[:, 5 * w + cw:5 * w + 2 * cw]
    uext_scr[SUBLANES:SUBLANES + tile, :] = u
    um1 = uext_scr[SUBLANES - 1:SUBLANES - 1 + tile, :]
    um2 = uext_scr[SUBLANES - 2:SUBLANES - 2 + tile, :]
    convy_scr[...] = convw_ref[0:1, :] * um2 + convw_ref[1:2, :] * um1 + convw_ref[2:3, :] * u
    tail = uext_scr[tile + SUBLANES - (CONV_K - 1):tile + SUBLANES, :]
    cnew_ref[0] = tail
    uext_scr[SUBLANES - (CONV_K - 1):SUBLANES, :] = jnp.where(seq_end, 0.0, tail)
    fill(len(fillers) // 3)

    outs = {}
    for hh in range(RET_HEADS):
        prev = r_scr[hh]
        state = jnp.where(seq_start, 0.0, prev)
        for c in range(n_chunks):
            s = (scores[c, hh] * dec_scr[hh]).astype(BF16)
            outs[c, hh] = _dot(s, vbs[c, hh]) + _dot(qbs[c, hh], state.astype(BF16)) * xi_scr[hh]
            state = state * math.exp(chunk * LOG_G[hh]) + updates[c, hh]
        state = jnp.where(valid, state, prev)
        r_scr[hh] = state
        ret_ref[0, hh] = state
    n_late = len(fillers)
    for n, (c, hh) in enumerate(units):
        fill(((n + 1) * n_late) // len(units) - (n * n_late) // len(units))
        r0, c0 = c * chunk, hh * RET_HD
        g = proj_scr[r0:r0 + chunk, 3 * w + c0:3 * w + c0 + RET_HD]
        mix_scr[r0:r0 + chunk, c0:c0 + RET_HD] = _groupnorm_gate(
            outs[c, hh], g, gng_ref[:, c0:c0 + RET_HD]).astype(BF16)
    assert not fillers

    down = _dot(act_scr[...], wd_ref[...])
    proj_piece(4 * w)()
    y_ref[...] = _rmsnorm(x1_scr[...] + down, gf_ref[...])
    mix_scr[:, w:w + cw] = (proj_scr[:, 4 * w:4 * w + cw] * convy_scr[...]).astype(BF16)
    xkeep_scr[...] = x

    @pl.when(i == n_tiles)
    def _():
        for cp in weight_stores:
            cp.wait()


def _sample_kernel(x_ref, cs_ref, sn_ref, cst_ref, sret_ref, g1_ref, win_ref, convw_ref, gng_ref,
                   wout_ref, g2_ref, wg_ref, wu_ref, wd_ref, gf_ref,
                   y_ref, cnew_ref, retnew_ref,
                   proj_scr, mix_scr, x1_scr, h_scr, mask_scr, xi_scr, zeta_scr):
    tile = x_ref.shape[0]
    nseq = sret_ref.shape[0]
    seqlen = tile // nseq
    assert seqlen == SUBLANES, "decode sequences must fill exactly one f32 sublane tile"

    @pl.when(pl.program_id(0) == 0)
    def _():
        i = lax.broadcasted_iota(jnp.int32, (tile, tile), 0)
        j = lax.broadcasted_iota(jnp.int32, (tile, tile), 1)
        keep = (i >= j) & ((i // seqlen) == (j // seqlen))
        diff = jnp.where(keep, i - j, 0).astype(F32)
        ii = lax.broadcasted_iota(jnp.int32, (tile, RET_HD), 0)
        pos = (ii % seqlen).astype(F32)
        for hh in range(RET_HEADS):
            mask_scr[hh] = jnp.where(keep, jnp.exp(diff * LOG_G[hh]), 0.0)
            xi_scr[hh] = jnp.exp((pos + 1.0) * LOG_G[hh])
            zeta_scr[hh] = jnp.exp((seqlen - 1.0 - pos) * LOG_G[hh])

    x = x_ref[...]
    _in_proj(x, g1_ref, win_ref, h_scr, proj_scr)

    w = RET_WIDTH
    cs = cs_ref[...]
    sn = sn_ref[...]
    pair_rows = 2 * seqlen
    rowblk = lax.broadcasted_iota(jnp.int32, (tile, 2 * RET_HD), 0) // seqlen
    colhalf = lax.broadcasted_iota(jnp.int32, (tile, 2 * RET_HD), 1) // RET_HD
    for hh in range(RET_HEADS):
        c0 = hh * RET_HD
        q = _rotary(proj_scr[:, c0:c0 + RET_HD], cs, sn)
        k = _rotary(proj_scr[:, w + c0:w + c0 + RET_HD], cs, sn) * (RET_HD ** -0.5)
        v = proj_scr[:, 2 * w + c0:2 * w + c0 + RET_HD]
        g = proj_scr[:, 3 * w + c0:3 * w + c0 + RET_HD]
        qb, kb, vb = q.astype(BF16), k.astype(BF16), v.astype(BF16)
        s = _dot_nt(qb, kb) * mask_scr[hh]
        o_intra = _dot(s.astype(BF16), vb)
        kzt = (k * zeta_scr[hh]).T.astype(BF16)
        vv = jnp.concatenate([v, v], axis=1)
        g_chunk = math.exp(seqlen * LOG_G[hh])
        cross = []
        for p in range(nseq // 2):
            sa, sb = 2 * p, 2 * p + 1
            ra = sret_ref[sa, hh]
            rb = sret_ref[sb, hh]
            rcat = jnp.concatenate([ra, rb], axis=1).astype(BF16)
            pr = _dot(qb[p * pair_rows:(p + 1) * pair_rows, :], rcat)
            cross.append(pr[0:seqlen, 0:RET_HD])
            cross.append(pr[seqlen:pair_rows, RET_HD:2 * RET_HD])
            vpair = jnp.where(rowblk == sa + colhalf, vv, 0.0).astype(BF16)
            upd = _dot(kzt, vpair)
            retnew_ref[sa, hh] = ra * g_chunk + upd[:, 0:RET_HD]
            retnew_ref[sb, hh] = rb * g_chunk + upd[:, RET_HD:2 * RET_HD]
        o = o_intra + jnp.concatenate(cross, axis=0) * xi_scr[hh]
        mix_scr[:, c0:c0 + RET_HD] = _groupnorm_gate(o, g, gng_ref[:, c0:c0 + RET_HD]).astype(BF16)

    cw = w
    u = proj_scr[:, 5 * w:5 * w + cw] * proj_scr[:, 5 * w + cw:5 * w + 2 * cw]
    cst = cst_ref[...]
    r8 = lax.broadcasted_iota(jnp.int32, (tile, cw), 0) % seqlen
    um1 = jnp.where(r8 == 0, pltpu.roll(cst, tile - 1, axis=0), pltpu.roll(u, 1, axis=0))
    um2 = jnp.where(r8 < 2, cst, pltpu.roll(u, 2, axis=0))
    conv_y = convw_ref[0:1, :] * um2 + convw_ref[1:2, :] * um1 + convw_ref[2:3, :] * u
    mix_scr[:, w:w + cw] = (proj_scr[:, 4 * w:4 * w + cw] * conv_y).astype(BF16)
    cnew_ref[...] = u.reshape(nseq, seqlen, cw)[:, seqlen - (CONV_K - 1):, :]

    _out_proj(x, mix_scr, wout_ref, g2_ref, x1_scr, h_scr)
    _ffn(wg_ref, wu_ref, wd_ref, gf_ref, y_ref, x1_scr, h_scr)


def _resident(shape):
    nd = len(shape)
    return pl.BlockSpec(shape, lambda *_: (0,) * nd, pipeline_mode=pl.Buffered(1))


def _weight_specs(ws):
    return [_resident(wt.shape) for wt in ws]


def _prompt_call(x, ws):
    batch, seq, d = x.shape
    tile = PROMPT_TILE
    tiles_per_seq = seq // tile
    n_tiles = batch * tiles_per_seq
    cs, sn = _rope_tables(np.arange(seq))
    in_cols = ws[1].shape[1]
    cw = ws[2].shape[1]
    mixer_tile = lambda i: jnp.minimum(i, n_tiles - 1)
    dense_tile = lambda i: jnp.maximum(i - 1, 0)
    mats = [ws[m] for m in MATRIX_SLOTS]
    w_in, w_out, w_gate, w_up, w_down = mats
    assert w_gate.shape == w_up.shape and w_out.shape[1] == w_down.shape[1]
    weight_specs = [pl.BlockSpec(memory_space=pl.ANY) if m in MATRIX_SLOTS else _resident(wt.shape)
                    for m, wt in enumerate(ws)]
    outs = pl.pallas_call(
        functools.partial(_prompt_kernel, n_tiles=n_tiles, tiles_per_seq=tiles_per_seq),
        grid=(n_tiles + 1,),
        in_specs=[pl.BlockSpec((tile, d), lambda i: (mixer_tile(i), 0)),
                  pl.BlockSpec((tile, RET_HD), lambda i: (mixer_tile(i) % tiles_per_seq, 0)),
                  pl.BlockSpec((tile, RET_HD), lambda i: (mixer_tile(i) % tiles_per_seq, 0))] + weight_specs,
        out_specs=[pl.BlockSpec((tile, d), lambda i: (dense_tile(i), 0)),
                   pl.BlockSpec((1, CONV_K - 1, cw), lambda i: (mixer_tile(i) // tiles_per_seq, 0, 0)),
                   pl.BlockSpec((1, RET_HEADS, RET_HD, RET_HD),
                                lambda i: (mixer_tile(i) // tiles_per_seq, 0, 0, 0))]
                  + [pl.BlockSpec(memory_space=pl.ANY)] * len(mats),
        out_shape=[jax.ShapeDtypeStruct((batch * seq, d), F32),
                   jax.ShapeDtypeStruct((batch, CONV_K - 1, cw), F32),
                   jax.ShapeDtypeStruct((batch, RET_HEADS, RET_HD, RET_HD), F32)]
                  + [jax.ShapeDtypeStruct(m.shape, BF16) for m in mats],
        scratch_shapes=[pltpu.VMEM(m.shape, BF16) for m in mats] + [
                        pltpu.SemaphoreType.DMA((tile // STAGE_ROWS,)),
                        pltpu.SemaphoreType.DMA((len(mats),)),
                        pltpu.VMEM((tile, in_cols), F32),
                        pltpu.VMEM((tile + SUBLANES, cw), F32),
                        pltpu.VMEM((tile, cw), F32),
                        pltpu.VMEM((tile, d), BF16),
                        pltpu.VMEM((tile, d), F32),
                        pltpu.VMEM((tile, d), F32),
                        pltpu.VMEM((tile, d), BF16),
                        pltpu.VMEM((tile, d), BF16),
                        pltpu.VMEM((tile, ws[6].shape[1]), BF16),
                        pltpu.VMEM((RET_HEADS, RET_HD, RET_HD), F32),
                        pltpu.VMEM((RET_HEADS, RET_CHUNK, RET_CHUNK), F32),
                        pltpu.VMEM((RET_HEADS, RET_CHUNK, RET_HD), F32),
                        pltpu.VMEM((RET_HEADS, RET_CHUNK, RET_HD), F32)],
        compiler_params=pltpu.CompilerParams(dimension_semantics=("arbitrary",),
                                             vmem_limit_bytes=VMEM_LIMIT_BYTES),
        name="prompt_layer",
    )(x.reshape(batch * seq, d), cs, sn, *ws)
    y, cnew, ret = outs[:3]
    ws_bf16 = list(ws)
    for m, wt in zip(MATRIX_SLOTS, outs[3:]):
        ws_bf16[m] = wt
    return y.reshape(batch, seq, d), cnew, ret, tuple(ws_bf16)


def _sample_call(x, state_conv, state_ret, ws):
    nseq_all, seqlen, d = x.shape
    nseq = SAMPLE_SEQS
    tile = nseq * seqlen
    cs8, sn8 = _rope_tables(PAST_LEN + np.arange(seqlen))
    cs, sn = jnp.tile(cs8, (nseq, 1)), jnp.tile(sn8, (nseq, 1))
    in_cols = ws[1].shape[1]
    cw = ws[2].shape[1]
    cst = jnp.pad(state_conv, ((0, 0), (0, seqlen - (CONV_K - 1)), (0, 0))).reshape(nseq_all * seqlen, cw)
    y, cnew, rnew = pl.pallas_call(
        _sample_kernel,
        grid=(nseq_all // nseq,),
        in_specs=[pl.BlockSpec((tile, d), lambda i: (i, 0)),
                  _resident((tile, RET_HD)), _resident((tile, RET_HD)),
                  pl.BlockSpec((tile, cw), lambda i: (i, 0)),
                  pl.BlockSpec((nseq, RET_HEADS, RET_HD, RET_HD), lambda i: (i, 0, 0, 0))] + _weight_specs(ws),
        out_specs=[pl.BlockSpec((tile, d), lambda i: (i, 0)),
                   pl.BlockSpec((nseq, CONV_K - 1, cw), lambda i: (i, 0, 0)),
                   pl.BlockSpec((nseq, RET_HEADS, RET_HD, RET_HD), lambda i: (i, 0, 0, 0))],
        out_shape=[jax.ShapeDtypeStruct((nseq_all * seqlen, d), F32),
                   jax.ShapeDtypeStruct((nseq_all, CONV_K - 1, cw), F32),
                   jax.ShapeDtypeStruct((nseq_all, RET_HEADS, RET_HD, RET_HD), F32)],
        scratch_shapes=[pltpu.VMEM((tile, in_cols), F32),
                        pltpu.VMEM((tile, d), BF16),
                        pltpu.VMEM((tile, d), F32),
                        pltpu.VMEM((tile, d), BF16),
                        pltpu.VMEM((RET_HEADS, tile, tile), F32),
                        pltpu.VMEM((RET_HEADS, tile, RET_HD), F32),
                        pltpu.VMEM((RET_HEADS, tile, RET_HD), F32)],
        compiler_params=pltpu.CompilerParams(dimension_semantics=("arbitrary",),
                                             vmem_limit_bytes=VMEM_LIMIT_BYTES),
        name="sample_layer",
    )(x.reshape(nseq_all * seqlen, d), cs, sn, cst, state_ret, *ws)
    return y.reshape(nseq_all, seqlen, d), cnew, rnew


def kernel(x_prompt, x_sample, state_conv, state_ret, norm1_g, w_in, conv_w, ret_gn_g, w_out, norm2_g,
           w_gate, w_up, w_down, norm_f_g):
    depth = w_in.shape[0]
    assert depth == 1, "the fused layer kernels take a single layer"
    row = lambda g: g.reshape(1, -1)
    ws = (row(norm1_g[0]), w_in[0], conv_w[0], row(ret_gn_g[0]), w_out[0],
          row(norm2_g[0]), w_gate[0], w_up[0], w_down[0], row(norm_f_g))
    y_p, cnew_p, ret_p, ws_bf16 = _prompt_call(x_prompt, ws)
    y_s, cnew_s, ret_s = _sample_call(x_sample, state_conv[0], state_ret[0], ws_bf16)
    return (y_p, y_s, cnew_p[None], ret_p[None], cnew_s[None], ret_s[None])
```

```python
import functools
import math

import numpy as np
import jax
import jax.numpy as jnp
from jax import lax
from jax.experimental import pallas as pl
from jax.experimental.pallas import tpu as pltpu

F32 = jnp.float32
BF16 = jnp.bfloat16

RET_HEADS = 4
RET_HD = 128
RET_WIDTH = RET_HEADS * RET_HD
CONV_K = 3
RET_CHUNK = 128
ROPE_BASE = 10000.0
NORM_EPS = 1e-6
GN_EPS = 1e-5
PAST_LEN = 16384

LOG_G = tuple(math.log1p(-(2.0 ** (-5.0 - h))) for h in range(RET_HEADS))

SUBLANES = 8
MXU_N = 256
MATRIX_SLOTS = (1, 4, 6, 7, 8)
STAGE_ROWS = 128
STAGE_SLOTS = 8
PROMPT_TILE = 256
SAMPLE_SEQS = 16
VMEM_LIMIT_BYTES = 56 * 1024 * 1024


def _ff_chunks(d_ff, width=1536):
    edges = list(range(0, d_ff, width)) + [d_ff]
    return tuple(zip(edges[:-1], edges[1:]))


def _rope_tables(pos):
    half = RET_HD // 2
    inv = ROPE_BASE ** (-np.arange(half, dtype=np.float64) / half)
    ang = np.asarray(pos, np.float64)[:, None] * inv[None, :]
    cos, sin = np.cos(ang), np.sin(ang)
    cs = np.concatenate([cos, cos], axis=1).astype(np.float32)
    sn = np.concatenate([-sin, sin], axis=1).astype(np.float32)
    return jnp.asarray(cs), jnp.asarray(sn)


def _rmsnorm(x, g):
    ms = jnp.mean(x * x, axis=-1, keepdims=True)
    return x * lax.rsqrt(ms + NORM_EPS) * g


def _silu(x):
    return x * (1.0 / (1.0 + jnp.exp(-x)))


def _rotary(t, cs, sn):
    return t * cs + pltpu.roll(t, RET_HD // 2, axis=1) * sn


def _dot(a, b):
    return jnp.dot(a, b, preferred_element_type=F32)


def _dot_nt(a, b):
    return lax.dot_general(a, b, (((1,), (1,)), ((), ())), preferred_element_type=F32)


def _dot_tn(a, b):
    return lax.dot_general(a, b, (((0,), (0,)), ((), ())), preferred_element_type=F32)


def _groupnorm_gate(o, g, gn_g):
    mu = jnp.mean(o, axis=-1, keepdims=True)
    d = o - mu
    var = jnp.mean(d * d, axis=-1, keepdims=True)
    return _silu(g) * (d * lax.rsqrt(var + GN_EPS) * gn_g)


def _in_proj(x, g1_ref, win_ref, h_scr, proj_scr):
    h_scr[...] = _rmsnorm(x, g1_ref[...]).astype(BF16)
    proj_scr[...] = _dot(h_scr[...], win_ref[...])


def _out_proj(x, mix_scr, wout_ref, g2_ref, x1_scr, h_scr):
    x1_scr[...] = x + _dot(mix_scr[...], wout_ref[...])
    h_scr[...] = _rmsnorm(x1_scr[...], g2_ref[...]).astype(BF16)


def _ffn(wg_ref, wu_ref, wd_ref, gf_ref, y_ref, x1_scr, h_scr):
    for c0, c1 in _ff_chunks(wg_ref.shape[1]):
        gate = _dot(h_scr[...], wg_ref[:, c0:c1])
        up = _dot(h_scr[...], wu_ref[:, c0:c1])
        act = (_silu(gate) * up).astype(BF16)
        x1_scr[...] += _dot(act, wd_ref[c0:c1, :])
    y_ref[...] = _rmsnorm(x1_scr[...], gf_ref[...])


def _load_weights_bf16(pairs, stage, sem):
    rb = STAGE_ROWS
    slots, width = stage.shape[0] // rb, stage.shape[1]
    assert slots >= 2 and slots <= sem.shape[0]
    chunks = []
    for w_hbm, w_bf in pairs:
        rows, cols = w_hbm.shape
        assert rows % rb == 0 and cols <= width
        chunks += [(w_hbm, w_bf, r0, cols) for r0 in range(0, rows, rb)]

    def slot_view(n):
        s0 = (n % slots) * rb
        return stage.at[s0:s0 + rb, 0:chunks[n][3]]

    def chunk_copy(n):
        w_hbm, _, r0, _ = chunks[n]
        return pltpu.make_async_copy(w_hbm.at[r0:r0 + rb, :], slot_view(n), sem.at[n % slots])

    for n in range(min(slots - 1, len(chunks))):
        chunk_copy(n).start(priority=n % 2)
    for n, (_, w_bf, r0, cols) in enumerate(chunks):
        chunk_copy(n).wait()
        w_bf[r0:r0 + rb, :] = slot_view(n)[...].astype(BF16)
        if n + slots - 1 < len(chunks):
            chunk_copy(n + slots - 1).start(priority=(n + slots - 1) % 2)


def _prompt_kernel(x_ref, cs_ref, sn_ref, g1_ref, win_hbm, convw_ref, gng_ref, wout_hbm, g2_ref,
                   wg_hbm, wu_hbm, wd_hbm, gf_ref,
                   y_ref, cnew_ref, ret_ref, win_out, wout_out, wg_out, wu_out, wd_out,
                   win_ref, wout_ref, wg_ref, wu_ref, wd_ref, stage_scr, load_sem, store_sem,
                   proj_scr, uext_scr, convy_scr, mix_scr, xkeep_scr, x1_scr, h_scr, h2_scr, act_scr, r_scr,
                   dec_scr, xi_scr, zeta_scr, *, n_tiles, tiles_per_seq):
    tile = x_ref.shape[0]
    i = pl.program_id(0)
    chunk = RET_CHUNK
    valid = i < n_tiles
    t = jnp.minimum(i, n_tiles - 1) % tiles_per_seq
    seq_start = t == 0
    seq_end = t == tiles_per_seq - 1
    weight_stores = [pltpu.make_async_copy(src, dst, store_sem.at[n]) for n, (src, dst) in enumerate(
        ((win_ref, win_out), (wout_ref, wout_out), (wg_ref, wg_out), (wu_ref, wu_out), (wd_ref, wd_out)))]

    @pl.when(i == 0)
    def _():
        _load_weights_bf16(((win_hbm, win_ref), (wout_hbm, wout_ref), (wg_hbm, wg_ref), (wu_hbm, wu_ref),
                            (wd_hbm, wd_ref)), stage_scr, load_sem)
        for cp in weight_stores:
            cp.start()
        ii = lax.broadcasted_iota(jnp.int32, (chunk, chunk), 0)
        jj = lax.broadcasted_iota(jnp.int32, (chunk, chunk), 1)
        causal = ii >= jj
        diff = jnp.where(causal, ii - jj, 0).astype(F32)
        i_f = ii.astype(F32)
        for hh in range(RET_HEADS):
            dec_scr[hh] = jnp.where(causal, jnp.exp(diff * LOG_G[hh]), 0.0)
            xi_scr[hh] = jnp.exp((i_f + 1.0) * LOG_G[hh])
            zeta_scr[hh] = jnp.exp((chunk - 1.0 - i_f) * LOG_G[hh])
        r_scr[...] = jnp.zeros_like(r_scr)
        uext_scr[0:SUBLANES, :] = jnp.zeros((SUBLANES, uext_scr.shape[1]), F32)
        mix_scr[...] = jnp.zeros_like(mix_scr)
        xkeep_scr[...] = jnp.zeros_like(xkeep_scr)

    w = RET_WIDTH
    cw = w
    d_ff = wg_ref.shape[1]

    x1_scr[...] = xkeep_scr[...] + _dot(mix_scr[...], wout_ref[...])
    x = x_ref[...]
    h_scr[...] = _rmsnorm(x, g1_ref[...]).astype(BF16)
    proj_scr[:, 0:2 * w] = _dot(h_scr[...], win_ref[:, 0:2 * w])
    h2_scr[...] = _rmsnorm(x1_scr[...], g2_ref[...]).astype(BF16)
    proj_scr[:, 2 * w:4 * w] = _dot(h_scr[...], win_ref[:, 2 * w:4 * w])

    def ffn_piece(c0):
        def emit():
            gate = _dot(h2_scr[...], wg_ref[:, c0:c0 + MXU_N])
            up = _dot(h2_scr[...], wu_ref[:, c0:c0 + MXU_N])
            act_scr[:, c0:c0 + MXU_N] = (_silu(gate) * up).astype(BF16)
        return emit

    def proj_piece(c0):
        def emit():
            proj_scr[:, c0:c0 + cw] = _dot(h_scr[...], win_ref[:, c0:c0 + cw])
        return emit

    fillers = ([proj_piece(5 * w), proj_piece(6 * w)]
               + [ffn_piece(c0) for c0 in range(0, d_ff, MXU_N)])

    def fill(n):
        for _ in range(min(n, len(fillers))):
            fillers.pop(0)()

    n_chunks = tile // chunk
    units = [(c, hh) for c in range(n_chunks) for hh in range(RET_HEADS)]
    qbs, vbs, scores, updates = {}, {}, {}, {}
    for c, hh in units:
        r0, c0 = c * chunk, hh * RET_HD
        cs = cs_ref[r0:r0 + chunk, :]
        sn = sn_ref[r0:r0 + chunk, :]
        q = _rotary(proj_scr[r0:r0 + chunk, c0:c0 + RET_HD], cs, sn)
        k = _rotary(proj_scr[r0:r0 + chunk, w + c0:w + c0 + RET_HD], cs, sn) * (RET_HD ** -0.5)
        qb, kb = q.astype(BF16), k.astype(BF16)
        vb = proj_scr[r0:r0 + chunk, 2 * w + c0:2 * w + c0 + RET_HD].astype(BF16)
        kz = (k * zeta_scr[hh]).astype(BF16)
        qbs[c, hh], vbs[c, hh] = qb, vb
        scores[c, hh] = _dot_nt(qb, kb)
        updates[c, hh] = _dot_tn(kz, vb)
    fill(2)

    u = proj_scr[:, 5 * w:5 * w + cw] * proj_scr[:, 5 * w + cw:5 * w + 2 * cw]
    uext_scr[SUBLANES:SUBLANES + tile, :] = u
    um1 = uext_scr[SUBLANES - 1:SUBLANES - 1 + tile, :]
    um2 = uext_scr[SUBLANES - 2:SUBLANES - 2 + tile, :]
    convy_scr[...] = convw_ref[0:1, :] * um2 + convw_ref[1:2, :] * um1 + convw_ref[2:3, :] * u
    tail = uext_scr[tile + SUBLANES - (CONV_K - 1):tile + SUBLANES, :]
    cnew_ref[0] = tail
    uext_scr[SUBLANES - (CONV_K - 1):SUBLANES, :] = jnp.where(seq_end, 0.0, tail)
    fill(len(fillers) // 3)

    outs = {}
    for hh in range(RET_HEADS):
        prev = r_scr[hh]
        state = jnp.where(seq_start, 0.0, prev)
        for c in range(n_chunks):
            s = (scores[c, hh] * dec_scr[hh]).astype(BF16)
            outs[c, hh] = _dot(s, vbs[c, hh]) + _dot(qbs[c, hh], state.astype(BF16)) * xi_scr[hh]
            state = state * math.exp(chunk * LOG_G[hh]) + updates[c, hh]
        state = jnp.where(valid, state, prev)
        r_scr[hh] = state
        ret_ref[0, hh] = state
    n_late = len(fillers)
    for n, (c, hh) in enumerate(units):
        fill(((n + 1) * n_late) // len(units) - (n * n_late) // len(units))
        r0, c0 = c * chunk, hh * RET_HD
        g = proj_scr[r0:r0 + chunk, 3 * w + c0:3 * w + c0 + RET_HD]
        mix_scr[r0:r0 + chunk, c0:c0 + RET_HD] = _groupnorm_gate(
            outs[c, hh], g, gng_ref[:, c0:c0 + RET_HD]).astype(BF16)
    assert not fillers

    down = _dot(act_scr[...], wd_ref[...])
    proj_piece(4 * w)()
    y_ref[...] = _rmsnorm(x1_scr[...] + down, gf_ref[...])
    mix_scr[:, w:w + cw] = (proj_scr[:, 4 * w:4 * w + cw] * convy_scr[...]).astype(BF16)
    xkeep_scr[...] = x

    @pl.when(i == n_tiles)
    def _():
        for cp in weight_stores:
            cp.wait()


def _sample_kernel(x_ref, cs_ref, sn_ref, cst_ref, sret_ref, g1_ref, win_ref, convw_ref, gng_ref,
                   wout_ref, g2_ref, wg_ref, wu_ref, wd_ref, gf_ref,
                   y_ref, cnew_ref, retnew_ref,
                   proj_scr, mix_scr, x1_scr, h_scr, mask_scr, xi_scr, zeta_scr):
    tile = x_ref.shape[0]
    nseq = sret_ref.shape[0]
    seqlen = tile // nseq
    assert seqlen == SUBLANES, "decode sequences must fill exactly one f32 sublane tile"

    @pl.when(pl.program_id(0) == 0)
    def _():
        i = lax.broadcasted_iota(jnp.int32, (tile, tile), 0)
        j = lax.broadcasted_iota(jnp.int32, (tile, tile), 1)
        keep = (i >= j) & ((i // seqlen) == (j // seqlen))
        diff = jnp.where(keep, i - j, 0).astype(F32)
        ii = lax.broadcasted_iota(jnp.int32, (tile, RET_HD), 0)
        pos = (ii % seqlen).astype(F32)
        for hh in range(RET_HEADS):
            mask_scr[hh] = jnp.where(keep, jnp.exp(diff * LOG_G[hh]), 0.0)
            xi_scr[hh] = jnp.exp((pos + 1.0) * LOG_G[hh])
            zeta_scr[hh] = jnp.exp((seqlen - 1.0 - pos) * LOG_G[hh])

    x = x_ref[...]
    _in_proj(x, g1_ref, win_ref, h_scr, proj_scr)

    w = RET_WIDTH
    cs = cs_ref[...]
    sn = sn_ref[...]
    pair_rows = 2 * seqlen
    rowblk = lax.broadcasted_iota(jnp.int32, (tile, 2 * RET_HD), 0) // seqlen
    colhalf = lax.broadcasted_iota(jnp.int32, (tile, 2 * RET_HD), 1) // RET_HD
    for hh in range(RET_HEADS):
        c0 = hh * RET_HD
        q = _rotary(proj_scr[:, c0:c0 + RET_HD], cs, sn)
        k = _rotary(proj_scr[:, w + c0:w + c0 + RET_HD], cs, sn) * (RET_HD ** -0.5)
        v = proj_scr[:, 2 * w + c0:2 * w + c0 + RET_HD]
        g = proj_scr[:, 3 * w + c0:3 * w + c0 + RET_HD]
        qb, kb, vb = q.astype(BF16), k.astype(BF16), v.astype(BF16)
        s = _dot_nt(qb, kb) * mask_scr[hh]
        o_intra = _dot(s.astype(BF16), vb)
        kzt = (k * zeta_scr[hh]).T.astype(BF16)
        vv = jnp.concatenate([v, v], axis=1)
        g_chunk = math.exp(seqlen * LOG_G[hh])
        cross = []
        for p in range(nseq // 2):
            sa, sb = 2 * p, 2 * p + 1
            ra = sret_ref[sa, hh]
            rb = sret_ref[sb, hh]
            rcat = jnp.concatenate([ra, rb], axis=1).astype(BF16)
            pr = _dot(qb[p * pair_rows:(p + 1) * pair_rows, :], rcat)
            cross.append(pr[0:seqlen, 0:RET_HD])
            cross.append(pr[seqlen:pair_rows, RET_HD:2 * RET_HD])
            vpair = jnp.where(rowblk == sa + colhalf, vv, 0.0).astype(BF16)
            upd = _dot(kzt, vpair)
            retnew_ref[sa, hh] = ra * g_chunk + upd[:, 0:RET_HD]
            retnew_ref[sb, hh] = rb * g_chunk + upd[:, RET_HD:2 * RET_HD]
        o = o_intra + jnp.concatenate(cross, axis=0) * xi_scr[hh]
        mix_scr[:, c0:c0 + RET_HD] = _groupnorm_gate(o, g, gng_ref[:, c0:c0 + RET_HD]).astype(BF16)

    cw = w
    u = proj_scr[:, 5 * w:5 * w + cw] * proj_scr[:, 5 * w + cw:5 * w + 2 * cw]
    cst = cst_ref[...]
    older = jnp.broadcast_to(cst[:, 0:1, :], (nseq, seqlen, cw)).reshape(tile, cw)
    newer = jnp.broadcast_to(cst[:, 1:2, :], (nseq, seqlen, cw)).reshape(tile, cw)
    r8 = lax.broadcasted_iota(jnp.int32, (tile, cw), 0) % seqlen
    um1 = jnp.where(r8 == 0, newer, pltpu.roll(u, 1, axis=0))
    um2 = jnp.where(r8 == 0, older, jnp.where(r8 == 1, newer, pltpu.roll(u, 2, axis=0)))
    conv_y = convw_ref[0:1, :] * um2 + convw_ref[1:2, :] * um1 + convw_ref[2:3, :] * u
    mix_scr[:, w:w + cw] = (proj_scr[:, 4 * w:4 * w + cw] * conv_y).astype(BF16)
    cnew_ref[...] = u.reshape(nseq, seqlen, cw)[:, seqlen - (CONV_K - 1):, :]

    _out_proj(x, mix_scr, wout_ref, g2_ref, x1_scr, h_scr)
    _ffn(wg_ref, wu_ref, wd_ref, gf_ref, y_ref, x1_scr, h_scr)


def _resident(shape):
    nd = len(shape)
    return pl.BlockSpec(shape, lambda *_: (0,) * nd, pipeline_mode=pl.Buffered(1))


def _weight_specs(ws):
    return [_resident(wt.shape) for wt in ws]


def _prompt_call(x, ws):
    batch, seq, d = x.shape
    tile = PROMPT_TILE
    tiles_per_seq = seq // tile
    n_tiles = batch * tiles_per_seq
    cs, sn = _rope_tables(np.arange(seq))
    in_cols = ws[1].shape[1]
    cw = ws[2].shape[1]
    mixer_tile = lambda i: jnp.minimum(i, n_tiles - 1)
    dense_tile = lambda i: jnp.maximum(i - 1, 0)
    mats = [ws[m] for m in MATRIX_SLOTS]
    w_in, w_out, w_gate, w_up, w_down = mats
    assert w_gate.shape == w_up.shape and w_out.shape[1] == w_down.shape[1]
    weight_specs = [pl.BlockSpec(memory_space=pl.ANY) if m in MATRIX_SLOTS else _resident(wt.shape)
                    for m, wt in enumerate(ws)]
    outs = pl.pallas_call(
        functools.partial(_prompt_kernel, n_tiles=n_tiles, tiles_per_seq=tiles_per_seq),
        grid=(n_tiles + 1,),
        in_specs=[pl.BlockSpec((tile, d), lambda i: (mixer_tile(i), 0)),
                  pl.BlockSpec((tile, RET_HD), lambda i: (mixer_tile(i) % tiles_per_seq, 0)),
                  pl.BlockSpec((tile, RET_HD), lambda i: (mixer_tile(i) % tiles_per_seq, 0))] + weight_specs,
        out_specs=[pl.BlockSpec((tile, d), lambda i: (dense_tile(i), 0)),
                   pl.BlockSpec((1, CONV_K - 1, cw), lambda i: (mixer_tile(i) // tiles_per_seq, 0, 0)),
                   pl.BlockSpec((1, RET_HEADS, RET_HD, RET_HD),
                                lambda i: (mixer_tile(i) // tiles_per_seq, 0, 0, 0))]
                  + [pl.BlockSpec(memory_space=pl.ANY)] * len(mats),
        out_shape=[jax.ShapeDtypeStruct((batch * seq, d), F32),
                   jax.ShapeDtypeStruct((batch, CONV_K - 1, cw), F32),
                   jax.ShapeDtypeStruct((batch, RET_HEADS, RET_HD, RET_HD), F32)]
                  + [jax.ShapeDtypeStruct(m.shape, BF16) for m in mats],
        scratch_shapes=[pltpu.VMEM(m.shape, BF16) for m in mats] + [
                        pltpu.VMEM((STAGE_SLOTS * STAGE_ROWS, max(m.shape[1] for m in mats)), F32),
                        pltpu.SemaphoreType.DMA((STAGE_SLOTS,)),
                        pltpu.SemaphoreType.DMA((len(mats),)),
                        pltpu.VMEM((tile, in_cols), F32),
                        pltpu.VMEM((tile + SUBLANES, cw), F32),
                        pltpu.VMEM((tile, cw), F32),
                        pltpu.VMEM((tile, d), BF16),
                        pltpu.VMEM((tile, d), F32),
                        pltpu.VMEM((tile, d), F32),
                        pltpu.VMEM((tile, d), BF16),
                        pltpu.VMEM((tile, d), BF16),
                        pltpu.VMEM((tile, ws[6].shape[1]), BF16),
                        pltpu.VMEM((RET_HEADS, RET_HD, RET_HD), F32),
                        pltpu.VMEM((RET_HEADS, RET_CHUNK, RET_CHUNK), F32),
                        pltpu.VMEM((RET_HEADS, RET_CHUNK, RET_HD), F32),
                        pltpu.VMEM((RET_HEADS, RET_CHUNK, RET_HD), F32)],
        compiler_params=pltpu.CompilerParams(dimension_semantics=("arbitrary",),
                                             vmem_limit_bytes=VMEM_LIMIT_BYTES),
        name="prompt_layer",
    )(x.reshape(batch * seq, d), cs, sn, *ws)
    y, cnew, ret = outs[:3]
    ws_bf16 = list(ws)
    for m, wt in zip(MATRIX_SLOTS, outs[3:]):
        ws_bf16[m] = wt
    return y.reshape(batch, seq, d), cnew, ret, tuple(ws_bf16)


def _sample_call(x, state_conv, state_ret, ws):
    nseq_all, seqlen, d = x.shape
    nseq = SAMPLE_SEQS
    tile = nseq * seqlen
    cs, sn = _rope_tables(np.tile(PAST_LEN + np.arange(seqlen), nseq))
    in_cols = ws[1].shape[1]
    cw = ws[2].shape[1]
    y, cnew, rnew = pl.pallas_call(
        _sample_kernel,
        grid=(nseq_all // nseq,),
        in_specs=[pl.BlockSpec((tile, d), lambda i: (i, 0)),
                  _resident((tile, RET_HD)), _resident((tile, RET_HD)),
                  pl.BlockSpec((nseq, CONV_K - 1, cw), lambda i: (i, 0, 0)),
                  pl.BlockSpec((nseq, RET_HEADS, RET_HD, RET_HD), lambda i: (i, 0, 0, 0))] + _weight_specs(ws),
        out_specs=[pl.BlockSpec((tile, d), lambda i: (i, 0)),
                   pl.BlockSpec((nseq, CONV_K - 1, cw), lambda i: (i, 0, 0)),
                   pl.BlockSpec((nseq, RET_HEADS, RET_HD, RET_HD), lambda i: (i, 0, 0, 0))],
        out_shape=[jax.ShapeDtypeStruct((nseq_all * seqlen, d), F32),
                   jax.ShapeDtypeStruct((nseq_all, CONV_K - 1, cw), F32),
                   jax.ShapeDtypeStruct((nseq_all, RET_HEADS, RET_HD, RET_HD), F32)],
        scratch_shapes=[pltpu.VMEM((tile, in_cols), F32),
                        pltpu.VMEM((tile, d), BF16),
                        pltpu.VMEM((tile, d), F32),
                        pltpu.VMEM((tile, d), BF16),
                        pltpu.VMEM((RET_HEADS, tile, tile), F32),
                        pltpu.VMEM((RET_HEADS, tile, RET_HD), F32),
                        pltpu.VMEM((RET_HEADS, tile, RET_HD), F32)],
        compiler_params=pltpu.CompilerParams(dimension_semantics=("arbitrary",),
                                             vmem_limit_bytes=VMEM_LIMIT_BYTES),
        name="sample_layer",
    )(x.reshape(nseq_all * seqlen, d), cs, sn, state_conv, state_ret, *ws)
    return y.reshape(nseq_all, seqlen, d), cnew, rnew


def kernel(x_prompt, x_sample, state_conv, state_ret, norm1_g, w_in, conv_w, ret_gn_g, w_out, norm2_g,
           w_gate, w_up, w_down, norm_f_g):
    depth = w_in.shape[0]
    assert depth == 1, "the fused layer kernels take a single layer"
    row = lambda g: g.reshape(1, -1)
    ws = (row(norm1_g[0]), w_in[0], conv_w[0], row(ret_gn_g[0]), w_out[0],
          row(norm2_g[0]), w_gate[0], w_up[0], w_down[0], row(norm_f_g))
    y_p, cnew_p, ret_p, ws_bf16 = _prompt_call(x_prompt, ws)
    y_s, cnew_s, ret_s = _sample_call(x_sample, state_conv[0], state_ret[0], ws_bf16)
    return (y_p, y_s, cnew_p[None], ret_p[None], cnew_s[None], ret_s[None])
```

```python
import functools
import math

import numpy as np
import jax
import jax.numpy as jnp
from jax import lax
from jax.experimental import pallas as pl
from jax.experimental.pallas import tpu as pltpu

F32 = jnp.float32
BF16 = jnp.bfloat16

RET_HEADS = 4
RET_HD = 128
RET_WIDTH = RET_HEADS * RET_HD
CONV_K = 3
RET_CHUNK = 128
ROPE_BASE = 10000.0
NORM_EPS = 1e-6
GN_EPS = 1e-5
PAST_LEN = 16384

LOG_G = tuple(math.log1p(-(2.0 ** (-5.0 - h))) for h in range(RET_HEADS))

SUBLANES = 8
MXU_N = 256
MATRIX_SLOTS = (1, 4, 6, 7, 8)
STAGE_ROWS = 128
STAGE_SLOTS = 6
PROMPT_TILE = 256
SAMPLE_SEQS = 16
VMEM_LIMIT_BYTES = 56 * 1024 * 1024


def _ff_chunks(d_ff, width=1536):
    edges = list(range(0, d_ff, width)) + [d_ff]
    return tuple(zip(edges[:-1], edges[1:]))


def _rope_tables(pos):
    half = RET_HD // 2
    inv = ROPE_BASE ** (-np.arange(half, dtype=np.float64) / half)
    ang = np.asarray(pos, np.float64)[:, None] * inv[None, :]
    cos, sin = np.cos(ang), np.sin(ang)
    cs = np.concatenate([cos, cos], axis=1).astype(np.float32)
    sn = np.concatenate([-sin, sin], axis=1).astype(np.float32)
    return jnp.asarray(cs), jnp.asarray(sn)


def _rmsnorm(x, g):
    ms = jnp.mean(x * x, axis=-1, keepdims=True)
    return x * lax.rsqrt(ms + NORM_EPS) * g


def _silu(x):
    return x * (1.0 / (1.0 + jnp.exp(-x)))


def _rotary(t, cs, sn):
    return t * cs + pltpu.roll(t, RET_HD // 2, axis=1) * sn


def _dot(a, b):
    return jnp.dot(a, b, preferred_element_type=F32)


def _dot_nt(a, b):
    return lax.dot_general(a, b, (((1,), (1,)), ((), ())), preferred_element_type=F32)


def _dot_tn(a, b):
    return lax.dot_general(a, b, (((0,), (0,)), ((), ())), preferred_element_type=F32)


def _groupnorm_gate(o, g, gn_g):
    mu = jnp.mean(o, axis=-1, keepdims=True)
    d = o - mu
    var = jnp.mean(d * d, axis=-1, keepdims=True)
    return _silu(g) * (d * lax.rsqrt(var + GN_EPS) * gn_g)


def _in_proj(x, g1_ref, win_ref, h_scr, proj_scr):
    h_scr[...] = _rmsnorm(x, g1_ref[...]).astype(BF16)
    proj_scr[...] = _dot(h_scr[...], win_ref[...])


def _out_proj(x, mix_scr, wout_ref, g2_ref, x1_scr, h_scr):
    x1_scr[...] = x + _dot(mix_scr[...], wout_ref[...])
    h_scr[...] = _rmsnorm(x1_scr[...], g2_ref[...]).astype(BF16)


def _ffn(wg_ref, wu_ref, wd_ref, gf_ref, y_ref, x1_scr, h_scr):
    for c0, c1 in _ff_chunks(wg_ref.shape[1]):
        gate = _dot(h_scr[...], wg_ref[:, c0:c1])
        up = _dot(h_scr[...], wu_ref[:, c0:c1])
        act = (_silu(gate) * up).astype(BF16)
        x1_scr[...] += _dot(act, wd_ref[c0:c1, :])
    y_ref[...] = _rmsnorm(x1_scr[...], gf_ref[...])


def _load_weights_bf16(pairs, stage, sem):
    rb = STAGE_ROWS
    slots, width = stage.shape[0] // rb, stage.shape[1]
    assert slots >= 2 and slots <= sem.shape[0]
    chunks = []
    for w_hbm, w_bf in pairs:
        rows, cols = w_hbm.shape
        assert rows % rb == 0 and cols <= width
        chunks += [(w_hbm, w_bf, r0, cols) for r0 in range(0, rows, rb)]

    def slot_view(n):
        s0 = (n % slots) * rb
        return stage.at[s0:s0 + rb, 0:chunks[n][3]]

    def chunk_copy(n):
        w_hbm, _, r0, _ = chunks[n]
        return pltpu.make_async_copy(w_hbm.at[r0:r0 + rb, :], slot_view(n), sem.at[n % slots])

    for n in range(min(slots - 1, len(chunks))):
        chunk_copy(n).start(priority=n % 2)
    for n, (_, w_bf, r0, cols) in enumerate(chunks):
        chunk_copy(n).wait()
        w_bf[r0:r0 + rb, :] = slot_view(n)[...].astype(BF16)
        if n + slots - 1 < len(chunks):
            chunk_copy(n + slots - 1).start(priority=(n + slots - 1) % 2)


def _prompt_kernel(x_ref, cs_ref, sn_ref, g1_ref, win_hbm, convw_ref, gng_ref, wout_hbm, g2_ref,
                   wg_hbm, wu_hbm, wd_hbm, gf_ref,
                   y_ref, cnew_ref, ret_ref, win_out, wout_out, wg_out, wu_out, wd_out,
                   win_ref, wout_ref, wg_ref, wu_ref, wd_ref, stage_scr, load_sem, store_sem,
                   proj_scr, uext_scr, convy_scr, mix_scr, xkeep_scr, x1_scr, h_scr, h2_scr, act_scr, r_scr,
                   dec_scr, xi_scr, zeta_scr, *, n_tiles, tiles_per_seq):
    tile = x_ref.shape[0]
    i = pl.program_id(0)
    chunk = RET_CHUNK
    t = jnp.minimum(i, n_tiles - 1) % tiles_per_seq
    seq_start = t == 0
    seq_end = t == tiles_per_seq - 1
    w = RET_WIDTH
    cw = w
    d_ff = wg_ref.shape[1]
    weight_stores = [pltpu.make_async_copy(src, dst, store_sem.at[n]) for n, (src, dst) in enumerate(
        ((win_ref, win_out), (wout_ref, wout_out), (wg_ref, wg_out), (wu_ref, wu_out), (wd_ref, wd_out)))]

    def setup():
        _load_weights_bf16(((win_hbm, win_ref), (wout_hbm, wout_ref), (wg_hbm, wg_ref), (wu_hbm, wu_ref),
                            (wd_hbm, wd_ref)), stage_scr, load_sem)
        for cp in weight_stores:
            cp.start()
        ii = lax.broadcasted_iota(jnp.int32, (chunk, chunk), 0)
        jj = lax.broadcasted_iota(jnp.int32, (chunk, chunk), 1)
        causal = ii >= jj
        diff = jnp.where(causal, ii - jj, 0).astype(F32)
        i_f = ii.astype(F32)
        for hh in range(RET_HEADS):
            dec_scr[hh] = jnp.where(causal, jnp.exp(diff * LOG_G[hh]), 0.0)
            xi_scr[hh] = jnp.exp((i_f + 1.0) * LOG_G[hh])
            zeta_scr[hh] = jnp.exp((chunk - 1.0 - i_f) * LOG_G[hh])
        r_scr[...] = jnp.zeros_like(r_scr)
        uext_scr[0:SUBLANES, :] = jnp.zeros((SUBLANES, uext_scr.shape[1]), F32)

    def ffn_piece(c0):
        def emit():
            gate = _dot(h2_scr[...], wg_ref[:, c0:c0 + MXU_N])
            up = _dot(h2_scr[...], wu_ref[:, c0:c0 + MXU_N])
            act_scr[:, c0:c0 + MXU_N] = (_silu(gate) * up).astype(BF16)
        return emit

    def proj_piece(c0):
        def emit():
            proj_scr[:, c0:c0 + cw] = _dot(h_scr[...], win_ref[:, c0:c0 + cw])
        return emit

    def step(mixer, dense):
        fillers = []

        def fill(n):
            for _ in range(min(n, len(fillers))):
                fillers.pop(0)()

        if dense:
            x1_scr[...] = xkeep_scr[...] + _dot(mix_scr[...], wout_ref[...])
        if mixer:
            x = x_ref[...]
            h_scr[...] = _rmsnorm(x, g1_ref[...]).astype(BF16)
            proj_scr[:, 0:2 * w] = _dot(h_scr[...], win_ref[:, 0:2 * w])
        if dense:
            h2_scr[...] = _rmsnorm(x1_scr[...], g2_ref[...]).astype(BF16)
            fillers += [ffn_piece(c0) for c0 in range(0, d_ff, MXU_N)]
        if not mixer:
            fill(len(fillers))
        else:
            proj_scr[:, 2 * w:4 * w] = _dot(h_scr[...], win_ref[:, 2 * w:4 * w])
            fillers[0:0] = [proj_piece(5 * w), proj_piece(6 * w)]

            n_chunks = tile // chunk
            units = [(c, hh) for c in range(n_chunks) for hh in range(RET_HEADS)]
            qbs, vbs, scores, updates = {}, {}, {}, {}
            for c, hh in units:
                r0, c0 = c * chunk, hh * RET_HD
                cs = cs_ref[r0:r0 + chunk, :]
                sn = sn_ref[r0:r0 + chunk, :]
                q = _rotary(proj_scr[r0:r0 + chunk, c0:c0 + RET_HD], cs, sn)
                k = _rotary(proj_scr[r0:r0 + chunk, w + c0:w + c0 + RET_HD], cs, sn) * (RET_HD ** -0.5)
                qb, kb = q.astype(BF16), k.astype(BF16)
                vb = proj_scr[r0:r0 + chunk, 2 * w + c0:2 * w + c0 + RET_HD].astype(BF16)
                kz = (k * zeta_scr[hh]).astype(BF16)
                qbs[c, hh], vbs[c, hh] = qb, vb
                scores[c, hh] = _dot_nt(qb, kb)
                updates[c, hh] = _dot_tn(kz, vb)
            fill(2)

            u = proj_scr[:, 5 * w:5 * w + cw] * proj_scr[:, 5 * w + cw:5 * w + 2 * cw]
            uext_scr[SUBLANES:SUBLANES + tile, :] = u
            um1 = uext_scr[SUBLANES - 1:SUBLANES - 1 + tile, :]
            um2 = uext_scr[SUBLANES - 2:SUBLANES - 2 + tile, :]
            convy_scr[...] = (convw_ref[0, 0:1, :] * um2 + convw_ref[0, 1:2, :] * um1
                              + convw_ref[0, 2:3, :] * u)
            tail = uext_scr[tile + SUBLANES - (CONV_K - 1):tile + SUBLANES, :]
            cnew_ref[0] = tail
            uext_scr[SUBLANES - (CONV_K - 1):SUBLANES, :] = jnp.where(seq_end, 0.0, tail)
            fill(len(fillers) - len(units) + 1)

            outs = {}
            for hh in range(RET_HEADS):
                state = jnp.where(seq_start, 0.0, r_scr[hh])
                for c in range(n_chunks):
                    s = (scores[c, hh] * dec_scr[hh]).astype(BF16)
                    outs[c, hh] = _dot(s, vbs[c, hh]) + _dot(qbs[c, hh], state.astype(BF16)) * xi_scr[hh]
                    state = state * math.exp(chunk * LOG_G[hh]) + updates[c, hh]
                r_scr[hh] = state
                ret_ref[0, hh] = state
            for c, hh in units:
                fill(1)
                r0, c0 = c * chunk, hh * RET_HD
                g = proj_scr[r0:r0 + chunk, 3 * w + c0:3 * w + c0 + RET_HD]
                mix_scr[r0:r0 + chunk, c0:c0 + RET_HD] = _groupnorm_gate(
                    outs[c, hh], g, gng_ref[:, c0:c0 + RET_HD]).astype(BF16)
            fill(len(fillers))

        if dense:
            down = _dot(act_scr[...], wd_ref[...])
        if mixer:
            proj_piece(4 * w)()
        if dense:
            y_ref[...] = _rmsnorm(x1_scr[...] + down, gf_ref[...])
        if mixer:
            mix_scr[:, w:w + cw] = (proj_scr[:, 4 * w:4 * w + cw] * convy_scr[...]).astype(BF16)
            xkeep_scr[...] = x

    @pl.when(i == 0)
    def _():
        setup()
        step(mixer=True, dense=False)

    @pl.when((i > 0) & (i < n_tiles))
    def _():
        step(mixer=True, dense=True)

    @pl.when(i == n_tiles)
    def _():
        step(mixer=False, dense=True)
        for cp in weight_stores:
            cp.wait()


def _sample_kernel(x_ref, cs_ref, sn_ref, cst_ref, sret_ref, g1_ref, win_ref, convw_ref, gng_ref,
                   wout_ref, g2_ref, wg_ref, wu_ref, wd_ref, gf_ref,
                   y_ref, cnew_ref, retnew_ref,
                   proj_scr, mix_scr, x1_scr, h_scr, mask_scr, xi_scr, zeta_scr):
    tile = x_ref.shape[0]
    nseq = sret_ref.shape[0]
    seqlen = tile // nseq
    assert seqlen == SUBLANES, "decode sequences must fill exactly one f32 sublane tile"

    @pl.when(pl.program_id(0) == 0)
    def _():
        i = lax.broadcasted_iota(jnp.int32, (tile, tile), 0)
        j = lax.broadcasted_iota(jnp.int32, (tile, tile), 1)
        keep = (i >= j) & ((i // seqlen) == (j // seqlen))
        diff = jnp.where(keep, i - j, 0).astype(F32)
        ii = lax.broadcasted_iota(jnp.int32, (tile, RET_HD), 0)
        pos = (ii % seqlen).astype(F32)
        for hh in range(RET_HEADS):
            mask_scr[hh] = jnp.where(keep, jnp.exp(diff * LOG_G[hh]), 0.0)
            xi_scr[hh] = jnp.exp((pos + 1.0) * LOG_G[hh])
            zeta_scr[hh] = jnp.exp((seqlen - 1.0 - pos) * LOG_G[hh])

    x = x_ref[...]
    _in_proj(x, g1_ref, win_ref, h_scr, proj_scr)

    w = RET_WIDTH
    cs = cs_ref[...]
    sn = sn_ref[...]
    pair_rows = 2 * seqlen
    rowblk = lax.broadcasted_iota(jnp.int32, (tile, 2 * RET_HD), 0) // seqlen
    colhalf = lax.broadcasted_iota(jnp.int32, (tile, 2 * RET_HD), 1) // RET_HD
    for hh in range(RET_HEADS):
        c0 = hh * RET_HD
        q = _rotary(proj_scr[:, c0:c0 + RET_HD], cs, sn)
        k = _rotary(proj_scr[:, w + c0:w + c0 + RET_HD], cs, sn) * (RET_HD ** -0.5)
        v = proj_scr[:, 2 * w + c0:2 * w + c0 + RET_HD]
        g = proj_scr[:, 3 * w + c0:3 * w + c0 + RET_HD]
        qb, kb, vb = q.astype(BF16), k.astype(BF16), v.astype(BF16)
        s = _dot_nt(qb, kb) * mask_scr[hh]
        o_intra = _dot(s.astype(BF16), vb)
        kzt = (k * zeta_scr[hh]).T.astype(BF16)
        vv = jnp.concatenate([v, v], axis=1)
        g_chunk = math.exp(seqlen * LOG_G[hh])
        cross = []
        for p in range(nseq // 2):
            sa, sb = 2 * p, 2 * p + 1
            ra = sret_ref[sa, hh]
            rb = sret_ref[sb, hh]
            rcat = jnp.concatenate([ra, rb], axis=1).astype(BF16)
            pr = _dot(qb[p * pair_rows:(p + 1) * pair_rows, :], rcat)
            cross.append(pr[0:seqlen, 0:RET_HD])
            cross.append(pr[seqlen:pair_rows, RET_HD:2 * RET_HD])
            vpair = jnp.where(rowblk == sa + colhalf, vv, 0.0).astype(BF16)
            upd = _dot(kzt, vpair)
            retnew_ref[sa, hh] = ra * g_chunk + upd[:, 0:RET_HD]
            retnew_ref[sb, hh] = rb * g_chunk + upd[:, RET_HD:2 * RET_HD]
        o = o_intra + jnp.concatenate(cross, axis=0) * xi_scr[hh]
        mix_scr[:, c0:c0 + RET_HD] = _groupnorm_gate(o, g, gng_ref[:, c0:c0 + RET_HD]).astype(BF16)

    cw = w
    u = proj_scr[:, 5 * w:5 * w + cw] * proj_scr[:, 5 * w + cw:5 * w + 2 * cw]
    cst = cst_ref[...]
    older = jnp.broadcast_to(cst[:, 0:1, :], (nseq, seqlen, cw)).reshape(tile, cw)
    newer = jnp.broadcast_to(cst[:, 1:2, :], (nseq, seqlen, cw)).reshape(tile, cw)
    r8 = lax.broadcasted_iota(jnp.int32, (tile, cw), 0) % seqlen
    um1 = jnp.where(r8 == 0, newer, pltpu.roll(u, 1, axis=0))
    um2 = jnp.where(r8 == 0, older, jnp.where(r8 == 1, newer, pltpu.roll(u, 2, axis=0)))
    conv_y = convw_ref[0, 0:1, :] * um2 + convw_ref[0, 1:2, :] * um1 + convw_ref[0, 2:3, :] * u
    mix_scr[:, w:w + cw] = (proj_scr[:, 4 * w:4 * w + cw] * conv_y).astype(BF16)
    cnew_ref[...] = u.reshape(nseq, seqlen, cw)[:, seqlen - (CONV_K - 1):, :]

    _out_proj(x, mix_scr, wout_ref, g2_ref, x1_scr, h_scr)
    _ffn(wg_ref, wu_ref, wd_ref, gf_ref, y_ref, x1_scr, h_scr)


def _resident(shape):
    nd = len(shape)
    return pl.BlockSpec(shape, lambda *_: (0,) * nd, pipeline_mode=pl.Buffered(1))


def _weight_specs(ws):
    return [_resident(wt.shape) for wt in ws]


def _prompt_call(x, ws):
    batch, seq, d = x.shape
    tile = PROMPT_TILE
    tiles_per_seq = seq // tile
    n_tiles = batch * tiles_per_seq
    cs, sn = _rope_tables(np.arange(seq))
    in_cols = ws[1].shape[1]
    cw = ws[2].shape[-1]
    mixer_tile = lambda i: jnp.minimum(i, n_tiles - 1)
    dense_tile = lambda i: jnp.maximum(i - 1, 0)
    mats = [ws[m] for m in MATRIX_SLOTS]
    w_in, w_out, w_gate, w_up, w_down = mats
    assert w_gate.shape == w_up.shape and w_out.shape[1] == w_down.shape[1]
    weight_specs = [pl.BlockSpec(memory_space=pl.ANY) if m in MATRIX_SLOTS else _resident(wt.shape)
                    for m, wt in enumerate(ws)]
    outs = pl.pallas_call(
        functools.partial(_prompt_kernel, n_tiles=n_tiles, tiles_per_seq=tiles_per_seq),
        grid=(n_tiles + 1,),
        in_specs=[pl.BlockSpec((tile, d), lambda i: (mixer_tile(i), 0)),
                  pl.BlockSpec((tile, RET_HD), lambda i: (mixer_tile(i) % tiles_per_seq, 0)),
                  pl.BlockSpec((tile, RET_HD), lambda i: (mixer_tile(i) % tiles_per_seq, 0))] + weight_specs,
        out_specs=[pl.BlockSpec((tile, d), lambda i: (dense_tile(i), 0)),
                   pl.BlockSpec((1, CONV_K - 1, cw), lambda i: (mixer_tile(i) // tiles_per_seq, 0, 0)),
                   pl.BlockSpec((1, RET_HEADS, RET_HD, RET_HD),
                                lambda i: (mixer_tile(i) // tiles_per_seq, 0, 0, 0))]
                  + [pl.BlockSpec(memory_space=pl.ANY)] * len(mats),
        out_shape=[jax.ShapeDtypeStruct((batch * seq, d), F32),
                   jax.ShapeDtypeStruct((batch, CONV_K - 1, cw), F32),
                   jax.ShapeDtypeStruct((batch, RET_HEADS, RET_HD, RET_HD), F32)]
                  + [jax.ShapeDtypeStruct(m.shape, BF16) for m in mats],
        scratch_shapes=[pltpu.VMEM(m.shape, BF16) for m in mats] + [
                        pltpu.VMEM((STAGE_SLOTS * STAGE_ROWS, max(m.shape[1] for m in mats)), F32),
                        pltpu.SemaphoreType.DMA((STAGE_SLOTS,)),
                        pltpu.SemaphoreType.DMA((len(mats),)),
                        pltpu.VMEM((tile, in_cols), F32),
                        pltpu.VMEM((tile + SUBLANES, cw), F32),
                        pltpu.VMEM((tile, cw), F32),
                        pltpu.VMEM((tile, d), BF16),
                        pltpu.VMEM((tile, d), F32),
                        pltpu.VMEM((tile, d), F32),
                        pltpu.VMEM((tile, d), BF16),
                        pltpu.VMEM((tile, d), BF16),
                        pltpu.VMEM((tile, ws[6].shape[1]), BF16),
                        pltpu.VMEM((RET_HEADS, RET_HD, RET_HD), F32),
                        pltpu.VMEM((RET_HEADS, RET_CHUNK, RET_CHUNK), F32),
                        pltpu.VMEM((RET_HEADS, RET_CHUNK, RET_HD), F32),
                        pltpu.VMEM((RET_HEADS, RET_CHUNK, RET_HD), F32)],
        compiler_params=pltpu.CompilerParams(dimension_semantics=("arbitrary",),
                                             vmem_limit_bytes=VMEM_LIMIT_BYTES),
        name="prompt_layer",
    )(x.reshape(batch * seq, d), cs, sn, *ws)
    y, cnew, ret = outs[:3]
    ws_bf16 = list(ws)
    for m, wt in zip(MATRIX_SLOTS, outs[3:]):
        ws_bf16[m] = wt
    return y.reshape(batch, seq, d), cnew, ret, tuple(ws_bf16)


def _sample_call(x, state_conv, state_ret, ws):
    nseq_all, seqlen, d = x.shape
    nseq = SAMPLE_SEQS
    tile = nseq * seqlen
    cs, sn = _rope_tables(np.tile(PAST_LEN + np.arange(seqlen), nseq))
    in_cols = ws[1].shape[1]
    cw = ws[2].shape[-1]
    y, cnew, rnew = pl.pallas_call(
        _sample_kernel,
        grid=(nseq_all // nseq,),
        in_specs=[pl.BlockSpec((tile, d), lambda i: (i, 0)),
                  _resident((tile, RET_HD)), _resident((tile, RET_HD)),
                  pl.BlockSpec((nseq, CONV_K - 1, cw), lambda i: (i, 0, 0)),
                  pl.BlockSpec((nseq, RET_HEADS, RET_HD, RET_HD), lambda i: (i, 0, 0, 0))] + _weight_specs(ws),
        out_specs=[pl.BlockSpec((tile, d), lambda i: (i, 0)),
                   pl.BlockSpec((nseq, CONV_K - 1, cw), lambda i: (i, 0, 0)),
                   pl.BlockSpec((nseq, RET_HEADS, RET_HD, RET_HD), lambda i: (i, 0, 0, 0))],
        out_shape=[jax.ShapeDtypeStruct((nseq_all * seqlen, d), F32),
                   jax.ShapeDtypeStruct((nseq_all, CONV_K - 1, cw), F32),
                   jax.ShapeDtypeStruct((nseq_all, RET_HEADS, RET_HD, RET_HD), F32)],
        scratch_shapes=[pltpu.VMEM((tile, in_cols), F32),
                        pltpu.VMEM((tile, d), BF16),
                        pltpu.VMEM((tile, d), F32),
                        pltpu.VMEM((tile, d), BF16),
                        pltpu.VMEM((RET_HEADS, tile, tile), F32),
                        pltpu.VMEM((RET_HEADS, tile, RET_HD), F32),
                        pltpu.VMEM((RET_HEADS, tile, RET_HD), F32)],
        compiler_params=pltpu.CompilerParams(dimension_semantics=("arbitrary",),
                                             vmem_limit_bytes=VMEM_LIMIT_BYTES),
        name="sample_layer",
    )(x.reshape(nseq_all * seqlen, d), cs, sn, state_conv, state_ret, *ws)
    return y.reshape(nseq_all, seqlen, d), cnew, rnew


def kernel(x_prompt, x_sample, state_conv, state_ret, norm1_g, w_in, conv_w, ret_gn_g, w_out, norm2_g,
           w_gate, w_up, w_down, norm_f_g):
    depth = w_in.shape[0]
    assert depth == 1, "the fused layer kernels take a single layer"
    row = lambda g: g.reshape(1, -1)
    ws = (row(norm1_g[0]), w_in[0], conv_w, row(ret_gn_g[0]), w_out[0],
          row(norm2_g[0]), w_gate[0], w_up[0], w_down[0], row(norm_f_g))
    y_p, cnew_p, ret_p, ws_bf16 = _prompt_call(x_prompt, ws)
    y_s, cnew_s, ret_s = _sample_call(x_sample, state_conv[0], state_ret[0], ws_bf16)
    return (y_p, y_s, cnew_p[None], ret_p[None], cnew_s[None], ret_s[None])
```

```python
import functools
import math

import numpy as np
import jax
import jax.numpy as jnp
from jax import lax
from jax.experimental import pallas as pl
from jax.experimental.pallas import tpu as pltpu

F32 = jnp.float32
BF16 = jnp.bfloat16

RET_HEADS = 4
RET_HD = 128
RET_WIDTH = RET_HEADS * RET_HD
CONV_K = 3
RET_CHUNK = 128
ROPE_BASE = 10000.0
NORM_EPS = 1e-6
GN_EPS = 1e-5
PAST_LEN = 16384

LOG_G = tuple(math.log1p(-(2.0 ** (-5.0 - h))) for h in range(RET_HEADS))

SUBLANES = 8
MXU_N = 256
MATRIX_SLOTS = (1, 4, 6, 7, 8)
STAGE_ROWS = 128
STAGE_SLOTS = 8
PROMPT_TILE = 256
SAMPLE_SEQS = 16
VMEM_LIMIT_BYTES = 60 * 1024 * 1024


def _rope_tables(pos):
    half = RET_HD // 2
    inv = ROPE_BASE ** (-np.arange(half, dtype=np.float64) / half)
    ang = np.asarray(pos, np.float64)[:, None] * inv[None, :]
    cos, sin = np.cos(ang), np.sin(ang)
    cs = np.concatenate([cos, cos], axis=1).astype(np.float32)
    sn = np.concatenate([-sin, sin], axis=1).astype(np.float32)
    return jnp.asarray(cs), jnp.asarray(sn)


def _rmsnorm(x, g):
    ms = jnp.mean(x * x, axis=-1, keepdims=True)
    return x * lax.rsqrt(ms + NORM_EPS) * g


def _silu(x):
    return x * (1.0 / (1.0 + jnp.exp(-x)))


def _rotary(t, cs, sn):
    return t * cs + pltpu.roll(t, RET_HD // 2, axis=1) * sn


def _dot(a, b):
    return jnp.dot(a, b, preferred_element_type=F32)


def _dot_nt(a, b):
    return lax.dot_general(a, b, (((1,), (1,)), ((), ())), preferred_element_type=F32)


def _dot_tn(a, b):
    return lax.dot_general(a, b, (((0,), (0,)), ((), ())), preferred_element_type=F32)


def _groupnorm_gate(o, g, gn_g):
    mu = jnp.mean(o, axis=-1, keepdims=True)
    d = o - mu
    var = jnp.mean(d * d, axis=-1, keepdims=True)
    return _silu(g) * (d * lax.rsqrt(var + GN_EPS) * gn_g)


def _in_proj(x, g1_ref, win_ref, h_scr, proj_scr):
    h_scr[...] = _rmsnorm(x, g1_ref[...]).astype(BF16)
    proj_scr[...] = _dot(h_scr[...], win_ref[...])


def _out_proj(x, mix_scr, wout_ref, g2_ref, x1_scr, h_scr):
    x1_scr[...] = x + _dot(mix_scr[...], wout_ref[...])
    h_scr[...] = _rmsnorm(x1_scr[...], g2_ref[...]).astype(BF16)


def _swiglu_piece(h_scr, wgu_ref, act_scr, c0):
    r = _dot(h_scr[...], wgu_ref[:, 2 * c0:2 * c0 + 2 * MXU_N])
    act_scr[:, c0:c0 + MXU_N] = (_silu(r[:, 0:MXU_N]) * r[:, MXU_N:2 * MXU_N]).astype(BF16)


def _ffn(wgu_ref, wd_ref, gf_ref, y_ref, x1_scr, h_scr, act_scr):
    for c0 in range(0, wd_ref.shape[0], MXU_N):
        _swiglu_piece(h_scr, wgu_ref, act_scr, c0)
    y_ref[...] = _rmsnorm(x1_scr[...] + _dot(act_scr[...], wd_ref[...]), gf_ref[...])


def _load_weights_bf16(pairs, stage, sem):
    rb = STAGE_ROWS
    slots, width = stage.shape[0] // rb, stage.shape[1]
    assert slots >= 2 and slots <= sem.shape[0]
    chunks = []
    for w_hbm, w_bf, lane in pairs:
        rows, cols = w_hbm.shape
        assert rows % rb == 0 and cols <= width and (lane is None or cols % MXU_N == 0)
        chunks += [(w_hbm, w_bf, r0, cols, lane) for r0 in range(0, rows, rb)]

    def slot_view(n):
        s0 = (n % slots) * rb
        return stage.at[s0:s0 + rb, 0:chunks[n][3]]

    def chunk_copy(n):
        w_hbm, r0 = chunks[n][0], chunks[n][2]
        return pltpu.make_async_copy(w_hbm.at[r0:r0 + rb, :], slot_view(n), sem.at[n % slots])

    for n in range(min(slots - 1, len(chunks))):
        chunk_copy(n).start(priority=n % 2)
    for n, (_, w_bf, r0, cols, lane) in enumerate(chunks):
        chunk_copy(n).wait()
        if lane is None:
            w_bf[r0:r0 + rb, :] = slot_view(n)[...].astype(BF16)
        else:
            for c0 in range(0, cols, MXU_N):
                d0 = 2 * c0 + lane * MXU_N
                w_bf[r0:r0 + rb, d0:d0 + MXU_N] = slot_view(n)[:, c0:c0 + MXU_N].astype(BF16)
        if n + slots - 1 < len(chunks):
            chunk_copy(n + slots - 1).start(priority=(n + slots - 1) % 2)


def _prompt_kernel(x_ref, cs_ref, sn_ref, g1_ref, win_hbm, convw_ref, gng_ref, wout_hbm, g2_ref,
                   wg_hbm, wu_hbm, wd_hbm, gf_ref,
                   y_ref, cnew_ref, ret_ref, win_out, wout_out, wgu_out, wd_out,
                   win_ref, wout_ref, wgu_ref, wd_ref, stage_scr, load_sem, store_sem,
                   proj_scr, uext_scr, convy_scr, mix_scr, xkeep_scr, x1_scr, h_scr, h2_scr, act_scr, r_scr,
                   dec_scr, xi_scr, zeta_scr, *, n_tiles, tiles_per_seq):
    tile = x_ref.shape[0]
    i = pl.program_id(0)
    chunk = RET_CHUNK
    t = jnp.minimum(i, n_tiles - 1) % tiles_per_seq
    seq_start = t == 0
    seq_end = t == tiles_per_seq - 1
    w = RET_WIDTH
    cw = w
    d_ff = wd_ref.shape[0]
    weight_stores = [pltpu.make_async_copy(src, dst, store_sem.at[n]) for n, (src, dst) in enumerate(
        ((win_ref, win_out), (wout_ref, wout_out), (wgu_ref, wgu_out), (wd_ref, wd_out)))]

    def setup():
        _load_weights_bf16(((win_hbm, win_ref, None), (wout_hbm, wout_ref, None), (wg_hbm, wgu_ref, 0),
                            (wu_hbm, wgu_ref, 1), (wd_hbm, wd_ref, None)), stage_scr, load_sem)
        for cp in weight_stores:
            cp.start()
        ii = lax.broadcasted_iota(jnp.int32, (chunk, chunk), 0)
        jj = lax.broadcasted_iota(jnp.int32, (chunk, chunk), 1)
        causal = ii >= jj
        diff = jnp.where(causal, ii - jj, 0).astype(F32)
        i_f = ii.astype(F32)
        for hh in range(RET_HEADS):
            dec_scr[hh] = jnp.where(causal, jnp.exp(diff * LOG_G[hh]), 0.0)
            xi_scr[hh] = jnp.exp((i_f + 1.0) * LOG_G[hh])
            zeta_scr[hh] = jnp.exp((chunk - 1.0 - i_f) * LOG_G[hh])
        r_scr[...] = jnp.zeros_like(r_scr)
        uext_scr[0:SUBLANES, :] = jnp.zeros((SUBLANES, uext_scr.shape[1]), F32)

    def ffn_piece(c0):
        return functools.partial(_swiglu_piece, h2_scr, wgu_ref, act_scr, c0)

    def proj_piece(c0):
        def emit():
            proj_scr[:, c0:c0 + cw] = _dot(h_scr[...], win_ref[:, c0:c0 + cw])
        return emit

    def step(mixer, dense):
        fillers = []

        def fill(n):
            for _ in range(min(n, len(fillers))):
                fillers.pop(0)()

        if dense:
            x1_scr[...] = xkeep_scr[...] + _dot(mix_scr[...], wout_ref[...])
        if mixer:
            x = x_ref[...]
            h_scr[...] = _rmsnorm(x, g1_ref[...]).astype(BF16)
            proj_scr[:, 0:2 * w] = _dot(h_scr[...], win_ref[:, 0:2 * w])
        if dense:
            h2_scr[...] = _rmsnorm(x1_scr[...], g2_ref[...]).astype(BF16)
            fillers += [ffn_piece(c0) for c0 in range(0, d_ff, MXU_N)]
        if not mixer:
            fill(len(fillers))
        else:
            proj_scr[:, 2 * w:4 * w] = _dot(h_scr[...], win_ref[:, 2 * w:4 * w])
            fillers[0:0] = [proj_piece(5 * w), proj_piece(6 * w)]

            n_chunks = tile // chunk
            units = [(c, hh) for c in range(n_chunks) for hh in range(RET_HEADS)]
            qbs, vbs, scores, updates = {}, {}, {}, {}
            for c, hh in units:
                r0, c0 = c * chunk, hh * RET_HD
                cs = cs_ref[r0:r0 + chunk, :]
                sn = sn_ref[r0:r0 + chunk, :]
                q = _rotary(proj_scr[r0:r0 + chunk, c0:c0 + RET_HD], cs, sn)
                k = _rotary(proj_scr[r0:r0 + chunk, w + c0:w + c0 + RET_HD], cs, sn) * (RET_HD ** -0.5)
                qb, kb = q.astype(BF16), k.astype(BF16)
                vb = proj_scr[r0:r0 + chunk, 2 * w + c0:2 * w + c0 + RET_HD].astype(BF16)
                kz = (k * zeta_scr[hh]).astype(BF16)
                qbs[c, hh], vbs[c, hh] = qb, vb
                scores[c, hh] = _dot_nt(qb, kb)
                updates[c, hh] = _dot_tn(kz, vb)
            fill(2)

            u = proj_scr[:, 5 * w:5 * w + cw] * proj_scr[:, 5 * w + cw:5 * w + 2 * cw]
            uext_scr[SUBLANES:SUBLANES + tile, :] = u
            um1 = uext_scr[SUBLANES - 1:SUBLANES - 1 + tile, :]
            um2 = uext_scr[SUBLANES - 2:SUBLANES - 2 + tile, :]
            convy_scr[...] = (convw_ref[0, 0:1, :] * um2 + convw_ref[0, 1:2, :] * um1
                              + convw_ref[0, 2:3, :] * u)
            tail = uext_scr[tile + SUBLANES - (CONV_K - 1):tile + SUBLANES, :]
            cnew_ref[0] = tail
            uext_scr[SUBLANES - (CONV_K - 1):SUBLANES, :] = jnp.where(seq_end, 0.0, tail)
            fill(len(fillers) - len(units) + 1)

            outs = {}
            for hh in range(RET_HEADS):
                state = jnp.where(seq_start, 0.0, r_scr[hh])
                for c in range(n_chunks):
                    s = (scores[c, hh] * dec_scr[hh]).astype(BF16)
                    outs[c, hh] = _dot(s, vbs[c, hh]) + _dot(qbs[c, hh], state.astype(BF16)) * xi_scr[hh]
                    state = state * math.exp(chunk * LOG_G[hh]) + updates[c, hh]
                r_scr[hh] = state
                ret_ref[0, hh] = state
            for c, hh in units:
                fill(1)
                r0, c0 = c * chunk, hh * RET_HD
                g = proj_scr[r0:r0 + chunk, 3 * w + c0:3 * w + c0 + RET_HD]
                mix_scr[r0:r0 + chunk, c0:c0 + RET_HD] = _groupnorm_gate(
                    outs[c, hh], g, gng_ref[:, c0:c0 + RET_HD]).astype(BF16)
            fill(len(fillers))

        if dense:
            down = _dot(act_scr[...], wd_ref[...])
        if mixer:
            proj_piece(4 * w)()
        if dense:
            y_ref[...] = _rmsnorm(x1_scr[...] + down, gf_ref[...])
        if mixer:
            mix_scr[:, w:w + cw] = (proj_scr[:, 4 * w:4 * w + cw] * convy_scr[...]).astype(BF16)
            xkeep_scr[...] = x

    @pl.when(i == 0)
    def _():
        setup()
        step(mixer=True, dense=False)

    @pl.when((i > 0) & (i < n_tiles))
    def _():
        step(mixer=True, dense=True)

    @pl.when(i == n_tiles)
    def _():
        step(mixer=False, dense=True)
        for cp in weight_stores:
            cp.wait()


def _sample_kernel(x_ref, cs_ref, sn_ref, cst_ref, sret_ref, g1_ref, win_ref, convw_ref, gng_ref,
                   wout_ref, g2_ref, wgu_ref, wd_ref, gf_ref,
                   y_ref, cnew_ref, retnew_ref,
                   proj_scr, mix_scr, x1_scr, h_scr, act_scr, mask_scr, xi_scr, zeta_scr):
    tile = x_ref.shape[0]
    nseq = sret_ref.shape[0]
    seqlen = tile // nseq
    assert seqlen == SUBLANES, "decode sequences must fill exactly one f32 sublane tile"

    @pl.when(pl.program_id(0) == 0)
    def _():
        i = lax.broadcasted_iota(jnp.int32, (tile, tile), 0)
        j = lax.broadcasted_iota(jnp.int32, (tile, tile), 1)
        keep = (i >= j) & ((i // seqlen) == (j // seqlen))
        diff = jnp.where(keep, i - j, 0).astype(F32)
        ii = lax.broadcasted_iota(jnp.int32, (tile, RET_HD), 0)
        pos = (ii % seqlen).astype(F32)
        for hh in range(RET_HEADS):
            mask_scr[hh] = jnp.where(keep, jnp.exp(diff * LOG_G[hh]), 0.0)
            xi_scr[hh] = jnp.exp((pos + 1.0) * LOG_G[hh])
            zeta_scr[hh] = jnp.exp((seqlen - 1.0 - pos) * LOG_G[hh])

    x = x_ref[...]
    _in_proj(x, g1_ref, win_ref, h_scr, proj_scr)

    w = RET_WIDTH
    cs = cs_ref[...]
    sn = sn_ref[...]
    pair_rows = 2 * seqlen
    rowblk = lax.broadcasted_iota(jnp.int32, (tile, 2 * RET_HD), 0) // seqlen
    colhalf = lax.broadcasted_iota(jnp.int32, (tile, 2 * RET_HD), 1) // RET_HD
    for hh in range(RET_HEADS):
        c0 = hh * RET_HD
        q = _rotary(proj_scr[:, c0:c0 + RET_HD], cs, sn)
        k = _rotary(proj_scr[:, w + c0:w + c0 + RET_HD], cs, sn) * (RET_HD ** -0.5)
        v = proj_scr[:, 2 * w + c0:2 * w + c0 + RET_HD]
        g = proj_scr[:, 3 * w + c0:3 * w + c0 + RET_HD]
        qb, kb, vb = q.astype(BF16), k.astype(BF16), v.astype(BF16)
        s = _dot_nt(qb, kb) * mask_scr[hh]
        o_intra = _dot(s.astype(BF16), vb)
        kzt = (k * zeta_scr[hh]).T.astype(BF16)
        vv = jnp.concatenate([v, v], axis=1)
        g_chunk = math.exp(seqlen * LOG_G[hh])
        cross = []
        for p in range(nseq // 2):
            sa, sb = 2 * p, 2 * p + 1
            ra = sret_ref[sa, hh]
            rb = sret_ref[sb, hh]
            rcat = jnp.concatenate([ra, rb], axis=1).astype(BF16)
            pr = _dot(qb[p * pair_rows:(p + 1) * pair_rows, :], rcat)
            cross.append(pr[0:seqlen, 0:RET_HD])
            cross.append(pr[seqlen:pair_rows, RET_HD:2 * RET_HD])
            vpair = jnp.where(rowblk == sa + colhalf, vv, 0.0).astype(BF16)
            upd = _dot(kzt, vpair)
            retnew_ref[sa, hh] = ra * g_chunk + upd[:, 0:RET_HD]
            retnew_ref[sb, hh] = rb * g_chunk + upd[:, RET_HD:2 * RET_HD]
        o = o_intra + jnp.concatenate(cross, axis=0) * xi_scr[hh]
        mix_scr[:, c0:c0 + RET_HD] = _groupnorm_gate(o, g, gng_ref[:, c0:c0 + RET_HD]).astype(BF16)

    cw = w
    u = proj_scr[:, 5 * w:5 * w + cw] * proj_scr[:, 5 * w + cw:5 * w + 2 * cw]
    cst = cst_ref[...]
    older = jnp.broadcast_to(cst[:, 0:1, :], (nseq, seqlen, cw)).reshape(tile, cw)
    newer = jnp.broadcast_to(cst[:, 1:2, :], (nseq, seqlen, cw)).reshape(tile, cw)
    r8 = lax.broadcasted_iota(jnp.int32, (tile, cw), 0) % seqlen
    um1 = jnp.where(r8 == 0, newer, pltpu.roll(u, 1, axis=0))
    um2 = jnp.where(r8 == 0, older, jnp.where(r8 == 1, newer, pltpu.roll(u, 2, axis=0)))
    conv_y = convw_ref[0, 0:1, :] * um2 + convw_ref[0, 1:2, :] * um1 + convw_ref[0, 2:3, :] * u
    mix_scr[:, w:w + cw] = (proj_scr[:, 4 * w:4 * w + cw] * conv_y).astype(BF16)
    cnew_ref[...] = u.reshape(nseq, seqlen, cw)[:, seqlen - (CONV_K - 1):, :]

    _out_proj(x, mix_scr, wout_ref, g2_ref, x1_scr, h_scr)
    _ffn(wgu_ref, wd_ref, gf_ref, y_ref, x1_scr, h_scr, act_scr)


def _resident(shape):
    nd = len(shape)
    return pl.BlockSpec(shape, lambda *_: (0,) * nd, pipeline_mode=pl.Buffered(1))


def _weight_specs(ws):
    return [_resident(wt.shape) for wt in ws]


def _prompt_call(x, ws):
    batch, seq, d = x.shape
    tile = PROMPT_TILE
    tiles_per_seq = seq // tile
    n_tiles = batch * tiles_per_seq
    cs, sn = _rope_tables(np.arange(seq))
    in_cols = ws[1].shape[1]
    cw = ws[2].shape[-1]
    mixer_tile = lambda i: jnp.minimum(i, n_tiles - 1)
    dense_tile = lambda i: jnp.maximum(i - 1, 0)
    w_in, w_out, w_gate, w_up, w_down = [ws[m] for m in MATRIX_SLOTS]
    assert w_gate.shape == w_up.shape and w_gate.shape[1] == w_down.shape[0]
    d_ff = w_down.shape[0]
    mats = [jax.ShapeDtypeStruct(s, BF16) for s in (w_in.shape, w_out.shape, (d, 2 * d_ff), w_down.shape)]
    weight_specs = [pl.BlockSpec(memory_space=pl.ANY) if m in MATRIX_SLOTS else _resident(wt.shape)
                    for m, wt in enumerate(ws)]
    outs = pl.pallas_call(
        functools.partial(_prompt_kernel, n_tiles=n_tiles, tiles_per_seq=tiles_per_seq),
        grid=(n_tiles + 1,),
        in_specs=[pl.BlockSpec((tile, d), lambda i: (mixer_tile(i), 0)),
                  pl.BlockSpec((tile, RET_HD), lambda i: (mixer_tile(i) % tiles_per_seq, 0)),
                  pl.BlockSpec((tile, RET_HD), lambda i: (mixer_tile(i) % tiles_per_seq, 0))] + weight_specs,
        out_specs=[pl.BlockSpec((tile, d), lambda i: (dense_tile(i), 0)),
                   pl.BlockSpec((1, CONV_K - 1, cw), lambda i: (mixer_tile(i) // tiles_per_seq, 0, 0)),
                   pl.BlockSpec((1, RET_HEADS, RET_HD, RET_HD),
                                lambda i: (mixer_tile(i) // tiles_per_seq, 0, 0, 0))]
                  + [pl.BlockSpec(memory_space=pl.ANY)] * len(mats),
        out_shape=[jax.ShapeDtypeStruct((batch * seq, d), F32),
                   jax.ShapeDtypeStruct((batch, CONV_K - 1, cw), F32),
                   jax.ShapeDtypeStruct((batch, RET_HEADS, RET_HD, RET_HD), F32)]
                  + mats,
        scratch_shapes=[pltpu.VMEM(m.shape, BF16) for m in mats] + [
                        pltpu.VMEM((STAGE_SLOTS * STAGE_ROWS, max(in_cols, d_ff, d)), F32),
                        pltpu.SemaphoreType.DMA((STAGE_SLOTS,)),
                        pltpu.SemaphoreType.DMA((len(mats),)),
                        pltpu.VMEM((tile, in_cols), F32),
                        pltpu.VMEM((tile + SUBLANES, cw), F32),
                        pltpu.VMEM((tile, cw), F32),
                        pltpu.VMEM((tile, d), BF16),
                        pltpu.VMEM((tile, d), F32),
                        pltpu.VMEM((tile, d), F32),
                        pltpu.VMEM((tile, d), BF16),
                        pltpu.VMEM((tile, d), BF16),
                        pltpu.VMEM((tile, d_ff), BF16),
                        pltpu.VMEM((RET_HEADS, RET_HD, RET_HD), F32),
                        pltpu.VMEM((RET_HEADS, RET_CHUNK, RET_CHUNK), F32),
                        pltpu.VMEM((RET_HEADS, RET_CHUNK, RET_HD), F32),
                        pltpu.VMEM((RET_HEADS, RET_CHUNK, RET_HD), F32)],
        compiler_params=pltpu.CompilerParams(dimension_semantics=("arbitrary",),
                                             vmem_limit_bytes=VMEM_LIMIT_BYTES),
        name="prompt_layer",
    )(x.reshape(batch * seq, d), cs, sn, *ws)
    y, cnew, ret, win_bf, wout_bf, wgu_bf, wd_bf = outs
    ws_bf16 = (ws[0], win_bf, ws[2], ws[3], wout_bf, ws[5], wgu_bf, wd_bf, ws[9])
    return y.reshape(batch, seq, d), cnew, ret, ws_bf16


def _sample_call(x, state_conv, state_ret, ws):
    nseq_all, seqlen, d = x.shape
    nseq = SAMPLE_SEQS
    tile = nseq * seqlen
    cs, sn = _rope_tables(np.tile(PAST_LEN + np.arange(seqlen), nseq))
    in_cols = ws[1].shape[1]
    cw = ws[2].shape[-1]
    y, cnew, rnew = pl.pallas_call(
        _sample_kernel,
        grid=(nseq_all // nseq,),
        in_specs=[pl.BlockSpec((tile, d), lambda i: (i, 0)),
                  _resident((tile, RET_HD)), _resident((tile, RET_HD)),
                  pl.BlockSpec((nseq, CONV_K - 1, cw), lambda i: (i, 0, 0)),
                  pl.BlockSpec((nseq, RET_HEADS, RET_HD, RET_HD), lambda i: (i, 0, 0, 0))] + _weight_specs(ws),
        out_specs=[pl.BlockSpec((tile, d), lambda i: (i, 0)),
                   pl.BlockSpec((nseq, CONV_K - 1, cw), lambda i: (i, 0, 0)),
                   pl.BlockSpec((nseq, RET_HEADS, RET_HD, RET_HD), lambda i: (i, 0, 0, 0))],
        out_shape=[jax.ShapeDtypeStruct((nseq_all * seqlen, d), F32),
                   jax.ShapeDtypeStruct((nseq_all, CONV_K - 1, cw), F32),
                   jax.ShapeDtypeStruct((nseq_all, RET_HEADS, RET_HD, RET_HD), F32)],
        scratch_shapes=[pltpu.VMEM((tile, in_cols), F32),
                        pltpu.VMEM((tile, d), BF16),
                        pltpu.VMEM((tile, d), F32),
                        pltpu.VMEM((tile, d), BF16),
                        pltpu.VMEM((tile, ws[7].shape[0]), BF16),
                        pltpu.VMEM((RET_HEADS, tile, tile), F32),
                        pltpu.VMEM((RET_HEADS, tile, RET_HD), F32),
                        pltpu.VMEM((RET_HEADS, tile, RET_HD), F32)],
        compiler_params=pltpu.CompilerParams(dimension_semantics=("arbitrary",),
                                             vmem_limit_bytes=VMEM_LIMIT_BYTES),
        name="sample_layer",
    )(x.reshape(nseq_all * seqlen, d), cs, sn, state_conv, state_ret, *ws)
    return y.reshape(nseq_all, seqlen, d), cnew, rnew


def kernel(x_prompt, x_sample, state_conv, state_ret, norm1_g, w_in, conv_w, ret_gn_g, w_out, norm2_g,
           w_gate, w_up, w_down, norm_f_g):
    depth = w_in.shape[0]
    assert depth == 1, "the fused layer kernels take a single layer"
    row = lambda g: g.reshape(1, -1)
    ws = (row(norm1_g[0]), w_in[0], conv_w, row(ret_gn_g[0]), w_out[0],
          row(norm2_g[0]), w_gate[0], w_up[0], w_down[0], row(norm_f_g))
    y_p, cnew_p, ret_p, ws_bf16 = _prompt_call(x_prompt, ws)
    y_s, cnew_s, ret_s = _sample_call(x_sample, state_conv[0], state_ret[0], ws_bf16)
    return (y_p, y_s, cnew_p[None], ret_p[None], cnew_s[None], ret_s[None])
```

```python
import functools
import math

import numpy as np
import jax
import jax.numpy as jnp
from jax import lax
from jax.experimental import pallas as pl
from jax.experimental.pallas import tpu as pltpu

F32 = jnp.float32
BF16 = jnp.bfloat16

RET_HEADS = 4
RET_HD = 128
RET_WIDTH = RET_HEADS * RET_HD
CONV_K = 3
RET_CHUNK = 128
ROPE_BASE = 10000.0
NORM_EPS = 1e-6
GN_EPS = 1e-5
PAST_LEN = 16384

LOG_G = tuple(math.log1p(-(2.0 ** (-5.0 - h))) for h in range(RET_HEADS))

SUBLANES = 8
MXU_N = 256
MATRIX_SLOTS = (1, 4, 6, 7, 8)
STAGE_ROWS = 128
STAGE_SLOTS = 8
PROMPT_TILE = 256
SAMPLE_SEQS = 16
VMEM_LIMIT_BYTES = 60 * 1024 * 1024


def _rope_tables(pos):
    half = RET_HD // 2
    inv = ROPE_BASE ** (-np.arange(half, dtype=np.float64) / half)
    ang = np.asarray(pos, np.float64)[:, None] * inv[None, :]
    cos, sin = np.cos(ang), np.sin(ang)
    cs = np.concatenate([cos, cos], axis=1).astype(np.float32)
    sn = np.concatenate([-sin, sin], axis=1).astype(np.float32)
    return jnp.asarray(cs), jnp.asarray(sn)


def _rmsnorm(x, g):
    ms = jnp.mean(x * x, axis=-1, keepdims=True)
    return x * lax.rsqrt(ms + NORM_EPS) * g


def _silu(x):
    return x * (1.0 / (1.0 + jnp.exp(-x)))


def _rotary(t, cs, sn):
    return t * cs + pltpu.roll(t, RET_HD // 2, axis=1) * sn


def _dot(a, b):
    return jnp.dot(a, b, preferred_element_type=F32)


def _dot_nt(a, b):
    return lax.dot_general(a, b, (((1,), (1,)), ((), ())), preferred_element_type=F32)


def _dot_tn(a, b):
    return lax.dot_general(a, b, (((0,), (0,)), ((), ())), preferred_element_type=F32)


def _groupnorm_gate(o, g, gn_g):
    mu = jnp.mean(o, axis=-1, keepdims=True)
    d = o - mu
    var = jnp.mean(d * d, axis=-1, keepdims=True)
    return _silu(g) * (d * lax.rsqrt(var + GN_EPS) * gn_g)


def _swiglu_piece(h_scr, wgu_ref, act_scr, c0):
    r = _dot(h_scr[...], wgu_ref[:, 2 * c0:2 * c0 + 2 * MXU_N])
    act_scr[:, c0:c0 + MXU_N] = (_silu(r[:, 0:MXU_N]) * r[:, MXU_N:2 * MXU_N]).astype(BF16)


def _load_weights_bf16(pairs, stage, sem):
    rb = STAGE_ROWS
    slots, width = stage.shape[0] // rb, stage.shape[1]
    assert slots >= 2 and slots <= sem.shape[0]
    chunks = []
    for w_hbm, w_bf, lane in pairs:
        rows, cols = w_hbm.shape
        assert rows % rb == 0 and cols <= width and (lane is None or cols % MXU_N == 0)
        chunks += [(w_hbm, w_bf, r0, cols, lane) for r0 in range(0, rows, rb)]

    def slot_view(n):
        s0 = (n % slots) * rb
        return stage.at[s0:s0 + rb, 0:chunks[n][3]]

    def chunk_copy(n):
        w_hbm, r0 = chunks[n][0], chunks[n][2]
        return pltpu.make_async_copy(w_hbm.at[r0:r0 + rb, :], slot_view(n), sem.at[n % slots])

    for n in range(min(slots - 1, len(chunks))):
        chunk_copy(n).start(priority=n % 2)
    for n, (_, w_bf, r0, cols, lane) in enumerate(chunks):
        chunk_copy(n).wait()
        if lane is None:
            w_bf[r0:r0 + rb, :] = slot_view(n)[...].astype(BF16)
        else:
            for c0 in range(0, cols, MXU_N):
                d0 = 2 * c0 + lane * MXU_N
                w_bf[r0:r0 + rb, d0:d0 + MXU_N] = slot_view(n)[:, c0:c0 + MXU_N].astype(BF16)
        if n + slots - 1 < len(chunks):
            chunk_copy(n + slots - 1).start(priority=(n + slots - 1) % 2)


def _prompt_kernel(x_ref, cs_ref, sn_ref, g1_ref, win_hbm, convw_ref, gng_ref, wout_hbm, g2_ref,
                   wg_hbm, wu_hbm, wd_hbm, gf_ref,
                   y_ref, cnew_ref, ret_ref, win_out, wout_out, wgu_out, wd_out,
                   win_ref, wout_ref, wgu_ref, wd_ref, stage_scr, load_sem, store_sem,
                   proj_scr, uext_scr, convy_scr, mix_scr, xkeep_scr, x1_scr, h_scr, h2_scr, act_scr, r_scr,
                   dec_scr, xi_scr, zeta_scr, *, n_tiles, tiles_per_seq):
    tile = x_ref.shape[0]
    i = pl.program_id(0)
    chunk = RET_CHUNK
    t = jnp.minimum(i, n_tiles - 1) % tiles_per_seq
    seq_start = t == 0
    seq_end = t == tiles_per_seq - 1
    w = RET_WIDTH
    cw = w
    d_ff = wd_ref.shape[0]
    weight_stores = [pltpu.make_async_copy(src, dst, store_sem.at[n]) for n, (src, dst) in enumerate(
        ((win_ref, win_out), (wout_ref, wout_out), (wgu_ref, wgu_out), (wd_ref, wd_out)))]

    def setup():
        _load_weights_bf16(((win_hbm, win_ref, None), (wout_hbm, wout_ref, None), (wg_hbm, wgu_ref, 0),
                            (wu_hbm, wgu_ref, 1), (wd_hbm, wd_ref, None)), stage_scr, load_sem)
        for cp in weight_stores:
            cp.start()
        ii = lax.broadcasted_iota(jnp.int32, (chunk, chunk), 0)
        jj = lax.broadcasted_iota(jnp.int32, (chunk, chunk), 1)
        causal = ii >= jj
        diff = jnp.where(causal, ii - jj, 0).astype(F32)
        i_f = ii.astype(F32)
        for hh in range(RET_HEADS):
            dec_scr[hh] = jnp.where(causal, jnp.exp(diff * LOG_G[hh]), 0.0)
            xi_scr[hh] = jnp.exp((i_f + 1.0) * LOG_G[hh])
            zeta_scr[hh] = jnp.exp((chunk - 1.0 - i_f) * LOG_G[hh])
        r_scr[...] = jnp.zeros_like(r_scr)
        uext_scr[0:SUBLANES, :] = jnp.zeros((SUBLANES, uext_scr.shape[1]), F32)

    def ffn_piece(c0):
        return functools.partial(_swiglu_piece, h2_scr, wgu_ref, act_scr, c0)

    def proj_piece(c0):
        def emit():
            proj_scr[:, c0:c0 + cw] = _dot(h_scr[...], win_ref[:, c0:c0 + cw])
        return emit

    def step(mixer, dense):
        fillers = []

        def fill(n):
            for _ in range(min(n, len(fillers))):
                fillers.pop(0)()

        if dense:
            x1_scr[...] = xkeep_scr[...] + _dot(mix_scr[...], wout_ref[...])
        if mixer:
            x = x_ref[...]
            h_scr[...] = _rmsnorm(x, g1_ref[...]).astype(BF16)
            proj_scr[:, 0:2 * w] = _dot(h_scr[...], win_ref[:, 0:2 * w])
        if dense:
            h2_scr[...] = _rmsnorm(x1_scr[...], g2_ref[...]).astype(BF16)
            fillers += [ffn_piece(c0) for c0 in range(0, d_ff, MXU_N)]
        if not mixer:
            fill(len(fillers))
        else:
            proj_scr[:, 2 * w:4 * w] = _dot(h_scr[...], win_ref[:, 2 * w:4 * w])
            fillers[0:0] = [proj_piece(5 * w), proj_piece(6 * w)]

            n_chunks = tile // chunk
            units = [(c, hh) for c in range(n_chunks) for hh in range(RET_HEADS)]
            qbs, vbs, scores, updates = {}, {}, {}, {}
            for c, hh in units:
                r0, c0 = c * chunk, hh * RET_HD
                cs = cs_ref[r0:r0 + chunk, :]
                sn = sn_ref[r0:r0 + chunk, :]
                q = _rotary(proj_scr[r0:r0 + chunk, c0:c0 + RET_HD], cs, sn)
                k = _rotary(proj_scr[r0:r0 + chunk, w + c0:w + c0 + RET_HD], cs, sn) * (RET_HD ** -0.5)
                qb, kb = q.astype(BF16), k.astype(BF16)
                vb = proj_scr[r0:r0 + chunk, 2 * w + c0:2 * w + c0 + RET_HD].astype(BF16)
                kz = (k * zeta_scr[hh]).astype(BF16)
                qbs[c, hh], vbs[c, hh] = qb, vb
                scores[c, hh] = _dot_nt(qb, kb)
                updates[c, hh] = _dot_tn(kz, vb)
            fill(2)

            u = proj_scr[:, 5 * w:5 * w + cw] * proj_scr[:, 5 * w + cw:5 * w + 2 * cw]
            uext_scr[SUBLANES:SUBLANES + tile, :] = u
            um1 = uext_scr[SUBLANES - 1:SUBLANES - 1 + tile, :]
            um2 = uext_scr[SUBLANES - 2:SUBLANES - 2 + tile, :]
            convy_scr[...] = (convw_ref[0, 0:1, :] * um2 + convw_ref[0, 1:2, :] * um1
                              + convw_ref[0, 2:3, :] * u)
            tail = uext_scr[tile + SUBLANES - (CONV_K - 1):tile + SUBLANES, :]
            cnew_ref[0] = tail
            uext_scr[SUBLANES - (CONV_K - 1):SUBLANES, :] = jnp.where(seq_end, 0.0, tail)
            fill(len(fillers) - len(units) + 1)

            outs = {}
            for hh in range(RET_HEADS):
                state = jnp.where(seq_start, 0.0, r_scr[hh])
                for c in range(n_chunks):
                    s = (scores[c, hh] * dec_scr[hh]).astype(BF16)
                    outs[c, hh] = _dot(s, vbs[c, hh]) + _dot(qbs[c, hh], state.astype(BF16)) * xi_scr[hh]
                    state = state * math.exp(chunk * LOG_G[hh]) + updates[c, hh]
                r_scr[hh] = state
                ret_ref[0, hh] = state
            for c, hh in units:
                fill(1)
                r0, c0 = c * chunk, hh * RET_HD
                g = proj_scr[r0:r0 + chunk, 3 * w + c0:3 * w + c0 + RET_HD]
                mix_scr[r0:r0 + chunk, c0:c0 + RET_HD] = _groupnorm_gate(
                    outs[c, hh], g, gng_ref[:, c0:c0 + RET_HD]).astype(BF16)
            fill(len(fillers))

        if dense:
            down = _dot(act_scr[...], wd_ref[...])
        if mixer:
            proj_piece(4 * w)()
        if dense:
            y_ref[...] = _rmsnorm(x1_scr[...] + down, gf_ref[...])
        if mixer:
            mix_scr[:, w:w + cw] = (proj_scr[:, 4 * w:4 * w + cw] * convy_scr[...]).astype(BF16)
            xkeep_scr[...] = x

    @pl.when(i == 0)
    def _():
        setup()
        step(mixer=True, dense=False)

    @pl.when((i > 0) & (i < n_tiles))
    def _():
        step(mixer=True, dense=True)

    @pl.when(i == n_tiles)
    def _():
        step(mixer=False, dense=True)
        for cp in weight_stores:
            cp.wait()


def _sample_kernel(x_ref, cs_ref, sn_ref, cst_ref, sret_ref, g1_ref, win_hbm, convw_ref, gng_ref,
                   wout_hbm, g2_ref, wgu_hbm, wd_hbm, gf_ref,
                   y_ref, cnew_ref, retnew_ref,
                   win_ref, wout_ref, wgu_ref, wd_ref, load_sem,
                   proj_scr, convy_scr, mix_scr, xkeep_scr, x1_scr, h_scr, h2_scr, act_scr,
                   mask_scr, xi_scr, zeta_scr, *, n_tiles):
    tile = x_ref.shape[0]
    nseq = sret_ref.shape[0]
    seqlen = tile // nseq
    assert seqlen == SUBLANES, "decode sequences must fill exactly one f32 sublane tile"
    i = pl.program_id(0)
    w = RET_WIDTH
    cw = w
    d_ff = wd_ref.shape[0]
    weight_loads = [pltpu.make_async_copy(src, dst, load_sem.at[n]) for n, (src, dst) in enumerate(
        ((win_hbm, win_ref), (wout_hbm, wout_ref), (wgu_hbm, wgu_ref), (wd_hbm, wd_ref)))]

    def setup():
        for cp in weight_loads:
            cp.start()
        ii = lax.broadcasted_iota(jnp.int32, (tile, tile), 0)
        jj = lax.broadcasted_iota(jnp.int32, (tile, tile), 1)
        keep = (ii >= jj) & ((ii // seqlen) == (jj // seqlen))
        diff = jnp.where(keep, ii - jj, 0).astype(F32)
        pos = (lax.broadcasted_iota(jnp.int32, (tile, RET_HD), 0) % seqlen).astype(F32)
        for hh in range(RET_HEADS):
            mask_scr[hh] = jnp.where(keep, jnp.exp(diff * LOG_G[hh]), 0.0)
            xi_scr[hh] = jnp.exp((pos + 1.0) * LOG_G[hh])
            zeta_scr[hh] = jnp.exp((seqlen - 1.0 - pos) * LOG_G[hh])
        weight_loads[0].wait()

    def proj_piece(c0):
        def emit():
            proj_scr[:, c0:c0 + cw] = _dot(h_scr[...], win_ref[:, c0:c0 + cw])
        return emit

    def step(mixer, dense):
        fillers = []

        def fill(n):
            for _ in range(min(n, len(fillers))):
                fillers.pop(0)()

        if dense:
            x1_scr[...] = xkeep_scr[...] + _dot(mix_scr[...], wout_ref[...])
        if mixer:
            x = x_ref[...]
            h_scr[...] = _rmsnorm(x, g1_ref[...]).astype(BF16)
            proj_scr[:, 0:4 * w] = _dot(h_scr[...], win_ref[:, 0:4 * w])
        if dense:
            h2_scr[...] = _rmsnorm(x1_scr[...], g2_ref[...]).astype(BF16)
            fillers += [functools.partial(_swiglu_piece, h2_scr, wgu_ref, act_scr, c0)
                        for c0 in range(0, d_ff, MXU_N)]
        if mixer:
            fillers[0:0] = [proj_piece(5 * w), proj_piece(6 * w)]
            cs = cs_ref[...]
            sn = sn_ref[...]
            pair_rows = 2 * seqlen
            rowblk = lax.broadcasted_iota(jnp.int32, (tile, 2 * RET_HD), 0) // seqlen
            colhalf = lax.broadcasted_iota(jnp.int32, (tile, 2 * RET_HD), 1) // RET_HD
            per_head = -(-len(fillers) // RET_HEADS)
            for hh in range(RET_HEADS):
                c0 = hh * RET_HD
                q = _rotary(proj_scr[:, c0:c0 + RET_HD], cs, sn)
                k = _rotary(proj_scr[:, w + c0:w + c0 + RET_HD], cs, sn) * (RET_HD ** -0.5)
                v = proj_scr[:, 2 * w + c0:2 * w + c0 + RET_HD]
                g = proj_scr[:, 3 * w + c0:3 * w + c0 + RET_HD]
                qb, kb, vb = q.astype(BF16), k.astype(BF16), v.astype(BF16)
                s = _dot_nt(qb, kb) * mask_scr[hh]
                fill(1)
                o_intra = _dot(s.astype(BF16), vb)
                kzt = (k * zeta_scr[hh]).T.astype(BF16)
                vv = jnp.concatenate([v, v], axis=1)
                g_chunk = math.exp(seqlen * LOG_G[hh])
                cross = []
                for p in range(nseq // 2):
                    sa, sb = 2 * p, 2 * p + 1
                    ra = sret_ref[sa, hh]
                    rb = sret_ref[sb, hh]
                    rcat = jnp.concatenate([ra, rb], axis=1).astype(BF16)
                    pr = _dot(qb[p * pair_rows:(p + 1) * pair_rows, :], rcat)
                    cross.append(pr[0:seqlen, 0:RET_HD])
                    cross.append(pr[seqlen:pair_rows, RET_HD:2 * RET_HD])
                    vpair = jnp.where(rowblk == sa + colhalf, vv, 0.0).astype(BF16)
                    upd = _dot(kzt, vpair)
                    retnew_ref[sa, hh] = ra * g_chunk + upd[:, 0:RET_HD]
                    retnew_ref[sb, hh] = rb * g_chunk + upd[:, RET_HD:2 * RET_HD]
                fill(per_head - 1)
                o = o_intra + jnp.concatenate(cross, axis=0) * xi_scr[hh]
                mix_scr[:, c0:c0 + RET_HD] = _groupnorm_gate(o, g, gng_ref[:, c0:c0 + RET_HD]).astype(BF16)

            u = proj_scr[:, 5 * w:5 * w + cw] * proj_scr[:, 5 * w + cw:5 * w + 2 * cw]
            cst = cst_ref[...]
            older = jnp.broadcast_to(cst[:, 0:1, :], (nseq, seqlen, cw)).reshape(tile, cw)
            newer = jnp.broadcast_to(cst[:, 1:2, :], (nseq, seqlen, cw)).reshape(tile, cw)
            r8 = lax.broadcasted_iota(jnp.int32, (tile, cw), 0) % seqlen
            um1 = jnp.where(r8 == 0, newer, pltpu.roll(u, 1, axis=0))
            um2 = jnp.where(r8 == 0, older, jnp.where(r8 == 1, newer, pltpu.roll(u, 2, axis=0)))
            convy_scr[...] = (convw_ref[0, 0:1, :] * um2 + convw_ref[0, 1:2, :] * um1
                              + convw_ref[0, 2:3, :] * u)
            cnew_ref[...] = u.reshape(nseq, seqlen, cw)[:, seqlen - (CONV_K - 1):, :]
        fill(len(fillers))

        if dense:
            down = _dot(act_scr[...], wd_ref[...])
        if mixer:
            proj_piece(4 * w)()
        if dense:
            y_ref[...] = _rmsnorm(x1_scr[...] + down, gf_ref[...])
        if mixer:
            mix_scr[:, w:w + cw] = (proj_scr[:, 4 * w:4 * w + cw] * convy_scr[...]).astype(BF16)
            xkeep_scr[...] = x

    @pl.when(i == 0)
    def _():
        setup()
        step(mixer=True, dense=False)
        for cp in weight_loads[1:]:
            cp.wait()

    @pl.when((i > 0) & (i < n_tiles))
    def _():
        step(mixer=True, dense=True)

    @pl.when(i == n_tiles)
    def _():
        step(mixer=False, dense=True)


def _resident(shape):
    nd = len(shape)
    return pl.BlockSpec(shape, lambda *_: (0,) * nd, pipeline_mode=pl.Buffered(1))


def _prompt_call(x, ws):
    batch, seq, d = x.shape
    tile = PROMPT_TILE
    tiles_per_seq = seq // tile
    n_tiles = batch * tiles_per_seq
    cs, sn = _rope_tables(np.arange(seq))
    in_cols = ws[1].shape[1]
    cw = ws[2].shape[-1]
    mixer_tile = lambda i: jnp.minimum(i, n_tiles - 1)
    dense_tile = lambda i: jnp.maximum(i - 1, 0)
    w_in, w_out, w_gate, w_up, w_down = [ws[m] for m in MATRIX_SLOTS]
    assert w_gate.shape == w_up.shape and w_gate.shape[1] == w_down.shape[0]
    d_ff = w_down.shape[0]
    mats = [jax.ShapeDtypeStruct(s, BF16) for s in (w_in.shape, w_out.shape, (d, 2 * d_ff), w_down.shape)]
    weight_specs = [pl.BlockSpec(memory_space=pl.ANY) if m in MATRIX_SLOTS else _resident(wt.shape)
                    for m, wt in enumerate(ws)]
    outs = pl.pallas_call(
        functools.partial(_prompt_kernel, n_tiles=n_tiles, tiles_per_seq=tiles_per_seq),
        grid=(n_tiles + 1,),
        in_specs=[pl.BlockSpec((tile, d), lambda i: (mixer_tile(i), 0)),
                  pl.BlockSpec((tile, RET_HD), lambda i: (mixer_tile(i) % tiles_per_seq, 0)),
                  pl.BlockSpec((tile, RET_HD), lambda i: (mixer_tile(i) % tiles_per_seq, 0))] + weight_specs,
        out_specs=[pl.BlockSpec((tile, d), lambda i: (dense_tile(i), 0)),
                   pl.BlockSpec((1, CONV_K - 1, cw), lambda i: (mixer_tile(i) // tiles_per_seq, 0, 0)),
                   pl.BlockSpec((1, RET_HEADS, RET_HD, RET_HD),
                                lambda i: (mixer_tile(i) // tiles_per_seq, 0, 0, 0))]
                  + [pl.BlockSpec(memory_space=pl.ANY)] * len(mats),
        out_shape=[jax.ShapeDtypeStruct((batch * seq, d), F32),
                   jax.ShapeDtypeStruct((batch, CONV_K - 1, cw), F32),
                   jax.ShapeDtypeStruct((batch, RET_HEADS, RET_HD, RET_HD), F32)]
                  + mats,
        scratch_shapes=[pltpu.VMEM(m.shape, BF16) for m in mats] + [
                        pltpu.VMEM((STAGE_SLOTS * STAGE_ROWS, max(in_cols, d_ff, d)), F32),
                        pltpu.SemaphoreType.DMA((STAGE_SLOTS,)),
                        pltpu.SemaphoreType.DMA((len(mats),)),
                        pltpu.VMEM((tile, in_cols), F32),
                        pltpu.VMEM((tile + SUBLANES, cw), F32),
                        pltpu.VMEM((tile, cw), F32),
                        pltpu.VMEM((tile, d), BF16),
                        pltpu.VMEM((tile, d), F32),
                        pltpu.VMEM((tile, d), F32),
                        pltpu.VMEM((tile, d), BF16),
                        pltpu.VMEM((tile, d), BF16),
                        pltpu.VMEM((tile, d_ff), BF16),
                        pltpu.VMEM((RET_HEADS, RET_HD, RET_HD), F32),
                        pltpu.VMEM((RET_HEADS, RET_CHUNK, RET_CHUNK), F32),
                        pltpu.VMEM((RET_HEADS, RET_CHUNK, RET_HD), F32),
                        pltpu.VMEM((RET_HEADS, RET_CHUNK, RET_HD), F32)],
        compiler_params=pltpu.CompilerParams(dimension_semantics=("arbitrary",),
                                             vmem_limit_bytes=VMEM_LIMIT_BYTES),
        name="prompt_layer",
    )(x.reshape(batch * seq, d), cs, sn, *ws)
    y, cnew, ret, win_bf, wout_bf, wgu_bf, wd_bf = outs
    ws_bf16 = (ws[0], win_bf, ws[2], ws[3], wout_bf, ws[5], wgu_bf, wd_bf, ws[9])
    return y.reshape(batch, seq, d), cnew, ret, ws_bf16


def _sample_call(x, state_conv, state_ret, ws):
    nseq_all, seqlen, d = x.shape
    nseq = SAMPLE_SEQS
    tile = nseq * seqlen
    n_tiles = nseq_all // nseq
    cs, sn = _rope_tables(np.tile(PAST_LEN + np.arange(seqlen), nseq))
    in_cols = ws[1].shape[1]
    cw = ws[2].shape[-1]
    d_ff = ws[7].shape[0]
    matrix_slots = (1, 4, 6, 7)
    mixer_tile = lambda i: jnp.minimum(i, n_tiles - 1)
    dense_tile = lambda i: jnp.maximum(i - 1, 0)
    weight_specs = [pl.BlockSpec(memory_space=pl.ANY) if m in matrix_slots else _resident(wt.shape)
                    for m, wt in enumerate(ws)]
    y, cnew, rnew = pl.pallas_call(
        functools.partial(_sample_kernel, n_tiles=n_tiles),
        grid=(n_tiles + 1,),
        in_specs=[pl.BlockSpec((tile, d), lambda i: (mixer_tile(i), 0)),
                  _resident((tile, RET_HD)), _resident((tile, RET_HD)),
                  pl.BlockSpec((nseq, CONV_K - 1, cw), lambda i: (mixer_tile(i), 0, 0)),
                  pl.BlockSpec((nseq, RET_HEADS, RET_HD, RET_HD), lambda i: (mixer_tile(i), 0, 0, 0))]
                 + weight_specs,
        out_specs=[pl.BlockSpec((tile, d), lambda i: (dense_tile(i), 0)),
                   pl.BlockSpec((nseq, CONV_K - 1, cw), lambda i: (mixer_tile(i), 0, 0)),
                   pl.BlockSpec((nseq, RET_HEADS, RET_HD, RET_HD), lambda i: (mixer_tile(i), 0, 0, 0))],
        out_shape=[jax.ShapeDtypeStruct((nseq_all * seqlen, d), F32),
                   jax.ShapeDtypeStruct((nseq_all, CONV_K - 1, cw), F32),
                   jax.ShapeDtypeStruct((nseq_all, RET_HEADS, RET_HD, RET_HD), F32)],
        scratch_shapes=[pltpu.VMEM(ws[m].shape, BF16) for m in matrix_slots] + [
                        pltpu.SemaphoreType.DMA((len(matrix_slots),)),
                        pltpu.VMEM((tile, in_cols), F32),
                        pltpu.VMEM((tile, cw), F32),
                        pltpu.VMEM((tile, d), BF16),
                        pltpu.VMEM((tile, d), F32),
                        pltpu.VMEM((tile, d), F32),
                        pltpu.VMEM((tile, d), BF16),
                        pltpu.VMEM((tile, d), BF16),
                        pltpu.VMEM((tile, d_ff), BF16),
                        pltpu.VMEM((RET_HEADS, tile, tile), F32),
                        pltpu.VMEM((RET_HEADS, tile, RET_HD), F32),
                        pltpu.VMEM((RET_HEADS, tile, RET_HD), F32)],
        compiler_params=pltpu.CompilerParams(dimension_semantics=("arbitrary",),
                                             vmem_limit_bytes=VMEM_LIMIT_BYTES),
        name="sample_layer",
    )(x.reshape(nseq_all * seqlen, d), cs, sn, state_conv, state_ret, *ws)
    return y.reshape(nseq_all, seqlen, d), cnew, rnew


def kernel(x_prompt, x_sample, state_conv, state_ret, norm1_g, w_in, conv_w, ret_gn_g, w_out, norm2_g,
           w_gate, w_up, w_down, norm_f_g):
    depth = w_in.shape[0]
    assert depth == 1, "the fused layer kernels take a single layer"
    row = lambda g: g.reshape(1, -1)
    ws = (row(norm1_g[0]), w_in[0], conv_w, row(ret_gn_g[0]), w_out[0],
          row(norm2_g[0]), w_gate[0], w_up[0], w_down[0], row(norm_f_g))
    y_p, cnew_p, ret_p, ws_bf16 = _prompt_call(x_prompt, ws)
    y_s, cnew_s, ret_s = _sample_call(x_sample, state_conv[0], state_ret[0], ws_bf16)
    return (y_p, y_s, cnew_p[None], ret_p[None], cnew_s[None], ret_s[None])
```

```python
import functools
import math

import numpy as np
import jax
import jax.numpy as jnp
from jax import lax
from jax.experimental import pallas as pl
from jax.experimental.pallas import tpu as pltpu

F32 = jnp.float32
BF16 = jnp.bfloat16

RET_HEADS = 4
RET_HD = 128
RET_WIDTH = RET_HEADS * RET_HD
CONV_K = 3
RET_CHUNK = 128
ROPE_BASE = 10000.0
NORM_EPS = 1e-6
GN_EPS = 1e-5
PAST_LEN = 16384

LOG_G = tuple(math.log1p(-(2.0 ** (-5.0 - h))) for h in range(RET_HEADS))

SUBLANES = 8
MXU_N = 256
MATRIX_SLOTS = (1, 4, 6, 7, 8)
STAGE_ROWS = 64
STAGE_SLOTS = 16
PROMPT_TILE = 256
SAMPLE_SEQS = 16
STATE_SLOTS = 3
VMEM_LIMIT_BYTES = 60 * 1024 * 1024


def _rope_tables(pos):
    half = RET_HD // 2
    inv = ROPE_BASE ** (-np.arange(half, dtype=np.float64) / half)
    ang = np.asarray(pos, np.float64)[:, None] * inv[None, :]
    cos, sin = np.cos(ang), np.sin(ang)
    cs = np.concatenate([cos, cos], axis=1).astype(np.float32)
    sn = np.concatenate([-sin, sin], axis=1).astype(np.float32)
    return jnp.asarray(cs), jnp.asarray(sn)


def _rmsnorm(x, g):
    ms = jnp.mean(x * x, axis=-1, keepdims=True)
    return x * lax.rsqrt(ms + NORM_EPS) * g


def _silu(x):
    return x * (1.0 / (1.0 + jnp.exp(-x)))


def _rotary(t, cs, sn):
    return t * cs + pltpu.roll(t, RET_HD // 2, axis=1) * sn


def _dot(a, b):
    return jnp.dot(a, b, preferred_element_type=F32)


def _dot_nt(a, b):
    return lax.dot_general(a, b, (((1,), (1,)), ((), ())), preferred_element_type=F32)


def _dot_tn(a, b):
    return lax.dot_general(a, b, (((0,), (0,)), ((), ())), preferred_element_type=F32)


def _groupnorm_gate(o, g, gn_g):
    mu = jnp.mean(o, axis=-1, keepdims=True)
    d = o - mu
    var = jnp.mean(d * d, axis=-1, keepdims=True)
    return _silu(g) * (d * lax.rsqrt(var + GN_EPS) * gn_g)


def _swiglu_piece(h_scr, wgu_ref, act_scr, c0):
    r = _dot(h_scr[...], wgu_ref[:, 2 * c0:2 * c0 + 2 * MXU_N])
    act_scr[:, c0:c0 + MXU_N] = (_silu(r[:, 0:MXU_N]) * r[:, MXU_N:2 * MXU_N]).astype(BF16)


def _load_weights_bf16(pairs, stage, sem):
    rb = STAGE_ROWS
    slots, width = stage.shape[0] // rb, stage.shape[1]
    assert slots >= 2 and slots <= sem.shape[0]
    chunks = []
    for w_hbm, w_bf, lane in pairs:
        rows, cols = w_hbm.shape
        assert rows % rb == 0 and cols <= width and (lane is None or cols % MXU_N == 0)
        chunks += [(w_hbm, w_bf, r0, cols, lane) for r0 in range(0, rows, rb)]

    def slot_view(n):
        s0 = (n % slots) * rb
        return stage.at[s0:s0 + rb, 0:chunks[n][3]]

    def chunk_copy(n):
        w_hbm, r0 = chunks[n][0], chunks[n][2]
        return pltpu.make_async_copy(w_hbm.at[r0:r0 + rb, :], slot_view(n), sem.at[n % slots])

    for n in range(min(slots - 1, len(chunks))):
        chunk_copy(n).start(priority=n % 2)
    for n, (_, w_bf, r0, cols, lane) in enumerate(chunks):
        chunk_copy(n).wait()
        if lane is None:
            w_bf[r0:r0 + rb, :] = slot_view(n)[...].astype(BF16)
        else:
            for c0 in range(0, cols, MXU_N):
                d0 = 2 * c0 + lane * MXU_N
                w_bf[r0:r0 + rb, d0:d0 + MXU_N] = slot_view(n)[:, c0:c0 + MXU_N].astype(BF16)
        if n + slots - 1 < len(chunks):
            chunk_copy(n + slots - 1).start(priority=(n + slots - 1) % 2)


def _prompt_kernel(x_ref, cs_ref, sn_ref, g1_ref, win_hbm, convw_ref, gng_ref, wout_hbm, g2_ref,
                   wg_hbm, wu_hbm, wd_hbm, gf_ref,
                   y_ref, cnew_ref, ret_ref, win_out, wout_out, wgu_out, wd_out,
                   win_ref, wout_ref, wgu_ref, wd_ref, stage_scr, load_sem, store_sem,
                   proj_scr, uext_scr, convy_scr, mix_scr, xkeep_scr, x1_scr, h_scr, h2_scr, act_scr, r_scr,
                   dec_scr, xi_scr, zeta_scr, *, n_tiles, tiles_per_seq):
    tile = x_ref.shape[0]
    i = pl.program_id(0)
    chunk = RET_CHUNK
    t = jnp.minimum(i, n_tiles - 1) % tiles_per_seq
    seq_start = t == 0
    seq_end = t == tiles_per_seq - 1
    w = RET_WIDTH
    cw = w
    d_ff = wd_ref.shape[0]
    weight_stores = [pltpu.make_async_copy(src, dst, store_sem.at[n]) for n, (src, dst) in enumerate(
        ((win_ref, win_out), (wout_ref, wout_out), (wgu_ref, wgu_out), (wd_ref, wd_out)))]

    def setup():
        _load_weights_bf16(((win_hbm, win_ref, None), (wout_hbm, wout_ref, None), (wg_hbm, wgu_ref, 0),
                            (wu_hbm, wgu_ref, 1), (wd_hbm, wd_ref, None)), stage_scr, load_sem)
        for cp in weight_stores:
            cp.start()
        ii = lax.broadcasted_iota(jnp.int32, (chunk, chunk), 0)
        jj = lax.broadcasted_iota(jnp.int32, (chunk, chunk), 1)
        causal = ii >= jj
        diff = jnp.where(causal, ii - jj, 0).astype(F32)
        i_f = ii.astype(F32)
        for hh in range(RET_HEADS):
            dec_scr[hh] = jnp.where(causal, jnp.exp(diff * LOG_G[hh]), 0.0)
            xi_scr[hh] = jnp.exp((i_f + 1.0) * LOG_G[hh])
            zeta_scr[hh] = jnp.exp((chunk - 1.0 - i_f) * LOG_G[hh])
        r_scr[...] = jnp.zeros_like(r_scr)
        uext_scr[0:SUBLANES, :] = jnp.zeros((SUBLANES, uext_scr.shape[1]), F32)

    def ffn_piece(c0):
        return functools.partial(_swiglu_piece, h2_scr, wgu_ref, act_scr, c0)

    def proj_piece(c0):
        def emit():
            proj_scr[:, c0:c0 + cw] = _dot(h_scr[...], win_ref[:, c0:c0 + cw])
        return emit

    def step(mixer, dense):
        fillers = []

        def fill(n):
            for _ in range(min(n, len(fillers))):
                fillers.pop(0)()

        if dense:
            x1_scr[...] = xkeep_scr[...] + _dot(mix_scr[...], wout_ref[...])
        if mixer:
            x = x_ref[...]
            h_scr[...] = _rmsnorm(x, g1_ref[...]).astype(BF16)
            proj_scr[:, 0:2 * w] = _dot(h_scr[...], win_ref[:, 0:2 * w])
        if dense:
            h2_scr[...] = _rmsnorm(x1_scr[...], g2_ref[...]).astype(BF16)
            fillers += [ffn_piece(c0) for c0 in range(0, d_ff, MXU_N)]
        if not mixer:
            fill(len(fillers))
        else:
            proj_scr[:, 2 * w:4 * w] = _dot(h_scr[...], win_ref[:, 2 * w:4 * w])
            fillers[0:0] = [proj_piece(5 * w), proj_piece(6 * w)]

            n_chunks = tile // chunk
            units = [(c, hh) for c in range(n_chunks) for hh in range(RET_HEADS)]
            qbs, vbs, scores, updates = {}, {}, {}, {}
            for c, hh in units:
                r0, c0 = c * chunk, hh * RET_HD
                cs = cs_ref[r0:r0 + chunk, :]
                sn = sn_ref[r0:r0 + chunk, :]
                q = _rotary(proj_scr[r0:r0 + chunk, c0:c0 + RET_HD], cs, sn)
                k = _rotary(proj_scr[r0:r0 + chunk, w + c0:w + c0 + RET_HD], cs, sn) * (RET_HD ** -0.5)
                qb, kb = q.astype(BF16), k.astype(BF16)
                vb = proj_scr[r0:r0 + chunk, 2 * w + c0:2 * w + c0 + RET_HD].astype(BF16)
                kz = (k * zeta_scr[hh]).astype(BF16)
                qbs[c, hh], vbs[c, hh] = qb, vb
                scores[c, hh] = _dot_nt(qb, kb)
                updates[c, hh] = _dot_tn(kz, vb)
            fill(2)

            u = proj_scr[:, 5 * w:5 * w + cw] * proj_scr[:, 5 * w + cw:5 * w + 2 * cw]
            uext_scr[SUBLANES:SUBLANES + tile, :] = u
            um1 = uext_scr[SUBLANES - 1:SUBLANES - 1 + tile, :]
            um2 = uext_scr[SUBLANES - 2:SUBLANES - 2 + tile, :]
            convy_scr[...] = (convw_ref[0, 0:1, :] * um2 + convw_ref[0, 1:2, :] * um1
                              + convw_ref[0, 2:3, :] * u)
            tail = uext_scr[tile + SUBLANES - (CONV_K - 1):tile + SUBLANES, :]
            cnew_ref[0] = tail
            uext_scr[SUBLANES - (CONV_K - 1):SUBLANES, :] = jnp.where(seq_end, 0.0, tail)
            fill(len(fillers) - len(units) + 1)

            outs = {}
            for hh in range(RET_HEADS):
                state = jnp.where(seq_start, 0.0, r_scr[hh])
                for c in range(n_chunks):
                    s = (scores[c, hh] * dec_scr[hh]).astype(BF16)
                    outs[c, hh] = _dot(s, vbs[c, hh]) + _dot(qbs[c, hh], state.astype(BF16)) * xi_scr[hh]
                    state = state * math.exp(chunk * LOG_G[hh]) + updates[c, hh]
                r_scr[hh] = state
                ret_ref[0, hh] = state
            for c, hh in units:
                fill(1)
                r0, c0 = c * chunk, hh * RET_HD
                g = proj_scr[r0:r0 + chunk, 3 * w + c0:3 * w + c0 + RET_HD]
                mix_scr[r0:r0 + chunk, c0:c0 + RET_HD] = _groupnorm_gate(
                    outs[c, hh], g, gng_ref[:, c0:c0 + RET_HD]).astype(BF16)
            fill(len(fillers))

        if dense:
            down = _dot(act_scr[...], wd_ref[...])
        if mixer:
            proj_piece(4 * w)()
        if dense:
            y_ref[...] = _rmsnorm(x1_scr[...] + down, gf_ref[...])
        if mixer:
            mix_scr[:, w:w + cw] = (proj_scr[:, 4 * w:4 * w + cw] * convy_scr[...]).astype(BF16)
            xkeep_scr[...] = x

    @pl.when(i == 0)
    def _():
        setup()
        step(mixer=True, dense=False)

    @pl.when((i > 0) & (i < n_tiles))
    def _():
        step(mixer=True, dense=True)

    @pl.when(i == n_tiles)
    def _():
        step(mixer=False, dense=True)
        for cp in weight_stores:
            cp.wait()


def _sample_kernel(x_ref, cs_ref, sn_ref, cst_ref, sret_hbm, g1_ref, win_hbm, convw_ref, gng_ref,
                   wout_hbm, g2_ref, wgu_hbm, wd_hbm, gf_ref,
                   y_ref, cnew_ref, retnew_ref,
                   win_ref, wout_ref, wgu_ref, wd_ref, load_sem, sret_scr, state_sem,
                   proj_scr, convy_scr, mix_scr, xkeep_scr, x1_scr, h_scr, h2_scr, act_scr,
                   mask_scr, xi_scr, zeta_scr, *, n_tiles):
    tile = x_ref.shape[0]
    nseq = sret_scr.shape[1]
    seqlen = tile // nseq
    assert seqlen == SUBLANES, "decode sequences must fill exactly one f32 sublane tile"
    i = pl.program_id(0)
    w = RET_WIDTH
    cw = w
    d_ff = wd_ref.shape[0]
    weight_loads = [pltpu.make_async_copy(src, dst, load_sem.at[n]) for n, (src, dst) in enumerate(
        ((win_hbm, win_ref), (wout_hbm, wout_ref), (wgu_hbm, wgu_ref), (wd_hbm, wd_ref)))]

    def state_fetch(tile_idx):
        slot = tile_idx % STATE_SLOTS
        return pltpu.make_async_copy(sret_hbm.at[pl.ds(tile_idx * nseq, nseq)], sret_scr.at[slot],
                                     state_sem.at[slot])

    def setup():
        for ahead in range(min(STATE_SLOTS - 1, n_tiles)):
            state_fetch(ahead).start()
        for cp in weight_loads:
            cp.start()
        ii = lax.broadcasted_iota(jnp.int32, (tile, tile), 0)
        jj = lax.broadcasted_iota(jnp.int32, (tile, tile), 1)
        keep = (ii >= jj) & ((ii // seqlen) == (jj // seqlen))
        diff = jnp.where(keep, ii - jj, 0).astype(F32)
        pos = (lax.broadcasted_iota(jnp.int32, (tile, RET_HD), 0) % seqlen).astype(F32)
        for hh in range(RET_HEADS):
            mask_scr[hh] = jnp.where(keep, jnp.exp(diff * LOG_G[hh]), 0.0)
            xi_scr[hh] = jnp.exp((pos + 1.0) * LOG_G[hh])
            zeta_scr[hh] = jnp.exp((seqlen - 1.0 - pos) * LOG_G[hh])
        weight_loads[0].wait()

    def proj_piece(c0):
        def emit():
            proj_scr[:, c0:c0 + cw] = _dot(h_scr[...], win_ref[:, c0:c0 + cw])
        return emit

    def step(mixer, dense):
        fillers = []

        def fill(n):
            for _ in range(min(n, len(fillers))):
                fillers.pop(0)()

        if dense:
            x1_scr[...] = xkeep_scr[...] + _dot(mix_scr[...], wout_ref[...])
        if mixer:
            x = x_ref[...]
            h_scr[...] = _rmsnorm(x, g1_ref[...]).astype(BF16)
            proj_scr[:, 0:4 * w] = _dot(h_scr[...], win_ref[:, 0:4 * w])
        if dense:
            h2_scr[...] = _rmsnorm(x1_scr[...], g2_ref[...]).astype(BF16)
            fillers += [functools.partial(_swiglu_piece, h2_scr, wgu_ref, act_scr, c0)
                        for c0 in range(0, d_ff, MXU_N)]
        if mixer:
            fillers[0:0] = [proj_piece(5 * w), proj_piece(6 * w)]
            cs = cs_ref[...]
            sn = sn_ref[...]
            pair_rows = 2 * seqlen
            rowblk = lax.broadcasted_iota(jnp.int32, (tile, 2 * RET_HD), 0) // seqlen
            colhalf = lax.broadcasted_iota(jnp.int32, (tile, 2 * RET_HD), 1) // RET_HD
            per_head = -(-len(fillers) // RET_HEADS)
            for hh in range(RET_HEADS):
                c0 = hh * RET_HD
                q = _rotary(proj_scr[:, c0:c0 + RET_HD], cs, sn)
                k = _rotary(proj_scr[:, w + c0:w + c0 + RET_HD], cs, sn) * (RET_HD ** -0.5)
                v = proj_scr[:, 2 * w + c0:2 * w + c0 + RET_HD]
                g = proj_scr[:, 3 * w + c0:3 * w + c0 + RET_HD]
                qb, kb, vb = q.astype(BF16), k.astype(BF16), v.astype(BF16)
                s = _dot_nt(qb, kb) * mask_scr[hh]
                fill(1)
                o_intra = _dot(s.astype(BF16), vb)
                kzt = (k * zeta_scr[hh]).T.astype(BF16)
                vv = jnp.concatenate([v, v], axis=1)
                g_chunk = math.exp(seqlen * LOG_G[hh])
                cross = []
                for p in range(nseq // 2):
                    sa, sb = 2 * p, 2 * p + 1
                    ra = sret_scr[i % STATE_SLOTS, sa, hh]
                    rb = sret_scr[i % STATE_SLOTS, sb, hh]
                    rcat = jnp.concatenate([ra, rb], axis=1).astype(BF16)
                    pr = _dot(qb[p * pair_rows:(p + 1) * pair_rows, :], rcat)
                    cross.append(pr[0:seqlen, 0:RET_HD])
                    cross.append(pr[seqlen:pair_rows, RET_HD:2 * RET_HD])
                    vpair = jnp.where(rowblk == sa + colhalf, vv, 0.0).astype(BF16)
                    upd = _dot(kzt, vpair)
                    retnew_ref[sa, hh] = ra * g_chunk + upd[:, 0:RET_HD]
                    retnew_ref[sb, hh] = rb * g_chunk + upd[:, RET_HD:2 * RET_HD]
                fill(per_head - 1)
                o = o_intra + jnp.concatenate(cross, axis=0) * xi_scr[hh]
                mix_scr[:, c0:c0 + RET_HD] = _groupnorm_gate(o, g, gng_ref[:, c0:c0 + RET_HD]).astype(BF16)

            u = proj_scr[:, 5 * w:5 * w + cw] * proj_scr[:, 5 * w + cw:5 * w + 2 * cw]
            cst = cst_ref[...]
            older = jnp.broadcast_to(cst[:, 0:1, :], (nseq, seqlen, cw)).reshape(tile, cw)
            newer = jnp.broadcast_to(cst[:, 1:2, :], (nseq, seqlen, cw)).reshape(tile, cw)
            r8 = lax.broadcasted_iota(jnp.int32, (tile, cw), 0) % seqlen
            um1 = jnp.where(r8 == 0, newer, pltpu.roll(u, 1, axis=0))
            um2 = jnp.where(r8 == 0, older, jnp.where(r8 == 1, newer, pltpu.roll(u, 2, axis=0)))
            convy_scr[...] = (convw_ref[0, 0:1, :] * um2 + convw_ref[0, 1:2, :] * um1
                              + convw_ref[0, 2:3, :] * u)
            cnew_ref[...] = u.reshape(nseq, seqlen, cw)[:, seqlen - (CONV_K - 1):, :]
        fill(len(fillers))

        if dense:
            down = _dot(act_scr[...], wd_ref[...])
        if mixer:
            proj_piece(4 * w)()
        if dense:
            y_ref[...] = _rmsnorm(x1_scr[...] + down, gf_ref[...])
        if mixer:
            mix_scr[:, w:w + cw] = (proj_scr[:, 4 * w:4 * w + cw] * convy_scr[...]).astype(BF16)
            xkeep_scr[...] = x

    def advance_state_ring():
        @pl.when(i + STATE_SLOTS - 1 < n_tiles)
        def _():
            state_fetch(i + STATE_SLOTS - 1).start()

        state_fetch(i).wait()

    @pl.when(i == 0)
    def _():
        setup()
        advance_state_ring()
        step(mixer=True, dense=False)
        for cp in weight_loads[1:]:
            cp.wait()

    @pl.when((i > 0) & (i < n_tiles))
    def _():
        advance_state_ring()
        step(mixer=True, dense=True)

    @pl.when(i == n_tiles)
    def _():
        step(mixer=False, dense=True)


def _resident(shape):
    nd = len(shape)
    return pl.BlockSpec(shape, lambda *_: (0,) * nd, pipeline_mode=pl.Buffered(1))


def _prompt_call(x, ws):
    batch, seq, d = x.shape
    tile = PROMPT_TILE
    tiles_per_seq = seq // tile
    n_tiles = batch * tiles_per_seq
    cs, sn = _rope_tables(np.arange(seq))
    in_cols = ws[1].shape[1]
    cw = ws[2].shape[-1]
    mixer_tile = lambda i: jnp.minimum(i, n_tiles - 1)
    dense_tile = lambda i: jnp.maximum(i - 1, 0)
    w_in, w_out, w_gate, w_up, w_down = [ws[m] for m in MATRIX_SLOTS]
    assert w_gate.shape == w_up.shape and w_gate.shape[1] == w_down.shape[0]
    d_ff = w_down.shape[0]
    mats = [jax.ShapeDtypeStruct(s, BF16) for s in (w_in.shape, w_out.shape, (d, 2 * d_ff), w_down.shape)]
    weight_specs = [pl.BlockSpec(memory_space=pl.ANY) if m in MATRIX_SLOTS else _resident(wt.shape)
                    for m, wt in enumerate(ws)]
    outs = pl.pallas_call(
        functools.partial(_prompt_kernel, n_tiles=n_tiles, tiles_per_seq=tiles_per_seq),
        grid=(n_tiles + 1,),
        in_specs=[pl.BlockSpec((tile, d), lambda i: (mixer_tile(i), 0)),
                  pl.BlockSpec((tile, RET_HD), lambda i: (mixer_tile(i) % tiles_per_seq, 0)),
                  pl.BlockSpec((tile, RET_HD), lambda i: (mixer_tile(i) % tiles_per_seq, 0))] + weight_specs,
        out_specs=[pl.BlockSpec((tile, d), lambda i: (dense_tile(i), 0)),
                   pl.BlockSpec((1, CONV_K - 1, cw), lambda i: (mixer_tile(i) // tiles_per_seq, 0, 0)),
                   pl.BlockSpec((1, RET_HEADS, RET_HD, RET_HD),
                                lambda i: (mixer_tile(i) // tiles_per_seq, 0, 0, 0))]
                  + [pl.BlockSpec(memory_space=pl.ANY)] * len(mats),
        out_shape=[jax.ShapeDtypeStruct((batch * seq, d), F32),
                   jax.ShapeDtypeStruct((batch, CONV_K - 1, cw), F32),
                   jax.ShapeDtypeStruct((batch, RET_HEADS, RET_HD, RET_HD), F32)]
                  + mats,
        scratch_shapes=[pltpu.VMEM(m.shape, BF16) for m in mats] + [
                        pltpu.VMEM((STAGE_SLOTS * STAGE_ROWS, max(in_cols, d_ff, d)), F32),
                        pltpu.SemaphoreType.DMA((STAGE_SLOTS,)),
                        pltpu.SemaphoreType.DMA((len(mats),)),
                        pltpu.VMEM((tile, in_cols), F32),
                        pltpu.VMEM((tile + SUBLANES, cw), F32),
                        pltpu.VMEM((tile, cw), F32),
                        pltpu.VMEM((tile, d), BF16),
                        pltpu.VMEM((tile, d), F32),
                        pltpu.VMEM((tile, d), F32),
                        pltpu.VMEM((tile, d), BF16),
                        pltpu.VMEM((tile, d), BF16),
                        pltpu.VMEM((tile, d_ff), BF16),
                        pltpu.VMEM((RET_HEADS, RET_HD, RET_HD), F32),
                        pltpu.VMEM((RET_HEADS, RET_CHUNK, RET_CHUNK), F32),
                        pltpu.VMEM((RET_HEADS, RET_CHUNK, RET_HD), F32),
                        pltpu.VMEM((RET_HEADS, RET_CHUNK, RET_HD), F32)],
        compiler_params=pltpu.CompilerParams(dimension_semantics=("arbitrary",),
                                             vmem_limit_bytes=VMEM_LIMIT_BYTES),
        name="prompt_layer",
    )(x.reshape(batch * seq, d), cs, sn, *ws)
    y, cnew, ret, win_bf, wout_bf, wgu_bf, wd_bf = outs
    ws_bf16 = (ws[0], win_bf, ws[2], ws[3], wout_bf, ws[5], wgu_bf, wd_bf, ws[9])
    return y.reshape(batch, seq, d), cnew, ret, ws_bf16


def _sample_call(x, state_conv, state_ret, ws):
    nseq_all, seqlen, d = x.shape
    nseq = SAMPLE_SEQS
    tile = nseq * seqlen
    n_tiles = nseq_all // nseq
    cs, sn = _rope_tables(np.tile(PAST_LEN + np.arange(seqlen), nseq))
    in_cols = ws[1].shape[1]
    cw = ws[2].shape[-1]
    d_ff = ws[7].shape[0]
    matrix_slots = (1, 4, 6, 7)
    mixer_tile = lambda i: jnp.minimum(i, n_tiles - 1)
    dense_tile = lambda i: jnp.maximum(i - 1, 0)
    weight_specs = [pl.BlockSpec(memory_space=pl.ANY) if m in matrix_slots else _resident(wt.shape)
                    for m, wt in enumerate(ws)]
    y, cnew, rnew = pl.pallas_call(
        functools.partial(_sample_kernel, n_tiles=n_tiles),
        grid=(n_tiles + 1,),
        in_specs=[pl.BlockSpec((tile, d), lambda i: (mixer_tile(i), 0)),
                  _resident((tile, RET_HD)), _resident((tile, RET_HD)),
                  pl.BlockSpec((nseq, CONV_K - 1, cw), lambda i: (mixer_tile(i), 0, 0)),
                  pl.BlockSpec(memory_space=pl.ANY)]
                 + weight_specs,
        out_specs=[pl.BlockSpec((tile, d), lambda i: (dense_tile(i), 0)),
                   pl.BlockSpec((nseq, CONV_K - 1, cw), lambda i: (mixer_tile(i), 0, 0)),
                   pl.BlockSpec((nseq, RET_HEADS, RET_HD, RET_HD), lambda i: (mixer_tile(i), 0, 0, 0))],
        out_shape=[jax.ShapeDtypeStruct((nseq_all * seqlen, d), F32),
                   jax.ShapeDtypeStruct((nseq_all, CONV_K - 1, cw), F32),
                   jax.ShapeDtypeStruct((nseq_all, RET_HEADS, RET_HD, RET_HD), F32)],
        scratch_shapes=[pltpu.VMEM(ws[m].shape, BF16) for m in matrix_slots] + [
                        pltpu.SemaphoreType.DMA((len(matrix_slots),)),
                        pltpu.VMEM((STATE_SLOTS, nseq, RET_HEADS, RET_HD, RET_HD), F32),
                        pltpu.SemaphoreType.DMA((STATE_SLOTS,)),
                        pltpu.VMEM((tile, in_cols), F32),
                        pltpu.VMEM((tile, cw), F32),
                        pltpu.VMEM((tile, d), BF16),
                        pltpu.VMEM((tile, d), F32),
                        pltpu.VMEM((tile, d), F32),
                        pltpu.VMEM((tile, d), BF16),
                        pltpu.VMEM((tile, d), BF16),
                        pltpu.VMEM((tile, d_ff), BF16),
                        pltpu.VMEM((RET_HEADS, tile, tile), F32),
                        pltpu.VMEM((RET_HEADS, tile, RET_HD), F32),
                        pltpu.VMEM((RET_HEADS, tile, RET_HD), F32)],
        compiler_params=pltpu.CompilerParams(dimension_semantics=("arbitrary",),
                                             vmem_limit_bytes=VMEM_LIMIT_BYTES),
        name="sample_layer",
    )(x.reshape(nseq_all * seqlen, d), cs, sn, state_conv, state_ret, *ws)
    return y.reshape(nseq_all, seqlen, d), cnew, rnew


def kernel(x_prompt, x_sample, state_conv, state_ret, norm1_g, w_in, conv_w, ret_gn_g, w_out, norm2_g,
           w_gate, w_up, w_down, norm_f_g):
    depth = w_in.shape[0]
    assert depth == 1, "the fused layer kernels take a single layer"
    row = lambda g: g.reshape(1, -1)
    ws = (row(norm1_g[0]), w_in[0], conv_w, row(ret_gn_g[0]), w_out[0],
          row(norm2_g[0]), w_gate[0], w_up[0], w_down[0], row(norm_f_g))
    y_p, cnew_p, ret_p, ws_bf16 = _prompt_call(x_prompt, ws)
    y_s, cnew_s, ret_s = _sample_call(x_sample, state_conv[0], state_ret[0], ws_bf16)
    return (y_p, y_s, cnew_p[None], ret_p[None], cnew_s[None], ret_s[None])
```

```python
import functools
import math

import numpy as np
import jax
import jax.numpy as jnp
from jax import lax
from jax.experimental import pallas as pl
from jax.experimental.pallas import tpu as pltpu

F32 = jnp.float32
BF16 = jnp.bfloat16

RET_HEADS = 4
RET_HD = 128
RET_WIDTH = RET_HEADS * RET_HD
CONV_K = 3
RET_CHUNK = 128
ROPE_BASE = 10000.0
NORM_EPS = 1e-6
GN_EPS = 1e-5
PAST_LEN = 16384

LOG_G = tuple(math.log1p(-(2.0 ** (-5.0 - h))) for h in range(RET_HEADS))

SUBLANES = 8
MXU_N = 256
MATRIX_SLOTS = (1, 4, 6, 7, 8)
STAGE_ROWS = 128
STAGE_SLOTS = 8
PROMPT_TILE = 256
SAMPLE_SEQS = 32
DECODE_HALF = 128
VMEM_LIMIT_BYTES = 60 * 1024 * 1024


def _rope_tables(pos):
    half = RET_HD // 2
    inv = ROPE_BASE ** (-np.arange(half, dtype=np.float64) / half)
    ang = np.asarray(pos, np.float64)[:, None] * inv[None, :]
    cos, sin = np.cos(ang), np.sin(ang)
    cs = np.concatenate([cos, cos], axis=1).astype(np.float32)
    sn = np.concatenate([-sin, sin], axis=1).astype(np.float32)
    return jnp.asarray(cs), jnp.asarray(sn)


def _rmsnorm(x, g):
    ms = jnp.mean(x * x, axis=-1, keepdims=True)
    return x * lax.rsqrt(ms + NORM_EPS) * g


def _silu(x):
    return x * (1.0 / (1.0 + jnp.exp(-x)))


def _rotary(t, cs, sn):
    return t * cs + pltpu.roll(t, RET_HD // 2, axis=1) * sn


def _dot(a, b):
    return jnp.dot(a, b, preferred_element_type=F32)


def _dot_nt(a, b):
    return lax.dot_general(a, b, (((1,), (1,)), ((), ())), preferred_element_type=F32)


def _dot_tn(a, b):
    return lax.dot_general(a, b, (((0,), (0,)), ((), ())), preferred_element_type=F32)


def _groupnorm_gate(o, g, gn_g):
    mu = jnp.mean(o, axis=-1, keepdims=True)
    d = o - mu
    var = jnp.mean(d * d, axis=-1, keepdims=True)
    return _silu(g) * (d * lax.rsqrt(var + GN_EPS) * gn_g)


def _swiglu_piece(h_scr, wgu_ref, act_scr, c0):
    r = _dot(h_scr[...], wgu_ref[:, 2 * c0:2 * c0 + 2 * MXU_N])
    act_scr[:, c0:c0 + MXU_N] = (_silu(r[:, 0:MXU_N]) * r[:, MXU_N:2 * MXU_N]).astype(BF16)


def _load_weights_bf16(pairs, stage, sem, between):
    pause_after = pairs[0][0].shape[0] // STAGE_ROWS
    rb = STAGE_ROWS
    slots, width = stage.shape[0] // rb, stage.shape[1]
    assert slots >= 2 and slots <= sem.shape[0]
    chunks = []
    for w_hbm, w_bf, lane in pairs:
        rows, cols = w_hbm.shape
        assert rows % rb == 0 and cols <= width and (lane is None or cols % MXU_N == 0)
        chunks += [(w_hbm, w_bf, r0, cols, lane) for r0 in range(0, rows, rb)]

    def slot_view(n):
        s0 = (n % slots) * rb
        return stage.at[s0:s0 + rb, 0:chunks[n][3]]

    def chunk_copy(n):
        w_hbm, r0 = chunks[n][0], chunks[n][2]
        return pltpu.make_async_copy(w_hbm.at[r0:r0 + rb, :], slot_view(n), sem.at[n % slots])

    for n in range(min(slots - 1, len(chunks))):
        chunk_copy(n).start(priority=n % 2)
    for n, (_, w_bf, r0, cols, lane) in enumerate(chunks):
        chunk_copy(n).wait()
        if lane is None:
            w_bf[r0:r0 + rb, :] = slot_view(n)[...].astype(BF16)
        else:
            for c0 in range(0, cols, MXU_N):
                d0 = 2 * c0 + lane * MXU_N
                w_bf[r0:r0 + rb, d0:d0 + MXU_N] = slot_view(n)[:, c0:c0 + MXU_N].astype(BF16)
        if n + slots - 1 < len(chunks):
            chunk_copy(n + slots - 1).start(priority=(n + slots - 1) % 2)
        if n + 1 == pause_after:
            between()


def _prompt_kernel(x_ref, cs_ref, sn_ref, g1_ref, win_hbm, convw_ref, gng_ref, wout_hbm, g2_ref,
                   wg_hbm, wu_hbm, wd_hbm, gf_ref,
                   y_ref, cnew_ref, ret_ref, win_out, wout_out, wgu_out, wd_out,
                   win_ref, wout_ref, wgu_ref, wd_ref, stage_scr, load_sem, store_sem,
                   proj_scr, uext_scr, convy_scr, mix_scr, xkeep_scr, x1_scr, h_scr, h2_scr, act_scr, r_scr,
                   dec_scr, xi_scr, zeta_scr, *, n_tiles, tiles_per_seq):
    tile = x_ref.shape[0]
    i = pl.program_id(0)
    chunk = RET_CHUNK
    t = jnp.minimum(i, n_tiles - 1) % tiles_per_seq
    seq_start = t == 0
    seq_end = t == tiles_per_seq - 1
    w = RET_WIDTH
    cw = w
    d_ff = wd_ref.shape[0]
    weight_stores = [pltpu.make_async_copy(src, dst, store_sem.at[n]) for n, (src, dst) in enumerate(
        ((win_ref, win_out), (wout_ref, wout_out), (wgu_ref, wgu_out), (wd_ref, wd_out)))]

    def setup():
        ii = lax.broadcasted_iota(jnp.int32, (chunk, chunk), 0)
        jj = lax.broadcasted_iota(jnp.int32, (chunk, chunk), 1)
        causal = ii >= jj
        diff = jnp.where(causal, ii - jj, 0).astype(F32)
        i_f = ii.astype(F32)
        for hh in range(RET_HEADS):
            dec_scr[hh] = jnp.where(causal, jnp.exp(diff * LOG_G[hh]), 0.0)
            xi_scr[hh] = jnp.exp((i_f + 1.0) * LOG_G[hh])
            zeta_scr[hh] = jnp.exp((chunk - 1.0 - i_f) * LOG_G[hh])
        r_scr[...] = jnp.zeros_like(r_scr)
        uext_scr[0:SUBLANES, :] = jnp.zeros((SUBLANES, uext_scr.shape[1]), F32)

    def ffn_piece(c0):
        return functools.partial(_swiglu_piece, h2_scr, wgu_ref, act_scr, c0)

    def proj_piece(c0):
        def emit():
            proj_scr[:, c0:c0 + cw] = _dot(h_scr[...], win_ref[:, c0:c0 + cw])
        return emit

    def step(mixer, dense):
        fillers = []

        def fill(n):
            for _ in range(min(n, len(fillers))):
                fillers.pop(0)()

        if dense:
            x1_scr[...] = xkeep_scr[...] + _dot(mix_scr[...], wout_ref[...])
        if mixer:
            x = x_ref[...]
            h_scr[...] = _rmsnorm(x, g1_ref[...]).astype(BF16)
            proj_scr[:, 0:2 * w] = _dot(h_scr[...], win_ref[:, 0:2 * w])
        if dense:
            h2_scr[...] = _rmsnorm(x1_scr[...], g2_ref[...]).astype(BF16)
            fillers += [ffn_piece(c0) for c0 in range(0, d_ff, MXU_N)]
        if not mixer:
            fill(len(fillers))
        else:
            proj_scr[:, 2 * w:4 * w] = _dot(h_scr[...], win_ref[:, 2 * w:4 * w])
            fillers[0:0] = [proj_piece(5 * w), proj_piece(6 * w)]

            n_chunks = tile // chunk
            units = [(c, hh) for c in range(n_chunks) for hh in range(RET_HEADS)]
            qbs, vbs, scores, updates = {}, {}, {}, {}
            for c, hh in units:
                r0, c0 = c * chunk, hh * RET_HD
                cs = cs_ref[r0:r0 + chunk, :]
                sn = sn_ref[r0:r0 + chunk, :]
                q = _rotary(proj_scr[r0:r0 + chunk, c0:c0 + RET_HD], cs, sn)
                k = _rotary(proj_scr[r0:r0 + chunk, w + c0:w + c0 + RET_HD], cs, sn) * (RET_HD ** -0.5)
                qb, kb = q.astype(BF16), k.astype(BF16)
                vb = proj_scr[r0:r0 + chunk, 2 * w + c0:2 * w + c0 + RET_HD].astype(BF16)
                kz = (k * zeta_scr[hh]).astype(BF16)
                qbs[c, hh], vbs[c, hh] = qb, vb
                scores[c, hh] = _dot_nt(qb, kb)
                updates[c, hh] = _dot_tn(kz, vb)
            fill(2)

            u = proj_scr[:, 5 * w:5 * w + cw] * proj_scr[:, 5 * w + cw:5 * w + 2 * cw]
            uext_scr[SUBLANES:SUBLANES + tile, :] = u
            um1 = uext_scr[SUBLANES - 1:SUBLANES - 1 + tile, :]
            um2 = uext_scr[SUBLANES - 2:SUBLANES - 2 + tile, :]
            convy_scr[...] = (convw_ref[0, 0:1, :] * um2 + convw_ref[0, 1:2, :] * um1
                              + convw_ref[0, 2:3, :] * u)
            tail = uext_scr[tile + SUBLANES - (CONV_K - 1):tile + SUBLANES, :]
            cnew_ref[0] = tail
            uext_scr[SUBLANES - (CONV_K - 1):SUBLANES, :] = jnp.where(seq_end, 0.0, tail)
            fill(len(fillers) - len(units) + 1)

            outs = {}
            for hh in range(RET_HEADS):
                state = jnp.where(seq_start, 0.0, r_scr[hh])
                for c in range(n_chunks):
                    s = (scores[c, hh] * dec_scr[hh]).astype(BF16)
                    outs[c, hh] = _dot(s, vbs[c, hh]) + _dot(qbs[c, hh], state.astype(BF16)) * xi_scr[hh]
                    state = state * math.exp(chunk * LOG_G[hh]) + updates[c, hh]
                r_scr[hh] = state
                ret_ref[0, hh] = state
            for c, hh in units:
                fill(1)
                r0, c0 = c * chunk, hh * RET_HD
                g = proj_scr[r0:r0 + chunk, 3 * w + c0:3 * w + c0 + RET_HD]
                mix_scr[r0:r0 + chunk, c0:c0 + RET_HD] = _groupnorm_gate(
                    outs[c, hh], g, gng_ref[:, c0:c0 + RET_HD]).astype(BF16)
            fill(len(fillers))

        if dense:
            down = _dot(act_scr[...], wd_ref[...])
        if mixer:
            proj_piece(4 * w)()
        if dense:
            y_ref[...] = _rmsnorm(x1_scr[...] + down, gf_ref[...])
        if mixer:
            mix_scr[:, w:w + cw] = (proj_scr[:, 4 * w:4 * w + cw] * convy_scr[...]).astype(BF16)
            xkeep_scr[...] = x

    @pl.when(i == 0)
    def _():
        setup()
        _load_weights_bf16(((win_hbm, win_ref, None), (wout_hbm, wout_ref, None), (wg_hbm, wgu_ref, 0),
                            (wu_hbm, wgu_ref, 1), (wd_hbm, wd_ref, None)), stage_scr, load_sem,
                           between=functools.partial(step, mixer=True, dense=False))
        for cp in weight_stores:
            cp.start()

    @pl.when((i > 0) & (i < n_tiles))
    def _():
        step(mixer=True, dense=True)

    @pl.when(i == n_tiles)
    def _():
        step(mixer=False, dense=True)
        for cp in weight_stores:
            cp.wait()


def _sample_kernel(x_ref, cs_ref, sn_ref, cst_ref, sret_hbm, g1_ref, win_hbm, convw_ref, gng_ref,
                   wout_hbm, g2_ref, wgu_hbm, wd_hbm, gf_ref,
                   y_ref, cnew_ref, retnew_hbm,
                   win_ref, wout_ref, wgu_ref, wd_ref, load_sem, sin_scr, sout_scr, in_sem, out_sem,
                   proj_scr, convy_scr, mix_scr, xkeep_scr, x1_scr, h_scr, h2_scr, act_scr,
                   mask_scr, xi_scr, zeta_scr, *, n_tiles):
    tile = x_ref.shape[0]
    nseq = sin_scr.shape[1]
    seqlen = tile // nseq
    half = DECODE_HALF
    assert seqlen == SUBLANES, "decode sequences must fill exactly one f32 sublane tile"
    assert tile % half == 0 and RET_HEADS % 2 == 0
    i = pl.program_id(0)
    w = RET_WIDTH
    cw = w
    d_ff = wd_ref.shape[0]
    weight_loads = [pltpu.make_async_copy(src, dst, load_sem.at[n]) for n, (src, dst) in enumerate(
        ((win_hbm, win_ref), (wout_hbm, wout_ref), (wgu_hbm, wgu_ref), (wd_hbm, wd_ref)))]

    def state_fetch(t, hh):
        return pltpu.make_async_copy(sret_hbm.at[pl.ds(t * nseq, nseq), hh], sin_scr.at[hh % 2],
                                     in_sem.at[hh % 2])

    def state_store(t, hh):
        return pltpu.make_async_copy(sout_scr.at[hh % 2], retnew_hbm.at[pl.ds(t * nseq, nseq), hh],
                                     out_sem.at[hh % 2])

    def setup():
        state_fetch(0, 0).start()
        for cp in weight_loads:
            cp.start()
        ii = lax.broadcasted_iota(jnp.int32, (half, half), 0)
        jj = lax.broadcasted_iota(jnp.int32, (half, half), 1)
        keep = (ii >= jj) & ((ii // seqlen) == (jj // seqlen))
        diff = jnp.where(keep, ii - jj, 0).astype(F32)
        pos = (lax.broadcasted_iota(jnp.int32, (half, RET_HD), 0) % seqlen).astype(F32)
        for hh in range(RET_HEADS):
            mask_scr[hh] = jnp.where(keep, jnp.exp(diff * LOG_G[hh]), 0.0)
            xi_scr[hh] = jnp.exp((pos + 1.0) * LOG_G[hh])
            zeta_scr[hh] = jnp.exp((seqlen - 1.0 - pos) * LOG_G[hh])
        weight_loads[0].wait()

    def proj_piece(c0):
        def emit():
            proj_scr[:, c0:c0 + cw] = _dot(h_scr[...], win_ref[:, c0:c0 + cw])
        return emit

    def step(mixer, dense):
        fillers = []

        def fill(n):
            for _ in range(min(n, len(fillers))):
                fillers.pop(0)()

        if dense:
            x1_scr[...] = xkeep_scr[...] + _dot(mix_scr[...], wout_ref[...])
        if mixer:
            x = x_ref[...]
            h_scr[...] = _rmsnorm(x, g1_ref[...]).astype(BF16)
            proj_scr[:, 0:4 * w] = _dot(h_scr[...], win_ref[:, 0:4 * w])
        if dense:
            h2_scr[...] = _rmsnorm(x1_scr[...], g2_ref[...]).astype(BF16)
            fillers += [functools.partial(_swiglu_piece, h2_scr, wgu_ref, act_scr, c0)
                        for c0 in range(0, d_ff, MXU_N)]
        if mixer:
            fillers[0:0] = [proj_piece(5 * w), proj_piece(6 * w)]
            pair_rows = 2 * seqlen
            half_seqs = half // seqlen
            rowblk = lax.broadcasted_iota(jnp.int32, (half, 2 * RET_HD), 0) // seqlen
            colhalf = lax.broadcasted_iota(jnp.int32, (half, 2 * RET_HD), 1) // RET_HD
            for hh in range(RET_HEADS):
                c0 = hh * RET_HD
                slot = hh % 2
                state_fetch(i, hh).wait()
                if hh + 1 < RET_HEADS:
                    state_fetch(i, hh + 1).start()
                else:
                    state_fetch(jnp.minimum(i + 1, n_tiles - 1), 0).start()
                if hh >= 2:
                    state_store(i, hh - 2).wait()
                g_chunk = math.exp(seqlen * LOG_G[hh])
                for r0 in range(0, tile, half):
                    cs = cs_ref[r0:r0 + half, :]
                    sn = sn_ref[r0:r0 + half, :]
                    q = _rotary(proj_scr[r0:r0 + half, c0:c0 + RET_HD], cs, sn)
                    k = _rotary(proj_scr[r0:r0 + half, w + c0:w + c0 + RET_HD], cs, sn) * (RET_HD ** -0.5)
                    v = proj_scr[r0:r0 + half, 2 * w + c0:2 * w + c0 + RET_HD]
                    g = proj_scr[r0:r0 + half, 3 * w + c0:3 * w + c0 + RET_HD]
                    qb, kb, vb = q.astype(BF16), k.astype(BF16), v.astype(BF16)
                    s = _dot_nt(qb, kb) * mask_scr[hh]
                    fill(1)
                    o_intra = _dot(s.astype(BF16), vb)
                    kzt = (k * zeta_scr[hh]).T.astype(BF16)
                    vv = jnp.concatenate([v, v], axis=1)
                    cross = []
                    for p in range(half_seqs // 2):
                        sa = r0 // seqlen + 2 * p
                        sb = sa + 1
                        ra = sin_scr[slot, sa]
                        rb = sin_scr[slot, sb]
                        rcat = jnp.concatenate([ra, rb], axis=1).astype(BF16)
                        pr = _dot(qb[p * pair_rows:(p + 1) * pair_rows, :], rcat)
                        cross.append(pr[0:seqlen, 0:RET_HD])
                        cross.append(pr[seqlen:pair_rows, RET_HD:2 * RET_HD])
                        vpair = jnp.where(rowblk == 2 * p + colhalf, vv, 0.0).astype(BF16)
                        upd = _dot(kzt, vpair)
                        sout_scr[slot, sa] = ra * g_chunk + upd[:, 0:RET_HD]
                        sout_scr[slot, sb] = rb * g_chunk + upd[:, RET_HD:2 * RET_HD]
                    fill(1)
                    o = o_intra + jnp.concatenate(cross, axis=0) * xi_scr[hh]
                    mix_scr[r0:r0 + half, c0:c0 + RET_HD] = _groupnorm_gate(
                        o, g, gng_ref[:, c0:c0 + RET_HD]).astype(BF16)
                state_store(i, hh).start()

            u = proj_scr[:, 5 * w:5 * w + cw] * proj_scr[:, 5 * w + cw:5 * w + 2 * cw]
            cst = cst_ref[...]
            older = jnp.broadcast_to(cst[:, 0:1, :], (nseq, seqlen, cw)).reshape(tile, cw)
            newer = jnp.broadcast_to(cst[:, 1:2, :], (nseq, seqlen, cw)).reshape(tile, cw)
            r8 = lax.broadcasted_iota(jnp.int32, (tile, cw), 0) % seqlen
            um1 = jnp.where(r8 == 0, newer, pltpu.roll(u, 1, axis=0))
            um2 = jnp.where(r8 == 0, older, jnp.where(r8 == 1, newer, pltpu.roll(u, 2, axis=0)))
            convy_scr[...] = (convw_ref[0, 0:1, :] * um2 + convw_ref[0, 1:2, :] * um1
                              + convw_ref[0, 2:3, :] * u)
            cnew_ref[...] = u.reshape(nseq, seqlen, cw)[:, seqlen - (CONV_K - 1):, :]
        fill(len(fillers))

        if dense:
            down = _dot(act_scr[...], wd_ref[...])
        if mixer:
            proj_piece(4 * w)()
        if dense:
            y_ref[...] = _rmsnorm(x1_scr[...] + down, gf_ref[...])
        if mixer:
            mix_scr[:, w:w + cw] = (proj_scr[:, 4 * w:4 * w + cw] * convy_scr[...]).astype(BF16)
            xkeep_scr[...] = x
            for hh in range(RET_HEADS - 2, RET_HEADS):
                state_store(i, hh).wait()

    @pl.when(i == 0)
    def _():
        setup()
        step(mixer=True, dense=False)
        for cp in weight_loads[1:]:
            cp.wait()

    @pl.when((i > 0) & (i < n_tiles))
    def _():
        step(mixer=True, dense=True)

    @pl.when(i == n_tiles)
    def _():
        state_fetch(n_tiles - 1, 0).wait()
        step(mixer=False, dense=True)


def _resident(shape):
    nd = len(shape)
    return pl.BlockSpec(shape, lambda *_: (0,) * nd, pipeline_mode=pl.Buffered(1))


def _prompt_call(x, ws):
    batch, seq, d = x.shape
    tile = PROMPT_TILE
    tiles_per_seq = seq // tile
    n_tiles = batch * tiles_per_seq
    cs, sn = _rope_tables(np.arange(seq))
    in_cols = ws[1].shape[1]
    cw = ws[2].shape[-1]
    mixer_tile = lambda i: jnp.minimum(i, n_tiles - 1)
    dense_tile = lambda i: jnp.maximum(i - 1, 0)
    w_in, w_out, w_gate, w_up, w_down = [ws[m] for m in MATRIX_SLOTS]
    assert w_gate.shape == w_up.shape and w_gate.shape[1] == w_down.shape[0]
    d_ff = w_down.shape[0]
    mats = [jax.ShapeDtypeStruct(s, BF16) for s in (w_in.shape, w_out.shape, (d, 2 * d_ff), w_down.shape)]
    weight_specs = [pl.BlockSpec(memory_space=pl.ANY) if m in MATRIX_SLOTS else _resident(wt.shape)
                    for m, wt in enumerate(ws)]
    outs = pl.pallas_call(
        functools.partial(_prompt_kernel, n_tiles=n_tiles, tiles_per_seq=tiles_per_seq),
        grid=(n_tiles + 1,),
        in_specs=[pl.BlockSpec((tile, d), lambda i: (mixer_tile(i), 0)),
                  pl.BlockSpec((tile, RET_HD), lambda i: (mixer_tile(i) % tiles_per_seq, 0)),
                  pl.BlockSpec((tile, RET_HD), lambda i: (mixer_tile(i) % tiles_per_seq, 0))] + weight_specs,
        out_specs=[pl.BlockSpec((tile, d), lambda i: (dense_tile(i), 0)),
                   pl.BlockSpec((1, CONV_K - 1, cw), lambda i: (mixer_tile(i) // tiles_per_seq, 0, 0)),
                   pl.BlockSpec((1, RET_HEADS, RET_HD, RET_HD),
                                lambda i: (mixer_tile(i) // tiles_per_seq, 0, 0, 0))]
                  + [pl.BlockSpec(memory_space=pl.ANY)] * len(mats),
        out_shape=[jax.ShapeDtypeStruct((batch * seq, d), F32),
                   jax.ShapeDtypeStruct((batch, CONV_K - 1, cw), F32),
                   jax.ShapeDtypeStruct((batch, RET_HEADS, RET_HD, RET_HD), F32)]
                  + mats,
        scratch_shapes=[pltpu.VMEM(m.shape, BF16) for m in mats] + [
                        pltpu.VMEM((STAGE_SLOTS * STAGE_ROWS, max(in_cols, d_ff, d)), F32),
                        pltpu.SemaphoreType.DMA((STAGE_SLOTS,)),
                        pltpu.SemaphoreType.DMA((len(mats),)),
                        pltpu.VMEM((tile, in_cols), F32),
                        pltpu.VMEM((tile + SUBLANES, cw), F32),
                        pltpu.VMEM((tile, cw), F32),
                        pltpu.VMEM((tile, d), BF16),
                        pltpu.VMEM((tile, d), F32),
                        pltpu.VMEM((tile, d), F32),
                        pltpu.VMEM((tile, d), BF16),
                        pltpu.VMEM((tile, d), BF16),
                        pltpu.VMEM((tile, d_ff), BF16),
                        pltpu.VMEM((RET_HEADS, RET_HD, RET_HD), F32),
                        pltpu.VMEM((RET_HEADS, RET_CHUNK, RET_CHUNK), F32),
                        pltpu.VMEM((RET_HEADS, RET_CHUNK, RET_HD), F32),
                        pltpu.VMEM((RET_HEADS, RET_CHUNK, RET_HD), F32)],
        compiler_params=pltpu.CompilerParams(dimension_semantics=("arbitrary",),
                                             vmem_limit_bytes=VMEM_LIMIT_BYTES),
        name="prompt_layer",
    )(x.reshape(batch * seq, d), cs, sn, *ws)
    y, cnew, ret, win_bf, wout_bf, wgu_bf, wd_bf = outs
    ws_bf16 = (ws[0], win_bf, ws[2], ws[3], wout_bf, ws[5], wgu_bf, wd_bf, ws[9])
    return y.reshape(batch, seq, d), cnew, ret, ws_bf16


def _sample_call(x, state_conv, state_ret, ws):
    nseq_all, seqlen, d = x.shape
    nseq = SAMPLE_SEQS
    tile = nseq * seqlen
    n_tiles = nseq_all // nseq
    cs, sn = _rope_tables(np.tile(PAST_LEN + np.arange(seqlen), nseq))
    in_cols = ws[1].shape[1]
    cw = ws[2].shape[-1]
    d_ff = ws[7].shape[0]
    matrix_slots = (1, 4, 6, 7)
    mixer_tile = lambda i: jnp.minimum(i, n_tiles - 1)
    dense_tile = lambda i: jnp.maximum(i - 1, 0)
    weight_specs = [pl.BlockSpec(memory_space=pl.ANY) if m in matrix_slots else _resident(wt.shape)
                    for m, wt in enumerate(ws)]
    head_state = (nseq, RET_HD, RET_HD)
    y, cnew, rnew = pl.pallas_call(
        functools.partial(_sample_kernel, n_tiles=n_tiles),
        grid=(n_tiles + 1,),
        in_specs=[pl.BlockSpec((tile, d), lambda i: (mixer_tile(i), 0)),
                  _resident((tile, RET_HD)), _resident((tile, RET_HD)),
                  pl.BlockSpec((nseq, CONV_K - 1, cw), lambda i: (mixer_tile(i), 0, 0)),
                  pl.BlockSpec(memory_space=pl.ANY)]
                 + weight_specs,
        out_specs=[pl.BlockSpec((tile, d), lambda i: (dense_tile(i), 0)),
                   pl.BlockSpec((nseq, CONV_K - 1, cw), lambda i: (mixer_tile(i), 0, 0)),
                   pl.BlockSpec(memory_space=pl.ANY)],
        out_shape=[jax.ShapeDtypeStruct((nseq_all * seqlen, d), F32),
                   jax.ShapeDtypeStruct((nseq_all, CONV_K - 1, cw), F32),
                   jax.ShapeDtypeStruct((nseq_all, RET_HEADS, RET_HD, RET_HD), F32)],
        scratch_shapes=[pltpu.VMEM(ws[m].shape, BF16) for m in matrix_slots] + [
                        pltpu.SemaphoreType.DMA((len(matrix_slots),)),
                        pltpu.VMEM((2,) + head_state, F32),
                        pltpu.VMEM((2,) + head_state, F32),
                        pltpu.SemaphoreType.DMA((2,)),
                        pltpu.SemaphoreType.DMA((2,)),
                        pltpu.VMEM((tile, in_cols), F32),
                        pltpu.VMEM((tile, cw), F32),
                        pltpu.VMEM((tile, d), BF16),
                        pltpu.VMEM((tile, d), F32),
                        pltpu.VMEM((tile, d), F32),
                        pltpu.VMEM((tile, d), BF16),
                        pltpu.VMEM((tile, d), BF16),
                        pltpu.VMEM((tile, d_ff), BF16),
                        pltpu.VMEM((RET_HEADS, DECODE_HALF, DECODE_HALF), F32),
                        pltpu.VMEM((RET_HEADS, DECODE_HALF, RET_HD), F32),
                        pltpu.VMEM((RET_HEADS, DECODE_HALF, RET_HD), F32)],
        compiler_params=pltpu.CompilerParams(dimension_semantics=("arbitrary",),
                                             vmem_limit_bytes=VMEM_LIMIT_BYTES),
        name="sample_layer",
    )(x.reshape(nseq_all * seqlen, d), cs, sn, state_conv, state_ret, *ws)
    return y.reshape(nseq_all, seqlen, d), cnew, rnew


def kernel(x_prompt, x_sample, state_conv, state_ret, norm1_g, w_in, conv_w, ret_gn_g, w_out, norm2_g,
           w_gate, w_up, w_down, norm_f_g):
    depth = w_in.shape[0]
    assert depth == 1, "the fused layer kernels take a single layer"
    row = lambda g: g.reshape(1, -1)
    ws = (row(norm1_g[0]), w_in[0], conv_w, row(ret_gn_g[0]), w_out[0],
          row(norm2_g[0]), w_gate[0], w_up[0], w_down[0], row(norm_f_g))
    y_p, cnew_p, ret_p, ws_bf16 = _prompt_call(x_prompt, ws)
    y_s, cnew_s, ret_s = _sample_call(x_sample, state_conv[0], state_ret[0], ws_bf16)
    return (y_p, y_s, cnew_p[None], ret_p[None], cnew_s[None], ret_s[None])
```

```python
import functools
import math

import numpy as np
import jax
import jax.numpy as jnp
from jax import lax
from jax.experimental import pallas as pl
from jax.experimental.pallas import tpu as pltpu

F32 = jnp.float32
BF16 = jnp.bfloat16

RET_HEADS = 4
RET_HD = 128
RET_WIDTH = RET_HEADS * RET_HD
CONV_K = 3
ROPE_BASE = 10000.0
NORM_EPS = 1e-6
GN_EPS = 1e-5
PAST_LEN = 16384

LOG_G = tuple(math.log1p(-(2.0 ** (-5.0 - h))) for h in range(RET_HEADS))

SUBLANES = 8
MXU_N = 256
MATRIX_SLOTS = (1, 4, 6, 7, 8)
STAGE_ROWS = 128
STAGE_SLOTS = 8
PROMPT_TILE = 256
PROMPT_CHUNK = 256
SAMPLE_SEQS = 16
VMEM_LIMIT_BYTES = 60 * 1024 * 1024


def _rope_tables(pos):
    half = RET_HD // 2
    inv = ROPE_BASE ** (-np.arange(half, dtype=np.float64) / half)
    ang = np.asarray(pos, np.float64)[:, None] * inv[None, :]
    cos, sin = np.cos(ang), np.sin(ang)
    cs = np.concatenate([cos, cos], axis=1).astype(np.float32)
    sn = np.concatenate([-sin, sin], axis=1).astype(np.float32)
    return jnp.asarray(cs), jnp.asarray(sn)


def _rmsnorm(x, g):
    ms = jnp.mean(x * x, axis=-1, keepdims=True)
    return x * lax.rsqrt(ms + NORM_EPS) * g


def _silu(x):
    return x * (1.0 / (1.0 + jnp.exp(-x)))


def _rotary(t, cs, sn):
    return t * cs + pltpu.roll(t, RET_HD // 2, axis=1) * sn


def _dot(a, b):
    return jnp.dot(a, b, preferred_element_type=F32)


def _dot_nt(a, b):
    return lax.dot_general(a, b, (((1,), (1,)), ((), ())), preferred_element_type=F32)


def _dot_tn(a, b):
    return lax.dot_general(a, b, (((0,), (0,)), ((), ())), preferred_element_type=F32)


def _groupnorm_gate(o, g, gn_g):
    mu = jnp.mean(o, axis=-1, keepdims=True)
    d = o - mu
    var = jnp.mean(d * d, axis=-1, keepdims=True)
    return _silu(g) * (d * lax.rsqrt(var + GN_EPS) * gn_g)


def _swiglu_piece(h_scr, wgu_ref, act_scr, c0):
    r = _dot(h_scr[...], wgu_ref[:, 2 * c0:2 * c0 + 2 * MXU_N])
    act_scr[:, c0:c0 + MXU_N] = (_silu(r[:, 0:MXU_N]) * r[:, MXU_N:2 * MXU_N]).astype(BF16)


def _load_weights_bf16(pairs, stage, sem):
    rb = STAGE_ROWS
    slots, width = stage.shape[0] // rb, stage.shape[1]
    assert slots >= 2 and slots <= sem.shape[0]
    chunks = []
    for w_hbm, w_bf, lane in pairs:
        rows, cols = w_hbm.shape
        assert rows % rb == 0 and cols <= width and (lane is None or cols % MXU_N == 0)
        chunks += [(w_hbm, w_bf, r0, cols, lane) for r0 in range(0, rows, rb)]

    def slot_view(n):
        s0 = (n % slots) * rb
        return stage.at[s0:s0 + rb, 0:chunks[n][3]]

    def chunk_copy(n):
        w_hbm, r0 = chunks[n][0], chunks[n][2]
        return pltpu.make_async_copy(w_hbm.at[r0:r0 + rb, :], slot_view(n), sem.at[n % slots])

    for n in range(min(slots - 1, len(chunks))):
        chunk_copy(n).start(priority=n % 2)
    for n, (_, w_bf, r0, cols, lane) in enumerate(chunks):
        chunk_copy(n).wait()
        if lane is None:
            w_bf[r0:r0 + rb, :] = slot_view(n)[...].astype(BF16)
        else:
            for c0 in range(0, cols, MXU_N):
                d0 = 2 * c0 + lane * MXU_N
                w_bf[r0:r0 + rb, d0:d0 + MXU_N] = slot_view(n)[:, c0:c0 + MXU_N].astype(BF16)
        if n + slots - 1 < len(chunks):
            chunk_copy(n + slots - 1).start(priority=(n + slots - 1) % 2)


def _prompt_kernel(x_ref, cs_ref, sn_ref, g1_ref, win_hbm, convw_ref, gng_ref, wout_hbm, g2_ref,
                   wg_hbm, wu_hbm, wd_hbm, gf_ref,
                   y_ref, cnew_ref, ret_ref, win_out, wout_out, wgu_out, wd_out,
                   win_ref, wout_ref, wgu_ref, wd_ref, stage_scr, load_sem, store_sem,
                   proj_scr, uext_scr, convy_scr, mix_scr, xkeep_scr, x1_scr, h_scr, h2_scr, act_scr, r_scr,
                   dec_scr, xi_scr, zeta_scr, *, n_tiles, tiles_per_seq):
    tile = x_ref.shape[0]
    i = pl.program_id(0)
    chunk = PROMPT_CHUNK
    assert tile % chunk == 0
    t = jnp.minimum(i, n_tiles - 1) % tiles_per_seq
    seq_start = t == 0
    seq_end = t == tiles_per_seq - 1
    w = RET_WIDTH
    cw = w
    d_ff = wd_ref.shape[0]
    weight_stores = [pltpu.make_async_copy(src, dst, store_sem.at[n]) for n, (src, dst) in enumerate(
        ((win_ref, win_out), (wout_ref, wout_out), (wgu_ref, wgu_out), (wd_ref, wd_out)))]

    def setup():
        _load_weights_bf16(((win_hbm, win_ref, None), (wout_hbm, wout_ref, None), (wg_hbm, wgu_ref, 0),
                            (wu_hbm, wgu_ref, 1), (wd_hbm, wd_ref, None)), stage_scr, load_sem)
        for cp in weight_stores:
            cp.start()
        ii = lax.broadcasted_iota(jnp.int32, (chunk, chunk), 0)
        jj = lax.broadcasted_iota(jnp.int32, (chunk, chunk), 1)
        causal = ii >= jj
        diff = jnp.where(causal, ii - jj, 0).astype(F32)
        i_f = lax.broadcasted_iota(jnp.int32, (chunk, RET_HD), 0).astype(F32)
        for hh in range(RET_HEADS):
            dec_scr[hh] = jnp.where(causal, jnp.exp(diff * LOG_G[hh]), 0.0)
            xi_scr[hh] = jnp.exp((i_f + 1.0) * LOG_G[hh])
            zeta_scr[hh] = jnp.exp((chunk - 1.0 - i_f) * LOG_G[hh])
        r_scr[...] = jnp.zeros_like(r_scr)
        uext_scr[0:SUBLANES, :] = jnp.zeros((SUBLANES, uext_scr.shape[1]), F32)

    def ffn_piece(c0):
        return functools.partial(_swiglu_piece, h2_scr, wgu_ref, act_scr, c0)

    def proj_piece(c0):
        def emit():
            proj_scr[:, c0:c0 + cw] = _dot(h_scr[...], win_ref[:, c0:c0 + cw])
        return emit

    def step(mixer, dense):
        fillers = []

        def fill(n):
            for _ in range(min(n, len(fillers))):
                fillers.pop(0)()

        if dense:
            x1_scr[...] = xkeep_scr[...] + _dot(mix_scr[...], wout_ref[...])
        if mixer:
            x = x_ref[...]
            h_scr[...] = _rmsnorm(x, g1_ref[...]).astype(BF16)
            proj_scr[:, 0:2 * w] = _dot(h_scr[...], win_ref[:, 0:2 * w])
        if dense:
            h2_scr[...] = _rmsnorm(x1_scr[...], g2_ref[...]).astype(BF16)
            fillers += [ffn_piece(c0) for c0 in range(0, d_ff, MXU_N)]
        if not mixer:
            fill(len(fillers))
        else:
            proj_scr[:, 2 * w:4 * w] = _dot(h_scr[...], win_ref[:, 2 * w:4 * w])
            fillers[0:0] = [proj_piece(5 * w), proj_piece(6 * w)]

            n_chunks = tile // chunk
            units = [(c, hh) for c in range(n_chunks) for hh in range(RET_HEADS)]
            qbs, vbs, scores, updates = {}, {}, {}, {}
            for c, hh in units:
                r0, c0 = c * chunk, hh * RET_HD
                cs = cs_ref[r0:r0 + chunk, :]
                sn = sn_ref[r0:r0 + chunk, :]
                q = _rotary(proj_scr[r0:r0 + chunk, c0:c0 + RET_HD], cs, sn)
                k = _rotary(proj_scr[r0:r0 + chunk, w + c0:w + c0 + RET_HD], cs, sn) * (RET_HD ** -0.5)
                qb, kb = q.astype(BF16), k.astype(BF16)
                vb = proj_scr[r0:r0 + chunk, 2 * w + c0:2 * w + c0 + RET_HD].astype(BF16)
                kz = (k * zeta_scr[hh]).astype(BF16)
                qbs[c, hh], vbs[c, hh] = qb, vb
                scores[c, hh] = _dot_nt(qb, kb)
                updates[c, hh] = _dot_tn(kz, vb)
            fill(2)

            u = proj_scr[:, 5 * w:5 * w + cw] * proj_scr[:, 5 * w + cw:5 * w + 2 * cw]
            uext_scr[SUBLANES:SUBLANES + tile, :] = u
            um1 = uext_scr[SUBLANES - 1:SUBLANES - 1 + tile, :]
            um2 = uext_scr[SUBLANES - 2:SUBLANES - 2 + tile, :]
            convy_scr[...] = (convw_ref[0, 0:1, :] * um2 + convw_ref[0, 1:2, :] * um1
                              + convw_ref[0, 2:3, :] * u)
            tail = uext_scr[tile + SUBLANES - (CONV_K - 1):tile + SUBLANES, :]
            cnew_ref[0] = tail
            uext_scr[SUBLANES - (CONV_K - 1):SUBLANES, :] = jnp.where(seq_end, 0.0, tail)
            fill(len(fillers) - 2 * len(units) + 1)

            outs = {}
            for hh in range(RET_HEADS):
                state = jnp.where(seq_start, 0.0, r_scr[hh])
                for c in range(n_chunks):
                    s = (scores[c, hh] * dec_scr[hh]).astype(BF16)
                    outs[c, hh] = _dot(s, vbs[c, hh]) + _dot(qbs[c, hh], state.astype(BF16)) * xi_scr[hh]
                    state = state * math.exp(chunk * LOG_G[hh]) + updates[c, hh]
                r_scr[hh] = state
                ret_ref[0, hh] = state
            for c, hh in units:
                fill(2)
                r0, c0 = c * chunk, hh * RET_HD
                g = proj_scr[r0:r0 + chunk, 3 * w + c0:3 * w + c0 + RET_HD]
                mix_scr[r0:r0 + chunk, c0:c0 + RET_HD] = _groupnorm_gate(
                    outs[c, hh], g, gng_ref[:, c0:c0 + RET_HD]).astype(BF16)
            fill(len(fillers))

        if dense:
            down = _dot(act_scr[...], wd_ref[...])
        if mixer:
            proj_piece(4 * w)()
        if dense:
            y_ref[...] = _rmsnorm(x1_scr[...] + down, gf_ref[...])
        if mixer:
            mix_scr[:, w:w + cw] = (proj_scr[:, 4 * w:4 * w + cw] * convy_scr[...]).astype(BF16)
            xkeep_scr[...] = x

    @pl.when(i == 0)
    def _():
        setup()
        step(mixer=True, dense=False)

    @pl.when((i > 0) & (i < n_tiles))
    def _():
        step(mixer=True, dense=True)

    @pl.when(i == n_tiles)
    def _():
        step(mixer=False, dense=True)
        for cp in weight_stores:
            cp.wait()


def _sample_kernel(x_ref, cs_ref, sn_ref, cst_ref, sret_ref, g1_ref, win_hbm, convw_ref, gng_ref,
                   wout_hbm, g2_ref, wgu_hbm, wd_hbm, gf_ref,
                   y_ref, cnew_ref, retnew_ref,
                   win_ref, wout_ref, wgu_ref, wd_ref, load_sem,
                   proj_scr, convy_scr, mix_scr, xkeep_scr, x1_scr, h_scr, h2_scr, act_scr,
                   mask_scr, xi_scr, zeta_scr, *, n_tiles):
    tile = x_ref.shape[0]
    nseq = sret_ref.shape[0]
    seqlen = tile // nseq
    assert seqlen == SUBLANES, "decode sequences must fill exactly one f32 sublane tile"
    i = pl.program_id(0)
    w = RET_WIDTH
    cw = w
    d_ff = wd_ref.shape[0]
    weight_loads = [pltpu.make_async_copy(src, dst, load_sem.at[n]) for n, (src, dst) in enumerate(
        ((win_hbm, win_ref), (wout_hbm, wout_ref), (wgu_hbm, wgu_ref), (wd_hbm, wd_ref)))]

    def setup():
        for cp in weight_loads:
            cp.start()
        ii = lax.broadcasted_iota(jnp.int32, (tile, tile), 0)
        jj = lax.broadcasted_iota(jnp.int32, (tile, tile), 1)
        keep = (ii >= jj) & ((ii // seqlen) == (jj // seqlen))
        diff = jnp.where(keep, ii - jj, 0).astype(F32)
        pos = (lax.broadcasted_iota(jnp.int32, (tile, RET_HD), 0) % seqlen).astype(F32)
        for hh in range(RET_HEADS):
            mask_scr[hh] = jnp.where(keep, jnp.exp(diff * LOG_G[hh]), 0.0)
            xi_scr[hh] = jnp.exp((pos + 1.0) * LOG_G[hh])
            zeta_scr[hh] = jnp.exp((seqlen - 1.0 - pos) * LOG_G[hh])
        weight_loads[0].wait()

    def proj_piece(c0):
        def emit():
            proj_scr[:, c0:c0 + cw] = _dot(h_scr[...], win_ref[:, c0:c0 + cw])
        return emit

    def step(mixer, dense):
        fillers = []

        def fill(n):
            for _ in range(min(n, len(fillers))):
                fillers.pop(0)()

        if dense:
            x1_scr[...] = xkeep_scr[...] + _dot(mix_scr[...], wout_ref[...])
        if mixer:
            x = x_ref[...]
            h_scr[...] = _rmsnorm(x, g1_ref[...]).astype(BF16)
            proj_scr[:, 0:4 * w] = _dot(h_scr[...], win_ref[:, 0:4 * w])
        if dense:
            h2_scr[...] = _rmsnorm(x1_scr[...], g2_ref[...]).astype(BF16)
            fillers += [functools.partial(_swiglu_piece, h2_scr, wgu_ref, act_scr, c0)
                        for c0 in range(0, d_ff, MXU_N)]
        if mixer:
            fillers[0:0] = [proj_piece(5 * w), proj_piece(6 * w)]
            cs = cs_ref[...]
            sn = sn_ref[...]
            pair_rows = 2 * seqlen
            rowblk = lax.broadcasted_iota(jnp.int32, (tile, 2 * RET_HD), 0) // seqlen
            colhalf = lax.broadcasted_iota(jnp.int32, (tile, 2 * RET_HD), 1) // RET_HD
            per_head = -(-len(fillers) // RET_HEADS)
            for hh in range(RET_HEADS):
                c0 = hh * RET_HD
                q = _rotary(proj_scr[:, c0:c0 + RET_HD], cs, sn)
                k = _rotary(proj_scr[:, w + c0:w + c0 + RET_HD], cs, sn) * (RET_HD ** -0.5)
                v = proj_scr[:, 2 * w + c0:2 * w + c0 + RET_HD]
                g = proj_scr[:, 3 * w + c0:3 * w + c0 + RET_HD]
                qb, kb, vb = q.astype(BF16), k.astype(BF16), v.astype(BF16)
                s = _dot_nt(qb, kb) * mask_scr[hh]
                fill(1)
                o_intra = _dot(s.astype(BF16), vb)
                kzt = (k * zeta_scr[hh]).T.astype(BF16)
                vv = jnp.concatenate([v, v], axis=1)
                g_chunk = math.exp(seqlen * LOG_G[hh])
                cross = []
                for p in range(nseq // 2):
                    sa, sb = 2 * p, 2 * p + 1
                    ra = sret_ref[sa, hh]
                    rb = sret_ref[sb, hh]
                    rcat = jnp.concatenate([ra, rb], axis=1).astype(BF16)
                    pr = _dot(qb[p * pair_rows:(p + 1) * pair_rows, :], rcat)
                    cross.append(pr[0:seqlen, 0:RET_HD])
                    cross.append(pr[seqlen:pair_rows, RET_HD:2 * RET_HD])
                    vpair = jnp.where(rowblk == sa + colhalf, vv, 0.0).astype(BF16)
                    upd = _dot(kzt, vpair)
                    retnew_ref[sa, hh] = ra * g_chunk + upd[:, 0:RET_HD]
                    retnew_ref[sb, hh] = rb * g_chunk + upd[:, RET_HD:2 * RET_HD]
                fill(per_head - 1)
                o = o_intra + jnp.concatenate(cross, axis=0) * xi_scr[hh]
                mix_scr[:, c0:c0 + RET_HD] = _groupnorm_gate(o, g, gng_ref[:, c0:c0 + RET_HD]).astype(BF16)

            u = proj_scr[:, 5 * w:5 * w + cw] * proj_scr[:, 5 * w + cw:5 * w + 2 * cw]
            cst = cst_ref[...]
            older = jnp.broadcast_to(cst[:, 0:1, :], (nseq, seqlen, cw)).reshape(tile, cw)
            newer = jnp.broadcast_to(cst[:, 1:2, :], (nseq, seqlen, cw)).reshape(tile, cw)
            r8 = lax.broadcasted_iota(jnp.int32, (tile, cw), 0) % seqlen
            um1 = jnp.where(r8 == 0, newer, pltpu.roll(u, 1, axis=0))
            um2 = jnp.where(r8 == 0, older, jnp.where(r8 == 1, newer, pltpu.roll(u, 2, axis=0)))
            convy_scr[...] = (convw_ref[0, 0:1, :] * um2 + convw_ref[0, 1:2, :] * um1
                              + convw_ref[0, 2:3, :] * u)
            cnew_ref[...] = u.reshape(nseq, seqlen, cw)[:, seqlen - (CONV_K - 1):, :]
        fill(len(fillers))

        if dense:
            down = _dot(act_scr[...], wd_ref[...])
        if mixer:
            proj_piece(4 * w)()
        if dense:
            y_ref[...] = _rmsnorm(x1_scr[...] + down, gf_ref[...])
        if mixer:
            mix_scr[:, w:w + cw] = (proj_scr[:, 4 * w:4 * w + cw] * convy_scr[...]).astype(BF16)
            xkeep_scr[...] = x

    @pl.when(i == 0)
    def _():
        setup()
        step(mixer=True, dense=False)
        for cp in weight_loads[1:]:
            cp.wait()

    @pl.when((i > 0) & (i < n_tiles))
    def _():
        step(mixer=True, dense=True)

    @pl.when(i == n_tiles)
    def _():
        step(mixer=False, dense=True)


def _resident(shape):
    nd = len(shape)
    return pl.BlockSpec(shape, lambda *_: (0,) * nd, pipeline_mode=pl.Buffered(1))


def _prompt_call(x, ws):
    batch, seq, d = x.shape
    tile = PROMPT_TILE
    tiles_per_seq = seq // tile
    n_tiles = batch * tiles_per_seq
    cs, sn = _rope_tables(np.arange(seq))
    in_cols = ws[1].shape[1]
    cw = ws[2].shape[-1]
    mixer_tile = lambda i: jnp.minimum(i, n_tiles - 1)
    dense_tile = lambda i: jnp.maximum(i - 1, 0)
    w_in, w_out, w_gate, w_up, w_down = [ws[m] for m in MATRIX_SLOTS]
    assert w_gate.shape == w_up.shape and w_gate.shape[1] == w_down.shape[0]
    d_ff = w_down.shape[0]
    mats = [jax.ShapeDtypeStruct(s, BF16) for s in (w_in.shape, w_out.shape, (d, 2 * d_ff), w_down.shape)]
    weight_specs = [pl.BlockSpec(memory_space=pl.ANY) if m in MATRIX_SLOTS else _resident(wt.shape)
                    for m, wt in enumerate(ws)]
    outs = pl.pallas_call(
        functools.partial(_prompt_kernel, n_tiles=n_tiles, tiles_per_seq=tiles_per_seq),
        grid=(n_tiles + 1,),
        in_specs=[pl.BlockSpec((tile, d), lambda i: (mixer_tile(i), 0)),
                  pl.BlockSpec((tile, RET_HD), lambda i: (mixer_tile(i) % tiles_per_seq, 0)),
                  pl.BlockSpec((tile, RET_HD), lambda i: (mixer_tile(i) % tiles_per_seq, 0))] + weight_specs,
        out_specs=[pl.BlockSpec((tile, d), lambda i: (dense_tile(i), 0)),
                   pl.BlockSpec((1, CONV_K - 1, cw), lambda i: (mixer_tile(i) // tiles_per_seq, 0, 0)),
                   pl.BlockSpec((1, RET_HEADS, RET_HD, RET_HD),
                                lambda i: (mixer_tile(i) // tiles_per_seq, 0, 0, 0))]
                  + [pl.BlockSpec(memory_space=pl.ANY)] * len(mats),
        out_shape=[jax.ShapeDtypeStruct((batch * seq, d), F32),
                   jax.ShapeDtypeStruct((batch, CONV_K - 1, cw), F32),
                   jax.ShapeDtypeStruct((batch, RET_HEADS, RET_HD, RET_HD), F32)]
                  + mats,
        scratch_shapes=[pltpu.VMEM(m.shape, BF16) for m in mats] + [
                        pltpu.VMEM((STAGE_SLOTS * STAGE_ROWS, max(in_cols, d_ff, d)), F32),
                        pltpu.SemaphoreType.DMA((STAGE_SLOTS,)),
                        pltpu.SemaphoreType.DMA((len(mats),)),
                        pltpu.VMEM((tile, in_cols), F32),
                        pltpu.VMEM((tile + SUBLANES, cw), F32),
                        pltpu.VMEM((tile, cw), F32),
                        pltpu.VMEM((tile, d), BF16),
                        pltpu.VMEM((tile, d), F32),
                        pltpu.VMEM((tile, d), F32),
                        pltpu.VMEM((tile, d), BF16),
                        pltpu.VMEM((tile, d), BF16),
                        pltpu.VMEM((tile, d_ff), BF16),
                        pltpu.VMEM((RET_HEADS, RET_HD, RET_HD), F32),
                        pltpu.VMEM((RET_HEADS, PROMPT_CHUNK, PROMPT_CHUNK), F32),
                        pltpu.VMEM((RET_HEADS, PROMPT_CHUNK, RET_HD), F32),
                        pltpu.VMEM((RET_HEADS, PROMPT_CHUNK, RET_HD), F32)],
        compiler_params=pltpu.CompilerParams(dimension_semantics=("arbitrary",),
                                             vmem_limit_bytes=VMEM_LIMIT_BYTES),
        name="prompt_layer",
    )(x.reshape(batch * seq, d), cs, sn, *ws)
    y, cnew, ret, win_bf, wout_bf, wgu_bf, wd_bf = outs
    ws_bf16 = (ws[0], win_bf, ws[2], ws[3], wout_bf, ws[5], wgu_bf, wd_bf, ws[9])
    return y.reshape(batch, seq, d), cnew, ret, ws_bf16


def _sample_call(x, state_conv, state_ret, ws):
    nseq_all, seqlen, d = x.shape
    nseq = SAMPLE_SEQS
    tile = nseq * seqlen
    n_tiles = nseq_all // nseq
    cs, sn = _rope_tables(np.tile(PAST_LEN + np.arange(seqlen), nseq))
    in_cols = ws[1].shape[1]
    cw = ws[2].shape[-1]
    d_ff = ws[7].shape[0]
    matrix_slots = (1, 4, 6, 7)
    mixer_tile = lambda i: jnp.minimum(i, n_tiles - 1)
    dense_tile = lambda i: jnp.maximum(i - 1, 0)
    weight_specs = [pl.BlockSpec(memory_space=pl.ANY) if m in matrix_slots else _resident(wt.shape)
                    for m, wt in enumerate(ws)]
    y, cnew, rnew = pl.pallas_call(
        functools.partial(_sample_kernel, n_tiles=n_tiles),
        grid=(n_tiles + 1,),
        in_specs=[pl.BlockSpec((tile, d), lambda i: (mixer_tile(i), 0)),
                  _resident((tile, RET_HD)), _resident((tile, RET_HD)),
                  pl.BlockSpec((nseq, CONV_K - 1, cw), lambda i: (mixer_tile(i), 0, 0)),
                  pl.BlockSpec((nseq, RET_HEADS, RET_HD, RET_HD), lambda i: (mixer_tile(i), 0, 0, 0))]
                 + weight_specs,
        out_specs=[pl.BlockSpec((tile, d), lambda i: (dense_tile(i), 0)),
                   pl.BlockSpec((nseq, CONV_K - 1, cw), lambda i: (mixer_tile(i), 0, 0)),
                   pl.BlockSpec((nseq, RET_HEADS, RET_HD, RET_HD), lambda i: (mixer_tile(i), 0, 0, 0))],
        out_shape=[jax.ShapeDtypeStruct((nseq_all * seqlen, d), F32),
                   jax.ShapeDtypeStruct((nseq_all, CONV_K - 1, cw), F32),
                   jax.ShapeDtypeStruct((nseq_all, RET_HEADS, RET_HD, RET_HD), F32)],
        scratch_shapes=[pltpu.VMEM(ws[m].shape, BF16) for m in matrix_slots] + [
                        pltpu.SemaphoreType.DMA((len(matrix_slots),)),
                        pltpu.VMEM((tile, in_cols), F32),
                        pltpu.VMEM((tile, cw), F32),
                        pltpu.VMEM((tile, d), BF16),
                        pltpu.VMEM((tile, d), F32),
                        pltpu.VMEM((tile, d), F32),
                        pltpu.VMEM((tile, d), BF16),
                        pltpu.VMEM((tile, d), BF16),
                        pltpu.VMEM((tile, d_ff), BF16),
                        pltpu.VMEM((RET_HEADS, tile, tile), F32),
                        pltpu.VMEM((RET_HEADS, tile, RET_HD), F32),
                        pltpu.VMEM((RET_HEADS, tile, RET_HD), F32)],
        compiler_params=pltpu.CompilerParams(dimension_semantics=("arbitrary",),
                                             vmem_limit_bytes=VMEM_LIMIT_BYTES),
        name="sample_layer",
    )(x.reshape(nseq_all * seqlen, d), cs, sn, state_conv, state_ret, *ws)
    return y.reshape(nseq_all, seqlen, d), cnew, rnew


def kernel(x_prompt, x_sample, state_conv, state_ret, norm1_g, w_in, conv_w, ret_gn_g, w_out, norm2_g,
           w_gate, w_up, w_down, norm_f_g):
    depth = w_in.shape[0]
    assert depth == 1, "the fused layer kernels take a single layer"
    row = lambda g: g.reshape(1, -1)
    ws = (row(norm1_g[0]), w_in[0], conv_w, row(ret_gn_g[0]), w_out[0],
          row(norm2_g[0]), w_gate[0], w_up[0], w_down[0], row(norm_f_g))
    y_p, cnew_p, ret_p, ws_bf16 = _prompt_call(x_prompt, ws)
    y_s, cnew_s, ret_s = _sample_call(x_sample, state_conv[0], state_ret[0], ws_bf16)
    return (y_p, y_s, cnew_p[None], ret_p[None], cnew_s[None], ret_s[None])
```

```python
import functools
import math

import numpy as np
import jax
import jax.numpy as jnp
from jax import lax
from jax.experimental import pallas as pl
from jax.experimental.pallas import tpu as pltpu

F32 = jnp.float32
BF16 = jnp.bfloat16

RET_HEADS = 4
RET_HD = 128
RET_WIDTH = RET_HEADS * RET_HD
CONV_K = 3
ROPE_BASE = 10000.0
NORM_EPS = 1e-6
GN_EPS = 1e-5
PAST_LEN = 16384

LOG_G = tuple(math.log1p(-(2.0 ** (-5.0 - h))) for h in range(RET_HEADS))

SUBLANES = 8
MXU_N = 256
MATRIX_SLOTS = (1, 4, 6, 7, 8)
STAGE_ROWS = 128
STAGE_SLOTS = 8
PROMPT_TILE = 256
PROMPT_CHUNK = 128
SAMPLE_SEQS = 16
VMEM_LIMIT_BYTES = 60 * 1024 * 1024


def _rope_tables(pos):
    half = RET_HD // 2
    inv = ROPE_BASE ** (-np.arange(half, dtype=np.float64) / half)
    ang = np.asarray(pos, np.float64)[:, None] * inv[None, :]
    cos, sin = np.cos(ang), np.sin(ang)
    cs = np.concatenate([cos, cos], axis=1).astype(np.float32)
    sn = np.concatenate([-sin, sin], axis=1).astype(np.float32)
    return jnp.asarray(cs), jnp.asarray(sn)


def _rmsnorm(x, g):
    ms = jnp.mean(x * x, axis=-1, keepdims=True)
    return x * lax.rsqrt(ms + NORM_EPS) * g


def _silu(x):
    return x * (1.0 / (1.0 + jnp.exp(-x)))


def _rotary(t, cs, sn):
    return t * cs + pltpu.roll(t, RET_HD // 2, axis=1) * sn


def _dot(a, b):
    return jnp.dot(a, b, preferred_element_type=F32)


def _dot_nt(a, b):
    return lax.dot_general(a, b, (((1,), (1,)), ((), ())), preferred_element_type=F32)


def _dot_tn(a, b):
    return lax.dot_general(a, b, (((0,), (0,)), ((), ())), preferred_element_type=F32)


def _groupnorm_gate(o, g, gn_g):
    mu = jnp.mean(o, axis=-1, keepdims=True)
    d = o - mu
    var = jnp.mean(d * d, axis=-1, keepdims=True)
    return _silu(g) * (d * lax.rsqrt(var + GN_EPS) * gn_g)


def _swiglu_piece(h_scr, wgu_ref, act_scr, c0):
    r = _dot(h_scr[...], wgu_ref[:, 2 * c0:2 * c0 + 2 * MXU_N])
    act_scr[:, c0:c0 + MXU_N] = (_silu(r[:, 0:MXU_N]) * r[:, MXU_N:2 * MXU_N]).astype(BF16)


def _load_weights_bf16(pairs, stage, sem):
    rb = STAGE_ROWS
    slots, width = stage.shape[0] // rb, stage.shape[1]
    assert slots >= 2 and slots <= sem.shape[0]
    chunks = []
    for w_hbm, w_bf, lane in pairs:
        rows, cols = w_hbm.shape
        assert rows % rb == 0 and cols <= width and (lane is None or cols % MXU_N == 0)
        chunks += [(w_hbm, w_bf, r0, cols, lane) for r0 in range(0, rows, rb)]

    def slot_view(n):
        s0 = (n % slots) * rb
        return stage.at[s0:s0 + rb, 0:chunks[n][3]]

    def chunk_copy(n):
        w_hbm, r0 = chunks[n][0], chunks[n][2]
        return pltpu.make_async_copy(w_hbm.at[r0:r0 + rb, :], slot_view(n), sem.at[n % slots])

    for n in range(min(slots - 1, len(chunks))):
        chunk_copy(n).start(priority=n % 2)
    for n, (_, w_bf, r0, cols, lane) in enumerate(chunks):
        chunk_copy(n).wait()
        if lane is None:
            w_bf[r0:r0 + rb, :] = slot_view(n)[...].astype(BF16)
        else:
            for c0 in range(0, cols, MXU_N):
                d0 = 2 * c0 + lane * MXU_N
                w_bf[r0:r0 + rb, d0:d0 + MXU_N] = slot_view(n)[:, c0:c0 + MXU_N].astype(BF16)
        if n + slots - 1 < len(chunks):
            chunk_copy(n + slots - 1).start(priority=(n + slots - 1) % 2)


def _prompt_kernel(x_ref, cs_ref, sn_ref, g1_ref, win_hbm, convw_ref, gng_ref, wout_hbm, g2_ref,
                   wg_hbm, wu_hbm, wd_hbm, gf_ref,
                   y_ref, cnew_ref, ret_ref, win_out, wout_out, wgu_out, wd_out,
                   win_ref, wout_ref, wgu_ref, wd_ref, stage_scr, load_sem, store_sem,
                   proj_scr, uext_scr, convy_scr, mix_scr, xkeep_scr, x1_scr, h_scr, h2_scr, act_scr, r_scr,
                   dec_scr, xi_scr, zeta_scr, *, n_tiles, tiles_per_seq):
    tile = x_ref.shape[0]
    i = pl.program_id(0)
    chunk = PROMPT_CHUNK
    assert tile % chunk == 0
    t = jnp.minimum(i, n_tiles - 1) % tiles_per_seq
    seq_start = t == 0
    seq_end = t == tiles_per_seq - 1
    w = RET_WIDTH
    cw = w
    d_ff = wd_ref.shape[0]
    weight_stores = [pltpu.make_async_copy(src, dst, store_sem.at[n]) for n, (src, dst) in enumerate(
        ((win_ref, win_out), (wout_ref, wout_out), (wgu_ref, wgu_out), (wd_ref, wd_out)))]

    def setup():
        _load_weights_bf16(((win_hbm, win_ref, None), (wout_hbm, wout_ref, None), (wg_hbm, wgu_ref, 0),
                            (wu_hbm, wgu_ref, 1), (wd_hbm, wd_ref, None)), stage_scr, load_sem)
        for cp in weight_stores:
            cp.start()
        ii = lax.broadcasted_iota(jnp.int32, (chunk, chunk), 0)
        jj = lax.broadcasted_iota(jnp.int32, (chunk, chunk), 1)
        causal = ii >= jj
        diff = jnp.where(causal, ii - jj, 0).astype(F32)
        i_f = lax.broadcasted_iota(jnp.int32, (chunk, RET_HD), 0).astype(F32)
        for hh in range(RET_HEADS):
            dec_scr[hh] = jnp.where(causal, jnp.exp(diff * LOG_G[hh]), 0.0)
            xi_scr[hh] = jnp.exp((i_f + 1.0) * LOG_G[hh])
            zeta_scr[hh] = jnp.exp((chunk - 1.0 - i_f) * LOG_G[hh])
        r_scr[...] = jnp.zeros_like(r_scr)
        uext_scr[0:SUBLANES, :] = jnp.zeros((SUBLANES, uext_scr.shape[1]), F32)

    def ffn_piece(c0):
        return functools.partial(_swiglu_piece, h2_scr, wgu_ref, act_scr, c0)

    def proj_piece(c0):
        def emit():
            proj_scr[:, c0:c0 + cw] = _dot(h_scr[...], win_ref[:, c0:c0 + cw])
        return emit

    def step(mixer, dense):
        fillers = []

        def fill(n):
            for _ in range(min(n, len(fillers))):
                fillers.pop(0)()

        if dense:
            x1_scr[...] = xkeep_scr[...] + _dot(mix_scr[...], wout_ref[...])
        if mixer:
            x = x_ref[...]
            h_scr[...] = _rmsnorm(x, g1_ref[...]).astype(BF16)
            proj_scr[:, 0:2 * w] = _dot(h_scr[...], win_ref[:, 0:2 * w])
        if dense:
            h2_scr[...] = _rmsnorm(x1_scr[...], g2_ref[...]).astype(BF16)
            fillers += [ffn_piece(c0) for c0 in range(0, d_ff, MXU_N)]
        if not mixer:
            fill(len(fillers))
        else:
            proj_scr[:, 2 * w:4 * w] = _dot(h_scr[...], win_ref[:, 2 * w:4 * w])
            fillers[0:0] = [proj_piece(5 * w), proj_piece(6 * w)]

            n_chunks = tile // chunk
            units = [(c, hh) for c in range(n_chunks) for hh in range(RET_HEADS)]
            qbs, vbs, scores, updates = {}, {}, {}, {}
            for c, hh in units:
                r0, c0 = c * chunk, hh * RET_HD
                cs = cs_ref[r0:r0 + chunk, :]
                sn = sn_ref[r0:r0 + chunk, :]
                q = _rotary(proj_scr[r0:r0 + chunk, c0:c0 + RET_HD], cs, sn)
                k = _rotary(proj_scr[r0:r0 + chunk, w + c0:w + c0 + RET_HD], cs, sn) * (RET_HD ** -0.5)
                qb, kb = q.astype(BF16), k.astype(BF16)
                vb = proj_scr[r0:r0 + chunk, 2 * w + c0:2 * w + c0 + RET_HD].astype(BF16)
                kz = (k * zeta_scr[hh]).astype(BF16)
                qbs[c, hh], vbs[c, hh] = qb, vb
                scores[c, hh] = _dot_nt(qb, kb)
                updates[c, hh] = _dot_tn(kz, vb)
            fill(2)

            u = proj_scr[:, 5 * w:5 * w + cw] * proj_scr[:, 5 * w + cw:5 * w + 2 * cw]
            uext_scr[SUBLANES:SUBLANES + tile, :] = u
            um1 = uext_scr[SUBLANES - 1:SUBLANES - 1 + tile, :]
            um2 = uext_scr[SUBLANES - 2:SUBLANES - 2 + tile, :]
            convy_scr[...] = (convw_ref[0, 0:1, :] * um2 + convw_ref[0, 1:2, :] * um1
                              + convw_ref[0, 2:3, :] * u)
            tail = uext_scr[tile + SUBLANES - (CONV_K - 1):tile + SUBLANES, :]
            cnew_ref[0] = tail
            uext_scr[SUBLANES - (CONV_K - 1):SUBLANES, :] = jnp.where(seq_end, 0.0, tail)
            fill(len(fillers) - len(units) + 1)

            outs = {}
            for hh in range(RET_HEADS):
                state = jnp.where(seq_start, 0.0, r_scr[hh])
                for c in range(n_chunks):
                    s = (scores[c, hh] * dec_scr[hh]).astype(BF16)
                    outs[c, hh] = _dot(s, vbs[c, hh]) + _dot(qbs[c, hh], state.astype(BF16)) * xi_scr[hh]
                    state = state * math.exp(chunk * LOG_G[hh]) + updates[c, hh]
                r_scr[hh] = state
                ret_ref[0, hh] = state
            for c, hh in units:
                fill(1)
                r0, c0 = c * chunk, hh * RET_HD
                g = proj_scr[r0:r0 + chunk, 3 * w + c0:3 * w + c0 + RET_HD]
                mix_scr[r0:r0 + chunk, c0:c0 + RET_HD] = _groupnorm_gate(
                    outs[c, hh], g, gng_ref[:, c0:c0 + RET_HD]).astype(BF16)
            fill(len(fillers))

        if dense:
            down = _dot(act_scr[...], wd_ref[...])
        if mixer:
            proj_piece(4 * w)()
        if dense:
            y_ref[...] = _rmsnorm(x1_scr[...] + down, gf_ref[...])
        if mixer:
            mix_scr[:, w:w + cw] = (proj_scr[:, 4 * w:4 * w + cw] * convy_scr[...]).astype(BF16)
            xkeep_scr[...] = x

    @pl.when(i == 0)
    def _():
        setup()
        step(mixer=True, dense=False)

    @pl.when((i > 0) & (i < n_tiles))
    def _():
        step(mixer=True, dense=True)

    @pl.when(i == n_tiles)
    def _():
        step(mixer=False, dense=True)
        for cp in weight_stores:
            cp.wait()


def _sample_kernel(x_ref, cs_ref, sn_ref, cst_ref, sret_ref, g1_ref, win_hbm, convw_ref, gng_ref,
                   wout_hbm, g2_ref, wgu_hbm, wd_hbm, gf_ref,
                   y_ref, cnew_ref, retnew_ref,
                   win_ref, wout_ref, wgu_ref, wd_ref, load_sem,
                   proj_scr, convy_scr, mix_scr, xkeep_scr, x1_scr, h_scr, h2_scr, act_scr,
                   mask_scr, xi_scr, zeta_scr, *, n_tiles):
    tile = x_ref.shape[0]
    nseq = sret_ref.shape[0]
    seqlen = tile // nseq
    assert seqlen == SUBLANES, "decode sequences must fill exactly one f32 sublane tile"
    i = pl.program_id(0)
    w = RET_WIDTH
    cw = w
    d_ff = wd_ref.shape[0]
    weight_loads = [pltpu.make_async_copy(src, dst, load_sem.at[n]) for n, (src, dst) in enumerate(
        ((win_hbm, win_ref), (wout_hbm, wout_ref), (wgu_hbm, wgu_ref), (wd_hbm, wd_ref)))]

    def setup():
        for cp in weight_loads:
            cp.start()
        ii = lax.broadcasted_iota(jnp.int32, (tile, tile), 0)
        jj = lax.broadcasted_iota(jnp.int32, (tile, tile), 1)
        keep = (ii >= jj) & ((ii // seqlen) == (jj // seqlen))
        diff = jnp.where(keep, ii - jj, 0).astype(F32)
        pos = (lax.broadcasted_iota(jnp.int32, (tile, RET_HD), 0) % seqlen).astype(F32)
        for hh in range(RET_HEADS):
            mask_scr[hh] = jnp.where(keep, jnp.exp(diff * LOG_G[hh]), 0.0)
            xi_scr[hh] = jnp.exp((pos + 1.0) * LOG_G[hh])
            zeta_scr[hh] = jnp.exp((seqlen - 1.0 - pos) * LOG_G[hh])
        weight_loads[0].wait()

    def proj_piece(c0):
        def emit():
            proj_scr[:, c0:c0 + cw] = _dot(h_scr[...], win_ref[:, c0:c0 + cw])
        return emit

    def step(mixer, dense, wait_weights=False):
        fillers = []

        def fill(n):
            for _ in range(min(n, len(fillers))):
                fillers.pop(0)()

        def arrive(n):
            if wait_weights:
                weight_loads[n].wait()

        if dense:
            arrive(1)
            x1_scr[...] = xkeep_scr[...] + _dot(mix_scr[...], wout_ref[...])
        if mixer:
            x = x_ref[...]
            h_scr[...] = _rmsnorm(x, g1_ref[...]).astype(BF16)
            proj_scr[:, 0:4 * w] = _dot(h_scr[...], win_ref[:, 0:4 * w])
        if dense:
            h2_scr[...] = _rmsnorm(x1_scr[...], g2_ref[...]).astype(BF16)
            fillers += [functools.partial(_swiglu_piece, h2_scr, wgu_ref, act_scr, c0)
                        for c0 in range(0, d_ff, MXU_N)]
            fillers[0:1] = [lambda first=fillers[0]: (arrive(2), first())]
        if mixer:
            fillers[0:0] = [proj_piece(5 * w), proj_piece(6 * w)]
            cs = cs_ref[...]
            sn = sn_ref[...]
            pair_rows = 2 * seqlen
            rowblk = lax.broadcasted_iota(jnp.int32, (tile, 2 * RET_HD), 0) // seqlen
            colhalf = lax.broadcasted_iota(jnp.int32, (tile, 2 * RET_HD), 1) // RET_HD
            per_head = -(-len(fillers) // RET_HEADS)
            for hh in range(RET_HEADS):
                c0 = hh * RET_HD
                q = _rotary(proj_scr[:, c0:c0 + RET_HD], cs, sn)
                k = _rotary(proj_scr[:, w + c0:w + c0 + RET_HD], cs, sn) * (RET_HD ** -0.5)
                v = proj_scr[:, 2 * w + c0:2 * w + c0 + RET_HD]
                g = proj_scr[:, 3 * w + c0:3 * w + c0 + RET_HD]
                qb, kb, vb = q.astype(BF16), k.astype(BF16), v.astype(BF16)
                s = _dot_nt(qb, kb) * mask_scr[hh]
                fill(1)
                o_intra = _dot(s.astype(BF16), vb)
                kzt = (k * zeta_scr[hh]).T.astype(BF16)
                vv = jnp.concatenate([v, v], axis=1)
                g_chunk = math.exp(seqlen * LOG_G[hh])
                cross = []
                for p in range(nseq // 2):
                    sa, sb = 2 * p, 2 * p + 1
                    ra = sret_ref[sa, hh]
                    rb = sret_ref[sb, hh]
                    rcat = jnp.concatenate([ra, rb], axis=1).astype(BF16)
                    pr = _dot(qb[p * pair_rows:(p + 1) * pair_rows, :], rcat)
                    cross.append(pr[0:seqlen, 0:RET_HD])
                    cross.append(pr[seqlen:pair_rows, RET_HD:2 * RET_HD])
                    vpair = jnp.where(rowblk == sa + colhalf, vv, 0.0).astype(BF16)
                    upd = _dot(kzt, vpair)
                    retnew_ref[sa, hh] = ra * g_chunk + upd[:, 0:RET_HD]
                    retnew_ref[sb, hh] = rb * g_chunk + upd[:, RET_HD:2 * RET_HD]
                fill(per_head - 1)
                o = o_intra + jnp.concatenate(cross, axis=0) * xi_scr[hh]
                mix_scr[:, c0:c0 + RET_HD] = _groupnorm_gate(o, g, gng_ref[:, c0:c0 + RET_HD]).astype(BF16)

            u = proj_scr[:, 5 * w:5 * w + cw] * proj_scr[:, 5 * w + cw:5 * w + 2 * cw]
            cst = cst_ref[...]
            older = jnp.broadcast_to(cst[:, 0:1, :], (nseq, seqlen, cw)).reshape(tile, cw)
            newer = jnp.broadcast_to(cst[:, 1:2, :], (nseq, seqlen, cw)).reshape(tile, cw)
            r8 = lax.broadcasted_iota(jnp.int32, (tile, cw), 0) % seqlen
            um1 = jnp.where(r8 == 0, newer, pltpu.roll(u, 1, axis=0))
            um2 = jnp.where(r8 == 0, older, jnp.where(r8 == 1, newer, pltpu.roll(u, 2, axis=0)))
            convy_scr[...] = (convw_ref[0, 0:1, :] * um2 + convw_ref[0, 1:2, :] * um1
                              + convw_ref[0, 2:3, :] * u)
            cnew_ref[...] = u.reshape(nseq, seqlen, cw)[:, seqlen - (CONV_K - 1):, :]
        fill(len(fillers))

        if dense:
            arrive(3)
            down = _dot(act_scr[...], wd_ref[...])
        if mixer:
            proj_piece(4 * w)()
        if dense:
            y_ref[...] = _rmsnorm(x1_scr[...] + down, gf_ref[...])
        if mixer:
            mix_scr[:, w:w + cw] = (proj_scr[:, 4 * w:4 * w + cw] * convy_scr[...]).astype(BF16)
            xkeep_scr[...] = x

    assert n_tiles >= 2
    @pl.when(i == 0)
    def _():
        setup()
        step(mixer=True, dense=False)

    @pl.when(i == 1)
    def _():
        step(mixer=True, dense=True, wait_weights=True)

    @pl.when((i > 1) & (i < n_tiles))
    def _():
        step(mixer=True, dense=True)

    @pl.when(i == n_tiles)
    def _():
        step(mixer=False, dense=True)


def _resident(shape):
    nd = len(shape)
    return pl.BlockSpec(shape, lambda *_: (0,) * nd, pipeline_mode=pl.Buffered(1))


def _prompt_call(x, ws):
    batch, seq, d = x.shape
    tile = PROMPT_TILE
    tiles_per_seq = seq // tile
    n_tiles = batch * tiles_per_seq
    cs, sn = _rope_tables(np.arange(seq))
    in_cols = ws[1].shape[1]
    cw = ws[2].shape[-1]
    mixer_tile = lambda i: jnp.minimum(i, n_tiles - 1)
    dense_tile = lambda i: jnp.maximum(i - 1, 0)
    w_in, w_out, w_gate, w_up, w_down = [ws[m] for m in MATRIX_SLOTS]
    assert w_gate.shape == w_up.shape and w_gate.shape[1] == w_down.shape[0]
    d_ff = w_down.shape[0]
    mats = [jax.ShapeDtypeStruct(s, BF16) for s in (w_in.shape, w_out.shape, (d, 2 * d_ff), w_down.shape)]
    weight_specs = [pl.BlockSpec(memory_space=pl.ANY) if m in MATRIX_SLOTS else _resident(wt.shape)
                    for m, wt in enumerate(ws)]
    outs = pl.pallas_call(
        functools.partial(_prompt_kernel, n_tiles=n_tiles, tiles_per_seq=tiles_per_seq),
        grid=(n_tiles + 1,),
        in_specs=[pl.BlockSpec((tile, d), lambda i: (mixer_tile(i), 0)),
                  pl.BlockSpec((tile, RET_HD), lambda i: (mixer_tile(i) % tiles_per_seq, 0)),
                  pl.BlockSpec((tile, RET_HD), lambda i: (mixer_tile(i) % tiles_per_seq, 0))] + weight_specs,
        out_specs=[pl.BlockSpec((tile, d), lambda i: (dense_tile(i), 0)),
                   pl.BlockSpec((1, CONV_K - 1, cw), lambda i: (mixer_tile(i) // tiles_per_seq, 0, 0)),
                   pl.BlockSpec((1, RET_HEADS, RET_HD, RET_HD),
                                lambda i: (mixer_tile(i) // tiles_per_seq, 0, 0, 0))]
                  + [pl.BlockSpec(memory_space=pl.ANY)] * len(mats),
        out_shape=[jax.ShapeDtypeStruct((batch * seq, d), F32),
                   jax.ShapeDtypeStruct((batch, CONV_K - 1, cw), F32),
                   jax.ShapeDtypeStruct((batch, RET_HEADS, RET_HD, RET_HD), F32)]
                  + mats,
        scratch_shapes=[pltpu.VMEM(m.shape, BF16) for m in mats] + [
                        pltpu.VMEM((STAGE_SLOTS * STAGE_ROWS, max(in_cols, d_ff, d)), F32),
                        pltpu.SemaphoreType.DMA((STAGE_SLOTS,)),
                        pltpu.SemaphoreType.DMA((len(mats),)),
                        pltpu.VMEM((tile, in_cols), F32),
                        pltpu.VMEM((tile + SUBLANES, cw), F32),
                        pltpu.VMEM((tile, cw), F32),
                        pltpu.VMEM((tile, d), BF16),
                        pltpu.VMEM((tile, d), F32),
                        pltpu.VMEM((tile, d), F32),
                        pltpu.VMEM((tile, d), BF16),
                        pltpu.VMEM((tile, d), BF16),
                        pltpu.VMEM((tile, d_ff), BF16),
                        pltpu.VMEM((RET_HEADS, RET_HD, RET_HD), F32),
                        pltpu.VMEM((RET_HEADS, PROMPT_CHUNK, PROMPT_CHUNK), F32),
                        pltpu.VMEM((RET_HEADS, PROMPT_CHUNK, RET_HD), F32),
                        pltpu.VMEM((RET_HEADS, PROMPT_CHUNK, RET_HD), F32)],
        compiler_params=pltpu.CompilerParams(dimension_semantics=("arbitrary",),
                                             vmem_limit_bytes=VMEM_LIMIT_BYTES),
        name="prompt_layer",
    )(x.reshape(batch * seq, d), cs, sn, *ws)
    y, cnew, ret, win_bf, wout_bf, wgu_bf, wd_bf = outs
    ws_bf16 = (ws[0], win_bf, ws[2], ws[3], wout_bf, ws[5], wgu_bf, wd_bf, ws[9])
    return y.reshape(batch, seq, d), cnew, ret, ws_bf16


def _sample_call(x, state_conv, state_ret, ws):
    nseq_all, seqlen, d = x.shape
    nseq = SAMPLE_SEQS
    tile = nseq * seqlen
    n_tiles = nseq_all // nseq
    cs, sn = _rope_tables(np.tile(PAST_LEN + np.arange(seqlen), nseq))
    in_cols = ws[1].shape[1]
    cw = ws[2].shape[-1]
    d_ff = ws[7].shape[0]
    matrix_slots = (1, 4, 6, 7)
    mixer_tile = lambda i: jnp.minimum(i, n_tiles - 1)
    dense_tile = lambda i: jnp.maximum(i - 1, 0)
    weight_specs = [pl.BlockSpec(memory_space=pl.ANY) if m in matrix_slots else _resident(wt.shape)
                    for m, wt in enumerate(ws)]
    y, cnew, rnew = pl.pallas_call(
        functools.partial(_sample_kernel, n_tiles=n_tiles),
        grid=(n_tiles + 1,),
        in_specs=[pl.BlockSpec((tile, d), lambda i: (mixer_tile(i), 0)),
                  _resident((tile, RET_HD)), _resident((tile, RET_HD)),
                  pl.BlockSpec((nseq, CONV_K - 1, cw), lambda i: (mixer_tile(i), 0, 0)),
                  pl.BlockSpec((nseq, RET_HEADS, RET_HD, RET_HD), lambda i: (mixer_tile(i), 0, 0, 0))]
                 + weight_specs,
        out_specs=[pl.BlockSpec((tile, d), lambda i: (dense_tile(i), 0)),
                   pl.BlockSpec((nseq, CONV_K - 1, cw), lambda i: (mixer_tile(i), 0, 0)),
                   pl.BlockSpec((nseq, RET_HEADS, RET_HD, RET_HD), lambda i: (mixer_tile(i), 0, 0, 0))],
        out_shape=[jax.ShapeDtypeStruct((nseq_all * seqlen, d), F32),
                   jax.ShapeDtypeStruct((nseq_all, CONV_K - 1, cw), F32),
                   jax.ShapeDtypeStruct((nseq_all, RET_HEADS, RET_HD, RET_HD), F32)],
        scratch_shapes=[pltpu.VMEM(ws[m].shape, BF16) for m in matrix_slots] + [
                        pltpu.SemaphoreType.DMA((len(matrix_slots),)),
                        pltpu.VMEM((tile, in_cols), F32),
                        pltpu.VMEM((tile, cw), F32),
                        pltpu.VMEM((tile, d), BF16),
                        pltpu.VMEM((tile, d), F32),
                        pltpu.VMEM((tile, d), F32),
                        pltpu.VMEM((tile, d), BF16),
                        pltpu.VMEM((tile, d), BF16),
                        pltpu.VMEM((tile, d_ff), BF16),
                        pltpu.VMEM((RET_HEADS, tile, tile), F32),
                        pltpu.VMEM((RET_HEADS, tile, RET_HD), F32),
                        pltpu.VMEM((RET_HEADS, tile, RET_HD), F32)],
        compiler_params=pltpu.CompilerParams(dimension_semantics=("arbitrary",),
                                             vmem_limit_bytes=VMEM_LIMIT_BYTES),
        name="sample_layer",
    )(x.reshape(nseq_all * seqlen, d), cs, sn, state_conv, state_ret, *ws)
    return y.reshape(nseq_all, seqlen, d), cnew, rnew


def kernel(x_prompt, x_sample, state_conv, state_ret, norm1_g, w_in, conv_w, ret_gn_g, w_out, norm2_g,
           w_gate, w_up, w_down, norm_f_g):
    depth = w_in.shape[0]
    assert depth == 1, "the fused layer kernels take a single layer"
    row = lambda g: g.reshape(1, -1)
    ws = (row(norm1_g[0]), w_in[0], conv_w, row(ret_gn_g[0]), w_out[0],
          row(norm2_g[0]), w_gate[0], w_up[0], w_down[0], row(norm_f_g))
    y_p, cnew_p, ret_p, ws_bf16 = _prompt_call(x_prompt, ws)
    y_s, cnew_s, ret_s = _sample_call(x_sample, state_conv[0], state_ret[0], ws_bf16)
    return (y_p, y_s, cnew_p[None], ret_p[None], cnew_s[None], ret_s[None])
```

```python
import functools
import math

import numpy as np
import jax
import jax.numpy as jnp
from jax import lax
from jax.experimental import pallas as pl
from jax.experimental.pallas import tpu as pltpu

F32 = jnp.float32
BF16 = jnp.bfloat16

RET_HEADS = 4
RET_HD = 128
RET_WIDTH = RET_HEADS * RET_HD
CONV_K = 3
ROPE_BASE = 10000.0
NORM_EPS = 1e-6
GN_EPS = 1e-5
PAST_LEN = 16384

LOG_G = tuple(math.log1p(-(2.0 ** (-5.0 - h))) for h in range(RET_HEADS))

SUBLANES = 8
MXU_N = 256
MATRIX_SLOTS = (1, 4, 6, 7, 8)
STAGE_ROWS = 128
STAGE_SLOTS = 8
PROMPT_TILE = 256
PROMPT_CHUNK = 128
SAMPLE_SEQS = 16
VMEM_LIMIT_BYTES = 60 * 1024 * 1024


def _rope_tables(pos):
    half = RET_HD // 2
    inv = ROPE_BASE ** (-np.arange(half, dtype=np.float64) / half)
    ang = np.asarray(pos, np.float64)[:, None] * inv[None, :]
    cos, sin = np.cos(ang), np.sin(ang)
    cs = np.concatenate([cos, cos], axis=1).astype(np.float32)
    sn = np.concatenate([-sin, sin], axis=1).astype(np.float32)
    return jnp.asarray(cs), jnp.asarray(sn)


def _rmsnorm(x, g):
    ms = jnp.mean(x * x, axis=-1, keepdims=True)
    return x * lax.rsqrt(ms + NORM_EPS) * g


def _silu(x):
    return x * (1.0 / (1.0 + jnp.exp(-x)))


def _rotary(t, cs, sn):
    return t * cs + pltpu.roll(t, RET_HD // 2, axis=1) * sn


def _dot(a, b):
    return jnp.dot(a, b, preferred_element_type=F32)


def _dot_nt(a, b):
    return lax.dot_general(a, b, (((1,), (1,)), ((), ())), preferred_element_type=F32)


def _dot_tn(a, b):
    return lax.dot_general(a, b, (((0,), (0,)), ((), ())), preferred_element_type=F32)


def _groupnorm_gate(o, g, gn_g):
    mu = jnp.mean(o, axis=-1, keepdims=True)
    d = o - mu
    var = jnp.mean(d * d, axis=-1, keepdims=True)
    return _silu(g) * (d * lax.rsqrt(var + GN_EPS) * gn_g)


def _swiglu_piece(h_scr, wgu_ref, act_scr, c0):
    r = _dot(h_scr[...], wgu_ref[:, 2 * c0:2 * c0 + 2 * MXU_N])
    act_scr[:, c0:c0 + MXU_N] = (_silu(r[:, 0:MXU_N]) * r[:, MXU_N:2 * MXU_N]).astype(BF16)


def _load_weights_bf16(pairs, stage, sem):
    rb = STAGE_ROWS
    slots, width = stage.shape[0] // rb, stage.shape[1]
    assert slots >= 2 and slots <= sem.shape[0]
    chunks = []
    for w_hbm, w_bf, lane in pairs:
        rows, cols = w_hbm.shape
        assert rows % rb == 0 and cols <= width and (lane is None or cols % MXU_N == 0)
        chunks += [(w_hbm, w_bf, r0, cols, lane) for r0 in range(0, rows, rb)]

    def slot_view(n):
        s0 = (n % slots) * rb
        return stage.at[s0:s0 + rb, 0:chunks[n][3]]

    def chunk_copy(n):
        w_hbm, r0 = chunks[n][0], chunks[n][2]
        return pltpu.make_async_copy(w_hbm.at[r0:r0 + rb, :], slot_view(n), sem.at[n % slots])

    for n in range(min(slots - 1, len(chunks))):
        chunk_copy(n).start()
    for n, (_, w_bf, r0, cols, lane) in enumerate(chunks):
        chunk_copy(n).wait()
        if n + slots - 1 < len(chunks):
            chunk_copy(n + slots - 1).start()
        if lane is None:
            w_bf[r0:r0 + rb, :] = slot_view(n)[...].astype(BF16)
        else:
            for c0 in range(0, cols, MXU_N):
                d0 = 2 * c0 + lane * MXU_N
                w_bf[r0:r0 + rb, d0:d0 + MXU_N] = slot_view(n)[:, c0:c0 + MXU_N].astype(BF16)


def _prompt_kernel(x_ref, cs_ref, sn_ref, g1_ref, win_hbm, convw_ref, gng_ref, wout_hbm, g2_ref,
                   wg_hbm, wu_hbm, wd_hbm, gf_ref,
                   y_ref, cnew_ref, ret_ref, win_out, wout_out, wgu_out, wd_out,
                   win_ref, wout_ref, wgu_ref, wd_ref, stage_scr, load_sem, store_sem,
                   proj_scr, uext_scr, convy_scr, mix_scr, xkeep_scr, x1_scr, h_scr, h2_scr, act_scr, r_scr,
                   dec_scr, xi_scr, zeta_scr, *, n_tiles, tiles_per_seq):
    tile = x_ref.shape[0]
    i = pl.program_id(0)
    chunk = PROMPT_CHUNK
    assert tile % chunk == 0
    t = jnp.minimum(i, n_tiles - 1) % tiles_per_seq
    seq_start = t == 0
    seq_end = t == tiles_per_seq - 1
    w = RET_WIDTH
    cw = w
    d_ff = wd_ref.shape[0]
    weight_stores = [pltpu.make_async_copy(src, dst, store_sem.at[n]) for n, (src, dst) in enumerate(
        ((win_ref, win_out), (wout_ref, wout_out), (wgu_ref, wgu_out), (wd_ref, wd_out)))]

    def setup():
        _load_weights_bf16(((win_hbm, win_ref, None), (wout_hbm, wout_ref, None), (wg_hbm, wgu_ref, 0),
                            (wu_hbm, wgu_ref, 1), (wd_hbm, wd_ref, None)), stage_scr, load_sem)
        for cp in weight_stores:
            cp.start()
        ii = lax.broadcasted_iota(jnp.int32, (chunk, chunk), 0)
        jj = lax.broadcasted_iota(jnp.int32, (chunk, chunk), 1)
        causal = ii >= jj
        diff = jnp.where(causal, ii - jj, 0).astype(F32)
        i_f = lax.broadcasted_iota(jnp.int32, (chunk, RET_HD), 0).astype(F32)
        for hh in range(RET_HEADS):
            dec_scr[hh] = jnp.where(causal, jnp.exp(diff * LOG_G[hh]), 0.0)
            xi_scr[hh] = jnp.exp((i_f + 1.0) * LOG_G[hh])
            zeta_scr[hh] = jnp.exp((chunk - 1.0 - i_f) * LOG_G[hh])
        r_scr[...] = jnp.zeros_like(r_scr)
        uext_scr[0:SUBLANES, :] = jnp.zeros((SUBLANES, uext_scr.shape[1]), F32)

    def ffn_piece(c0):
        return functools.partial(_swiglu_piece, h2_scr, wgu_ref, act_scr, c0)

    def proj_piece(c0):
        def emit():
            proj_scr[:, c0:c0 + cw] = _dot(h_scr[...], win_ref[:, c0:c0 + cw])
        return emit

    def step(mixer, dense):
        fillers = []

        def fill(n):
            for _ in range(min(n, len(fillers))):
                fillers.pop(0)()

        if dense:
            x1_scr[...] = xkeep_scr[...] + _dot(mix_scr[...], wout_ref[...])
        if mixer:
            x = x_ref[...]
            h_scr[...] = _rmsnorm(x, g1_ref[...]).astype(BF16)
            proj_scr[:, 0:2 * w] = _dot(h_scr[...], win_ref[:, 0:2 * w])
        if dense:
            h2_scr[...] = _rmsnorm(x1_scr[...], g2_ref[...]).astype(BF16)
            fillers += [ffn_piece(c0) for c0 in range(0, d_ff, MXU_N)]
        if not mixer:
            fill(len(fillers))
        else:
            proj_scr[:, 2 * w:4 * w] = _dot(h_scr[...], win_ref[:, 2 * w:4 * w])
            fillers[0:0] = [proj_piece(5 * w), proj_piece(6 * w)]

            n_chunks = tile // chunk
            units = [(c, hh) for c in range(n_chunks) for hh in range(RET_HEADS)]
            qbs, vbs, scores, updates = {}, {}, {}, {}
            for c, hh in units:
                r0, c0 = c * chunk, hh * RET_HD
                cs = cs_ref[r0:r0 + chunk, :]
                sn = sn_ref[r0:r0 + chunk, :]
                q = _rotary(proj_scr[r0:r0 + chunk, c0:c0 + RET_HD], cs, sn)
                k = _rotary(proj_scr[r0:r0 + chunk, w + c0:w + c0 + RET_HD], cs, sn) * (RET_HD ** -0.5)
                qb, kb = q.astype(BF16), k.astype(BF16)
                vb = proj_scr[r0:r0 + chunk, 2 * w + c0:2 * w + c0 + RET_HD].astype(BF16)
                kz = (k * zeta_scr[hh]).astype(BF16)
                qbs[c, hh], vbs[c, hh] = qb, vb
                scores[c, hh] = _dot_nt(qb, kb)
                updates[c, hh] = _dot_tn(kz, vb)
            fill(2)

            u = proj_scr[:, 5 * w:5 * w + cw] * proj_scr[:, 5 * w + cw:5 * w + 2 * cw]
            uext_scr[SUBLANES:SUBLANES + tile, :] = u
            um1 = uext_scr[SUBLANES - 1:SUBLANES - 1 + tile, :]
            um2 = uext_scr[SUBLANES - 2:SUBLANES - 2 + tile, :]
            convy_scr[...] = (convw_ref[0, 0:1, :] * um2 + convw_ref[0, 1:2, :] * um1
                              + convw_ref[0, 2:3, :] * u)
            tail = uext_scr[tile + SUBLANES - (CONV_K - 1):tile + SUBLANES, :]
            cnew_ref[0] = tail
            uext_scr[SUBLANES - (CONV_K - 1):SUBLANES, :] = jnp.where(seq_end, 0.0, tail)
            fill(len(fillers) - len(units) + 1)

            outs = {}
            for hh in range(RET_HEADS):
                state = jnp.where(seq_start, 0.0, r_scr[hh])
                for c in range(n_chunks):
                    s = (scores[c, hh] * dec_scr[hh]).astype(BF16)
                    outs[c, hh] = _dot(s, vbs[c, hh]) + _dot(qbs[c, hh], state.astype(BF16)) * xi_scr[hh]
                    state = state * math.exp(chunk * LOG_G[hh]) + updates[c, hh]
                r_scr[hh] = state
                ret_ref[0, hh] = state
            for c, hh in units:
                fill(1)
                r0, c0 = c * chunk, hh * RET_HD
                g = proj_scr[r0:r0 + chunk, 3 * w + c0:3 * w + c0 + RET_HD]
                mix_scr[r0:r0 + chunk, c0:c0 + RET_HD] = _groupnorm_gate(
                    outs[c, hh], g, gng_ref[:, c0:c0 + RET_HD]).astype(BF16)
            fill(len(fillers))

        if dense:
            down = _dot(act_scr[...], wd_ref[...])
        if mixer:
            proj_piece(4 * w)()
        if dense:
            y_ref[...] = _rmsnorm(x1_scr[...] + down, gf_ref[...])
        if mixer:
            mix_scr[:, w:w + cw] = (proj_scr[:, 4 * w:4 * w + cw] * convy_scr[...]).astype(BF16)
            xkeep_scr[...] = x

    @pl.when(i == 0)
    def _():
        setup()
        step(mixer=True, dense=False)

    @pl.when((i > 0) & (i < n_tiles))
    def _():
        step(mixer=True, dense=True)

    @pl.when(i == n_tiles)
    def _():
        step(mixer=False, dense=True)
        for cp in weight_stores:
            cp.wait()


def _sample_kernel(x_ref, cs_ref, sn_ref, cst_ref, sret_ref, g1_ref, win_hbm, convw_ref, gng_ref,
                   wout_hbm, g2_ref, wgu_hbm, wd_hbm, gf_ref,
                   y_ref, cnew_ref, retnew_ref,
                   win_ref, wout_ref, wgu_ref, wd_ref, load_sem,
                   proj_scr, convy_scr, mix_scr, xkeep_scr, x1_scr, h_scr, h2_scr, act_scr,
                   mask_scr, xi_scr, zeta_scr, *, n_tiles):
    tile = x_ref.shape[0]
    nseq = sret_ref.shape[0]
    seqlen = tile // nseq
    assert seqlen == SUBLANES, "decode sequences must fill exactly one f32 sublane tile"
    i = pl.program_id(0)
    w = RET_WIDTH
    cw = w
    d_ff = wd_ref.shape[0]
    weight_loads = [pltpu.make_async_copy(src, dst, load_sem.at[n]) for n, (src, dst) in enumerate(
        ((win_hbm, win_ref), (wout_hbm, wout_ref), (wgu_hbm, wgu_ref), (wd_hbm, wd_ref)))]

    def setup():
        for cp in weight_loads:
            cp.start()
        ii = lax.broadcasted_iota(jnp.int32, (tile, tile), 0)
        jj = lax.broadcasted_iota(jnp.int32, (tile, tile), 1)
        keep = (ii >= jj) & ((ii // seqlen) == (jj // seqlen))
        diff = jnp.where(keep, ii - jj, 0).astype(F32)
        pos = (lax.broadcasted_iota(jnp.int32, (tile, RET_HD), 0) % seqlen).astype(F32)
        for hh in range(RET_HEADS):
            mask_scr[hh] = jnp.where(keep, jnp.exp(diff * LOG_G[hh]), 0.0)
            xi_scr[hh] = jnp.exp((pos + 1.0) * LOG_G[hh])
            zeta_scr[hh] = jnp.exp((seqlen - 1.0 - pos) * LOG_G[hh])
        weight_loads[0].wait()

    def proj_piece(c0):
        def emit():
            proj_scr[:, c0:c0 + cw] = _dot(h_scr[...], win_ref[:, c0:c0 + cw])
        return emit

    def step(mixer, dense, wait_weights=False):
        fillers = []

        def fill(n):
            for _ in range(min(n, len(fillers))):
                fillers.pop(0)()

        def arrive(n):
            if wait_weights:
                weight_loads[n].wait()

        if dense:
            arrive(1)
            x1_scr[...] = xkeep_scr[...] + _dot(mix_scr[...], wout_ref[...])
        if mixer:
            x = x_ref[...]
            h_scr[...] = _rmsnorm(x, g1_ref[...]).astype(BF16)
            proj_scr[:, 0:4 * w] = _dot(h_scr[...], win_ref[:, 0:4 * w])
        if dense:
            h2_scr[...] = _rmsnorm(x1_scr[...], g2_ref[...]).astype(BF16)
            fillers += [functools.partial(_swiglu_piece, h2_scr, wgu_ref, act_scr, c0)
                        for c0 in range(0, d_ff, MXU_N)]
            fillers[0:1] = [lambda first=fillers[0]: (arrive(2), first())]
        if mixer:
            fillers[0:0] = [proj_piece(5 * w), proj_piece(6 * w)]
            cs = cs_ref[...]
            sn = sn_ref[...]
            pair_rows = 2 * seqlen
            rowblk = lax.broadcasted_iota(jnp.int32, (tile, 2 * RET_HD), 0) // seqlen
            colhalf = lax.broadcasted_iota(jnp.int32, (tile, 2 * RET_HD), 1) // RET_HD
            per_head = -(-len(fillers) // RET_HEADS)
            for hh in range(RET_HEADS):
                c0 = hh * RET_HD
                q = _rotary(proj_scr[:, c0:c0 + RET_HD], cs, sn)
                k = _rotary(proj_scr[:, w + c0:w + c0 + RET_HD], cs, sn) * (RET_HD ** -0.5)
                v = proj_scr[:, 2 * w + c0:2 * w + c0 + RET_HD]
                g = proj_scr[:, 3 * w + c0:3 * w + c0 + RET_HD]
                qb, kb, vb = q.astype(BF16), k.astype(BF16), v.astype(BF16)
                s = _dot_nt(qb, kb) * mask_scr[hh]
                fill(1)
                o_intra = _dot(s.astype(BF16), vb)
                kzt = (k * zeta_scr[hh]).T.astype(BF16)
                vv = jnp.concatenate([v, v], axis=1)
                g_chunk = math.exp(seqlen * LOG_G[hh])
                cross = []
                for p in range(nseq // 2):
                    sa, sb = 2 * p, 2 * p + 1
                    ra = sret_ref[sa, hh]
                    rb = sret_ref[sb, hh]
                    rcat = jnp.concatenate([ra, rb], axis=1).astype(BF16)
                    pr = _dot(qb[p * pair_rows:(p + 1) * pair_rows, :], rcat)
                    cross.append(pr[0:seqlen, 0:RET_HD])
                    cross.append(pr[seqlen:pair_rows, RET_HD:2 * RET_HD])
                    vpair = jnp.where(rowblk == sa + colhalf, vv, 0.0).astype(BF16)
                    upd = _dot(kzt, vpair)
                    retnew_ref[sa, hh] = ra * g_chunk + upd[:, 0:RET_HD]
                    retnew_ref[sb, hh] = rb * g_chunk + upd[:, RET_HD:2 * RET_HD]
                fill(per_head - 1)
                o = o_intra + jnp.concatenate(cross, axis=0) * xi_scr[hh]
                mix_scr[:, c0:c0 + RET_HD] = _groupnorm_gate(o, g, gng_ref[:, c0:c0 + RET_HD]).astype(BF16)

            u = proj_scr[:, 5 * w:5 * w + cw] * proj_scr[:, 5 * w + cw:5 * w + 2 * cw]
            cst = cst_ref[...]
            older = jnp.broadcast_to(cst[:, 0:1, :], (nseq, seqlen, cw)).reshape(tile, cw)
            newer = jnp.broadcast_to(cst[:, 1:2, :], (nseq, seqlen, cw)).reshape(tile, cw)
            r8 = lax.broadcasted_iota(jnp.int32, (tile, cw), 0) % seqlen
            um1 = jnp.where(r8 == 0, newer, pltpu.roll(u, 1, axis=0))
            um2 = jnp.where(r8 == 0, older, jnp.where(r8 == 1, newer, pltpu.roll(u, 2, axis=0)))
            convy_scr[...] = (convw_ref[0, 0:1, :] * um2 + convw_ref[0, 1:2, :] * um1
                              + convw_ref[0, 2:3, :] * u)
            cnew_ref[...] = u.reshape(nseq, seqlen, cw)[:, seqlen - (CONV_K - 1):, :]
        fill(len(fillers))

        if dense:
            arrive(3)
            down = _dot(act_scr[...], wd_ref[...])
        if mixer:
            proj_piece(4 * w)()
        if dense:
            y_ref[...] = _rmsnorm(x1_scr[...] + down, gf_ref[...])
        if mixer:
            mix_scr[:, w:w + cw] = (proj_scr[:, 4 * w:4 * w + cw] * convy_scr[...]).astype(BF16)
            xkeep_scr[...] = x

    assert n_tiles >= 2
    @pl.when(i == 0)
    def _():
        setup()
        step(mixer=True, dense=False)

    @pl.when(i == 1)
    def _():
        step(mixer=True, dense=True, wait_weights=True)

    @pl.when((i > 1) & (i < n_tiles))
    def _():
        step(mixer=True, dense=True)

    @pl.when(i == n_tiles)
    def _():
        step(mixer=False, dense=True)


def _resident(shape):
    nd = len(shape)
    return pl.BlockSpec(shape, lambda *_: (0,) * nd, pipeline_mode=pl.Buffered(1))


def _prompt_call(x, ws):
    batch, seq, d = x.shape
    tile = PROMPT_TILE
    tiles_per_seq = seq // tile
    n_tiles = batch * tiles_per_seq
    cs, sn = _rope_tables(np.arange(seq))
    in_cols = ws[1].shape[1]
    cw = ws[2].shape[-1]
    mixer_tile = lambda i: jnp.minimum(i, n_tiles - 1)
    dense_tile = lambda i: jnp.maximum(i - 1, 0)
    w_in, w_out, w_gate, w_up, w_down = [ws[m] for m in MATRIX_SLOTS]
    assert w_gate.shape == w_up.shape and w_gate.shape[1] == w_down.shape[0]
    d_ff = w_down.shape[0]
    mats = [jax.ShapeDtypeStruct(s, BF16) for s in (w_in.shape, w_out.shape, (d, 2 * d_ff), w_down.shape)]
    weight_specs = [pl.BlockSpec(memory_space=pl.ANY) if m in MATRIX_SLOTS else _resident(wt.shape)
                    for m, wt in enumerate(ws)]
    outs = pl.pallas_call(
        functools.partial(_prompt_kernel, n_tiles=n_tiles, tiles_per_seq=tiles_per_seq),
        grid=(n_tiles + 1,),
        in_specs=[pl.BlockSpec((tile, d), lambda i: (mixer_tile(i), 0)),
                  pl.BlockSpec((tile, RET_HD), lambda i: (mixer_tile(i) % tiles_per_seq, 0)),
                  pl.BlockSpec((tile, RET_HD), lambda i: (mixer_tile(i) % tiles_per_seq, 0))] + weight_specs,
        out_specs=[pl.BlockSpec((tile, d), lambda i: (dense_tile(i), 0)),
                   pl.BlockSpec((1, CONV_K - 1, cw), lambda i: (mixer_tile(i) // tiles_per_seq, 0, 0)),
                   pl.BlockSpec((1, RET_HEADS, RET_HD, RET_HD),
                                lambda i: (mixer_tile(i) // tiles_per_seq, 0, 0, 0))]
                  + [pl.BlockSpec(memory_space=pl.ANY)] * len(mats),
        out_shape=[jax.ShapeDtypeStruct((batch * seq, d), F32),
                   jax.ShapeDtypeStruct((batch, CONV_K - 1, cw), F32),
                   jax.ShapeDtypeStruct((batch, RET_HEADS, RET_HD, RET_HD), F32)]
                  + mats,
        scratch_shapes=[pltpu.VMEM(m.shape, BF16) for m in mats] + [
                        pltpu.VMEM((STAGE_SLOTS * STAGE_ROWS, max(in_cols, d_ff, d)), F32),
                        pltpu.SemaphoreType.DMA((STAGE_SLOTS,)),
                        pltpu.SemaphoreType.DMA((len(mats),)),
                        pltpu.VMEM((tile, in_cols), F32),
                        pltpu.VMEM((tile + SUBLANES, cw), F32),
                        pltpu.VMEM((tile, cw), F32),
                        pltpu.VMEM((tile, d), BF16),
                        pltpu.VMEM((tile, d), F32),
                        pltpu.VMEM((tile, d), F32),
                        pltpu.VMEM((tile, d), BF16),
                        pltpu.VMEM((tile, d), BF16),
                        pltpu.VMEM((tile, d_ff), BF16),
                        pltpu.VMEM((RET_HEADS, RET_HD, RET_HD), F32),
                        pltpu.VMEM((RET_HEADS, PROMPT_CHUNK, PROMPT_CHUNK), F32),
                        pltpu.VMEM((RET_HEADS, PROMPT_CHUNK, RET_HD), F32),
                        pltpu.VMEM((RET_HEADS, PROMPT_CHUNK, RET_HD), F32)],
        compiler_params=pltpu.CompilerParams(dimension_semantics=("arbitrary",),
                                             vmem_limit_bytes=VMEM_LIMIT_BYTES),
        name="prompt_layer",
    )(x.reshape(batch * seq, d), cs, sn, *ws)
    y, cnew, ret, win_bf, wout_bf, wgu_bf, wd_bf = outs
    ws_bf16 = (ws[0], win_bf, ws[2], ws[3], wout_bf, ws[5], wgu_bf, wd_bf, ws[9])
    return y.reshape(batch, seq, d), cnew, ret, ws_bf16


def _sample_call(x, state_conv, state_ret, ws):
    nseq_all, seqlen, d = x.shape
    nseq = SAMPLE_SEQS
    tile = nseq * seqlen
    n_tiles = nseq_all // nseq
    cs, sn = _rope_tables(np.tile(PAST_LEN + np.arange(seqlen), nseq))
    in_cols = ws[1].shape[1]
    cw = ws[2].shape[-1]
    d_ff = ws[7].shape[0]
    matrix_slots = (1, 4, 6, 7)
    mixer_tile = lambda i: jnp.minimum(i, n_tiles - 1)
    dense_tile = lambda i: jnp.maximum(i - 1, 0)
    weight_specs = [pl.BlockSpec(memory_space=pl.ANY) if m in matrix_slots else _resident(wt.shape)
                    for m, wt in enumerate(ws)]
    y, cnew, rnew = pl.pallas_call(
        functools.partial(_sample_kernel, n_tiles=n_tiles),
        grid=(n_tiles + 1,),
        in_specs=[pl.BlockSpec((tile, d), lambda i: (mixer_tile(i), 0)),
                  _resident((tile, RET_HD)), _resident((tile, RET_HD)),
                  pl.BlockSpec((nseq, CONV_K - 1, cw), lambda i: (mixer_tile(i), 0, 0)),
                  pl.BlockSpec((nseq, RET_HEADS, RET_HD, RET_HD), lambda i: (mixer_tile(i), 0, 0, 0))]
                 + weight_specs,
        out_specs=[pl.BlockSpec((tile, d), lambda i: (dense_tile(i), 0)),
                   pl.BlockSpec((nseq, CONV_K - 1, cw), lambda i: (mixer_tile(i), 0, 0)),
                   pl.BlockSpec((nseq, RET_HEADS, RET_HD, RET_HD), lambda i: (mixer_tile(i), 0, 0, 0))],
        out_shape=[jax.ShapeDtypeStruct((nseq_all * seqlen, d), F32),
                   jax.ShapeDtypeStruct((nseq_all, CONV_K - 1, cw), F32),
                   jax.ShapeDtypeStruct((nseq_all, RET_HEADS, RET_HD, RET_HD), F32)],
        scratch_shapes=[pltpu.VMEM(ws[m].shape, BF16) for m in matrix_slots] + [
                        pltpu.SemaphoreType.DMA((len(matrix_slots),)),
                        pltpu.VMEM((tile, in_cols), F32),
                        pltpu.VMEM((tile, cw), F32),
                        pltpu.VMEM((tile, d), BF16),
                        pltpu.VMEM((tile, d), F32),
                        pltpu.VMEM((tile, d), F32),
                        pltpu.VMEM((tile, d), BF16),
                        pltpu.VMEM((tile, d), BF16),
                        pltpu.VMEM((tile, d_ff), BF16),
                        pltpu.VMEM((RET_HEADS, tile, tile), F32),
                        pltpu.VMEM((RET_HEADS, tile, RET_HD), F32),
                        pltpu.VMEM((RET_HEADS, tile, RET_HD), F32)],
        compiler_params=pltpu.CompilerParams(dimension_semantics=("arbitrary",),
                                             vmem_limit_bytes=VMEM_LIMIT_BYTES),
        name="sample_layer",
    )(x.reshape(nseq_all * seqlen, d), cs, sn, state_conv, state_ret, *ws)
    return y.reshape(nseq_all, seqlen, d), cnew, rnew


def kernel(x_prompt, x_sample, state_conv, state_ret, norm1_g, w_in, conv_w, ret_gn_g, w_out, norm2_g,
           w_gate, w_up, w_down, norm_f_g):
    depth = w_in.shape[0]
    assert depth == 1, "the fused layer kernels take a single layer"
    row = lambda g: g.reshape(1, -1)
    ws = (row(norm1_g[0]), w_in[0], conv_w, row(ret_gn_g[0]), w_out[0],
          row(norm2_g[0]), w_gate[0], w_up[0], w_down[0], row(norm_f_g))
    y_p, cnew_p, ret_p, ws_bf16 = _prompt_call(x_prompt, ws)
    y_s, cnew_s, ret_s = _sample_call(x_sample, state_conv[0], state_ret[0], ws_bf16)
    return (y_p, y_s, cnew_p[None], ret_p[None], cnew_s[None], ret_s[None])
```

```python
import functools
import math

import numpy as np
import jax
import jax.numpy as jnp
from jax import lax
from jax.experimental import pallas as pl
from jax.experimental.pallas import tpu as pltpu

F32 = jnp.float32
BF16 = jnp.bfloat16

RET_HEADS = 4
RET_HD = 128
RET_WIDTH = RET_HEADS * RET_HD
CONV_K = 3
ROPE_BASE = 10000.0
NORM_EPS = 1e-6
GN_EPS = 1e-5
PAST_LEN = 16384

LOG_G = tuple(math.log1p(-(2.0 ** (-5.0 - h))) for h in range(RET_HEADS))

SUBLANES = 8
MXU_N = 256
MATRIX_SLOTS = (1, 4, 6, 7, 8)
STAGE_ROWS = 128
STAGE_SLOTS = 8
PROMPT_TILE = 256
PROMPT_CHUNK = 128
SAMPLE_SEQS = 16
VMEM_LIMIT_BYTES = 60 * 1024 * 1024


def _rope_tables(pos):
    half = RET_HD // 2
    inv = ROPE_BASE ** (-np.arange(half, dtype=np.float64) / half)
    ang = np.asarray(pos, np.float64)[:, None] * inv[None, :]
    cos, sin = np.cos(ang), np.sin(ang)
    cs = np.concatenate([cos, cos], axis=1).astype(np.float32)
    sn = np.concatenate([-sin, sin], axis=1).astype(np.float32)
    return jnp.asarray(cs), jnp.asarray(sn)


def _rmsnorm(x, g):
    ms = jnp.mean(x * x, axis=-1, keepdims=True)
    return x * lax.rsqrt(ms + NORM_EPS) * g


def _silu(x):
    return x * (1.0 / (1.0 + jnp.exp(-x)))


def _rotary(t, cs, sn):
    return t * cs + pltpu.roll(t, RET_HD // 2, axis=1) * sn


def _dot(a, b):
    return jnp.dot(a, b, preferred_element_type=F32)


def _dot_nt(a, b):
    return lax.dot_general(a, b, (((1,), (1,)), ((), ())), preferred_element_type=F32)


def _dot_tn(a, b):
    return lax.dot_general(a, b, (((0,), (0,)), ((), ())), preferred_element_type=F32)


def _groupnorm_gate(o, g, gn_g):
    mu = jnp.mean(o, axis=-1, keepdims=True)
    d = o - mu
    var = jnp.mean(d * d, axis=-1, keepdims=True)
    return _silu(g) * (d * lax.rsqrt(var + GN_EPS) * gn_g)


def _swiglu_piece(h_scr, wgu_ref, act_scr, c0):
    r = _dot(h_scr[...], wgu_ref[:, 2 * c0:2 * c0 + 2 * MXU_N])
    act_scr[:, c0:c0 + MXU_N] = (_silu(r[:, 0:MXU_N]) * r[:, MXU_N:2 * MXU_N]).astype(BF16)


def _load_weights_bf16(pairs, stage, sem):
    rb = STAGE_ROWS
    slots, width = stage.shape[0] // rb, stage.shape[1]
    assert slots >= 2 and slots <= sem.shape[0]
    chunks = []
    for w_hbm, w_bf, lane in pairs:
        rows, cols = w_hbm.shape
        assert rows % rb == 0 and cols <= width and (lane is None or cols % MXU_N == 0)
        chunks += [(w_hbm, w_bf, r0, cols, lane) for r0 in range(0, rows, rb)]

    def slot_view(n):
        s0 = (n % slots) * rb
        return stage.at[s0:s0 + rb, 0:chunks[n][3]]

    def chunk_copy(n):
        w_hbm, r0 = chunks[n][0], chunks[n][2]
        return pltpu.make_async_copy(w_hbm.at[r0:r0 + rb, :], slot_view(n), sem.at[n % slots])

    for n in range(min(slots - 1, len(chunks))):
        chunk_copy(n).start()
    for n, (_, w_bf, r0, cols, lane) in enumerate(chunks):
        chunk_copy(n).wait()
        if n + slots - 1 < len(chunks):
            chunk_copy(n + slots - 1).start()
        if lane is None:
            w_bf[r0:r0 + rb, :] = slot_view(n)[...].astype(BF16)
        else:
            for c0 in range(0, cols, MXU_N):
                d0 = 2 * c0 + lane * MXU_N
                w_bf[r0:r0 + rb, d0:d0 + MXU_N] = slot_view(n)[:, c0:c0 + MXU_N].astype(BF16)


def _prompt_kernel(x_ref, cs_ref, sn_ref, g1_ref, win_hbm, convw_ref, gng_ref, wout_hbm, g2_ref,
                   wg_hbm, wu_hbm, wd_hbm, gf_ref,
                   y_ref, cnew_ref, ret_ref, win_out, wout_out, wgu_out, wd_out,
                   win_ref, wout_ref, wgu_ref, wd_ref, stage_scr, load_sem, store_sem,
                   proj_scr, uext_scr, convy_scr, mix_scr, xkeep_scr, x1_scr, h_scr, h2_scr, act_scr, r_scr,
                   dec_scr, xi_scr, zeta_scr, *, n_tiles, tiles_per_seq):
    tile = x_ref.shape[0]
    i = pl.program_id(0)
    chunk = PROMPT_CHUNK
    assert tile % chunk == 0
    t = jnp.minimum(i, n_tiles - 1) % tiles_per_seq
    seq_start = t == 0
    seq_end = t == tiles_per_seq - 1
    w = RET_WIDTH
    cw = w
    d_ff = wd_ref.shape[0]
    weight_stores = [pltpu.make_async_copy(src, dst, store_sem.at[n]) for n, (src, dst) in enumerate(
        ((win_ref, win_out), (wout_ref, wout_out), (wgu_ref, wgu_out), (wd_ref, wd_out)))]

    d_model = wout_ref.shape[0]
    n_col_slots = (stage_scr.shape[1] - d_model) // MXU_N
    n_row_slots = stage_scr.shape[0] // MXU_N
    assert stage_scr.shape[0] >= d_model and n_col_slots >= 2 and n_row_slots >= 2
    assert n_col_slots + n_row_slots <= load_sem.shape[0]
    late, late_groups = [], []
    for r0 in range(0, d_model, MXU_N):
        late.append(("row", wout_hbm.at[r0:r0 + MXU_N, :], wout_ref.at[r0:r0 + MXU_N, :]))
    late_groups.append(len(late))
    for c0 in range(0, d_ff, MXU_N):
        late.append(("col", wg_hbm.at[:, c0:c0 + MXU_N], wgu_ref.at[:, 2 * c0:2 * c0 + MXU_N]))
        late.append(("col", wu_hbm.at[:, c0:c0 + MXU_N], wgu_ref.at[:, 2 * c0 + MXU_N:2 * c0 + 2 * MXU_N]))
        late.append(("row", wd_hbm.at[c0:c0 + MXU_N, :], wd_ref.at[c0:c0 + MXU_N, :]))
        late_groups.append(len(late))
    late_of_kind = {kind: [n for n, c in enumerate(late) if c[0] == kind] for kind in ("row", "col")}
    late_slots = {"row": n_row_slots, "col": n_col_slots}

    def late_stage(n):
        kind = late[n][0]
        slot = late_of_kind[kind].index(n) % late_slots[kind]
        if kind == "col":
            return stage_scr.at[0:d_model, slot * MXU_N:(slot + 1) * MXU_N], load_sem.at[slot]
        c0 = n_col_slots * MXU_N
        return (stage_scr.at[slot * MXU_N:(slot + 1) * MXU_N, c0:c0 + d_model],
                load_sem.at[n_col_slots + slot])

    def late_copy(n):
        view, sem = late_stage(n)
        return pltpu.make_async_copy(late[n][1], view, sem)

    def late_prime():
        for kind, members in late_of_kind.items():
            for n in members[:late_slots[kind]]:
                late_copy(n).start()

    def late_arrive(group):
        lo = late_groups[group - 1] if group else 0
        for n in range(lo, late_groups[group]):
            kind, _, dst = late[n]
            late_copy(n).wait()
            dst[...] = late_stage(n)[0][...].astype(BF16)
            nxt = late_of_kind[kind].index(n) + late_slots[kind]
            if nxt < len(late_of_kind[kind]):
                late_copy(late_of_kind[kind][nxt]).start()

    def setup():
        _load_weights_bf16(((win_hbm, win_ref, None),), stage_scr, load_sem)
        late_prime()
        ii = lax.broadcasted_iota(jnp.int32, (chunk, chunk), 0)
        jj = lax.broadcasted_iota(jnp.int32, (chunk, chunk), 1)
        causal = ii >= jj
        diff = jnp.where(causal, ii - jj, 0).astype(F32)
        i_f = lax.broadcasted_iota(jnp.int32, (chunk, RET_HD), 0).astype(F32)
        for hh in range(RET_HEADS):
            dec_scr[hh] = jnp.where(causal, jnp.exp(diff * LOG_G[hh]), 0.0)
            xi_scr[hh] = jnp.exp((i_f + 1.0) * LOG_G[hh])
            zeta_scr[hh] = jnp.exp((chunk - 1.0 - i_f) * LOG_G[hh])
        r_scr[...] = jnp.zeros_like(r_scr)
        uext_scr[0:SUBLANES, :] = jnp.zeros((SUBLANES, uext_scr.shape[1]), F32)

    def ffn_piece(c0):
        return functools.partial(_swiglu_piece, h2_scr, wgu_ref, act_scr, c0)

    def proj_piece(c0):
        def emit():
            proj_scr[:, c0:c0 + cw] = _dot(h_scr[...], win_ref[:, c0:c0 + cw])
        return emit

    def step(mixer, dense, weights_arriving=False):
        fillers = []

        def fill(n):
            for _ in range(min(n, len(fillers))):
                fillers.pop(0)()

        def with_arrival(group, piece):
            def emit():
                late_arrive(group)
                piece()
            return emit if weights_arriving else piece

        if dense:
            with_arrival(0, lambda: None)()
            x1_scr[...] = xkeep_scr[...] + _dot(mix_scr[...], wout_ref[...])
        if mixer:
            x = x_ref[...]
            h_scr[...] = _rmsnorm(x, g1_ref[...]).astype(BF16)
            proj_scr[:, 0:2 * w] = _dot(h_scr[...], win_ref[:, 0:2 * w])
        if dense:
            h2_scr[...] = _rmsnorm(x1_scr[...], g2_ref[...]).astype(BF16)
            fillers += [with_arrival(1 + c0 // MXU_N, ffn_piece(c0)) for c0 in range(0, d_ff, MXU_N)]
        if not mixer:
            fill(len(fillers))
        else:
            proj_scr[:, 2 * w:4 * w] = _dot(h_scr[...], win_ref[:, 2 * w:4 * w])
            fillers[0:0] = [proj_piece(5 * w), proj_piece(6 * w)]

            n_chunks = tile // chunk
            units = [(c, hh) for c in range(n_chunks) for hh in range(RET_HEADS)]
            qbs, vbs, scores, updates = {}, {}, {}, {}
            for c, hh in units:
                r0, c0 = c * chunk, hh * RET_HD
                cs = cs_ref[r0:r0 + chunk, :]
                sn = sn_ref[r0:r0 + chunk, :]
                q = _rotary(proj_scr[r0:r0 + chunk, c0:c0 + RET_HD], cs, sn)
                k = _rotary(proj_scr[r0:r0 + chunk, w + c0:w + c0 + RET_HD], cs, sn) * (RET_HD ** -0.5)
                qb, kb = q.astype(BF16), k.astype(BF16)
                vb = proj_scr[r0:r0 + chunk, 2 * w + c0:2 * w + c0 + RET_HD].astype(BF16)
                kz = (k * zeta_scr[hh]).astype(BF16)
                qbs[c, hh], vbs[c, hh] = qb, vb
                scores[c, hh] = _dot_nt(qb, kb)
                updates[c, hh] = _dot_tn(kz, vb)
            fill(2)

            u = proj_scr[:, 5 * w:5 * w + cw] * proj_scr[:, 5 * w + cw:5 * w + 2 * cw]
            uext_scr[SUBLANES:SUBLANES + tile, :] = u
            um1 = uext_scr[SUBLANES - 1:SUBLANES - 1 + tile, :]
            um2 = uext_scr[SUBLANES - 2:SUBLANES - 2 + tile, :]
            convy_scr[...] = (convw_ref[0, 0:1, :] * um2 + convw_ref[0, 1:2, :] * um1
                              + convw_ref[0, 2:3, :] * u)
            tail = uext_scr[tile + SUBLANES - (CONV_K - 1):tile + SUBLANES, :]
            cnew_ref[0] = tail
            uext_scr[SUBLANES - (CONV_K - 1):SUBLANES, :] = jnp.where(seq_end, 0.0, tail)
            fill(len(fillers) - len(units) + 1)

            outs = {}
            for hh in range(RET_HEADS):
                state = jnp.where(seq_start, 0.0, r_scr[hh])
                for c in range(n_chunks):
                    s = (scores[c, hh] * dec_scr[hh]).astype(BF16)
                    outs[c, hh] = _dot(s, vbs[c, hh]) + _dot(qbs[c, hh], state.astype(BF16)) * xi_scr[hh]
                    state = state * math.exp(chunk * LOG_G[hh]) + updates[c, hh]
                r_scr[hh] = state
                ret_ref[0, hh] = state
            for c, hh in units:
                fill(1)
                r0, c0 = c * chunk, hh * RET_HD
                g = proj_scr[r0:r0 + chunk, 3 * w + c0:3 * w + c0 + RET_HD]
                mix_scr[r0:r0 + chunk, c0:c0 + RET_HD] = _groupnorm_gate(
                    outs[c, hh], g, gng_ref[:, c0:c0 + RET_HD]).astype(BF16)
            fill(len(fillers))

        if dense:
            down = _dot(act_scr[...], wd_ref[...])
        if mixer:
            proj_piece(4 * w)()
        if dense:
            y_ref[...] = _rmsnorm(x1_scr[...] + down, gf_ref[...])
        if mixer:
            mix_scr[:, w:w + cw] = (proj_scr[:, 4 * w:4 * w + cw] * convy_scr[...]).astype(BF16)
            xkeep_scr[...] = x

    assert n_tiles >= 2
    @pl.when(i == 0)
    def _():
        setup()
        step(mixer=True, dense=False)

    @pl.when(i == 1)
    def _():
        step(mixer=True, dense=True, weights_arriving=True)
        for cp in weight_stores:
            cp.start()

    @pl.when((i > 1) & (i < n_tiles))
    def _():
        step(mixer=True, dense=True)

    @pl.when(i == n_tiles)
    def _():
        step(mixer=False, dense=True)
        for cp in weight_stores:
            cp.wait()


def _sample_kernel(x_ref, cs_ref, sn_ref, cst_ref, sret_ref, g1_ref, win_hbm, convw_ref, gng_ref,
                   wout_hbm, g2_ref, wgu_hbm, wd_hbm, gf_ref,
                   y_ref, cnew_ref, retnew_ref,
                   win_ref, wout_ref, wgu_ref, wd_ref, load_sem,
                   proj_scr, convy_scr, mix_scr, xkeep_scr, x1_scr, h_scr, h2_scr, act_scr,
                   mask_scr, xi_scr, zeta_scr, *, n_tiles):
    tile = x_ref.shape[0]
    nseq = sret_ref.shape[0]
    seqlen = tile // nseq
    assert seqlen == SUBLANES, "decode sequences must fill exactly one f32 sublane tile"
    i = pl.program_id(0)
    w = RET_WIDTH
    cw = w
    d_ff = wd_ref.shape[0]
    weight_loads = [pltpu.make_async_copy(src, dst, load_sem.at[n]) for n, (src, dst) in enumerate(
        ((win_hbm, win_ref), (wout_hbm, wout_ref), (wgu_hbm, wgu_ref), (wd_hbm, wd_ref)))]

    def setup():
        for cp in weight_loads:
            cp.start()
        ii = lax.broadcasted_iota(jnp.int32, (tile, tile), 0)
        jj = lax.broadcasted_iota(jnp.int32, (tile, tile), 1)
        keep = (ii >= jj) & ((ii // seqlen) == (jj // seqlen))
        diff = jnp.where(keep, ii - jj, 0).astype(F32)
        pos = (lax.broadcasted_iota(jnp.int32, (tile, RET_HD), 0) % seqlen).astype(F32)
        for hh in range(RET_HEADS):
            mask_scr[hh] = jnp.where(keep, jnp.exp(diff * LOG_G[hh]), 0.0)
            xi_scr[hh] = jnp.exp((pos + 1.0) * LOG_G[hh])
            zeta_scr[hh] = jnp.exp((seqlen - 1.0 - pos) * LOG_G[hh])
        weight_loads[0].wait()

    def proj_piece(c0):
        def emit():
            proj_scr[:, c0:c0 + cw] = _dot(h_scr[...], win_ref[:, c0:c0 + cw])
        return emit

    def step(mixer, dense, wait_weights=False):
        fillers = []

        def fill(n):
            for _ in range(min(n, len(fillers))):
                fillers.pop(0)()

        def arrive(n):
            if wait_weights:
                weight_loads[n].wait()

        if dense:
            arrive(1)
            x1_scr[...] = xkeep_scr[...] + _dot(mix_scr[...], wout_ref[...])
        if mixer:
            x = x_ref[...]
            h_scr[...] = _rmsnorm(x, g1_ref[...]).astype(BF16)
            proj_scr[:, 0:4 * w] = _dot(h_scr[...], win_ref[:, 0:4 * w])
        if dense:
            h2_scr[...] = _rmsnorm(x1_scr[...], g2_ref[...]).astype(BF16)
            fillers += [functools.partial(_swiglu_piece, h2_scr, wgu_ref, act_scr, c0)
                        for c0 in range(0, d_ff, MXU_N)]
            fillers[0:1] = [lambda first=fillers[0]: (arrive(2), first())]
        if mixer:
            fillers[0:0] = [proj_piece(5 * w), proj_piece(6 * w)]
            cs = cs_ref[...]
            sn = sn_ref[...]
            pair_rows = 2 * seqlen
            rowblk = lax.broadcasted_iota(jnp.int32, (tile, 2 * RET_HD), 0) // seqlen
            colhalf = lax.broadcasted_iota(jnp.int32, (tile, 2 * RET_HD), 1) // RET_HD
            per_head = -(-len(fillers) // RET_HEADS)
            for hh in range(RET_HEADS):
                c0 = hh * RET_HD
                q = _rotary(proj_scr[:, c0:c0 + RET_HD], cs, sn)
                k = _rotary(proj_scr[:, w + c0:w + c0 + RET_HD], cs, sn) * (RET_HD ** -0.5)
                v = proj_scr[:, 2 * w + c0:2 * w + c0 + RET_HD]
                g = proj_scr[:, 3 * w + c0:3 * w + c0 + RET_HD]
                qb, kb, vb = q.astype(BF16), k.astype(BF16), v.astype(BF16)
                s = _dot_nt(qb, kb) * mask_scr[hh]
                fill(1)
                o_intra = _dot(s.astype(BF16), vb)
                kzt = (k * zeta_scr[hh]).T.astype(BF16)
                vv = jnp.concatenate([v, v], axis=1)
                g_chunk = math.exp(seqlen * LOG_G[hh])
                cross = []
                for p in range(nseq // 2):
                    sa, sb = 2 * p, 2 * p + 1
                    ra = sret_ref[sa, hh]
                    rb = sret_ref[sb, hh]
                    rcat = jnp.concatenate([ra, rb], axis=1).astype(BF16)
                    pr = _dot(qb[p * pair_rows:(p + 1) * pair_rows, :], rcat)
                    cross.append(pr[0:seqlen, 0:RET_HD])
                    cross.append(pr[seqlen:pair_rows, RET_HD:2 * RET_HD])
                    vpair = jnp.where(rowblk == sa + colhalf, vv, 0.0).astype(BF16)
                    upd = _dot(kzt, vpair)
                    retnew_ref[sa, hh] = ra * g_chunk + upd[:, 0:RET_HD]
                    retnew_ref[sb, hh] = rb * g_chunk + upd[:, RET_HD:2 * RET_HD]
                fill(per_head - 1)
                o = o_intra + jnp.concatenate(cross, axis=0) * xi_scr[hh]
                mix_scr[:, c0:c0 + RET_HD] = _groupnorm_gate(o, g, gng_ref[:, c0:c0 + RET_HD]).astype(BF16)

            u = proj_scr[:, 5 * w:5 * w + cw] * proj_scr[:, 5 * w + cw:5 * w + 2 * cw]
            cst = cst_ref[...]
            older = jnp.broadcast_to(cst[:, 0:1, :], (nseq, seqlen, cw)).reshape(tile, cw)
            newer = jnp.broadcast_to(cst[:, 1:2, :], (nseq, seqlen, cw)).reshape(tile, cw)
            r8 = lax.broadcasted_iota(jnp.int32, (tile, cw), 0) % seqlen
            um1 = jnp.where(r8 == 0, newer, pltpu.roll(u, 1, axis=0))
            um2 = jnp.where(r8 == 0, older, jnp.where(r8 == 1, newer, pltpu.roll(u, 2, axis=0)))
            convy_scr[...] = (convw_ref[0, 0:1, :] * um2 + convw_ref[0, 1:2, :] * um1
                              + convw_ref[0, 2:3, :] * u)
            cnew_ref[...] = u.reshape(nseq, seqlen, cw)[:, seqlen - (CONV_K - 1):, :]
        fill(len(fillers))

        if dense:
            arrive(3)
            down = _dot(act_scr[...], wd_ref[...])
        if mixer:
            proj_piece(4 * w)()
        if dense:
            y_ref[...] = _rmsnorm(x1_scr[...] + down, gf_ref[...])
        if mixer:
            mix_scr[:, w:w + cw] = (proj_scr[:, 4 * w:4 * w + cw] * convy_scr[...]).astype(BF16)
            xkeep_scr[...] = x

    assert n_tiles >= 2
    @pl.when(i == 0)
    def _():
        setup()
        step(mixer=True, dense=False)

    @pl.when(i == 1)
    def _():
        step(mixer=True, dense=True, wait_weights=True)

    @pl.when((i > 1) & (i < n_tiles))
    def _():
        step(mixer=True, dense=True)

    @pl.when(i == n_tiles)
    def _():
        step(mixer=False, dense=True)


def _resident(shape):
    nd = len(shape)
    return pl.BlockSpec(shape, lambda *_: (0,) * nd, pipeline_mode=pl.Buffered(1))


def _prompt_call(x, ws):
    batch, seq, d = x.shape
    tile = PROMPT_TILE
    tiles_per_seq = seq // tile
    n_tiles = batch * tiles_per_seq
    cs, sn = _rope_tables(np.arange(seq))
    in_cols = ws[1].shape[1]
    cw = ws[2].shape[-1]
    mixer_tile = lambda i: jnp.minimum(i, n_tiles - 1)
    dense_tile = lambda i: jnp.maximum(i - 1, 0)
    w_in, w_out, w_gate, w_up, w_down = [ws[m] for m in MATRIX_SLOTS]
    assert w_gate.shape == w_up.shape and w_gate.shape[1] == w_down.shape[0]
    d_ff = w_down.shape[0]
    mats = [jax.ShapeDtypeStruct(s, BF16) for s in (w_in.shape, w_out.shape, (d, 2 * d_ff), w_down.shape)]
    weight_specs = [pl.BlockSpec(memory_space=pl.ANY) if m in MATRIX_SLOTS else _resident(wt.shape)
                    for m, wt in enumerate(ws)]
    stage_shape = (STAGE_SLOTS * STAGE_ROWS, max(in_cols, d_ff, d))
    late_slots = (stage_shape[1] - d) // MXU_N + stage_shape[0] // MXU_N
    outs = pl.pallas_call(
        functools.partial(_prompt_kernel, n_tiles=n_tiles, tiles_per_seq=tiles_per_seq),
        grid=(n_tiles + 1,),
        in_specs=[pl.BlockSpec((tile, d), lambda i: (mixer_tile(i), 0)),
                  pl.BlockSpec((tile, RET_HD), lambda i: (mixer_tile(i) % tiles_per_seq, 0)),
                  pl.BlockSpec((tile, RET_HD), lambda i: (mixer_tile(i) % tiles_per_seq, 0))] + weight_specs,
        out_specs=[pl.BlockSpec((tile, d), lambda i: (dense_tile(i), 0)),
                   pl.BlockSpec((1, CONV_K - 1, cw), lambda i: (mixer_tile(i) // tiles_per_seq, 0, 0)),
                   pl.BlockSpec((1, RET_HEADS, RET_HD, RET_HD),
                                lambda i: (mixer_tile(i) // tiles_per_seq, 0, 0, 0))]
                  + [pl.BlockSpec(memory_space=pl.ANY)] * len(mats),
        out_shape=[jax.ShapeDtypeStruct((batch * seq, d), F32),
                   jax.ShapeDtypeStruct((batch, CONV_K - 1, cw), F32),
                   jax.ShapeDtypeStruct((batch, RET_HEADS, RET_HD, RET_HD), F32)]
                  + mats,
        scratch_shapes=[pltpu.VMEM(m.shape, BF16) for m in mats] + [
                        pltpu.VMEM(stage_shape, F32),
                        pltpu.SemaphoreType.DMA((max(STAGE_SLOTS, late_slots),)),
                        pltpu.SemaphoreType.DMA((len(mats),)),
                        pltpu.VMEM((tile, in_cols), F32),
                        pltpu.VMEM((tile + SUBLANES, cw), F32),
                        pltpu.VMEM((tile, cw), F32),
                        pltpu.VMEM((tile, d), BF16),
                        pltpu.VMEM((tile, d), F32),
                        pltpu.VMEM((tile, d), F32),
                        pltpu.VMEM((tile, d), BF16),
                        pltpu.VMEM((tile, d), BF16),
                        pltpu.VMEM((tile, d_ff), BF16),
                        pltpu.VMEM((RET_HEADS, RET_HD, RET_HD), F32),
                        pltpu.VMEM((RET_HEADS, PROMPT_CHUNK, PROMPT_CHUNK), F32),
                        pltpu.VMEM((RET_HEADS, PROMPT_CHUNK, RET_HD), F32),
                        pltpu.VMEM((RET_HEADS, PROMPT_CHUNK, RET_HD), F32)],
        compiler_params=pltpu.CompilerParams(dimension_semantics=("arbitrary",),
                                             vmem_limit_bytes=VMEM_LIMIT_BYTES),
        name="prompt_layer",
    )(x.reshape(batch * seq, d), cs, sn, *ws)
    y, cnew, ret, win_bf, wout_bf, wgu_bf, wd_bf = outs
    ws_bf16 = (ws[0], win_bf, ws[2], ws[3], wout_bf, ws[5], wgu_bf, wd_bf, ws[9])
    return y.reshape(batch, seq, d), cnew, ret, ws_bf16


def _sample_call(x, state_conv, state_ret, ws):
    nseq_all, seqlen, d = x.shape
    nseq = SAMPLE_SEQS
    tile = nseq * seqlen
    n_tiles = nseq_all // nseq
    cs, sn = _rope_tables(np.tile(PAST_LEN + np.arange(seqlen), nseq))
    in_cols = ws[1].shape[1]
    cw = ws[2].shape[-1]
    d_ff = ws[7].shape[0]
    matrix_slots = (1, 4, 6, 7)
    mixer_tile = lambda i: jnp.minimum(i, n_tiles - 1)
    dense_tile = lambda i: jnp.maximum(i - 1, 0)
    weight_specs = [pl.BlockSpec(memory_space=pl.ANY) if m in matrix_slots else _resident(wt.shape)
                    for m, wt in enumerate(ws)]
    y, cnew, rnew = pl.pallas_call(
        functools.partial(_sample_kernel, n_tiles=n_tiles),
        grid=(n_tiles + 1,),
        in_specs=[pl.BlockSpec((tile, d), lambda i: (mixer_tile(i), 0)),
                  _resident((tile, RET_HD)), _resident((tile, RET_HD)),
                  pl.BlockSpec((nseq, CONV_K - 1, cw), lambda i: (mixer_tile(i), 0, 0)),
                  pl.BlockSpec((nseq, RET_HEADS, RET_HD, RET_HD), lambda i: (mixer_tile(i), 0, 0, 0))]
                 + weight_specs,
        out_specs=[pl.BlockSpec((tile, d), lambda i: (dense_tile(i), 0)),
                   pl.BlockSpec((nseq, CONV_K - 1, cw), lambda i: (mixer_tile(i), 0, 0)),
                   pl.BlockSpec((nseq, RET_HEADS, RET_HD, RET_HD), lambda i: (mixer_tile(i), 0, 0, 0))],
        out_shape=[jax.ShapeDtypeStruct((nseq_all * seqlen, d), F32),
                   jax.ShapeDtypeStruct((nseq_all, CONV_K - 1, cw), F32),
                   jax.ShapeDtypeStruct((nseq_all, RET_HEADS, RET_HD, RET_HD), F32)],
        scratch_shapes=[pltpu.VMEM(ws[m].shape, BF16) for m in matrix_slots] + [
                        pltpu.SemaphoreType.DMA((len(matrix_slots),)),
                        pltpu.VMEM((tile, in_cols), F32),
                        pltpu.VMEM((tile, cw), F32),
                        pltpu.VMEM((tile, d), BF16),
                        pltpu.VMEM((tile, d), F32),
                        pltpu.VMEM((tile, d), F32),
                        pltpu.VMEM((tile, d), BF16),
                        pltpu.VMEM((tile, d), BF16),
                        pltpu.VMEM((tile, d_ff), BF16),
                        pltpu.VMEM((RET_HEADS, tile, tile), F32),
                        pltpu.VMEM((RET_HEADS, tile, RET_HD), F32),
                        pltpu.VMEM((RET_HEADS, tile, RET_HD), F32)],
        compiler_params=pltpu.CompilerParams(dimension_semantics=("arbitrary",),
                                             vmem_limit_bytes=VMEM_LIMIT_BYTES),
        name="sample_layer",
    )(x.reshape(nseq_all * seqlen, d), cs, sn, state_conv, state_ret, *ws)
    return y.reshape(nseq_all, seqlen, d), cnew, rnew


def kernel(x_prompt, x_sample, state_conv, state_ret, norm1_g, w_in, conv_w, ret_gn_g, w_out, norm2_g,
           w_gate, w_up, w_down, norm_f_g):
    depth = w_in.shape[0]
    assert depth == 1, "the fused layer kernels take a single layer"
    row = lambda g: g.reshape(1, -1)
    ws = (row(norm1_g[0]), w_in[0], conv_w, row(ret_gn_g[0]), w_out[0],
          row(norm2_g[0]), w_gate[0], w_up[0], w_down[0], row(norm_f_g))
    y_p, cnew_p, ret_p, ws_bf16 = _prompt_call(x_prompt, ws)
    y_s, cnew_s, ret_s = _sample_call(x_sample, state_conv[0], state_ret[0], ws_bf16)
    return (y_p, y_s, cnew_p[None], ret_p[None], cnew_s[None], ret_s[None])
```

```python
import functools
import math

import numpy as np
import jax
import jax.numpy as jnp
from jax import lax
from jax.experimental import pallas as pl
from jax.experimental.pallas import tpu as pltpu

F32 = jnp.float32
BF16 = jnp.bfloat16

RET_HEADS = 4
RET_HD = 128
RET_WIDTH = RET_HEADS * RET_HD
CONV_K = 3
ROPE_BASE = 10000.0
NORM_EPS = 1e-6
GN_EPS = 1e-5
PAST_LEN = 16384

LOG_G = tuple(math.log1p(-(2.0 ** (-5.0 - h))) for h in range(RET_HEADS))

SUBLANES = 8
MXU_N = 256
MATRIX_SLOTS = (1, 4, 6, 7, 8)
STAGE_ROWS = 128
STAGE_SLOTS = 8
PROMPT_TILE = 256
PROMPT_CHUNK = 128
SAMPLE_SEQS = 32
DECODE_HALF = 128
VMEM_LIMIT_BYTES = 60 * 1024 * 1024


def _rope_tables(pos):
    half = RET_HD // 2
    inv = ROPE_BASE ** (-np.arange(half, dtype=np.float64) / half)
    ang = np.asarray(pos, np.float64)[:, None] * inv[None, :]
    cos, sin = np.cos(ang), np.sin(ang)
    cs = np.concatenate([cos, cos], axis=1).astype(np.float32)
    sn = np.concatenate([-sin, sin], axis=1).astype(np.float32)
    return jnp.asarray(cs), jnp.asarray(sn)


def _rmsnorm(x, g):
    ms = jnp.mean(x * x, axis=-1, keepdims=True)
    return x * lax.rsqrt(ms + NORM_EPS) * g


def _silu(x):
    return x * (1.0 / (1.0 + jnp.exp(-x)))


def _rotary(t, cs, sn):
    return t * cs + pltpu.roll(t, RET_HD // 2, axis=1) * sn


def _dot(a, b):
    return jnp.dot(a, b, preferred_element_type=F32)


def _dot_nt(a, b):
    return lax.dot_general(a, b, (((1,), (1,)), ((), ())), preferred_element_type=F32)


def _dot_tn(a, b):
    return lax.dot_general(a, b, (((0,), (0,)), ((), ())), preferred_element_type=F32)


def _groupnorm_gate(o, g, gn_g):
    mu = jnp.mean(o, axis=-1, keepdims=True)
    d = o - mu
    var = jnp.mean(d * d, axis=-1, keepdims=True)
    return _silu(g) * (d * lax.rsqrt(var + GN_EPS) * gn_g)


def _swiglu_piece(h_scr, wgu_ref, act_scr, c0):
    r = _dot(h_scr[...], wgu_ref[:, 2 * c0:2 * c0 + 2 * MXU_N])
    act_scr[:, c0:c0 + MXU_N] = (_silu(r[:, 0:MXU_N]) * r[:, MXU_N:2 * MXU_N]).astype(BF16)


def _load_weights_bf16(pairs, stage, sem):
    rb = STAGE_ROWS
    slots, width = stage.shape[0] // rb, stage.shape[1]
    assert slots >= 2 and slots <= sem.shape[0]
    chunks = []
    for w_hbm, w_bf, lane in pairs:
        rows, cols = w_hbm.shape
        assert rows % rb == 0 and cols <= width and (lane is None or cols % MXU_N == 0)
        chunks += [(w_hbm, w_bf, r0, cols, lane) for r0 in range(0, rows, rb)]

    def slot_view(n):
        s0 = (n % slots) * rb
        return stage.at[s0:s0 + rb, 0:chunks[n][3]]

    def chunk_copy(n):
        w_hbm, r0 = chunks[n][0], chunks[n][2]
        return pltpu.make_async_copy(w_hbm.at[r0:r0 + rb, :], slot_view(n), sem.at[n % slots])

    for n in range(min(slots - 1, len(chunks))):
        chunk_copy(n).start()
    for n, (_, w_bf, r0, cols, lane) in enumerate(chunks):
        chunk_copy(n).wait()
        if n + slots - 1 < len(chunks):
            chunk_copy(n + slots - 1).start()
        if lane is None:
            w_bf[r0:r0 + rb, :] = slot_view(n)[...].astype(BF16)
        else:
            for c0 in range(0, cols, MXU_N):
                d0 = 2 * c0 + lane * MXU_N
                w_bf[r0:r0 + rb, d0:d0 + MXU_N] = slot_view(n)[:, c0:c0 + MXU_N].astype(BF16)


def _prompt_kernel(x_ref, cs_ref, sn_ref, g1_ref, win_hbm, convw_ref, gng_ref, wout_hbm, g2_ref,
                   wg_hbm, wu_hbm, wd_hbm, gf_ref,
                   y_ref, cnew_ref, ret_ref, win_out, wout_out, wgu_out, wd_out,
                   win_ref, wout_ref, wgu_ref, wd_ref, stage_scr, load_sem, store_sem,
                   proj_scr, uext_scr, convy_scr, mix_scr, xkeep_scr, x1_scr, h_scr, h2_scr, act_scr, r_scr,
                   dec_scr, xi_scr, zeta_scr, *, n_tiles, tiles_per_seq):
    tile = x_ref.shape[0]
    i = pl.program_id(0)
    chunk = PROMPT_CHUNK
    assert tile % chunk == 0
    t = jnp.minimum(i, n_tiles - 1) % tiles_per_seq
    seq_start = t == 0
    seq_end = t == tiles_per_seq - 1
    w = RET_WIDTH
    cw = w
    d_ff = wd_ref.shape[0]
    weight_stores = [pltpu.make_async_copy(src, dst, store_sem.at[n]) for n, (src, dst) in enumerate(
        ((win_ref, win_out), (wout_ref, wout_out), (wgu_ref, wgu_out), (wd_ref, wd_out)))]

    def setup():
        _load_weights_bf16(((win_hbm, win_ref, None), (wout_hbm, wout_ref, None), (wg_hbm, wgu_ref, 0),
                            (wu_hbm, wgu_ref, 1), (wd_hbm, wd_ref, None)), stage_scr, load_sem)
        for cp in weight_stores:
            cp.start()
        ii = lax.broadcasted_iota(jnp.int32, (chunk, chunk), 0)
        jj = lax.broadcasted_iota(jnp.int32, (chunk, chunk), 1)
        causal = ii >= jj
        diff = jnp.where(causal, ii - jj, 0).astype(F32)
        i_f = lax.broadcasted_iota(jnp.int32, (chunk, RET_HD), 0).astype(F32)
        for hh in range(RET_HEADS):
            dec_scr[hh] = jnp.where(causal, jnp.exp(diff * LOG_G[hh]), 0.0)
            xi_scr[hh] = jnp.exp((i_f + 1.0) * LOG_G[hh])
            zeta_scr[hh] = jnp.exp((chunk - 1.0 - i_f) * LOG_G[hh])
        r_scr[...] = jnp.zeros_like(r_scr)
        uext_scr[0:SUBLANES, :] = jnp.zeros((SUBLANES, uext_scr.shape[1]), F32)

    def ffn_piece(c0):
        return functools.partial(_swiglu_piece, h2_scr, wgu_ref, act_scr, c0)

    def proj_piece(c0):
        def emit():
            proj_scr[:, c0:c0 + cw] = _dot(h_scr[...], win_ref[:, c0:c0 + cw])
        return emit

    def step(mixer, dense):
        fillers = []

        def fill(n):
            for _ in range(min(n, len(fillers))):
                fillers.pop(0)()

        if dense:
            x1_scr[...] = xkeep_scr[...] + _dot(mix_scr[...], wout_ref[...])
        if mixer:
            x = x_ref[...]
            h_scr[...] = _rmsnorm(x, g1_ref[...]).astype(BF16)
            proj_scr[:, 0:2 * w] = _dot(h_scr[...], win_ref[:, 0:2 * w])
        if dense:
            h2_scr[...] = _rmsnorm(x1_scr[...], g2_ref[...]).astype(BF16)
            fillers += [ffn_piece(c0) for c0 in range(0, d_ff, MXU_N)]
        if not mixer:
            fill(len(fillers))
        else:
            proj_scr[:, 2 * w:4 * w] = _dot(h_scr[...], win_ref[:, 2 * w:4 * w])
            fillers[0:0] = [proj_piece(5 * w), proj_piece(6 * w)]

            n_chunks = tile // chunk
            units = [(c, hh) for c in range(n_chunks) for hh in range(RET_HEADS)]
            qbs, vbs, scores, updates = {}, {}, {}, {}
            for c, hh in units:
                r0, c0 = c * chunk, hh * RET_HD
                cs = cs_ref[r0:r0 + chunk, :]
                sn = sn_ref[r0:r0 + chunk, :]
                q = _rotary(proj_scr[r0:r0 + chunk, c0:c0 + RET_HD], cs, sn)
                k = _rotary(proj_scr[r0:r0 + chunk, w + c0:w + c0 + RET_HD], cs, sn) * (RET_HD ** -0.5)
                qb, kb = q.astype(BF16), k.astype(BF16)
                vb = proj_scr[r0:r0 + chunk, 2 * w + c0:2 * w + c0 + RET_HD].astype(BF16)
                kz = (k * zeta_scr[hh]).astype(BF16)
                qbs[c, hh], vbs[c, hh] = qb, vb
                scores[c, hh] = _dot_nt(qb, kb)
                updates[c, hh] = _dot_tn(kz, vb)
            fill(2)

            u = proj_scr[:, 5 * w:5 * w + cw] * proj_scr[:, 5 * w + cw:5 * w + 2 * cw]
            uext_scr[SUBLANES:SUBLANES + tile, :] = u
            um1 = uext_scr[SUBLANES - 1:SUBLANES - 1 + tile, :]
            um2 = uext_scr[SUBLANES - 2:SUBLANES - 2 + tile, :]
            convy_scr[...] = (convw_ref[0, 0:1, :] * um2 + convw_ref[0, 1:2, :] * um1
                              + convw_ref[0, 2:3, :] * u)
            tail = uext_scr[tile + SUBLANES - (CONV_K - 1):tile + SUBLANES, :]
            cnew_ref[0] = tail
            uext_scr[SUBLANES - (CONV_K - 1):SUBLANES, :] = jnp.where(seq_end, 0.0, tail)
            fill(len(fillers) - len(units) + 1)

            outs = {}
            for hh in range(RET_HEADS):
                state = jnp.where(seq_start, 0.0, r_scr[hh])
                for c in range(n_chunks):
                    s = (scores[c, hh] * dec_scr[hh]).astype(BF16)
                    outs[c, hh] = _dot(s, vbs[c, hh]) + _dot(qbs[c, hh], state.astype(BF16)) * xi_scr[hh]
                    state = state * math.exp(chunk * LOG_G[hh]) + updates[c, hh]
                r_scr[hh] = state
                ret_ref[0, hh] = state
            for c, hh in units:
                fill(1)
                r0, c0 = c * chunk, hh * RET_HD
                g = proj_scr[r0:r0 + chunk, 3 * w + c0:3 * w + c0 + RET_HD]
                mix_scr[r0:r0 + chunk, c0:c0 + RET_HD] = _groupnorm_gate(
                    outs[c, hh], g, gng_ref[:, c0:c0 + RET_HD]).astype(BF16)
            fill(len(fillers))

        if dense:
            down = _dot(act_scr[...], wd_ref[...])
        if mixer:
            proj_piece(4 * w)()
        if dense:
            y_ref[...] = _rmsnorm(x1_scr[...] + down, gf_ref[...])
        if mixer:
            mix_scr[:, w:w + cw] = (proj_scr[:, 4 * w:4 * w + cw] * convy_scr[...]).astype(BF16)
            xkeep_scr[...] = x

    @pl.when(i == 0)
    def _():
        setup()
        step(mixer=True, dense=False)

    @pl.when((i > 0) & (i < n_tiles))
    def _():
        step(mixer=True, dense=True)

    @pl.when(i == n_tiles)
    def _():
        step(mixer=False, dense=True)
        for cp in weight_stores:
            cp.wait()


def _sample_kernel(x_ref, cs_ref, sn_ref, cst_ref, sret_hbm, g1_ref, win_hbm, convw_ref, gng_ref,
                   wout_hbm, g2_ref, wgu_hbm, wd_hbm, gf_ref,
                   y_ref, cnew_ref, retnew_hbm,
                   win_ref, wout_ref, wgu_ref, wd_ref, load_sem, sin_scr, sout_scr, in_sem, out_sem,
                   proj_scr, convy_scr, mix_scr, xkeep_scr, x1_scr, h_scr, h2_scr, act_scr,
                   mask_scr, xi_scr, zeta_scr, *, n_tiles):
    tile = x_ref.shape[0]
    nseq = sin_scr.shape[1]
    seqlen = tile // nseq
    half = DECODE_HALF
    n_pairs = RET_HEADS // 2
    assert seqlen == SUBLANES, "decode sequences must fill exactly one f32 sublane tile"
    assert tile % half == 0 and n_pairs == sin_scr.shape[0] == 2 and n_tiles >= 2
    i = pl.program_id(0)
    w = RET_WIDTH
    cw = w
    d_ff = wd_ref.shape[0]
    weight_loads = [pltpu.make_async_copy(src, dst, load_sem.at[n]) for n, (src, dst) in enumerate(
        ((win_hbm, win_ref), (wout_hbm, wout_ref), (wgu_hbm, wgu_ref), (wd_hbm, wd_ref)))]

    def state_fetch(t, g):
        return pltpu.make_async_copy(sret_hbm.at[pl.ds(t * nseq, nseq), 2 * g:2 * g + 2], sin_scr.at[g],
                                     in_sem.at[g])

    def state_store(t, g):
        return pltpu.make_async_copy(sout_scr.at[g], retnew_hbm.at[pl.ds(t * nseq, nseq), 2 * g:2 * g + 2],
                                     out_sem.at[g])

    def setup():
        state_fetch(0, 0).start()
        for cp in weight_loads:
            cp.start()
        ii = lax.broadcasted_iota(jnp.int32, (half, half), 0)
        jj = lax.broadcasted_iota(jnp.int32, (half, half), 1)
        keep = (ii >= jj) & ((ii // seqlen) == (jj // seqlen))
        diff = jnp.where(keep, ii - jj, 0).astype(F32)
        pos = (lax.broadcasted_iota(jnp.int32, (half, RET_HD), 0) % seqlen).astype(F32)
        for hh in range(RET_HEADS):
            mask_scr[hh] = jnp.where(keep, jnp.exp(diff * LOG_G[hh]), 0.0)
            xi_scr[hh] = jnp.exp((pos + 1.0) * LOG_G[hh])
            zeta_scr[hh] = jnp.exp((seqlen - 1.0 - pos) * LOG_G[hh])
        weight_loads[0].wait()

    def proj_piece(c0):
        def emit():
            proj_scr[:, c0:c0 + cw] = _dot(h_scr[...], win_ref[:, c0:c0 + cw])
        return emit

    def step(mixer, dense, wait_weights=False, first=False):
        fillers = []

        def fill(n):
            for _ in range(min(n, len(fillers))):
                fillers.pop(0)()

        def arrive(n):
            if wait_weights:
                weight_loads[n].wait()

        if dense:
            arrive(1)
            x1_scr[...] = xkeep_scr[...] + _dot(mix_scr[...], wout_ref[...])
        if mixer:
            x = x_ref[...]
            h_scr[...] = _rmsnorm(x, g1_ref[...]).astype(BF16)
            proj_scr[:, 0:4 * w] = _dot(h_scr[...], win_ref[:, 0:4 * w])
        if dense:
            h2_scr[...] = _rmsnorm(x1_scr[...], g2_ref[...]).astype(BF16)
            fillers += [functools.partial(_swiglu_piece, h2_scr, wgu_ref, act_scr, c0)
                        for c0 in range(0, d_ff, MXU_N)]
            fillers[0:1] = [lambda first_piece=fillers[0]: (arrive(2), first_piece())]
        if mixer:
            fillers[0:0] = [proj_piece(5 * w), proj_piece(6 * w)]
            pair_rows = 2 * seqlen
            half_seqs = half // seqlen
            rowblk = lax.broadcasted_iota(jnp.int32, (half, 2 * RET_HD), 0) // seqlen
            colhalf = lax.broadcasted_iota(jnp.int32, (half, 2 * RET_HD), 1) // RET_HD
            per_slot = -(-len(fillers) // (2 * RET_HEADS * (tile // half)))
            for g in range(n_pairs):
                state_fetch(i, g).wait()
                if g + 1 < n_pairs:
                    state_fetch(i, g + 1).start()
                else:
                    state_fetch(jnp.minimum(i + 1, n_tiles - 1), 0).start()
                if not first:
                    state_store(i - 1, g).wait()
                for hh in range(2 * g, 2 * g + 2):
                    c0 = hh * RET_HD
                    g_chunk = math.exp(seqlen * LOG_G[hh])
                    for r0 in range(0, tile, half):
                        cs = cs_ref[r0:r0 + half, :]
                        sn = sn_ref[r0:r0 + half, :]
                        q = _rotary(proj_scr[r0:r0 + half, c0:c0 + RET_HD], cs, sn)
                        k = _rotary(proj_scr[r0:r0 + half, w + c0:w + c0 + RET_HD], cs, sn) * (RET_HD ** -0.5)
                        v = proj_scr[r0:r0 + half, 2 * w + c0:2 * w + c0 + RET_HD]
                        gate = proj_scr[r0:r0 + half, 3 * w + c0:3 * w + c0 + RET_HD]
                        qb, kb, vb = q.astype(BF16), k.astype(BF16), v.astype(BF16)
                        s = _dot_nt(qb, kb) * mask_scr[hh]
                        fill(per_slot)
                        o_intra = _dot(s.astype(BF16), vb)
                        kzt = (k * zeta_scr[hh]).T.astype(BF16)
                        vv = jnp.concatenate([v, v], axis=1)
                        cross = []
                        for p in range(half_seqs // 2):
                            sa = r0 // seqlen + 2 * p
                            sb = sa + 1
                            ra = sin_scr[g, sa, hh - 2 * g]
                            rb = sin_scr[g, sb, hh - 2 * g]
                            rcat = jnp.concatenate([ra, rb], axis=1).astype(BF16)
                            pr = _dot(qb[p * pair_rows:(p + 1) * pair_rows, :], rcat)
                            cross.append(pr[0:seqlen, 0:RET_HD])
                            cross.append(pr[seqlen:pair_rows, RET_HD:2 * RET_HD])
                            vpair = jnp.where(rowblk == 2 * p + colhalf, vv, 0.0).astype(BF16)
                            upd = _dot(kzt, vpair)
                            sout_scr[g, sa, hh - 2 * g] = ra * g_chunk + upd[:, 0:RET_HD]
                            sout_scr[g, sb, hh - 2 * g] = rb * g_chunk + upd[:, RET_HD:2 * RET_HD]
                        fill(per_slot)
                        o = o_intra + jnp.concatenate(cross, axis=0) * xi_scr[hh]
                        mix_scr[r0:r0 + half, c0:c0 + RET_HD] = _groupnorm_gate(
                            o, gate, gng_ref[:, c0:c0 + RET_HD]).astype(BF16)
                state_store(i, g).start()

            u = proj_scr[:, 5 * w:5 * w + cw] * proj_scr[:, 5 * w + cw:5 * w + 2 * cw]
            cst = cst_ref[...]
            older = jnp.broadcast_to(cst[:, 0:1, :], (nseq, seqlen, cw)).reshape(tile, cw)
            newer = jnp.broadcast_to(cst[:, 1:2, :], (nseq, seqlen, cw)).reshape(tile, cw)
            r8 = lax.broadcasted_iota(jnp.int32, (tile, cw), 0) % seqlen
            um1 = jnp.where(r8 == 0, newer, pltpu.roll(u, 1, axis=0))
            um2 = jnp.where(r8 == 0, older, jnp.where(r8 == 1, newer, pltpu.roll(u, 2, axis=0)))
            convy_scr[...] = (convw_ref[0, 0:1, :] * um2 + convw_ref[0, 1:2, :] * um1
                              + convw_ref[0, 2:3, :] * u)
            cnew_ref[...] = u.reshape(nseq, seqlen, cw)[:, seqlen - (CONV_K - 1):, :]
        fill(len(fillers))

        if dense:
            arrive(3)
            down = _dot(act_scr[...], wd_ref[...])
        if mixer:
            proj_piece(4 * w)()
        if dense:
            y_ref[...] = _rmsnorm(x1_scr[...] + down, gf_ref[...])
        if mixer:
            mix_scr[:, w:w + cw] = (proj_scr[:, 4 * w:4 * w + cw] * convy_scr[...]).astype(BF16)
            xkeep_scr[...] = x

    @pl.when(i == 0)
    def _():
        setup()
        step(mixer=True, dense=False, first=True)

    @pl.when(i == 1)
    def _():
        step(mixer=True, dense=True, wait_weights=True)

    @pl.when((i > 1) & (i < n_tiles))
    def _():
        step(mixer=True, dense=True)

    @pl.when(i == n_tiles)
    def _():
        state_fetch(n_tiles - 1, 0).wait()
        for g in range(n_pairs):
            state_store(n_tiles - 1, g).wait()
        step(mixer=False, dense=True)


def _resident(shape):
    nd = len(shape)
    return pl.BlockSpec(shape, lambda *_: (0,) * nd, pipeline_mode=pl.Buffered(1))


def _prompt_call(x, ws):
    batch, seq, d = x.shape
    tile = PROMPT_TILE
    tiles_per_seq = seq // tile
    n_tiles = batch * tiles_per_seq
    cs, sn = _rope_tables(np.arange(seq))
    in_cols = ws[1].shape[1]
    cw = ws[2].shape[-1]
    mixer_tile = lambda i: jnp.minimum(i, n_tiles - 1)
    dense_tile = lambda i: jnp.maximum(i - 1, 0)
    w_in, w_out, w_gate, w_up, w_down = [ws[m] for m in MATRIX_SLOTS]
    assert w_gate.shape == w_up.shape and w_gate.shape[1] == w_down.shape[0]
    d_ff = w_down.shape[0]
    mats = [jax.ShapeDtypeStruct(s, BF16) for s in (w_in.shape, w_out.shape, (d, 2 * d_ff), w_down.shape)]
    weight_specs = [pl.BlockSpec(memory_space=pl.ANY) if m in MATRIX_SLOTS else _resident(wt.shape)
                    for m, wt in enumerate(ws)]
    outs = pl.pallas_call(
        functools.partial(_prompt_kernel, n_tiles=n_tiles, tiles_per_seq=tiles_per_seq),
        grid=(n_tiles + 1,),
        in_specs=[pl.BlockSpec((tile, d), lambda i: (mixer_tile(i), 0)),
                  pl.BlockSpec((tile, RET_HD), lambda i: (mixer_tile(i) % tiles_per_seq, 0)),
                  pl.BlockSpec((tile, RET_HD), lambda i: (mixer_tile(i) % tiles_per_seq, 0))] + weight_specs,
        out_specs=[pl.BlockSpec((tile, d), lambda i: (dense_tile(i), 0)),
                   pl.BlockSpec((1, CONV_K - 1, cw), lambda i: (mixer_tile(i) // tiles_per_seq, 0, 0)),
                   pl.BlockSpec((1, RET_HEADS, RET_HD, RET_HD),
                                lambda i: (mixer_tile(i) // tiles_per_seq, 0, 0, 0))]
                  + [pl.BlockSpec(memory_space=pl.ANY)] * len(mats),
        out_shape=[jax.ShapeDtypeStruct((batch * seq, d), F32),
                   jax.ShapeDtypeStruct((batch, CONV_K - 1, cw), F32),
                   jax.ShapeDtypeStruct((batch, RET_HEADS, RET_HD, RET_HD), F32)]
                  + mats,
        scratch_shapes=[pltpu.VMEM(m.shape, BF16) for m in mats] + [
                        pltpu.VMEM((STAGE_SLOTS * STAGE_ROWS, max(in_cols, d_ff, d)), F32),
                        pltpu.SemaphoreType.DMA((STAGE_SLOTS,)),
                        pltpu.SemaphoreType.DMA((len(mats),)),
                        pltpu.VMEM((tile, in_cols), F32),
                        pltpu.VMEM((tile + SUBLANES, cw), F32),
                        pltpu.VMEM((tile, cw), F32),
                        pltpu.VMEM((tile, d), BF16),
                        pltpu.VMEM((tile, d), F32),
                        pltpu.VMEM((tile, d), F32),
                        pltpu.VMEM((tile, d), BF16),
                        pltpu.VMEM((tile, d), BF16),
                        pltpu.VMEM((tile, d_ff), BF16),
                        pltpu.VMEM((RET_HEADS, RET_HD, RET_HD), F32),
                        pltpu.VMEM((RET_HEADS, PROMPT_CHUNK, PROMPT_CHUNK), F32),
                        pltpu.VMEM((RET_HEADS, PROMPT_CHUNK, RET_HD), F32),
                        pltpu.VMEM((RET_HEADS, PROMPT_CHUNK, RET_HD), F32)],
        compiler_params=pltpu.CompilerParams(dimension_semantics=("arbitrary",),
                                             vmem_limit_bytes=VMEM_LIMIT_BYTES),
        name="prompt_layer",
    )(x.reshape(batch * seq, d), cs, sn, *ws)
    y, cnew, ret, win_bf, wout_bf, wgu_bf, wd_bf = outs
    ws_bf16 = (ws[0], win_bf, ws[2], ws[3], wout_bf, ws[5], wgu_bf, wd_bf, ws[9])
    return y.reshape(batch, seq, d), cnew, ret, ws_bf16


def _sample_call(x, state_conv, state_ret, ws):
    nseq_all, seqlen, d = x.shape
    nseq = SAMPLE_SEQS
    tile = nseq * seqlen
    n_tiles = nseq_all // nseq
    cs, sn = _rope_tables(np.tile(PAST_LEN + np.arange(seqlen), nseq))
    in_cols = ws[1].shape[1]
    cw = ws[2].shape[-1]
    d_ff = ws[7].shape[0]
    matrix_slots = (1, 4, 6, 7)
    mixer_tile = lambda i: jnp.minimum(i, n_tiles - 1)
    dense_tile = lambda i: jnp.maximum(i - 1, 0)
    weight_specs = [pl.BlockSpec(memory_space=pl.ANY) if m in matrix_slots else _resident(wt.shape)
                    for m, wt in enumerate(ws)]
    pair_state = (2, nseq, 2, RET_HD, RET_HD)
    y, cnew, rnew = pl.pallas_call(
        functools.partial(_sample_kernel, n_tiles=n_tiles),
        grid=(n_tiles + 1,),
        in_specs=[pl.BlockSpec((tile, d), lambda i: (mixer_tile(i), 0)),
                  _resident((tile, RET_HD)), _resident((tile, RET_HD)),
                  pl.BlockSpec((nseq, CONV_K - 1, cw), lambda i: (mixer_tile(i), 0, 0)),
                  pl.BlockSpec(memory_space=pl.ANY)]
                 + weight_specs,
        out_specs=[pl.BlockSpec((tile, d), lambda i: (dense_tile(i), 0)),
                   pl.BlockSpec((nseq, CONV_K - 1, cw), lambda i: (mixer_tile(i), 0, 0)),
                   pl.BlockSpec(memory_space=pl.ANY)],
        out_shape=[jax.ShapeDtypeStruct((nseq_all * seqlen, d), F32),
                   jax.ShapeDtypeStruct((nseq_all, CONV_K - 1, cw), F32),
                   jax.ShapeDtypeStruct((nseq_all, RET_HEADS, RET_HD, RET_HD), F32)],
        scratch_shapes=[pltpu.VMEM(ws[m].shape, BF16) for m in matrix_slots] + [
                        pltpu.SemaphoreType.DMA((len(matrix_slots),)),
                        pltpu.VMEM(pair_state, F32),
                        pltpu.VMEM(pair_state, F32),
                        pltpu.SemaphoreType.DMA((2,)),
                        pltpu.SemaphoreType.DMA((2,)),
                        pltpu.VMEM((tile, in_cols), F32),
                        pltpu.VMEM((tile, cw), F32),
                        pltpu.VMEM((tile, d), BF16),
                        pltpu.VMEM((tile, d), F32),
                        pltpu.VMEM((tile, d), F32),
                        pltpu.VMEM((tile, d), BF16),
                        pltpu.VMEM((tile, d), BF16),
                        pltpu.VMEM((tile, d_ff), BF16),
                        pltpu.VMEM((RET_HEADS, DECODE_HALF, DECODE_HALF), F32),
                        pltpu.VMEM((RET_HEADS, DECODE_HALF, RET_HD), F32),
                        pltpu.VMEM((RET_HEADS, DECODE_HALF, RET_HD), F32)],
        compiler_params=pltpu.CompilerParams(dimension_semantics=("arbitrary",),
                                             vmem_limit_bytes=VMEM_LIMIT_BYTES),
        name="sample_layer",
    )(x.reshape(nseq_all * seqlen, d), cs, sn, state_conv, state_ret, *ws)
    return y.reshape(nseq_all, seqlen, d), cnew, rnew


def kernel(x_prompt, x_sample, state_conv, state_ret, norm1_g, w_in, conv_w, ret_gn_g, w_out, norm2_g,
           w_gate, w_up, w_down, norm_f_g):
    depth = w_in.shape[0]
    assert depth == 1, "the fused layer kernels take a single layer"
    row = lambda g: g.reshape(1, -1)
    ws = (row(norm1_g[0]), w_in[0], conv_w, row(ret_gn_g[0]), w_out[0],
          row(norm2_g[0]), w_gate[0], w_up[0], w_down[0], row(norm_f_g))
    y_p, cnew_p, ret_p, ws_bf16 = _prompt_call(x_prompt, ws)
    y_s, cnew_s, ret_s = _sample_call(x_sample, state_conv[0], state_ret[0], ws_bf16)
    return (y_p, y_s, cnew_p[None], ret_p[None], cnew_s[None], ret_s[None])
```

```python
import functools
import math

import numpy as np
import jax
import jax.numpy as jnp
from jax import lax
from jax.experimental import pallas as pl
from jax.experimental.pallas import tpu as pltpu

F32 = jnp.float32
BF16 = jnp.bfloat16

RET_HEADS = 4
RET_HD = 128
RET_WIDTH = RET_HEADS * RET_HD
CONV_K = 3
ROPE_BASE = 10000.0
NORM_EPS = 1e-6
GN_EPS = 1e-5
PAST_LEN = 16384

LOG_G = tuple(math.log1p(-(2.0 ** (-5.0 - h))) for h in range(RET_HEADS))

SUBLANES = 8
MXU_N = 256
MATRIX_SLOTS = (1, 4, 6, 7, 8)
STAGE_ROWS = 128
STAGE_SLOTS = 6
PROMPT_TILE = 256
PROMPT_CHUNK = 128
DECODE_PAD = 128
VMEM_LIMIT_BYTES = 60 * 1024 * 1024


def _rope_tables(pos):
    half = RET_HD // 2
    inv = ROPE_BASE ** (-np.arange(half, dtype=np.float64) / half)
    ang = np.asarray(pos, np.float64)[:, None] * inv[None, :]
    cos, sin = np.cos(ang), np.sin(ang)
    cs = np.concatenate([cos, cos], axis=1).astype(np.float32)
    sn = np.concatenate([-sin, sin], axis=1).astype(np.float32)
    return jnp.asarray(cs), jnp.asarray(sn)


def _rmsnorm(x, g):
    ms = jnp.mean(x * x, axis=-1, keepdims=True)
    return x * lax.rsqrt(ms + NORM_EPS) * g


def _silu(x):
    return x * (1.0 / (1.0 + jnp.exp(-x)))


def _rotary(t, cs, sn):
    return t * cs + pltpu.roll(t, RET_HD // 2, axis=1) * sn


def _dot(a, b):
    return jnp.dot(a, b, preferred_element_type=F32)


def _dot_nt(a, b):
    return lax.dot_general(a, b, (((1,), (1,)), ((), ())), preferred_element_type=F32)


def _dot_tn(a, b):
    return lax.dot_general(a, b, (((0,), (0,)), ((), ())), preferred_element_type=F32)


def _groupnorm_gate(o, g, gn_g):
    mu = jnp.mean(o, axis=-1, keepdims=True)
    d = o - mu
    var = jnp.mean(d * d, axis=-1, keepdims=True)
    return _silu(g) * (d * lax.rsqrt(var + GN_EPS) * gn_g)


def _swiglu_piece(h_scr, wgu_ref, act_scr, c0):
    r = _dot(h_scr[...], wgu_ref[:, 2 * c0:2 * c0 + 2 * MXU_N])
    act_scr[:, c0:c0 + MXU_N] = (_silu(r[:, 0:MXU_N]) * r[:, MXU_N:2 * MXU_N]).astype(BF16)


def _load_weights_bf16(pairs, stage, sem):
    rb = STAGE_ROWS
    slots, width = stage.shape[0] // rb, stage.shape[1]
    assert slots >= 2 and slots <= sem.shape[0]
    chunks = []
    for w_hbm, w_bf, lane in pairs:
        rows, cols = w_hbm.shape
        assert rows % rb == 0 and cols <= width and (lane is None or cols % MXU_N == 0)
        chunks += [(w_hbm, w_bf, r0, cols, lane) for r0 in range(0, rows, rb)]

    def slot_view(n):
        s0 = (n % slots) * rb
        return stage.at[s0:s0 + rb, 0:chunks[n][3]]

    def chunk_copy(n):
        w_hbm, r0 = chunks[n][0], chunks[n][2]
        return pltpu.make_async_copy(w_hbm.at[r0:r0 + rb, :], slot_view(n), sem.at[n % slots])

    for n in range(min(slots - 1, len(chunks))):
        chunk_copy(n).start()
    for n, (_, w_bf, r0, cols, lane) in enumerate(chunks):
        chunk_copy(n).wait()
        if n + slots - 1 < len(chunks):
            chunk_copy(n + slots - 1).start()
        if lane is None:
            w_bf[r0:r0 + rb, :] = slot_view(n)[...].astype(BF16)
        else:
            for c0 in range(0, cols, MXU_N):
                d0 = 2 * c0 + lane * MXU_N
                w_bf[r0:r0 + rb, d0:d0 + MXU_N] = slot_view(n)[:, c0:c0 + MXU_N].astype(BF16)


def _layer_kernel(x_ref, cs_ref, sn_ref, xd_ref, csd_ref, snd_ref, cst_ref, sret_ref,
                  g1_ref, win_hbm, convw_ref, gng_ref, wout_hbm, g2_ref, wg_hbm, wu_hbm, wd_hbm, gf_ref,
                  y_ref, cnew_ref, ret_ref, yd_ref, cnewd_ref, retd_ref,
                  win_ref, wout_ref, wgu_ref, wd_ref, stage_scr, load_sem,
                  proj_scr, uext_scr, convy_scr, mix_scr, xkeep_scr, x1_scr, h_scr, h2_scr, act_scr, r_scr,
                  dec_scr, xi_scr, zeta_scr, pad_scr, maskd_scr, xid_scr, zetad_scr,
                  *, n_tiles, tiles_per_seq):
    tile = x_ref.shape[0]
    drows = xd_ref.shape[0]
    nseq = sret_ref.shape[0]
    seqlen = drows // nseq
    rows = tile + drows
    pad = DECODE_PAD
    assert seqlen == SUBLANES, "decode sequences must fill exactly one f32 sublane tile"
    assert nseq == 2 and drows <= pad and tile % 16 == 0 and drows % 16 == 0
    i = pl.program_id(0)
    chunk = PROMPT_CHUNK
    assert tile % chunk == 0
    t = jnp.minimum(i, n_tiles - 1) % tiles_per_seq
    seq_start = t == 0
    seq_end = t == tiles_per_seq - 1
    w = RET_WIDTH
    cw = w
    d_ff = wd_ref.shape[0]

    def setup():
        _load_weights_bf16(((win_hbm, win_ref, None), (wout_hbm, wout_ref, None), (wg_hbm, wgu_ref, 0),
                            (wu_hbm, wgu_ref, 1), (wd_hbm, wd_ref, None)), stage_scr, load_sem)
        ii = lax.broadcasted_iota(jnp.int32, (chunk, chunk), 0)
        jj = lax.broadcasted_iota(jnp.int32, (chunk, chunk), 1)
        causal = ii >= jj
        diff = jnp.where(causal, ii - jj, 0).astype(F32)
        i_f = lax.broadcasted_iota(jnp.int32, (chunk, RET_HD), 0).astype(F32)
        pi = lax.broadcasted_iota(jnp.int32, (pad, pad), 0)
        pj = lax.broadcasted_iota(jnp.int32, (pad, pad), 1)
        keep = (pi >= pj) & ((pi // seqlen) == (pj // seqlen))
        pdiff = jnp.where(keep, pi - pj, 0).astype(F32)
        pos = (lax.broadcasted_iota(jnp.int32, (pad, RET_HD), 0) % seqlen).astype(F32)
        for hh in range(RET_HEADS):
            dec_scr[hh] = jnp.where(causal, jnp.exp(diff * LOG_G[hh]), 0.0)
            xi_scr[hh] = jnp.exp((i_f + 1.0) * LOG_G[hh])
            zeta_scr[hh] = jnp.exp((chunk - 1.0 - i_f) * LOG_G[hh])
            maskd_scr[hh] = jnp.where(keep, jnp.exp(pdiff * LOG_G[hh]), 0.0)
            xid_scr[hh] = jnp.exp((pos + 1.0) * LOG_G[hh])
            zetad_scr[hh] = jnp.exp((seqlen - 1.0 - pos) * LOG_G[hh])
        r_scr[...] = jnp.zeros_like(r_scr)
        uext_scr[0:SUBLANES, :] = jnp.zeros((SUBLANES, uext_scr.shape[1]), F32)
        pad_scr[...] = jnp.zeros_like(pad_scr)

    def ffn_piece(c0):
        return functools.partial(_swiglu_piece, h2_scr, wgu_ref, act_scr, c0)

    def proj_piece(c0):
        def emit():
            proj_scr[:, c0:c0 + cw] = _dot(h_scr[...], win_ref[:, c0:c0 + cw])
        return emit

    def step(mixer, dense):
        fillers = []

        def fill(n):
            for _ in range(min(n, len(fillers))):
                fillers.pop(0)()

        if dense:
            x1_scr[...] = xkeep_scr[...] + _dot(mix_scr[...], wout_ref[...])
        if mixer:
            x = x_ref[...]
            xd = xd_ref[...]
            h_scr[0:tile, :] = _rmsnorm(x, g1_ref[...]).astype(BF16)
            h_scr[tile:rows, :] = _rmsnorm(xd, g1_ref[...]).astype(BF16)
            proj_scr[:, 0:2 * w] = _dot(h_scr[...], win_ref[:, 0:2 * w])
        if dense:
            h2_scr[...] = _rmsnorm(x1_scr[...], g2_ref[...]).astype(BF16)
            fillers += [ffn_piece(c0) for c0 in range(0, d_ff, MXU_N)]
        if not mixer:
            fill(len(fillers))
        else:
            proj_scr[:, 2 * w:4 * w] = _dot(h_scr[...], win_ref[:, 2 * w:4 * w])
            fillers[0:0] = [proj_piece(5 * w), proj_piece(6 * w)]

            n_chunks = tile // chunk
            units = [(c, hh) for c in range(n_chunks) for hh in range(RET_HEADS)]
            qbs, vbs, scores, updates = {}, {}, {}, {}
            for c, hh in units:
                r0, c0 = c * chunk, hh * RET_HD
                cs = cs_ref[r0:r0 + chunk, :]
                sn = sn_ref[r0:r0 + chunk, :]
                q = _rotary(proj_scr[r0:r0 + chunk, c0:c0 + RET_HD], cs, sn)
                k = _rotary(proj_scr[r0:r0 + chunk, w + c0:w + c0 + RET_HD], cs, sn) * (RET_HD ** -0.5)
                qb, kb = q.astype(BF16), k.astype(BF16)
                vb = proj_scr[r0:r0 + chunk, 2 * w + c0:2 * w + c0 + RET_HD].astype(BF16)
                kz = (k * zeta_scr[hh]).astype(BF16)
                qbs[c, hh], vbs[c, hh] = qb, vb
                scores[c, hh] = _dot_nt(qb, kb)
                updates[c, hh] = _dot_tn(kz, vb)

            csd = csd_ref[...]
            snd = snd_ref[...]
            rowblk = lax.broadcasted_iota(jnp.int32, (pad, 2 * RET_HD), 0) // seqlen
            colhalf = lax.broadcasted_iota(jnp.int32, (pad, 2 * RET_HD), 1) // RET_HD
            d_scores, d_vb, d_cross = {}, {}, {}
            for hh in range(RET_HEADS):
                c0 = hh * RET_HD
                pad_scr[0, 0:drows, :] = _rotary(proj_scr[tile:rows, c0:c0 + RET_HD], csd, snd)
                pad_scr[1, 0:drows, :] = (_rotary(proj_scr[tile:rows, w + c0:w + c0 + RET_HD], csd, snd)
                                          * (RET_HD ** -0.5))
                pad_scr[2, 0:drows, :] = proj_scr[tile:rows, 2 * w + c0:2 * w + c0 + RET_HD]
                q, k, v = pad_scr[0], pad_scr[1], pad_scr[2]
                qb, kb, vb = q.astype(BF16), k.astype(BF16), v.astype(BF16)
                d_scores[hh] = _dot_nt(qb, kb)
                d_vb[hh] = vb
                kzt = (k * zetad_scr[hh]).T.astype(BF16)
                ra = sret_ref[0, hh]
                rb = sret_ref[1, hh]
                rcat = jnp.concatenate([ra, rb], axis=1).astype(BF16)
                pr = _dot(qb[0:drows, :], rcat)
                d_cross[hh] = jnp.concatenate([pr[0:seqlen, 0:RET_HD], pr[seqlen:drows, RET_HD:2 * RET_HD]],
                                              axis=0)
                vpair = jnp.where(rowblk == colhalf, jnp.concatenate([v, v], axis=1), 0.0).astype(BF16)
                upd = _dot(kzt, vpair)
                g_chunk = math.exp(seqlen * LOG_G[hh])
                retd_ref[0, hh] = ra * g_chunk + upd[:, 0:RET_HD]
                retd_ref[1, hh] = rb * g_chunk + upd[:, RET_HD:2 * RET_HD]
            fill(2)

            u = proj_scr[0:tile, 5 * w:5 * w + cw] * proj_scr[0:tile, 5 * w + cw:5 * w + 2 * cw]
            uext_scr[SUBLANES:SUBLANES + tile, :] = u
            um1 = uext_scr[SUBLANES - 1:SUBLANES - 1 + tile, :]
            um2 = uext_scr[SUBLANES - 2:SUBLANES - 2 + tile, :]
            convy_scr[0:tile, :] = (convw_ref[0, 0:1, :] * um2 + convw_ref[0, 1:2, :] * um1
                                    + convw_ref[0, 2:3, :] * u)
            tail = uext_scr[tile + SUBLANES - (CONV_K - 1):tile + SUBLANES, :]
            cnew_ref[0] = tail
            uext_scr[SUBLANES - (CONV_K - 1):SUBLANES, :] = jnp.where(seq_end, 0.0, tail)

            ud = proj_scr[tile:rows, 5 * w:5 * w + cw] * proj_scr[tile:rows, 5 * w + cw:5 * w + 2 * cw]
            cst = cst_ref[...]
            older = jnp.broadcast_to(cst[:, 0:1, :], (nseq, seqlen, cw)).reshape(drows, cw)
            newer = jnp.broadcast_to(cst[:, 1:2, :], (nseq, seqlen, cw)).reshape(drows, cw)
            r8 = lax.broadcasted_iota(jnp.int32, (drows, cw), 0) % seqlen
            ud1 = jnp.where(r8 == 0, newer, pltpu.roll(ud, 1, axis=0))
            ud2 = jnp.where(r8 == 0, older, jnp.where(r8 == 1, newer, pltpu.roll(ud, 2, axis=0)))
            convy_scr[tile:rows, :] = (convw_ref[0, 0:1, :] * ud2 + convw_ref[0, 1:2, :] * ud1
                                       + convw_ref[0, 2:3, :] * ud)
            cnewd_ref[...] = ud.reshape(nseq, seqlen, cw)[:, seqlen - (CONV_K - 1):, :]
            fill(len(fillers) - len(units) + 1)

            outs = {}
            for hh in range(RET_HEADS):
                state = jnp.where(seq_start, 0.0, r_scr[hh])
                for c in range(n_chunks):
                    s = (scores[c, hh] * dec_scr[hh]).astype(BF16)
                    outs[c, hh] = _dot(s, vbs[c, hh]) + _dot(qbs[c, hh], state.astype(BF16)) * xi_scr[hh]
                    state = state * math.exp(chunk * LOG_G[hh]) + updates[c, hh]
                r_scr[hh] = state
                ret_ref[0, hh] = state
            d_outs = {}
            for hh in range(RET_HEADS):
                s = (d_scores[hh] * maskd_scr[hh]).astype(BF16)
                d_outs[hh] = _dot(s, d_vb[hh])[0:drows, :] + d_cross[hh] * xid_scr[hh, 0:drows, :]
            for c, hh in units:
                fill(1)
                r0, c0 = c * chunk, hh * RET_HD
                g = proj_scr[r0:r0 + chunk, 3 * w + c0:3 * w + c0 + RET_HD]
                mix_scr[r0:r0 + chunk, c0:c0 + RET_HD] = _groupnorm_gate(
                    outs[c, hh], g, gng_ref[:, c0:c0 + RET_HD]).astype(BF16)
            for hh in range(RET_HEADS):
                c0 = hh * RET_HD
                g = proj_scr[tile:rows, 3 * w + c0:3 * w + c0 + RET_HD]
                mix_scr[tile:rows, c0:c0 + RET_HD] = _groupnorm_gate(
                    d_outs[hh], g, gng_ref[:, c0:c0 + RET_HD]).astype(BF16)
            fill(len(fillers))

        if dense:
            down = _dot(act_scr[...], wd_ref[...])
        if mixer:
            proj_piece(4 * w)()
        if dense:
            y = _rmsnorm(x1_scr[...] + down, gf_ref[...])
            y_ref[...] = y[0:tile, :]
            yd_ref[...] = y[tile:rows, :]
        if mixer:
            mix_scr[:, w:w + cw] = (proj_scr[:, 4 * w:4 * w + cw] * convy_scr[...]).astype(BF16)
            xkeep_scr[0:tile, :] = x
            xkeep_scr[tile:rows, :] = xd

    @pl.when(i == 0)
    def _():
        setup()
        step(mixer=True, dense=False)

    @pl.when((i > 0) & (i < n_tiles))
    def _():
        step(mixer=True, dense=True)

    @pl.when(i == n_tiles)
    def _():
        step(mixer=False, dense=True)


def _resident(shape):
    nd = len(shape)
    return pl.BlockSpec(shape, lambda *_: (0,) * nd, pipeline_mode=pl.Buffered(1))


def _layer_call(x, xd, state_conv, state_ret, ws):
    batch, seq, d = x.shape
    nseq_all, seqlen, _ = xd.shape
    tile = PROMPT_TILE
    tiles_per_seq = seq // tile
    n_tiles = batch * tiles_per_seq
    assert nseq_all % n_tiles == 0, "every prompt tile carries the same number of decode sequences"
    nseq = nseq_all // n_tiles
    drows = nseq * seqlen
    rows = tile + drows
    cs, sn = _rope_tables(np.arange(seq))
    csd, snd = _rope_tables(np.tile(PAST_LEN + np.arange(seqlen), nseq))
    in_cols = ws[1].shape[1]
    cw = ws[2].shape[-1]
    mixer_tile = lambda i: jnp.minimum(i, n_tiles - 1)
    dense_tile = lambda i: jnp.maximum(i - 1, 0)
    w_in, w_out, w_gate, w_up, w_down = [ws[m] for m in MATRIX_SLOTS]
    assert w_gate.shape == w_up.shape and w_gate.shape[1] == w_down.shape[0]
    d_ff = w_down.shape[0]
    mats = [w_in.shape, w_out.shape, (d, 2 * d_ff), w_down.shape]
    weight_specs = [pl.BlockSpec(memory_space=pl.ANY) if m in MATRIX_SLOTS else _resident(wt.shape)
                    for m, wt in enumerate(ws)]
    state_block = (nseq, RET_HEADS, RET_HD, RET_HD)
    outs = pl.pallas_call(
        functools.partial(_layer_kernel, n_tiles=n_tiles, tiles_per_seq=tiles_per_seq),
        grid=(n_tiles + 1,),
        in_specs=[pl.BlockSpec((tile, d), lambda i: (mixer_tile(i), 0)),
                  pl.BlockSpec((tile, RET_HD), lambda i: (mixer_tile(i) % tiles_per_seq, 0)),
                  pl.BlockSpec((tile, RET_HD), lambda i: (mixer_tile(i) % tiles_per_seq, 0)),
                  pl.BlockSpec((drows, d), lambda i: (mixer_tile(i), 0)),
                  _resident((drows, RET_HD)), _resident((drows, RET_HD)),
                  pl.BlockSpec((nseq, CONV_K - 1, cw), lambda i: (mixer_tile(i), 0, 0)),
                  pl.BlockSpec(state_block, lambda i: (mixer_tile(i), 0, 0, 0))] + weight_specs,
        out_specs=[pl.BlockSpec((tile, d), lambda i: (dense_tile(i), 0)),
                   pl.BlockSpec((1, CONV_K - 1, cw), lambda i: (mixer_tile(i) // tiles_per_seq, 0, 0)),
                   pl.BlockSpec((1, RET_HEADS, RET_HD, RET_HD),
                                lambda i: (mixer_tile(i) // tiles_per_seq, 0, 0, 0)),
                   pl.BlockSpec((drows, d), lambda i: (dense_tile(i), 0)),
                   pl.BlockSpec((nseq, CONV_K - 1, cw), lambda i: (mixer_tile(i), 0, 0)),
                   pl.BlockSpec(state_block, lambda i: (mixer_tile(i), 0, 0, 0))],
        out_shape=[jax.ShapeDtypeStruct((batch * seq, d), F32),
                   jax.ShapeDtypeStruct((batch, CONV_K - 1, cw), F32),
                   jax.ShapeDtypeStruct((batch, RET_HEADS, RET_HD, RET_HD), F32),
                   jax.ShapeDtypeStruct((nseq_all * seqlen, d), F32),
                   jax.ShapeDtypeStruct((nseq_all, CONV_K - 1, cw), F32),
                   jax.ShapeDtypeStruct((nseq_all, RET_HEADS, RET_HD, RET_HD), F32)],
        scratch_shapes=[pltpu.VMEM(s, BF16) for s in mats] + [
                        pltpu.VMEM((STAGE_SLOTS * STAGE_ROWS, max(in_cols, d_ff, d)), F32),
                        pltpu.SemaphoreType.DMA((STAGE_SLOTS,)),
                        pltpu.VMEM((rows, in_cols), F32),
                        pltpu.VMEM((tile + SUBLANES, cw), F32),
                        pltpu.VMEM((rows, cw), F32),
                        pltpu.VMEM((rows, d), BF16),
                        pltpu.VMEM((rows, d), F32),
                        pltpu.VMEM((rows, d), F32),
                        pltpu.VMEM((rows, d), BF16),
                        pltpu.VMEM((rows, d), BF16),
                        pltpu.VMEM((rows, d_ff), BF16),
                        pltpu.VMEM((RET_HEADS, RET_HD, RET_HD), F32),
                        pltpu.VMEM((RET_HEADS, PROMPT_CHUNK, PROMPT_CHUNK), F32),
                        pltpu.VMEM((RET_HEADS, PROMPT_CHUNK, RET_HD), F32),
                        pltpu.VMEM((RET_HEADS, PROMPT_CHUNK, RET_HD), F32),
                        pltpu.VMEM((3, DECODE_PAD, RET_HD), F32),
                        pltpu.VMEM((RET_HEADS, DECODE_PAD, DECODE_PAD), F32),
                        pltpu.VMEM((RET_HEADS, DECODE_PAD, RET_HD), F32),
                        pltpu.VMEM((RET_HEADS, DECODE_PAD, RET_HD), F32)],
        compiler_params=pltpu.CompilerParams(dimension_semantics=("arbitrary",),
                                             vmem_limit_bytes=VMEM_LIMIT_BYTES),
        name="layer",
    )(x.reshape(batch * seq, d), cs, sn, xd.reshape(nseq_all * seqlen, d), csd, snd, state_conv, state_ret,
      *ws)
    y, cnew, ret, yd, cnewd, retd = outs
    return y.reshape(batch, seq, d), yd.reshape(nseq_all, seqlen, d), cnew, ret, cnewd, retd


def kernel(x_prompt, x_sample, state_conv, state_ret, norm1_g, w_in, conv_w, ret_gn_g, w_out, norm2_g,
           w_gate, w_up, w_down, norm_f_g):
    depth = w_in.shape[0]
    assert depth == 1, "the fused layer kernel takes a single layer"
    row = lambda g: g.reshape(1, -1)
    ws = (row(norm1_g[0]), w_in[0], conv_w, row(ret_gn_g[0]), w_out[0],
          row(norm2_g[0]), w_gate[0], w_up[0], w_down[0], row(norm_f_g))
    y_p, y_s, cnew_p, ret_p, cnew_s, ret_s = _layer_call(x_prompt, x_sample, state_conv[0], state_ret[0], ws)
    return (y_p, y_s, cnew_p[None], ret_p[None], cnew_s[None], ret_s[None])
```

```python
import functools
import math

import numpy as np
import jax
import jax.numpy as jnp
from jax import lax
from jax.experimental import pallas as pl
from jax.experimental.pallas import tpu as pltpu

F32 = jnp.float32
BF16 = jnp.bfloat16

RET_HEADS = 4
RET_HD = 128
RET_WIDTH = RET_HEADS * RET_HD
CONV_K = 3
ROPE_BASE = 10000.0
NORM_EPS = 1e-6
GN_EPS = 1e-5
PAST_LEN = 16384

LOG_G = tuple(math.log1p(-(2.0 ** (-5.0 - h))) for h in range(RET_HEADS))

SUBLANES = 8
MXU_N = 256
MATRIX_SLOTS = (1, 4, 6, 7, 8)
STAGE_ROWS = 128
STAGE_SLOTS = 6
PROMPT_TILE = 256
PROMPT_CHUNK = 128
DECODE_PAD = 128
VMEM_LIMIT_BYTES = 60 * 1024 * 1024


def _rope_tables(pos):
    half = RET_HD // 2
    inv = ROPE_BASE ** (-np.arange(half, dtype=np.float64) / half)
    ang = np.asarray(pos, np.float64)[:, None] * inv[None, :]
    cos, sin = np.cos(ang), np.sin(ang)
    cs = np.concatenate([cos, cos], axis=1).astype(np.float32)
    sn = np.concatenate([-sin, sin], axis=1).astype(np.float32)
    return jnp.asarray(cs), jnp.asarray(sn)


def _rmsnorm(x, g):
    ms = jnp.mean(x * x, axis=-1, keepdims=True)
    return x * lax.rsqrt(ms + NORM_EPS) * g


def _silu(x):
    return x * (1.0 / (1.0 + jnp.exp(-x)))


def _rotary(t, cs, sn):
    return t * cs + pltpu.roll(t, RET_HD // 2, axis=1) * sn


def _dot(a, b):
    return jnp.dot(a, b, preferred_element_type=F32)


def _dot_nt(a, b):
    return lax.dot_general(a, b, (((1,), (1,)), ((), ())), preferred_element_type=F32)


def _dot_tn(a, b):
    return lax.dot_general(a, b, (((0,), (0,)), ((), ())), preferred_element_type=F32)


def _groupnorm_gate(o, g, gn_g):
    mu = jnp.mean(o, axis=-1, keepdims=True)
    d = o - mu
    var = jnp.mean(d * d, axis=-1, keepdims=True)
    return _silu(g) * (d * lax.rsqrt(var + GN_EPS) * gn_g)


def _swiglu_piece(h_scr, wgu_ref, act_scr, c0):
    r = _dot(h_scr[...], wgu_ref[:, 2 * c0:2 * c0 + 2 * MXU_N])
    act_scr[:, c0:c0 + MXU_N] = (_silu(r[:, 0:MXU_N]) * r[:, MXU_N:2 * MXU_N]).astype(BF16)


def _load_weights_bf16(pairs, stage, sem):
    rb = STAGE_ROWS
    slots, width = stage.shape[0] // rb, stage.shape[1]
    assert slots >= 2 and slots <= sem.shape[0]
    chunks = []
    for w_hbm, w_bf, lane in pairs:
        rows, cols = w_hbm.shape
        assert rows % rb == 0 and cols <= width and (lane is None or cols % MXU_N == 0)
        chunks += [(w_hbm, w_bf, r0, cols, lane) for r0 in range(0, rows, rb)]

    def slot_view(n):
        s0 = (n % slots) * rb
        return stage.at[s0:s0 + rb, 0:chunks[n][3]]

    def chunk_copy(n):
        w_hbm, r0 = chunks[n][0], chunks[n][2]
        return pltpu.make_async_copy(w_hbm.at[r0:r0 + rb, :], slot_view(n), sem.at[n % slots])

    for n in range(min(slots - 1, len(chunks))):
        chunk_copy(n).start()
    for n, (_, w_bf, r0, cols, lane) in enumerate(chunks):
        chunk_copy(n).wait()
        if n + slots - 1 < len(chunks):
            chunk_copy(n + slots - 1).start()
        if lane is None:
            w_bf[r0:r0 + rb, :] = slot_view(n)[...].astype(BF16)
        else:
            for c0 in range(0, cols, MXU_N):
                d0 = 2 * c0 + lane * MXU_N
                w_bf[r0:r0 + rb, d0:d0 + MXU_N] = slot_view(n)[:, c0:c0 + MXU_N].astype(BF16)


def _layer_kernel(x_ref, cs_ref, sn_ref, xd_ref, csd_ref, snd_ref, cst_ref, sret_ref,
                  g1_ref, win_hbm, convw_ref, gng_ref, wout_hbm, g2_ref, wg_hbm, wu_hbm, wd_hbm, gf_ref,
                  y_ref, cnew_ref, ret_ref, yd_ref, cnewd_ref, retd_ref,
                  win_ref, wout_ref, wgu_ref, wd_ref, stage_scr, load_sem,
                  proj_scr, uext_scr, convy_scr, mix_scr, xkeep_scr, x1_scr, h_scr, h2_scr, act_scr, r_scr,
                  dec_scr, xi_scr, zeta_scr, pad_scr, maskd_scr, xid_scr, zetad_scr,
                  *, n_tiles, tiles_per_seq):
    tile = x_ref.shape[0]
    drows = xd_ref.shape[0]
    nseq = sret_ref.shape[0]
    seqlen = drows // nseq
    rows = tile + drows
    pad = DECODE_PAD
    assert seqlen == SUBLANES, "decode sequences must fill exactly one f32 sublane tile"
    assert nseq == 2 and drows <= pad and tile % 16 == 0 and drows % 16 == 0
    i = pl.program_id(0)
    chunk = PROMPT_CHUNK
    assert tile % chunk == 0
    t = jnp.minimum(i, n_tiles - 1) % tiles_per_seq
    seq_start = t == 0
    seq_end = t == tiles_per_seq - 1
    w = RET_WIDTH
    cw = w
    d_ff = wd_ref.shape[0]

    def setup():
        _load_weights_bf16(((win_hbm, win_ref, None), (wout_hbm, wout_ref, None), (wg_hbm, wgu_ref, 0),
                            (wu_hbm, wgu_ref, 1), (wd_hbm, wd_ref, None)), stage_scr, load_sem)
        ii = lax.broadcasted_iota(jnp.int32, (chunk, chunk), 0)
        jj = lax.broadcasted_iota(jnp.int32, (chunk, chunk), 1)
        causal = ii >= jj
        diff = jnp.where(causal, ii - jj, 0).astype(F32)
        i_f = lax.broadcasted_iota(jnp.int32, (chunk, RET_HD), 0).astype(F32)
        pi = lax.broadcasted_iota(jnp.int32, (pad, pad), 0)
        pj = lax.broadcasted_iota(jnp.int32, (pad, pad), 1)
        keep = (pi >= pj) & ((pi // seqlen) == (pj // seqlen))
        pdiff = jnp.where(keep, pi - pj, 0).astype(F32)
        pos = (lax.broadcasted_iota(jnp.int32, (pad, RET_HD), 0) % seqlen).astype(F32)
        for hh in range(RET_HEADS):
            dec_scr[hh] = jnp.where(causal, jnp.exp(diff * LOG_G[hh]), 0.0)
            xi_scr[hh] = jnp.exp((i_f + 1.0) * LOG_G[hh])
            zeta_scr[hh] = jnp.exp((chunk - 1.0 - i_f) * LOG_G[hh])
            maskd_scr[hh] = jnp.where(keep, jnp.exp(pdiff * LOG_G[hh]), 0.0)
            xid_scr[hh] = jnp.exp((pos + 1.0) * LOG_G[hh])
            zetad_scr[hh] = jnp.exp((seqlen - 1.0 - pos) * LOG_G[hh])
        r_scr[...] = jnp.zeros_like(r_scr)
        uext_scr[0:SUBLANES, :] = jnp.zeros((SUBLANES, uext_scr.shape[1]), F32)
        pad_scr[...] = jnp.zeros_like(pad_scr)

    def ffn_piece(c0):
        return functools.partial(_swiglu_piece, h2_scr, wgu_ref, act_scr, c0)

    def proj_piece(c0):
        def emit():
            proj_scr[:, c0:c0 + cw] = _dot(h_scr[...], win_ref[:, c0:c0 + cw])
        return emit

    def step(mixer, dense):
        fillers = []

        def fill(n):
            for _ in range(min(n, len(fillers))):
                fillers.pop(0)()

        if dense:
            x1_scr[...] = xkeep_scr[...] + _dot(mix_scr[...], wout_ref[...])
        if mixer:
            x = x_ref[...]
            xd = xd_ref[...]
            h_scr[0:tile, :] = _rmsnorm(x, g1_ref[...]).astype(BF16)
            h_scr[tile:rows, :] = _rmsnorm(xd, g1_ref[...]).astype(BF16)
            proj_scr[:, 0:2 * w] = _dot(h_scr[...], win_ref[:, 0:2 * w])
        if dense:
            h2_scr[...] = _rmsnorm(x1_scr[...], g2_ref[...]).astype(BF16)
            fillers += [ffn_piece(c0) for c0 in range(0, d_ff, MXU_N)]
        if not mixer:
            fill(len(fillers))
        else:
            proj_scr[:, 2 * w:4 * w] = _dot(h_scr[...], win_ref[:, 2 * w:4 * w])
            fillers[0:0] = [proj_piece(5 * w), proj_piece(6 * w)]

            n_chunks = tile // chunk
            units = [(c, hh) for c in range(n_chunks) for hh in range(RET_HEADS)]
            qbs, vbs, scores, updates = {}, {}, {}, {}
            for c, hh in units:
                r0, c0 = c * chunk, hh * RET_HD
                cs = cs_ref[r0:r0 + chunk, :]
                sn = sn_ref[r0:r0 + chunk, :]
                q = _rotary(proj_scr[r0:r0 + chunk, c0:c0 + RET_HD], cs, sn)
                k = _rotary(proj_scr[r0:r0 + chunk, w + c0:w + c0 + RET_HD], cs, sn) * (RET_HD ** -0.5)
                qb, kb = q.astype(BF16), k.astype(BF16)
                vb = proj_scr[r0:r0 + chunk, 2 * w + c0:2 * w + c0 + RET_HD].astype(BF16)
                kz = (k * zeta_scr[hh]).astype(BF16)
                qbs[c, hh], vbs[c, hh] = qb, vb
                scores[c, hh] = _dot_nt(qb, kb)
                updates[c, hh] = _dot_tn(kz, vb)

            csd = csd_ref[...]
            snd = snd_ref[...]
            rowblk = lax.broadcasted_iota(jnp.int32, (pad, 2 * RET_HD), 0) // seqlen
            colhalf = lax.broadcasted_iota(jnp.int32, (pad, 2 * RET_HD), 1) // RET_HD
            d_scores, d_vb, d_cross = {}, {}, {}
            for hh in range(RET_HEADS):
                c0 = hh * RET_HD
                pad_scr[0, 0:drows, :] = _rotary(proj_scr[tile:rows, c0:c0 + RET_HD], csd, snd)
                pad_scr[1, 0:drows, :] = (_rotary(proj_scr[tile:rows, w + c0:w + c0 + RET_HD], csd, snd)
                                          * (RET_HD ** -0.5))
                pad_scr[2, 0:drows, :] = proj_scr[tile:rows, 2 * w + c0:2 * w + c0 + RET_HD]
                q, k, v = pad_scr[0], pad_scr[1], pad_scr[2]
                qb, kb, vb = q.astype(BF16), k.astype(BF16), v.astype(BF16)
                d_scores[hh] = _dot_nt(qb[0:drows, :], kb)
                d_vb[hh] = vb
                kzt = (k * zetad_scr[hh]).T.astype(BF16)
                ra = sret_ref[0, hh]
                rb = sret_ref[1, hh]
                rcat = jnp.concatenate([ra, rb], axis=1).astype(BF16)
                pr = _dot(qb[0:drows, :], rcat)
                d_cross[hh] = jnp.concatenate([pr[0:seqlen, 0:RET_HD], pr[seqlen:drows, RET_HD:2 * RET_HD]],
                                              axis=0)
                vpair = jnp.where(rowblk == colhalf, jnp.concatenate([v, v], axis=1), 0.0).astype(BF16)
                upd = _dot(kzt, vpair)
                g_chunk = math.exp(seqlen * LOG_G[hh])
                retd_ref[0, hh] = ra * g_chunk + upd[:, 0:RET_HD]
                retd_ref[1, hh] = rb * g_chunk + upd[:, RET_HD:2 * RET_HD]
            fill(2)

            u = proj_scr[0:tile, 5 * w:5 * w + cw] * proj_scr[0:tile, 5 * w + cw:5 * w + 2 * cw]
            uext_scr[SUBLANES:SUBLANES + tile, :] = u
            um1 = uext_scr[SUBLANES - 1:SUBLANES - 1 + tile, :]
            um2 = uext_scr[SUBLANES - 2:SUBLANES - 2 + tile, :]
            convy_scr[0:tile, :] = (convw_ref[0, 0:1, :] * um2 + convw_ref[0, 1:2, :] * um1
                                    + convw_ref[0, 2:3, :] * u)
            tail = uext_scr[tile + SUBLANES - (CONV_K - 1):tile + SUBLANES, :]
            cnew_ref[0] = tail
            uext_scr[SUBLANES - (CONV_K - 1):SUBLANES, :] = jnp.where(seq_end, 0.0, tail)

            ud = proj_scr[tile:rows, 5 * w:5 * w + cw] * proj_scr[tile:rows, 5 * w + cw:5 * w + 2 * cw]
            cst = cst_ref[...]
            older = jnp.broadcast_to(cst[:, 0:1, :], (nseq, seqlen, cw)).reshape(drows, cw)
            newer = jnp.broadcast_to(cst[:, 1:2, :], (nseq, seqlen, cw)).reshape(drows, cw)
            r8 = lax.broadcasted_iota(jnp.int32, (drows, cw), 0) % seqlen
            ud1 = jnp.where(r8 == 0, newer, pltpu.roll(ud, 1, axis=0))
            ud2 = jnp.where(r8 == 0, older, jnp.where(r8 == 1, newer, pltpu.roll(ud, 2, axis=0)))
            convy_scr[tile:rows, :] = (convw_ref[0, 0:1, :] * ud2 + convw_ref[0, 1:2, :] * ud1
                                       + convw_ref[0, 2:3, :] * ud)
            cnewd_ref[...] = ud.reshape(nseq, seqlen, cw)[:, seqlen - (CONV_K - 1):, :]
            fill(len(fillers) - len(units) + 1)

            outs = {}
            for hh in range(RET_HEADS):
                state = jnp.where(seq_start, 0.0, r_scr[hh])
                for c in range(n_chunks):
                    s = (scores[c, hh] * dec_scr[hh]).astype(BF16)
                    outs[c, hh] = _dot(s, vbs[c, hh]) + _dot(qbs[c, hh], state.astype(BF16)) * xi_scr[hh]
                    state = state * math.exp(chunk * LOG_G[hh]) + updates[c, hh]
                r_scr[hh] = state
                ret_ref[0, hh] = state
            d_outs = {}
            for hh in range(RET_HEADS):
                s = (d_scores[hh] * maskd_scr[hh, 0:drows, :]).astype(BF16)
                d_outs[hh] = _dot(s, d_vb[hh]) + d_cross[hh] * xid_scr[hh, 0:drows, :]
            for c, hh in units:
                fill(1)
                r0, c0 = c * chunk, hh * RET_HD
                g = proj_scr[r0:r0 + chunk, 3 * w + c0:3 * w + c0 + RET_HD]
                mix_scr[r0:r0 + chunk, c0:c0 + RET_HD] = _groupnorm_gate(
                    outs[c, hh], g, gng_ref[:, c0:c0 + RET_HD]).astype(BF16)
            for hh in range(RET_HEADS):
                c0 = hh * RET_HD
                g = proj_scr[tile:rows, 3 * w + c0:3 * w + c0 + RET_HD]
                mix_scr[tile:rows, c0:c0 + RET_HD] = _groupnorm_gate(
                    d_outs[hh], g, gng_ref[:, c0:c0 + RET_HD]).astype(BF16)
            fill(len(fillers))

        if dense:
            down = _dot(act_scr[...], wd_ref[...])
        if mixer:
            proj_piece(4 * w)()
        if dense:
            y = _rmsnorm(x1_scr[...] + down, gf_ref[...])
            y_ref[...] = y[0:tile, :]
            yd_ref[...] = y[tile:rows, :]
        if mixer:
            mix_scr[:, w:w + cw] = (proj_scr[:, 4 * w:4 * w + cw] * convy_scr[...]).astype(BF16)
            xkeep_scr[0:tile, :] = x
            xkeep_scr[tile:rows, :] = xd

    @pl.when(i == 0)
    def _():
        setup()
        step(mixer=True, dense=False)

    @pl.when((i > 0) & (i < n_tiles))
    def _():
        step(mixer=True, dense=True)

    @pl.when(i == n_tiles)
    def _():
        step(mixer=False, dense=True)


def _resident(shape):
    nd = len(shape)
    return pl.BlockSpec(shape, lambda *_: (0,) * nd, pipeline_mode=pl.Buffered(1))


def _layer_call(x, xd, state_conv, state_ret, ws):
    batch, seq, d = x.shape
    nseq_all, seqlen, _ = xd.shape
    tile = PROMPT_TILE
    tiles_per_seq = seq // tile
    n_tiles = batch * tiles_per_seq
    assert nseq_all % n_tiles == 0, "every prompt tile carries the same number of decode sequences"
    nseq = nseq_all // n_tiles
    drows = nseq * seqlen
    rows = tile + drows
    cs, sn = _rope_tables(np.arange(seq))
    csd, snd = _rope_tables(np.tile(PAST_LEN + np.arange(seqlen), nseq))
    in_cols = ws[1].shape[1]
    cw = ws[2].shape[-1]
    mixer_tile = lambda i: jnp.minimum(i, n_tiles - 1)
    dense_tile = lambda i: jnp.maximum(i - 1, 0)
    w_in, w_out, w_gate, w_up, w_down = [ws[m] for m in MATRIX_SLOTS]
    assert w_gate.shape == w_up.shape and w_gate.shape[1] == w_down.shape[0]
    d_ff = w_down.shape[0]
    mats = [w_in.shape, w_out.shape, (d, 2 * d_ff), w_down.shape]
    weight_specs = [pl.BlockSpec(memory_space=pl.ANY) if m in MATRIX_SLOTS else _resident(wt.shape)
                    for m, wt in enumerate(ws)]
    state_block = (nseq, RET_HEADS, RET_HD, RET_HD)
    outs = pl.pallas_call(
        functools.partial(_layer_kernel, n_tiles=n_tiles, tiles_per_seq=tiles_per_seq),
        grid=(n_tiles + 1,),
        in_specs=[pl.BlockSpec((tile, d), lambda i: (mixer_tile(i), 0)),
                  pl.BlockSpec((tile, RET_HD), lambda i: (mixer_tile(i) % tiles_per_seq, 0)),
                  pl.BlockSpec((tile, RET_HD), lambda i: (mixer_tile(i) % tiles_per_seq, 0)),
                  pl.BlockSpec((drows, d), lambda i: (mixer_tile(i), 0)),
                  _resident((drows, RET_HD)), _resident((drows, RET_HD)),
                  pl.BlockSpec((nseq, CONV_K - 1, cw), lambda i: (mixer_tile(i), 0, 0)),
                  pl.BlockSpec(state_block, lambda i: (mixer_tile(i), 0, 0, 0))] + weight_specs,
        out_specs=[pl.BlockSpec((tile, d), lambda i: (dense_tile(i), 0)),
                   pl.BlockSpec((1, CONV_K - 1, cw), lambda i: (mixer_tile(i) // tiles_per_seq, 0, 0)),
                   pl.BlockSpec((1, RET_HEADS, RET_HD, RET_HD),
                                lambda i: (mixer_tile(i) // tiles_per_seq, 0, 0, 0)),
                   pl.BlockSpec((drows, d), lambda i: (dense_tile(i), 0)),
                   pl.BlockSpec((nseq, CONV_K - 1, cw), lambda i: (mixer_tile(i), 0, 0)),
                   pl.BlockSpec(state_block, lambda i: (mixer_tile(i), 0, 0, 0))],
        out_shape=[jax.ShapeDtypeStruct((batch * seq, d), F32),
                   jax.ShapeDtypeStruct((batch, CONV_K - 1, cw), F32),
                   jax.ShapeDtypeStruct((batch, RET_HEADS, RET_HD, RET_HD), F32),
                   jax.ShapeDtypeStruct((nseq_all * seqlen, d), F32),
                   jax.ShapeDtypeStruct((nseq_all, CONV_K - 1, cw), F32),
                   jax.ShapeDtypeStruct((nseq_all, RET_HEADS, RET_HD, RET_HD), F32)],
        scratch_shapes=[pltpu.VMEM(s, BF16) for s in mats] + [
                        pltpu.VMEM((STAGE_SLOTS * STAGE_ROWS, max(in_cols, d_ff, d)), F32),
                        pltpu.SemaphoreType.DMA((STAGE_SLOTS,)),
                        pltpu.VMEM((rows, in_cols), F32),
                        pltpu.VMEM((tile + SUBLANES, cw), F32),
                        pltpu.VMEM((rows, cw), F32),
                        pltpu.VMEM((rows, d), BF16),
                        pltpu.VMEM((rows, d), F32),
                        pltpu.VMEM((rows, d), F32),
                        pltpu.VMEM((rows, d), BF16),
                        pltpu.VMEM((rows, d), BF16),
                        pltpu.VMEM((rows, d_ff), BF16),
                        pltpu.VMEM((RET_HEADS, RET_HD, RET_HD), F32),
                        pltpu.VMEM((RET_HEADS, PROMPT_CHUNK, PROMPT_CHUNK), F32),
                        pltpu.VMEM((RET_HEADS, PROMPT_CHUNK, RET_HD), F32),
                        pltpu.VMEM((RET_HEADS, PROMPT_CHUNK, RET_HD), F32),
                        pltpu.VMEM((3, DECODE_PAD, RET_HD), F32),
                        pltpu.VMEM((RET_HEADS, DECODE_PAD, DECODE_PAD), F32),
                        pltpu.VMEM((RET_HEADS, DECODE_PAD, RET_HD), F32),
                        pltpu.VMEM((RET_HEADS, DECODE_PAD, RET_HD), F32)],
        compiler_params=pltpu.CompilerParams(dimension_semantics=("arbitrary",),
                                             vmem_limit_bytes=VMEM_LIMIT_BYTES),
        name="layer",
    )(x.reshape(batch * seq, d), cs, sn, xd.reshape(nseq_all * seqlen, d), csd, snd, state_conv, state_ret,
      *ws)
    y, cnew, ret, yd, cnewd, retd = outs
    return y.reshape(batch, seq, d), yd.reshape(nseq_all, seqlen, d), cnew, ret, cnewd, retd


def kernel(x_prompt, x_sample, state_conv, state_ret, norm1_g, w_in, conv_w, ret_gn_g, w_out, norm2_g,
           w_gate, w_up, w_down, norm_f_g):
    depth = w_in.shape[0]
    assert depth == 1, "the fused layer kernel takes a single layer"
    row = lambda g: g.reshape(1, -1)
    ws = (row(norm1_g[0]), w_in[0], conv_w, row(ret_gn_g[0]), w_out[0],
          row(norm2_g[0]), w_gate[0], w_up[0], w_down[0], row(norm_f_g))
    y_p, y_s, cnew_p, ret_p, cnew_s, ret_s = _layer_call(x_prompt, x_sample, state_conv[0], state_ret[0], ws)
    return (y_p, y_s, cnew_p[None], ret_p[None], cnew_s[None], ret_s[None])
```

```python
import functools
import math

import numpy as np
import jax
import jax.numpy as jnp
from jax import lax
from jax.experimental import pallas as pl
from jax.experimental.pallas import tpu as pltpu

F32 = jnp.float32
BF16 = jnp.bfloat16

RET_HEADS = 4
RET_HD = 128
RET_WIDTH = RET_HEADS * RET_HD
CONV_K = 3
ROPE_BASE = 10000.0
NORM_EPS = 1e-6
GN_EPS = 1e-5
PAST_LEN = 16384

LOG_G = tuple(math.log1p(-(2.0 ** (-5.0 - h))) for h in range(RET_HEADS))

SUBLANES = 8
MXU_N = 256
MATRIX_SLOTS = (1, 4, 6, 7, 8)
STAGE_ROWS = 128
STAGE_SLOTS = 8
PROMPT_TILE = 256
PROMPT_CHUNK = 128
DECODE_PAD = 128
VMEM_LIMIT_BYTES = 62 * 1024 * 1024


def _rope_tables(pos):
    half = RET_HD // 2
    inv = ROPE_BASE ** (-np.arange(half, dtype=np.float64) / half)
    ang = np.asarray(pos, np.float64)[:, None] * inv[None, :]
    cos, sin = np.cos(ang), np.sin(ang)
    cs = np.concatenate([cos, cos], axis=1).astype(np.float32)
    sn = np.concatenate([-sin, sin], axis=1).astype(np.float32)
    return jnp.asarray(cs), jnp.asarray(sn)


def _rmsnorm(x, g):
    ms = jnp.mean(x * x, axis=-1, keepdims=True)
    return x * lax.rsqrt(ms + NORM_EPS) * g


def _silu(x):
    return x * (1.0 / (1.0 + jnp.exp(-x)))


def _rotary(t, cs, sn):
    return t * cs + pltpu.roll(t, RET_HD // 2, axis=1) * sn


def _dot(a, b):
    return jnp.dot(a, b, preferred_element_type=F32)


def _dot_nt(a, b):
    return lax.dot_general(a, b, (((1,), (1,)), ((), ())), preferred_element_type=F32)


def _dot_tn(a, b):
    return lax.dot_general(a, b, (((0,), (0,)), ((), ())), preferred_element_type=F32)


def _groupnorm_gate(o, g, gn_g):
    mu = jnp.mean(o, axis=-1, keepdims=True)
    d = o - mu
    var = jnp.mean(d * d, axis=-1, keepdims=True)
    return _silu(g) * (d * lax.rsqrt(var + GN_EPS) * gn_g)


def _swiglu_piece(h_scr, wgu_ref, act_scr, c0):
    r = _dot(h_scr[...], wgu_ref[:, 2 * c0:2 * c0 + 2 * MXU_N])
    act_scr[:, c0:c0 + MXU_N] = (_silu(r[:, 0:MXU_N]) * r[:, MXU_N:2 * MXU_N]).astype(BF16)


def _load_weights_bf16(pairs, stage, sem):
    rb = STAGE_ROWS
    slots, width = stage.shape[0] // rb, stage.shape[1]
    assert slots >= 2 and slots <= sem.shape[0]
    chunks = []
    for w_hbm, w_bf, lane in pairs:
        rows, cols = w_hbm.shape
        assert rows % rb == 0 and cols <= width and (lane is None or cols % MXU_N == 0)
        chunks += [(w_hbm, w_bf, r0, cols, lane) for r0 in range(0, rows, rb)]

    def slot_view(n):
        s0 = (n % slots) * rb
        return stage.at[s0:s0 + rb, 0:chunks[n][3]]

    def chunk_copy(n):
        w_hbm, r0 = chunks[n][0], chunks[n][2]
        return pltpu.make_async_copy(w_hbm.at[r0:r0 + rb, :], slot_view(n), sem.at[n % slots])

    for n in range(min(slots - 1, len(chunks))):
        chunk_copy(n).start()
    for n, (_, w_bf, r0, cols, lane) in enumerate(chunks):
        chunk_copy(n).wait()
        if n + slots - 1 < len(chunks):
            chunk_copy(n + slots - 1).start()
        if lane is None:
            w_bf[r0:r0 + rb, :] = slot_view(n)[...].astype(BF16)
        else:
            for c0 in range(0, cols, MXU_N):
                d0 = 2 * c0 + lane * MXU_N
                w_bf[r0:r0 + rb, d0:d0 + MXU_N] = slot_view(n)[:, c0:c0 + MXU_N].astype(BF16)


def _layer_kernel(x_ref, cs_ref, sn_ref, xd_ref, csd_ref, snd_ref, cst_ref, sret_ref,
                  g1_ref, win_hbm, convw_ref, gng_ref, wout_hbm, g2_ref, wg_hbm, wu_hbm, wd_hbm, gf_ref,
                  y_ref, cnew_ref, ret_ref, yd_ref, cnewd_ref, retd_ref,
                  win_ref, wout_ref, wgu_ref, wd_ref, stage_scr, load_sem,
                  proj_scr, uext_scr, convy_scr, mix_scr, xkeep_scr, x1_scr, h_scr, h2_scr, act_scr, r_scr,
                  dec_scr, xi_scr, zeta_scr, pad_scr, maskd_scr, xid_scr, zetad_scr,
                  *, n_tiles, tiles_per_seq):
    tile = x_ref.shape[0]
    drows = xd_ref.shape[0]
    nseq = sret_ref.shape[0]
    seqlen = drows // nseq
    rows = tile + drows
    pad = DECODE_PAD
    assert seqlen == SUBLANES, "decode sequences must fill exactly one f32 sublane tile"
    assert nseq == 2 and drows <= pad and tile % 16 == 0 and drows % 16 == 0
    i = pl.program_id(0)
    chunk = PROMPT_CHUNK
    assert tile % chunk == 0
    t = jnp.minimum(i, n_tiles - 1) % tiles_per_seq
    seq_start = t == 0
    seq_end = t == tiles_per_seq - 1
    w = RET_WIDTH
    cw = w
    d_ff = wd_ref.shape[0]

    band_rows = stage_scr.shape[0]
    d_model = wd_ref.shape[1]
    wd_bands = [(r0, min(r0 + band_rows, d_ff)) for r0 in range(0, d_ff, band_rows)]
    assert len(wd_bands) * d_model <= stage_scr.shape[1] and len(wd_bands) <= load_sem.shape[0]

    def wd_band(n):
        r0, r1 = wd_bands[n]
        view = stage_scr.at[0:r1 - r0, n * d_model:(n + 1) * d_model]
        return pltpu.make_async_copy(wd_hbm.at[r0:r1, :], view, load_sem.at[n]), view

    def setup():
        _load_weights_bf16(((win_hbm, win_ref, None), (wout_hbm, wout_ref, None), (wg_hbm, wgu_ref, 0),
                            (wu_hbm, wgu_ref, 1)), stage_scr, load_sem)
        for n in range(len(wd_bands)):
            wd_band(n)[0].start()
        ii = lax.broadcasted_iota(jnp.int32, (chunk, chunk), 0)
        jj = lax.broadcasted_iota(jnp.int32, (chunk, chunk), 1)
        causal = ii >= jj
        diff = jnp.where(causal, ii - jj, 0).astype(F32)
        i_f = lax.broadcasted_iota(jnp.int32, (chunk, RET_HD), 0).astype(F32)
        pi = lax.broadcasted_iota(jnp.int32, (pad, pad), 0)
        pj = lax.broadcasted_iota(jnp.int32, (pad, pad), 1)
        keep = (pi >= pj) & ((pi // seqlen) == (pj // seqlen))
        pdiff = jnp.where(keep, pi - pj, 0).astype(F32)
        pos = (lax.broadcasted_iota(jnp.int32, (pad, RET_HD), 0) % seqlen).astype(F32)
        for hh in range(RET_HEADS):
            dec_scr[hh] = jnp.where(causal, jnp.exp(diff * LOG_G[hh]), 0.0)
            xi_scr[hh] = jnp.exp((i_f + 1.0) * LOG_G[hh])
            zeta_scr[hh] = jnp.exp((chunk - 1.0 - i_f) * LOG_G[hh])
            maskd_scr[hh] = jnp.where(keep, jnp.exp(pdiff * LOG_G[hh]), 0.0)
            xid_scr[hh] = jnp.exp((pos + 1.0) * LOG_G[hh])
            zetad_scr[hh] = jnp.exp((seqlen - 1.0 - pos) * LOG_G[hh])
        r_scr[...] = jnp.zeros_like(r_scr)
        uext_scr[0:SUBLANES, :] = jnp.zeros((SUBLANES, uext_scr.shape[1]), F32)
        pad_scr[...] = jnp.zeros_like(pad_scr)

    def ffn_piece(c0):
        return functools.partial(_swiglu_piece, h2_scr, wgu_ref, act_scr, c0)

    def proj_piece(c0):
        def emit():
            proj_scr[:, c0:c0 + cw] = _dot(h_scr[...], win_ref[:, c0:c0 + cw])
        return emit

    def step(mixer, dense):
        fillers = []

        def fill(n):
            for _ in range(min(n, len(fillers))):
                fillers.pop(0)()

        if dense:
            x1_scr[...] = xkeep_scr[...] + _dot(mix_scr[...], wout_ref[...])
        if mixer:
            x = x_ref[...]
            xd = xd_ref[...]
            h_scr[0:tile, :] = _rmsnorm(x, g1_ref[...]).astype(BF16)
            h_scr[tile:rows, :] = _rmsnorm(xd, g1_ref[...]).astype(BF16)
            proj_scr[:, 0:2 * w] = _dot(h_scr[...], win_ref[:, 0:2 * w])
        if dense:
            h2_scr[...] = _rmsnorm(x1_scr[...], g2_ref[...]).astype(BF16)
            fillers += [ffn_piece(c0) for c0 in range(0, d_ff, MXU_N)]
        if not mixer:
            fill(len(fillers))
        else:
            proj_scr[:, 2 * w:4 * w] = _dot(h_scr[...], win_ref[:, 2 * w:4 * w])
            fillers[0:0] = [proj_piece(5 * w), proj_piece(6 * w)]

            n_chunks = tile // chunk
            units = [(c, hh) for c in range(n_chunks) for hh in range(RET_HEADS)]
            qbs, vbs, scores, updates = {}, {}, {}, {}
            for c, hh in units:
                r0, c0 = c * chunk, hh * RET_HD
                cs = cs_ref[r0:r0 + chunk, :]
                sn = sn_ref[r0:r0 + chunk, :]
                q = _rotary(proj_scr[r0:r0 + chunk, c0:c0 + RET_HD], cs, sn)
                k = _rotary(proj_scr[r0:r0 + chunk, w + c0:w + c0 + RET_HD], cs, sn) * (RET_HD ** -0.5)
                qb, kb = q.astype(BF16), k.astype(BF16)
                vb = proj_scr[r0:r0 + chunk, 2 * w + c0:2 * w + c0 + RET_HD].astype(BF16)
                kz = (k * zeta_scr[hh]).astype(BF16)
                qbs[c, hh], vbs[c, hh] = qb, vb
                scores[c, hh] = _dot_nt(qb, kb)
                updates[c, hh] = _dot_tn(kz, vb)

            csd = csd_ref[...]
            snd = snd_ref[...]
            rowblk = lax.broadcasted_iota(jnp.int32, (pad, 2 * RET_HD), 0) // seqlen
            colhalf = lax.broadcasted_iota(jnp.int32, (pad, 2 * RET_HD), 1) // RET_HD
            d_scores, d_vb, d_cross = {}, {}, {}
            for hh in range(RET_HEADS):
                c0 = hh * RET_HD
                pad_scr[0, 0:drows, :] = _rotary(proj_scr[tile:rows, c0:c0 + RET_HD], csd, snd)
                pad_scr[1, 0:drows, :] = (_rotary(proj_scr[tile:rows, w + c0:w + c0 + RET_HD], csd, snd)
                                          * (RET_HD ** -0.5))
                pad_scr[2, 0:drows, :] = proj_scr[tile:rows, 2 * w + c0:2 * w + c0 + RET_HD]
                q, k, v = pad_scr[0], pad_scr[1], pad_scr[2]
                qb, kb, vb = q.astype(BF16), k.astype(BF16), v.astype(BF16)
                d_scores[hh] = _dot_nt(qb[0:drows, :], kb)
                d_vb[hh] = vb
                kzt = (k * zetad_scr[hh]).T.astype(BF16)
                ra = sret_ref[0, hh]
                rb = sret_ref[1, hh]
                rcat = jnp.concatenate([ra, rb], axis=1).astype(BF16)
                pr = _dot(qb[0:drows, :], rcat)
                d_cross[hh] = jnp.concatenate([pr[0:seqlen, 0:RET_HD], pr[seqlen:drows, RET_HD:2 * RET_HD]],
                                              axis=0)
                vpair = jnp.where(rowblk == colhalf, jnp.concatenate([v, v], axis=1), 0.0).astype(BF16)
                upd = _dot(kzt, vpair)
                g_chunk = math.exp(seqlen * LOG_G[hh])
                retd_ref[0, hh] = ra * g_chunk + upd[:, 0:RET_HD]
                retd_ref[1, hh] = rb * g_chunk + upd[:, RET_HD:2 * RET_HD]
            fill(2)

            u = proj_scr[0:tile, 5 * w:5 * w + cw] * proj_scr[0:tile, 5 * w + cw:5 * w + 2 * cw]
            uext_scr[SUBLANES:SUBLANES + tile, :] = u
            um1 = uext_scr[SUBLANES - 1:SUBLANES - 1 + tile, :]
            um2 = uext_scr[SUBLANES - 2:SUBLANES - 2 + tile, :]
            convy_scr[0:tile, :] = (convw_ref[0, 0:1, :] * um2 + convw_ref[0, 1:2, :] * um1
                                    + convw_ref[0, 2:3, :] * u)
            tail = uext_scr[tile + SUBLANES - (CONV_K - 1):tile + SUBLANES, :]
            cnew_ref[0] = tail
            uext_scr[SUBLANES - (CONV_K - 1):SUBLANES, :] = jnp.where(seq_end, 0.0, tail)

            ud = proj_scr[tile:rows, 5 * w:5 * w + cw] * proj_scr[tile:rows, 5 * w + cw:5 * w + 2 * cw]
            cst = cst_ref[...]
            older = jnp.broadcast_to(cst[:, 0:1, :], (nseq, seqlen, cw)).reshape(drows, cw)
            newer = jnp.broadcast_to(cst[:, 1:2, :], (nseq, seqlen, cw)).reshape(drows, cw)
            r8 = lax.broadcasted_iota(jnp.int32, (drows, cw), 0) % seqlen
            ud1 = jnp.where(r8 == 0, newer, pltpu.roll(ud, 1, axis=0))
            ud2 = jnp.where(r8 == 0, older, jnp.where(r8 == 1, newer, pltpu.roll(ud, 2, axis=0)))
            convy_scr[tile:rows, :] = (convw_ref[0, 0:1, :] * ud2 + convw_ref[0, 1:2, :] * ud1
                                       + convw_ref[0, 2:3, :] * ud)
            cnewd_ref[...] = ud.reshape(nseq, seqlen, cw)[:, seqlen - (CONV_K - 1):, :]
            fill(len(fillers) - len(units) + 1)

            outs = {}
            for hh in range(RET_HEADS):
                state = jnp.where(seq_start, 0.0, r_scr[hh])
                for c in range(n_chunks):
                    s = (scores[c, hh] * dec_scr[hh]).astype(BF16)
                    outs[c, hh] = _dot(s, vbs[c, hh]) + _dot(qbs[c, hh], state.astype(BF16)) * xi_scr[hh]
                    state = state * math.exp(chunk * LOG_G[hh]) + updates[c, hh]
                r_scr[hh] = state
                ret_ref[0, hh] = state
            d_outs = {}
            for hh in range(RET_HEADS):
                s = (d_scores[hh] * maskd_scr[hh, 0:drows, :]).astype(BF16)
                d_outs[hh] = _dot(s, d_vb[hh]) + d_cross[hh] * xid_scr[hh, 0:drows, :]
            for c, hh in units:
                fill(1)
                r0, c0 = c * chunk, hh * RET_HD
                g = proj_scr[r0:r0 + chunk, 3 * w + c0:3 * w + c0 + RET_HD]
                mix_scr[r0:r0 + chunk, c0:c0 + RET_HD] = _groupnorm_gate(
                    outs[c, hh], g, gng_ref[:, c0:c0 + RET_HD]).astype(BF16)
            for hh in range(RET_HEADS):
                c0 = hh * RET_HD
                g = proj_scr[tile:rows, 3 * w + c0:3 * w + c0 + RET_HD]
                mix_scr[tile:rows, c0:c0 + RET_HD] = _groupnorm_gate(
                    d_outs[hh], g, gng_ref[:, c0:c0 + RET_HD]).astype(BF16)
            fill(len(fillers))

        if dense:
            down = _dot(act_scr[...], wd_ref[...])
        if mixer:
            proj_piece(4 * w)()
        if dense:
            y = _rmsnorm(x1_scr[...] + down, gf_ref[...])
            y_ref[...] = y[0:tile, :]
            yd_ref[...] = y[tile:rows, :]
        if mixer:
            mix_scr[:, w:w + cw] = (proj_scr[:, 4 * w:4 * w + cw] * convy_scr[...]).astype(BF16)
            xkeep_scr[0:tile, :] = x
            xkeep_scr[tile:rows, :] = xd

    assert n_tiles >= 2
    @pl.when(i == 0)
    def _():
        setup()
        step(mixer=True, dense=False)

    @pl.when(i == 1)
    def _():
        for n, (r0, r1) in enumerate(wd_bands):
            copy, view = wd_band(n)
            copy.wait()
            wd_ref[r0:r1, :] = view[...].astype(BF16)
        step(mixer=True, dense=True)

    @pl.when((i > 1) & (i < n_tiles))
    def _():
        step(mixer=True, dense=True)

    @pl.when(i == n_tiles)
    def _():
        step(mixer=False, dense=True)


def _resident(shape):
    nd = len(shape)
    return pl.BlockSpec(shape, lambda *_: (0,) * nd, pipeline_mode=pl.Buffered(1))


def _layer_call(x, xd, state_conv, state_ret, ws):
    batch, seq, d = x.shape
    nseq_all, seqlen, _ = xd.shape
    tile = PROMPT_TILE
    tiles_per_seq = seq // tile
    n_tiles = batch * tiles_per_seq
    assert nseq_all % n_tiles == 0, "every prompt tile carries the same number of decode sequences"
    nseq = nseq_all // n_tiles
    drows = nseq * seqlen
    rows = tile + drows
    cs, sn = _rope_tables(np.arange(seq))
    csd, snd = _rope_tables(np.tile(PAST_LEN + np.arange(seqlen), nseq))
    in_cols = ws[1].shape[1]
    cw = ws[2].shape[-1]
    mixer_tile = lambda i: jnp.minimum(i, n_tiles - 1)
    dense_tile = lambda i: jnp.maximum(i - 1, 0)
    w_in, w_out, w_gate, w_up, w_down = [ws[m] for m in MATRIX_SLOTS]
    assert w_gate.shape == w_up.shape and w_gate.shape[1] == w_down.shape[0]
    d_ff = w_down.shape[0]
    mats = [w_in.shape, w_out.shape, (d, 2 * d_ff), w_down.shape]
    weight_specs = [pl.BlockSpec(memory_space=pl.ANY) if m in MATRIX_SLOTS else _resident(wt.shape)
                    for m, wt in enumerate(ws)]
    state_block = (nseq, RET_HEADS, RET_HD, RET_HD)
    outs = pl.pallas_call(
        functools.partial(_layer_kernel, n_tiles=n_tiles, tiles_per_seq=tiles_per_seq),
        grid=(n_tiles + 1,),
        in_specs=[pl.BlockSpec((tile, d), lambda i: (mixer_tile(i), 0)),
                  pl.BlockSpec((tile, RET_HD), lambda i: (mixer_tile(i) % tiles_per_seq, 0)),
                  pl.BlockSpec((tile, RET_HD), lambda i: (mixer_tile(i) % tiles_per_seq, 0)),
                  pl.BlockSpec((drows, d), lambda i: (mixer_tile(i), 0)),
                  _resident((drows, RET_HD)), _resident((drows, RET_HD)),
                  pl.BlockSpec((nseq, CONV_K - 1, cw), lambda i: (mixer_tile(i), 0, 0)),
                  pl.BlockSpec(state_block, lambda i: (mixer_tile(i), 0, 0, 0))] + weight_specs,
        out_specs=[pl.BlockSpec((tile, d), lambda i: (dense_tile(i), 0)),
                   pl.BlockSpec((1, CONV_K - 1, cw), lambda i: (mixer_tile(i) // tiles_per_seq, 0, 0)),
                   pl.BlockSpec((1, RET_HEADS, RET_HD, RET_HD),
                                lambda i: (mixer_tile(i) // tiles_per_seq, 0, 0, 0)),
                   pl.BlockSpec((drows, d), lambda i: (dense_tile(i), 0)),
                   pl.BlockSpec((nseq, CONV_K - 1, cw), lambda i: (mixer_tile(i), 0, 0)),
                   pl.BlockSpec(state_block, lambda i: (mixer_tile(i), 0, 0, 0))],
        out_shape=[jax.ShapeDtypeStruct((batch * seq, d), F32),
                   jax.ShapeDtypeStruct((batch, CONV_K - 1, cw), F32),
                   jax.ShapeDtypeStruct((batch, RET_HEADS, RET_HD, RET_HD), F32),
                   jax.ShapeDtypeStruct((nseq_all * seqlen, d), F32),
                   jax.ShapeDtypeStruct((nseq_all, CONV_K - 1, cw), F32),
                   jax.ShapeDtypeStruct((nseq_all, RET_HEADS, RET_HD, RET_HD), F32)],
        scratch_shapes=[pltpu.VMEM(s, BF16) for s in mats] + [
                        pltpu.VMEM((STAGE_SLOTS * STAGE_ROWS, max(in_cols, d_ff, d)), F32),
                        pltpu.SemaphoreType.DMA((STAGE_SLOTS,)),
                        pltpu.VMEM((rows, in_cols), F32),
                        pltpu.VMEM((tile + SUBLANES, cw), F32),
                        pltpu.VMEM((rows, cw), F32),
                        pltpu.VMEM((rows, d), BF16),
                        pltpu.VMEM((rows, d), F32),
                        pltpu.VMEM((rows, d), F32),
                        pltpu.VMEM((rows, d), BF16),
                        pltpu.VMEM((rows, d), BF16),
                        pltpu.VMEM((rows, d_ff), BF16),
                        pltpu.VMEM((RET_HEADS, RET_HD, RET_HD), F32),
                        pltpu.VMEM((RET_HEADS, PROMPT_CHUNK, PROMPT_CHUNK), F32),
                        pltpu.VMEM((RET_HEADS, PROMPT_CHUNK, RET_HD), F32),
                        pltpu.VMEM((RET_HEADS, PROMPT_CHUNK, RET_HD), F32),
                        pltpu.VMEM((3, DECODE_PAD, RET_HD), F32),
                        pltpu.VMEM((RET_HEADS, DECODE_PAD, DECODE_PAD), F32),
                        pltpu.VMEM((RET_HEADS, DECODE_PAD, RET_HD), F32),
                        pltpu.VMEM((RET_HEADS, DECODE_PAD, RET_HD), F32)],
        compiler_params=pltpu.CompilerParams(dimension_semantics=("arbitrary",),
                                             vmem_limit_bytes=VMEM_LIMIT_BYTES),
        name="layer",
    )(x.reshape(batch * seq, d), cs, sn, xd.reshape(nseq_all * seqlen, d), csd, snd, state_conv, state_ret,
      *ws)
    y, cnew, ret, yd, cnewd, retd = outs
    return y.reshape(batch, seq, d), yd.reshape(nseq_all, seqlen, d), cnew, ret, cnewd, retd


def kernel(x_prompt, x_sample, state_conv, state_ret, norm1_g, w_in, conv_w, ret_gn_g, w_out, norm2_g,
           w_gate, w_up, w_down, norm_f_g):
    depth = w_in.shape[0]
    assert depth == 1, "the fused layer kernel takes a single layer"
    row = lambda g: g.reshape(1, -1)
    ws = (row(norm1_g[0]), w_in[0], conv_w, row(ret_gn_g[0]), w_out[0],
          row(norm2_g[0]), w_gate[0], w_up[0], w_down[0], row(norm_f_g))
    y_p, y_s, cnew_p, ret_p, cnew_s, ret_s = _layer_call(x_prompt, x_sample, state_conv[0], state_ret[0], ws)
    return (y_p, y_s, cnew_p[None], ret_p[None], cnew_s[None], ret_s[None])
```

```python
import functools
import math

import numpy as np
import jax
import jax.numpy as jnp
from jax import lax
from jax.experimental import pallas as pl
from jax.experimental.pallas import tpu as pltpu

F32 = jnp.float32
BF16 = jnp.bfloat16

RET_HEADS = 4
RET_HD = 128
RET_WIDTH = RET_HEADS * RET_HD
CONV_K = 3
ROPE_BASE = 10000.0
NORM_EPS = 1e-6
GN_EPS = 1e-5
PAST_LEN = 16384

LOG_G = tuple(math.log1p(-(2.0 ** (-5.0 - h))) for h in range(RET_HEADS))

SUBLANES = 8
MXU_N = 256
MATRIX_SLOTS = (1, 4, 6, 7, 8)
STAGE_ROWS = 128
STAGE_SLOTS = 8
PROMPT_TILE = 256
PROMPT_CHUNK = 128
DECODE_PAD = 128
VMEM_LIMIT_BYTES = 62 * 1024 * 1024


def _rope_table(pos):
    half = RET_HD // 2
    inv = ROPE_BASE ** (-np.arange(half, dtype=np.float64) / half)
    ang = np.asarray(pos, np.float64)[:, None] * inv[None, :]
    cos, sin = np.cos(ang), np.sin(ang)
    return jnp.asarray(np.concatenate([cos, cos, -sin, sin], axis=1).astype(np.float32))


def _rmsnorm(x, g):
    ms = jnp.mean(x * x, axis=-1, keepdims=True)
    return x * lax.rsqrt(ms + NORM_EPS) * g


def _silu(x):
    return x * (1.0 / (1.0 + jnp.exp(-x)))


def _rotary(t, cs, sn):
    return t * cs + pltpu.roll(t, RET_HD // 2, axis=1) * sn


def _dot(a, b):
    return jnp.dot(a, b, preferred_element_type=F32)


def _dot_nt(a, b):
    return lax.dot_general(a, b, (((1,), (1,)), ((), ())), preferred_element_type=F32)


def _dot_tn(a, b):
    return lax.dot_general(a, b, (((0,), (0,)), ((), ())), preferred_element_type=F32)


def _groupnorm_gate(o, g, gn_g):
    mu = jnp.mean(o, axis=-1, keepdims=True)
    d = o - mu
    var = jnp.mean(d * d, axis=-1, keepdims=True)
    return _silu(g) * (d * lax.rsqrt(var + GN_EPS) * gn_g)


def _swiglu_piece(h_scr, wgu_ref, act_scr, c0):
    r = _dot(h_scr[...], wgu_ref[:, 2 * c0:2 * c0 + 2 * MXU_N])
    act_scr[:, c0:c0 + MXU_N] = (_silu(r[:, 0:MXU_N]) * r[:, MXU_N:2 * MXU_N]).astype(BF16)


def _load_weights_bf16(pairs, stage, sem):
    rb = STAGE_ROWS
    slots, width = stage.shape[0] // rb, stage.shape[1]
    assert slots >= 2 and slots <= sem.shape[0]
    chunks = []
    for w_hbm, w_bf, lane in pairs:
        rows, cols = w_hbm.shape
        assert rows % rb == 0 and cols <= width and (lane is None or cols % MXU_N == 0)
        chunks += [(w_hbm, w_bf, r0, cols, lane) for r0 in range(0, rows, rb)]

    def slot_view(n):
        s0 = (n % slots) * rb
        return stage.at[s0:s0 + rb, 0:chunks[n][3]]

    def chunk_copy(n):
        w_hbm, r0 = chunks[n][0], chunks[n][2]
        return pltpu.make_async_copy(w_hbm.at[r0:r0 + rb, :], slot_view(n), sem.at[n % slots])

    for n in range(min(slots - 1, len(chunks))):
        chunk_copy(n).start()
    for n, (_, w_bf, r0, cols, lane) in enumerate(chunks):
        chunk_copy(n).wait()
        if n + slots - 1 < len(chunks):
            chunk_copy(n + slots - 1).start()
        if lane is None:
            w_bf[r0:r0 + rb, :] = slot_view(n)[...].astype(BF16)
        else:
            for c0 in range(0, cols, MXU_N):
                d0 = 2 * c0 + lane * MXU_N
                w_bf[r0:r0 + rb, d0:d0 + MXU_N] = slot_view(n)[:, c0:c0 + MXU_N].astype(BF16)


def _layer_kernel(x_ref, rope_ref, xd_ref, roped_ref, cst_ref, sret_ref,
                  g1_ref, win_hbm, convw_ref, gng_ref, wout_hbm, g2_ref, wg_hbm, wu_hbm, wd_hbm, gf_ref,
                  y_ref, cnew_ref, ret_ref, yd_ref, cnewd_ref, retd_ref,
                  win_ref, wout_ref, wgu_ref, wd_ref, stage_scr, load_sem,
                  proj_scr, uext_scr, convy_scr, mix_scr, xkeep_scr, x1_scr, h_scr, h2_scr, act_scr, r_scr,
                  dec_scr, xi_scr, zeta_scr, pad_scr, maskd_scr, xid_scr, zetad_scr,
                  *, n_tiles, tiles_per_seq):
    tile = x_ref.shape[0]
    drows = xd_ref.shape[0]
    nseq = sret_ref.shape[0]
    seqlen = drows // nseq
    rows = tile + drows
    pad = DECODE_PAD
    assert seqlen == SUBLANES, "decode sequences must fill exactly one f32 sublane tile"
    assert nseq == 2 and drows <= pad and tile % 16 == 0 and drows % 16 == 0
    i = pl.program_id(0)
    chunk = PROMPT_CHUNK
    assert tile % chunk == 0
    t = jnp.minimum(i, n_tiles - 1) % tiles_per_seq
    seq_start = t == 0
    seq_end = t == tiles_per_seq - 1
    w = RET_WIDTH
    cw = w
    d_ff = wd_ref.shape[0]

    band_rows = stage_scr.shape[0]
    d_model = wd_ref.shape[1]
    wd_bands = [(r0, min(r0 + band_rows, d_ff)) for r0 in range(0, d_ff, band_rows)]
    assert len(wd_bands) * d_model <= stage_scr.shape[1] and len(wd_bands) <= load_sem.shape[0]

    def wd_band(n):
        r0, r1 = wd_bands[n]
        view = stage_scr.at[0:r1 - r0, n * d_model:(n + 1) * d_model]
        return pltpu.make_async_copy(wd_hbm.at[r0:r1, :], view, load_sem.at[n]), view

    def setup():
        _load_weights_bf16(((win_hbm, win_ref, None), (wout_hbm, wout_ref, None), (wg_hbm, wgu_ref, 0),
                            (wu_hbm, wgu_ref, 1)), stage_scr, load_sem)
        for n in range(len(wd_bands)):
            wd_band(n)[0].start()
        ii = lax.broadcasted_iota(jnp.int32, (chunk, chunk), 0)
        jj = lax.broadcasted_iota(jnp.int32, (chunk, chunk), 1)
        causal = ii >= jj
        diff = jnp.where(causal, ii - jj, 0).astype(F32)
        i_f = lax.broadcasted_iota(jnp.int32, (chunk, RET_HD), 0).astype(F32)
        pi = lax.broadcasted_iota(jnp.int32, (pad, pad), 0)
        pj = lax.broadcasted_iota(jnp.int32, (pad, pad), 1)
        keep = (pi >= pj) & ((pi // seqlen) == (pj // seqlen))
        pdiff = jnp.where(keep, pi - pj, 0).astype(F32)
        pos = (lax.broadcasted_iota(jnp.int32, (pad, RET_HD), 0) % seqlen).astype(F32)
        for hh in range(RET_HEADS):
            dec_scr[hh] = jnp.where(causal, jnp.exp(diff * LOG_G[hh]), 0.0)
            xi_scr[hh] = jnp.exp((i_f + 1.0) * LOG_G[hh])
            zeta_scr[hh] = jnp.exp((chunk - 1.0 - i_f) * LOG_G[hh])
            maskd_scr[hh] = jnp.where(keep, jnp.exp(pdiff * LOG_G[hh]), 0.0)
            xid_scr[hh] = jnp.exp((pos + 1.0) * LOG_G[hh])
            zetad_scr[hh] = jnp.exp((seqlen - 1.0 - pos) * LOG_G[hh])
        r_scr[...] = jnp.zeros_like(r_scr)
        uext_scr[0:SUBLANES, :] = jnp.zeros((SUBLANES, uext_scr.shape[1]), F32)
        pad_scr[...] = jnp.zeros_like(pad_scr)

    def ffn_piece(c0):
        return functools.partial(_swiglu_piece, h2_scr, wgu_ref, act_scr, c0)

    def proj_piece(c0):
        def emit():
            proj_scr[:, c0:c0 + cw] = _dot(h_scr[...], win_ref[:, c0:c0 + cw])
        return emit

    def step(mixer, dense):
        fillers = []

        def fill(n):
            for _ in range(min(n, len(fillers))):
                fillers.pop(0)()

        if dense:
            x1_scr[...] = xkeep_scr[...] + _dot(mix_scr[...], wout_ref[...])
        if mixer:
            x = x_ref[...]
            xd = xd_ref[...]
            h_scr[0:tile, :] = _rmsnorm(x, g1_ref[...]).astype(BF16)
            h_scr[tile:rows, :] = _rmsnorm(xd, g1_ref[...]).astype(BF16)
            proj_scr[:, 0:2 * w] = _dot(h_scr[...], win_ref[:, 0:2 * w])
        if dense:
            h2_scr[...] = _rmsnorm(x1_scr[...], g2_ref[...]).astype(BF16)
            fillers += [ffn_piece(c0) for c0 in range(0, d_ff, MXU_N)]
        if not mixer:
            fill(len(fillers))
        else:
            proj_scr[:, 2 * w:4 * w] = _dot(h_scr[...], win_ref[:, 2 * w:4 * w])
            fillers[0:0] = [proj_piece(5 * w), proj_piece(6 * w)]

            n_chunks = tile // chunk
            units = [(c, hh) for c in range(n_chunks) for hh in range(RET_HEADS)]
            qbs, vbs, scores, updates = {}, {}, {}, {}
            for c, hh in units:
                r0, c0 = c * chunk, hh * RET_HD
                cs = rope_ref[r0:r0 + chunk, 0:RET_HD]
                sn = rope_ref[r0:r0 + chunk, RET_HD:2 * RET_HD]
                q = _rotary(proj_scr[r0:r0 + chunk, c0:c0 + RET_HD], cs, sn)
                k = _rotary(proj_scr[r0:r0 + chunk, w + c0:w + c0 + RET_HD], cs, sn) * (RET_HD ** -0.5)
                qb, kb = q.astype(BF16), k.astype(BF16)
                vb = proj_scr[r0:r0 + chunk, 2 * w + c0:2 * w + c0 + RET_HD].astype(BF16)
                kz = (k * zeta_scr[hh]).astype(BF16)
                qbs[c, hh], vbs[c, hh] = qb, vb
                scores[c, hh] = _dot_nt(qb, kb)
                updates[c, hh] = _dot_tn(kz, vb)

            csd = roped_ref[:, 0:RET_HD]
            snd = roped_ref[:, RET_HD:2 * RET_HD]
            rowblk = lax.broadcasted_iota(jnp.int32, (pad, 2 * RET_HD), 0) // seqlen
            colhalf = lax.broadcasted_iota(jnp.int32, (pad, 2 * RET_HD), 1) // RET_HD
            d_scores, d_vb, d_cross = {}, {}, {}
            for hh in range(RET_HEADS):
                c0 = hh * RET_HD
                pad_scr[0, 0:drows, :] = _rotary(proj_scr[tile:rows, c0:c0 + RET_HD], csd, snd)
                pad_scr[1, 0:drows, :] = (_rotary(proj_scr[tile:rows, w + c0:w + c0 + RET_HD], csd, snd)
                                          * (RET_HD ** -0.5))
                pad_scr[2, 0:drows, :] = proj_scr[tile:rows, 2 * w + c0:2 * w + c0 + RET_HD]
                q, k, v = pad_scr[0], pad_scr[1], pad_scr[2]
                qb, kb, vb = q.astype(BF16), k.astype(BF16), v.astype(BF16)
                d_scores[hh] = _dot_nt(qb[0:drows, :], kb)
                d_vb[hh] = vb
                kzt = (k * zetad_scr[hh]).T.astype(BF16)
                ra = sret_ref[0, hh]
                rb = sret_ref[1, hh]
                rcat = jnp.concatenate([ra, rb], axis=1).astype(BF16)
                pr = _dot(qb[0:drows, :], rcat)
                d_cross[hh] = jnp.concatenate([pr[0:seqlen, 0:RET_HD], pr[seqlen:drows, RET_HD:2 * RET_HD]],
                                              axis=0)
                vpair = jnp.where(rowblk == colhalf, jnp.concatenate([v, v], axis=1), 0.0).astype(BF16)
                upd = _dot(kzt, vpair)
                g_chunk = math.exp(seqlen * LOG_G[hh])
                retd_ref[0, hh] = ra * g_chunk + upd[:, 0:RET_HD]
                retd_ref[1, hh] = rb * g_chunk + upd[:, RET_HD:2 * RET_HD]
            fill(2)

            u = proj_scr[0:tile, 5 * w:5 * w + cw] * proj_scr[0:tile, 5 * w + cw:5 * w + 2 * cw]
            uext_scr[SUBLANES:SUBLANES + tile, :] = u
            um1 = uext_scr[SUBLANES - 1:SUBLANES - 1 + tile, :]
            um2 = uext_scr[SUBLANES - 2:SUBLANES - 2 + tile, :]
            taps = [convw_ref[:, j * cw:(j + 1) * cw] for j in range(CONV_K)]
            convy_scr[0:tile, :] = taps[0] * um2 + taps[1] * um1 + taps[2] * u
            tail = uext_scr[tile + SUBLANES - (CONV_K - 1):tile + SUBLANES, :]
            cnew_ref[0] = tail
            uext_scr[SUBLANES - (CONV_K - 1):SUBLANES, :] = jnp.where(seq_end, 0.0, tail)

            ud = proj_scr[tile:rows, 5 * w:5 * w + cw] * proj_scr[tile:rows, 5 * w + cw:5 * w + 2 * cw]
            cst = cst_ref[...]
            older = jnp.broadcast_to(cst[:, 0:1, :], (nseq, seqlen, cw)).reshape(drows, cw)
            newer = jnp.broadcast_to(cst[:, 1:2, :], (nseq, seqlen, cw)).reshape(drows, cw)
            r8 = lax.broadcasted_iota(jnp.int32, (drows, cw), 0) % seqlen
            ud1 = jnp.where(r8 == 0, newer, pltpu.roll(ud, 1, axis=0))
            ud2 = jnp.where(r8 == 0, older, jnp.where(r8 == 1, newer, pltpu.roll(ud, 2, axis=0)))
            convy_scr[tile:rows, :] = taps[0] * ud2 + taps[1] * ud1 + taps[2] * ud
            cnewd_ref[...] = ud.reshape(nseq, seqlen, cw)[:, seqlen - (CONV_K - 1):, :]
            fill(len(fillers) - len(units) + 1)

            outs = {}
            for hh in range(RET_HEADS):
                state = jnp.where(seq_start, 0.0, r_scr[hh])
                for c in range(n_chunks):
                    s = (scores[c, hh] * dec_scr[hh]).astype(BF16)
                    outs[c, hh] = _dot(s, vbs[c, hh]) + _dot(qbs[c, hh], state.astype(BF16)) * xi_scr[hh]
                    state = state * math.exp(chunk * LOG_G[hh]) + updates[c, hh]
                r_scr[hh] = state
                ret_ref[0, hh] = state
            d_outs = {}
            for hh in range(RET_HEADS):
                s = (d_scores[hh] * maskd_scr[hh, 0:drows, :]).astype(BF16)
                d_outs[hh] = _dot(s, d_vb[hh]) + d_cross[hh] * xid_scr[hh, 0:drows, :]
            for c, hh in units:
                fill(1)
                r0, c0 = c * chunk, hh * RET_HD
                g = proj_scr[r0:r0 + chunk, 3 * w + c0:3 * w + c0 + RET_HD]
                mix_scr[r0:r0 + chunk, c0:c0 + RET_HD] = _groupnorm_gate(
                    outs[c, hh], g, gng_ref[:, c0:c0 + RET_HD]).astype(BF16)
            for hh in range(RET_HEADS):
                c0 = hh * RET_HD
                g = proj_scr[tile:rows, 3 * w + c0:3 * w + c0 + RET_HD]
                mix_scr[tile:rows, c0:c0 + RET_HD] = _groupnorm_gate(
                    d_outs[hh], g, gng_ref[:, c0:c0 + RET_HD]).astype(BF16)
            fill(len(fillers))

        if dense:
            down = _dot(act_scr[...], wd_ref[...])
        if mixer:
            proj_piece(4 * w)()
        if dense:
            y = _rmsnorm(x1_scr[...] + down, gf_ref[...])
            y_ref[...] = y[0:tile, :]
            yd_ref[...] = y[tile:rows, :]
        if mixer:
            mix_scr[:, w:w + cw] = (proj_scr[:, 4 * w:4 * w + cw] * convy_scr[...]).astype(BF16)
            xkeep_scr[0:tile, :] = x
            xkeep_scr[tile:rows, :] = xd

    assert n_tiles >= 2
    @pl.when(i == 0)
    def _():
        setup()
        step(mixer=True, dense=False)

    @pl.when(i == 1)
    def _():
        for n, (r0, r1) in enumerate(wd_bands):
            copy, view = wd_band(n)
            copy.wait()
            wd_ref[r0:r1, :] = view[...].astype(BF16)
        step(mixer=True, dense=True)

    @pl.when((i > 1) & (i < n_tiles))
    def _():
        step(mixer=True, dense=True)

    @pl.when(i == n_tiles)
    def _():
        step(mixer=False, dense=True)


def _resident(shape):
    nd = len(shape)
    return pl.BlockSpec(shape, lambda *_: (0,) * nd, pipeline_mode=pl.Buffered(1))


def _layer_call(x, xd, state_conv, state_ret, ws):
    batch, seq, d = x.shape
    nseq_all, seqlen, _ = xd.shape
    tile = PROMPT_TILE
    tiles_per_seq = seq // tile
    n_tiles = batch * tiles_per_seq
    assert nseq_all % n_tiles == 0, "every prompt tile carries the same number of decode sequences"
    nseq = nseq_all // n_tiles
    drows = nseq * seqlen
    rows = tile + drows
    rope = _rope_table(np.arange(seq))
    roped = _rope_table(np.tile(PAST_LEN + np.arange(seqlen), nseq))
    in_cols = ws[1].shape[1]
    cw = ws[2].shape[-1] // CONV_K
    mixer_tile = lambda i: jnp.minimum(i, n_tiles - 1)
    dense_tile = lambda i: jnp.maximum(i - 1, 0)
    w_in, w_out, w_gate, w_up, w_down = [ws[m] for m in MATRIX_SLOTS]
    assert w_gate.shape == w_up.shape and w_gate.shape[1] == w_down.shape[0]
    d_ff = w_down.shape[0]
    mats = [w_in.shape, w_out.shape, (d, 2 * d_ff), w_down.shape]
    weight_specs = [pl.BlockSpec(memory_space=pl.ANY) if m in MATRIX_SLOTS else _resident(wt.shape)
                    for m, wt in enumerate(ws)]
    state_block = (nseq, RET_HEADS, RET_HD, RET_HD)
    outs = pl.pallas_call(
        functools.partial(_layer_kernel, n_tiles=n_tiles, tiles_per_seq=tiles_per_seq),
        grid=(n_tiles + 1,),
        in_specs=[pl.BlockSpec((tile, d), lambda i: (mixer_tile(i), 0)),
                  pl.BlockSpec((tile, 2 * RET_HD), lambda i: (mixer_tile(i) % tiles_per_seq, 0)),
                  pl.BlockSpec((drows, d), lambda i: (mixer_tile(i), 0)),
                  _resident((drows, 2 * RET_HD)),
                  pl.BlockSpec((nseq, CONV_K - 1, cw), lambda i: (mixer_tile(i), 0, 0)),
                  pl.BlockSpec(state_block, lambda i: (mixer_tile(i), 0, 0, 0))] + weight_specs,
        out_specs=[pl.BlockSpec((tile, d), lambda i: (dense_tile(i), 0)),
                   pl.BlockSpec((1, CONV_K - 1, cw), lambda i: (mixer_tile(i) // tiles_per_seq, 0, 0)),
                   pl.BlockSpec((1, RET_HEADS, RET_HD, RET_HD),
                                lambda i: (mixer_tile(i) // tiles_per_seq, 0, 0, 0)),
                   pl.BlockSpec((drows, d), lambda i: (dense_tile(i), 0)),
                   pl.BlockSpec((nseq, CONV_K - 1, cw), lambda i: (mixer_tile(i), 0, 0)),
                   pl.BlockSpec(state_block, lambda i: (mixer_tile(i), 0, 0, 0))],
        out_shape=[jax.ShapeDtypeStruct((batch * seq, d), F32),
                   jax.ShapeDtypeStruct((batch, CONV_K - 1, cw), F32),
                   jax.ShapeDtypeStruct((batch, RET_HEADS, RET_HD, RET_HD), F32),
                   jax.ShapeDtypeStruct((nseq_all * seqlen, d), F32),
                   jax.ShapeDtypeStruct((nseq_all, CONV_K - 1, cw), F32),
                   jax.ShapeDtypeStruct((nseq_all, RET_HEADS, RET_HD, RET_HD), F32)],
        scratch_shapes=[pltpu.VMEM(s, BF16) for s in mats] + [
                        pltpu.VMEM((STAGE_SLOTS * STAGE_ROWS, max(in_cols, d_ff, d)), F32),
                        pltpu.SemaphoreType.DMA((STAGE_SLOTS,)),
                        pltpu.VMEM((rows, in_cols), F32),
                        pltpu.VMEM((tile + SUBLANES, cw), F32),
                        pltpu.VMEM((rows, cw), F32),
                        pltpu.VMEM((rows, d), BF16),
                        pltpu.VMEM((rows, d), F32),
                        pltpu.VMEM((rows, d), F32),
                        pltpu.VMEM((rows, d), BF16),
                        pltpu.VMEM((rows, d), BF16),
                        pltpu.VMEM((rows, d_ff), BF16),
                        pltpu.VMEM((RET_HEADS, RET_HD, RET_HD), F32),
                        pltpu.VMEM((RET_HEADS, PROMPT_CHUNK, PROMPT_CHUNK), F32),
                        pltpu.VMEM((RET_HEADS, PROMPT_CHUNK, RET_HD), F32),
                        pltpu.VMEM((RET_HEADS, PROMPT_CHUNK, RET_HD), F32),
                        pltpu.VMEM((3, DECODE_PAD, RET_HD), F32),
                        pltpu.VMEM((RET_HEADS, DECODE_PAD, DECODE_PAD), F32),
                        pltpu.VMEM((RET_HEADS, DECODE_PAD, RET_HD), F32),
                        pltpu.VMEM((RET_HEADS, DECODE_PAD, RET_HD), F32)],
        compiler_params=pltpu.CompilerParams(dimension_semantics=("arbitrary",),
                                             vmem_limit_bytes=VMEM_LIMIT_BYTES),
        name="layer",
    )(x.reshape(batch * seq, d), rope, xd.reshape(nseq_all * seqlen, d), roped, state_conv, state_ret,
      *ws)
    y, cnew, ret, yd, cnewd, retd = outs
    return y.reshape(batch, seq, d), yd.reshape(nseq_all, seqlen, d), cnew, ret, cnewd, retd


def kernel(x_prompt, x_sample, state_conv, state_ret, norm1_g, w_in, conv_w, ret_gn_g, w_out, norm2_g,
           w_gate, w_up, w_down, norm_f_g):
    depth = w_in.shape[0]
    assert depth == 1, "the fused layer kernel takes a single layer"
    row = lambda g: g.reshape(1, -1)
    ws = (row(norm1_g[0]), w_in[0], row(conv_w[0]), row(ret_gn_g[0]), w_out[0],
          row(norm2_g[0]), w_gate[0], w_up[0], w_down[0], row(norm_f_g))
    y_p, y_s, cnew_p, ret_p, cnew_s, ret_s = _layer_call(x_prompt, x_sample, state_conv[0], state_ret[0], ws)
    return (y_p, y_s, cnew_p[None], ret_p[None], cnew_s[None], ret_s[None])
```

```python
import functools
import math

import numpy as np
import jax
import jax.numpy as jnp
from jax import lax
from jax.experimental import pallas as pl
from jax.experimental.pallas import tpu as pltpu

F32 = jnp.float32
BF16 = jnp.bfloat16

RET_HEADS = 4
RET_HD = 128
RET_WIDTH = RET_HEADS * RET_HD
CONV_K = 3
ROPE_BASE = 10000.0
NORM_EPS = 1e-6
GN_EPS = 1e-5
PAST_LEN = 16384

LOG_G = tuple(math.log1p(-(2.0 ** (-5.0 - h))) for h in range(RET_HEADS))

SUBLANES = 8
MXU_N = 256
MATRIX_SLOTS = (1, 4, 6, 7, 8)
STAGE_ROWS = 128
STAGE_SLOTS = 8
PROMPT_TILE = 256
PROMPT_CHUNK = 128
DECODE_PAD = 128
VMEM_LIMIT_BYTES = 62 * 1024 * 1024


def _rope_tables(pos):
    half = RET_HD // 2
    inv = ROPE_BASE ** (-np.arange(half, dtype=np.float64) / half)
    ang = np.asarray(pos, np.float64)[:, None] * inv[None, :]
    cos, sin = np.cos(ang), np.sin(ang)
    cs = np.concatenate([cos, cos], axis=1).astype(np.float32)
    sn = np.concatenate([-sin, sin], axis=1).astype(np.float32)
    return jnp.asarray(cs), jnp.asarray(sn)


def _rmsnorm(x, g):
    ms = jnp.mean(x * x, axis=-1, keepdims=True)
    return x * lax.rsqrt(ms + NORM_EPS) * g


def _silu(x):
    return x * (1.0 / (1.0 + jnp.exp(-x)))


def _rotary(t, cs, sn):
    return t * cs + pltpu.roll(t, RET_HD // 2, axis=1) * sn


def _dot(a, b):
    return jnp.dot(a, b, preferred_element_type=F32)


def _dot_nt(a, b):
    return lax.dot_general(a, b, (((1,), (1,)), ((), ())), preferred_element_type=F32)


def _dot_tn(a, b):
    return lax.dot_general(a, b, (((0,), (0,)), ((), ())), preferred_element_type=F32)


def _groupnorm_gate(o, g, gn_g):
    mu = jnp.mean(o, axis=-1, keepdims=True)
    d = o - mu
    var = jnp.mean(d * d, axis=-1, keepdims=True)
    return _silu(g) * (d * lax.rsqrt(var + GN_EPS) * gn_g)


def _swiglu_piece(h_scr, wgu_ref, act_scr, c0):
    r = _dot(h_scr[...], wgu_ref[:, 2 * c0:2 * c0 + 2 * MXU_N])
    act_scr[:, c0:c0 + MXU_N] = (_silu(r[:, 0:MXU_N]) * r[:, MXU_N:2 * MXU_N]).astype(BF16)


def _load_weights_bf16(pairs, stage, sem):
    rb = STAGE_ROWS
    slots, width = stage.shape[0] // rb, stage.shape[1]
    assert slots >= 2 and slots <= sem.shape[0]
    chunks = []
    for w_hbm, w_bf, lane in pairs:
        rows, cols = w_hbm.shape
        assert rows % rb == 0 and cols <= width and (lane is None or cols % MXU_N == 0)
        chunks += [(w_hbm, w_bf, r0, cols, lane) for r0 in range(0, rows, rb)]

    def slot_view(n):
        s0 = (n % slots) * rb
        return stage.at[s0:s0 + rb, 0:chunks[n][3]]

    def chunk_copy(n):
        w_hbm, r0 = chunks[n][0], chunks[n][2]
        return pltpu.make_async_copy(w_hbm.at[r0:r0 + rb, :], slot_view(n), sem.at[n % slots])

    for n in range(min(slots - 1, len(chunks))):
        chunk_copy(n).start()
    for n, (_, w_bf, r0, cols, lane) in enumerate(chunks):
        chunk_copy(n).wait()
        if n + slots - 1 < len(chunks):
            chunk_copy(n + slots - 1).start()
        if lane is None:
            w_bf[r0:r0 + rb, :] = slot_view(n)[...].astype(BF16)
        else:
            for c0 in range(0, cols, MXU_N):
                d0 = 2 * c0 + lane * MXU_N
                w_bf[r0:r0 + rb, d0:d0 + MXU_N] = slot_view(n)[:, c0:c0 + MXU_N].astype(BF16)


def _layer_kernel(x_ref, cs_ref, sn_ref, xd_ref, csd_ref, snd_ref, cst_ref, sret_ref,
                  g1_ref, win_hbm, convw_ref, gng_ref, wout_hbm, g2_ref, wg_hbm, wu_hbm, wd_hbm, gf_ref,
                  y_ref, cnew_ref, ret_ref, yd_ref, cnewd_ref, retd_ref,
                  win_ref, wout_ref, wgu_ref, wd_ref, stage_scr, load_sem,
                  proj_scr, uext_scr, convy_scr, mix_scr, xkeep_scr, x1_scr, h_scr, h2_scr, act_scr, r_scr,
                  dec_scr, xi_scr, zeta_scr, pad_scr, maskd_scr, xid_scr, zetad_scr,
                  *, n_tiles, tiles_per_seq):
    tile = x_ref.shape[0]
    drows = xd_ref.shape[0]
    nseq = sret_ref.shape[0]
    seqlen = drows // nseq
    rows = tile + drows
    pad = DECODE_PAD
    assert seqlen == SUBLANES, "decode sequences must fill exactly one f32 sublane tile"
    assert nseq == 2 and drows <= pad and tile % 16 == 0 and drows % 16 == 0
    i = pl.program_id(0)
    chunk = PROMPT_CHUNK
    assert tile % chunk == 0
    t = jnp.minimum(i, n_tiles - 1) % tiles_per_seq
    seq_start = t == 0
    seq_end = t == tiles_per_seq - 1
    w = RET_WIDTH
    cw = w
    d_ff = wd_ref.shape[0]

    band_rows = stage_scr.shape[0]
    d_model = wd_ref.shape[1]
    wd_bands = [(r0, min(r0 + band_rows, d_ff)) for r0 in range(0, d_ff, band_rows)]
    assert len(wd_bands) * d_model <= stage_scr.shape[1] and len(wd_bands) <= load_sem.shape[0]

    def wd_band(n):
        r0, r1 = wd_bands[n]
        view = stage_scr.at[0:r1 - r0, n * d_model:(n + 1) * d_model]
        return pltpu.make_async_copy(wd_hbm.at[r0:r1, :], view, load_sem.at[n]), view

    def setup():
        _load_weights_bf16(((win_hbm, win_ref, None), (wout_hbm, wout_ref, None), (wg_hbm, wgu_ref, 0),
                            (wu_hbm, wgu_ref, 1)), stage_scr, load_sem)
        for n in range(len(wd_bands)):
            wd_band(n)[0].start()
        ii = lax.broadcasted_iota(jnp.int32, (chunk, chunk), 0)
        jj = lax.broadcasted_iota(jnp.int32, (chunk, chunk), 1)
        causal = ii >= jj
        diff = jnp.where(causal, ii - jj, 0).astype(F32)
        i_f = lax.broadcasted_iota(jnp.int32, (chunk, RET_HD), 0).astype(F32)
        pi = lax.broadcasted_iota(jnp.int32, (pad, pad), 0)
        pj = lax.broadcasted_iota(jnp.int32, (pad, pad), 1)
        keep = (pi >= pj) & ((pi // seqlen) == (pj // seqlen))
        pdiff = jnp.where(keep, pi - pj, 0).astype(F32)
        pos = (lax.broadcasted_iota(jnp.int32, (pad, RET_HD), 0) % seqlen).astype(F32)
        for hh in range(RET_HEADS):
            dec_scr[hh] = jnp.where(causal, jnp.exp(diff * LOG_G[hh]), 0.0)
            xi_scr[hh] = jnp.exp((i_f + 1.0) * LOG_G[hh])
            zeta_scr[hh] = jnp.exp((chunk - 1.0 - i_f) * LOG_G[hh])
            maskd_scr[hh] = jnp.where(keep, jnp.exp(pdiff * LOG_G[hh]), 0.0)
            xid_scr[hh] = jnp.exp((pos + 1.0) * LOG_G[hh])
            zetad_scr[hh] = jnp.exp((seqlen - 1.0 - pos) * LOG_G[hh])
        r_scr[...] = jnp.zeros_like(r_scr)
        uext_scr[0:SUBLANES, :] = jnp.zeros((SUBLANES, uext_scr.shape[1]), F32)
        pad_scr[...] = jnp.zeros_like(pad_scr)

    def ffn_piece(c0):
        return functools.partial(_swiglu_piece, h2_scr, wgu_ref, act_scr, c0)

    def proj_piece(c0):
        def emit():
            proj_scr[:, c0:c0 + cw] = _dot(h_scr[...], win_ref[:, c0:c0 + cw])
        return emit

    def step(mixer, dense):
        fillers = []

        def fill(n):
            for _ in range(min(n, len(fillers))):
                fillers.pop(0)()

        if dense:
            x1_scr[...] = xkeep_scr[...] + _dot(mix_scr[...], wout_ref[...])
        if mixer:
            x = x_ref[...]
            xd = xd_ref[...]
            h_scr[0:tile, :] = _rmsnorm(x, g1_ref[...]).astype(BF16)
            h_scr[tile:rows, :] = _rmsnorm(xd, g1_ref[...]).astype(BF16)
            proj_scr[:, 0:2 * w] = _dot(h_scr[...], win_ref[:, 0:2 * w])
        if dense:
            h2_scr[...] = _rmsnorm(x1_scr[...], g2_ref[...]).astype(BF16)
            fillers += [ffn_piece(c0) for c0 in range(0, d_ff, MXU_N)]
        if not mixer:
            fill(len(fillers))
        else:
            proj_scr[:, 2 * w:4 * w] = _dot(h_scr[...], win_ref[:, 2 * w:4 * w])
            fillers[0:0] = [proj_piece(5 * w), proj_piece(6 * w)]

            n_chunks = tile // chunk
            units = [(c, hh) for c in range(n_chunks) for hh in range(RET_HEADS)]
            qbs, vbs, scores, updates = {}, {}, {}, {}
            for c, hh in units:
                r0, c0 = c * chunk, hh * RET_HD
                cs = cs_ref[r0:r0 + chunk, :]
                sn = sn_ref[r0:r0 + chunk, :]
                q = _rotary(proj_scr[r0:r0 + chunk, c0:c0 + RET_HD], cs, sn)
                k = _rotary(proj_scr[r0:r0 + chunk, w + c0:w + c0 + RET_HD], cs, sn) * (RET_HD ** -0.5)
                qb, kb = q.astype(BF16), k.astype(BF16)
                vb = proj_scr[r0:r0 + chunk, 2 * w + c0:2 * w + c0 + RET_HD].astype(BF16)
                kz = (k * zeta_scr[hh]).astype(BF16)
                qbs[c, hh], vbs[c, hh] = qb, vb
                scores[c, hh] = _dot_nt(qb, kb)
                updates[c, hh] = _dot_tn(kz, vb)

            csd = csd_ref[...]
            snd = snd_ref[...]
            rowblk = lax.broadcasted_iota(jnp.int32, (pad, 2 * RET_HD), 0) // seqlen
            colhalf = lax.broadcasted_iota(jnp.int32, (pad, 2 * RET_HD), 1) // RET_HD
            d_scores, d_vb, d_cross = {}, {}, {}
            for hh in range(RET_HEADS):
                c0 = hh * RET_HD
                pad_scr[0, 0:drows, :] = _rotary(proj_scr[tile:rows, c0:c0 + RET_HD], csd, snd)
                pad_scr[1, 0:drows, :] = (_rotary(proj_scr[tile:rows, w + c0:w + c0 + RET_HD], csd, snd)
                                          * (RET_HD ** -0.5))
                pad_scr[2, 0:drows, :] = proj_scr[tile:rows, 2 * w + c0:2 * w + c0 + RET_HD]
                q, k, v = pad_scr[0], pad_scr[1], pad_scr[2]
                qb, kb, vb = q.astype(BF16), k.astype(BF16), v.astype(BF16)
                d_scores[hh] = _dot_nt(qb[0:drows, :], kb)
                d_vb[hh] = vb
                kzt = (k * zetad_scr[hh]).T.astype(BF16)
                ra = sret_ref[0, hh]
                rb = sret_ref[1, hh]
                rcat = jnp.concatenate([ra, rb], axis=1).astype(BF16)
                pr = _dot(qb[0:drows, :], rcat)
                d_cross[hh] = jnp.concatenate([pr[0:seqlen, 0:RET_HD], pr[seqlen:drows, RET_HD:2 * RET_HD]],
                                              axis=0)
                vpair = jnp.where(rowblk == colhalf, jnp.concatenate([v, v], axis=1), 0.0).astype(BF16)
                upd = _dot(kzt, vpair)
                g_chunk = math.exp(seqlen * LOG_G[hh])
                retd_ref[0, hh] = ra * g_chunk + upd[:, 0:RET_HD]
                retd_ref[1, hh] = rb * g_chunk + upd[:, RET_HD:2 * RET_HD]
            fill(2)

            u = proj_scr[0:tile, 5 * w:5 * w + cw] * proj_scr[0:tile, 5 * w + cw:5 * w + 2 * cw]
            uext_scr[SUBLANES:SUBLANES + tile, :] = u
            um1 = uext_scr[SUBLANES - 1:SUBLANES - 1 + tile, :]
            um2 = uext_scr[SUBLANES - 2:SUBLANES - 2 + tile, :]
            taps = [convw_ref[:, j * cw:(j + 1) * cw] for j in range(CONV_K)]
            convy_scr[0:tile, :] = taps[0] * um2 + taps[1] * um1 + taps[2] * u
            tail = uext_scr[tile + SUBLANES - (CONV_K - 1):tile + SUBLANES, :]
            cnew_ref[0] = tail
            uext_scr[SUBLANES - (CONV_K - 1):SUBLANES, :] = jnp.where(seq_end, 0.0, tail)

            ud = proj_scr[tile:rows, 5 * w:5 * w + cw] * proj_scr[tile:rows, 5 * w + cw:5 * w + 2 * cw]
            cst = cst_ref[...]
            older = jnp.broadcast_to(cst[:, 0:1, :], (nseq, seqlen, cw)).reshape(drows, cw)
            newer = jnp.broadcast_to(cst[:, 1:2, :], (nseq, seqlen, cw)).reshape(drows, cw)
            r8 = lax.broadcasted_iota(jnp.int32, (drows, cw), 0) % seqlen
            ud1 = jnp.where(r8 == 0, newer, pltpu.roll(ud, 1, axis=0))
            ud2 = jnp.where(r8 == 0, older, jnp.where(r8 == 1, newer, pltpu.roll(ud, 2, axis=0)))
            convy_scr[tile:rows, :] = taps[0] * ud2 + taps[1] * ud1 + taps[2] * ud
            cnewd_ref[...] = ud.reshape(nseq, seqlen, cw)[:, seqlen - (CONV_K - 1):, :]
            fill(len(fillers) - len(units) + 1)

            outs = {}
            for hh in range(RET_HEADS):
                state = jnp.where(seq_start, 0.0, r_scr[hh])
                for c in range(n_chunks):
                    s = (scores[c, hh] * dec_scr[hh]).astype(BF16)
                    outs[c, hh] = _dot(s, vbs[c, hh]) + _dot(qbs[c, hh], state.astype(BF16)) * xi_scr[hh]
                    state = state * math.exp(chunk * LOG_G[hh]) + updates[c, hh]
                r_scr[hh] = state
                ret_ref[0, hh] = state
            d_outs = {}
            for hh in range(RET_HEADS):
                s = (d_scores[hh] * maskd_scr[hh, 0:drows, :]).astype(BF16)
                d_outs[hh] = _dot(s, d_vb[hh]) + d_cross[hh] * xid_scr[hh, 0:drows, :]
            for c, hh in units:
                fill(1)
                r0, c0 = c * chunk, hh * RET_HD
                g = proj_scr[r0:r0 + chunk, 3 * w + c0:3 * w + c0 + RET_HD]
                mix_scr[r0:r0 + chunk, c0:c0 + RET_HD] = _groupnorm_gate(
                    outs[c, hh], g, gng_ref[:, c0:c0 + RET_HD]).astype(BF16)
            for hh in range(RET_HEADS):
                c0 = hh * RET_HD
                g = proj_scr[tile:rows, 3 * w + c0:3 * w + c0 + RET_HD]
                mix_scr[tile:rows, c0:c0 + RET_HD] = _groupnorm_gate(
                    d_outs[hh], g, gng_ref[:, c0:c0 + RET_HD]).astype(BF16)
            fill(len(fillers))

        if dense:
            down = _dot(act_scr[...], wd_ref[...])
        if mixer:
            proj_piece(4 * w)()
        if dense:
            y = _rmsnorm(x1_scr[...] + down, gf_ref[...])
            y_ref[...] = y[0:tile, :]
            yd_ref[...] = y[tile:rows, :]
        if mixer:
            mix_scr[:, w:w + cw] = (proj_scr[:, 4 * w:4 * w + cw] * convy_scr[...]).astype(BF16)
            xkeep_scr[0:tile, :] = x
            xkeep_scr[tile:rows, :] = xd

    assert n_tiles >= 2
    @pl.when(i == 0)
    def _():
        setup()

    @pl.when(i < 1)
    def _():
        step(mixer=True, dense=False)

    @pl.when(i == 1)
    def _():
        for n, (r0, r1) in enumerate(wd_bands):
            copy, view = wd_band(n)
            copy.wait()
            wd_ref[r0:r1, :] = view[...].astype(BF16)
        step(mixer=True, dense=True)

    @pl.when((i > 1) & (i < n_tiles))
    def _():
        step(mixer=True, dense=True)

    @pl.when(i == n_tiles)
    def _():
        step(mixer=False, dense=True)


def _resident(shape):
    nd = len(shape)
    return pl.BlockSpec(shape, lambda *_: (0,) * nd, pipeline_mode=pl.Buffered(1))


def _layer_call(x, xd, state_conv, state_ret, ws):
    batch, seq, d = x.shape
    nseq_all, seqlen, _ = xd.shape
    tile = PROMPT_TILE
    tiles_per_seq = seq // tile
    n_tiles = batch * tiles_per_seq
    assert nseq_all % n_tiles == 0, "every prompt tile carries the same number of decode sequences"
    nseq = nseq_all // n_tiles
    drows = nseq * seqlen
    rows = tile + drows
    cs, sn = _rope_tables(np.arange(seq))
    csd, snd = _rope_tables(np.tile(PAST_LEN + np.arange(seqlen), nseq))
    in_cols = ws[1].shape[1]
    cw = ws[2].shape[-1] // CONV_K
    mixer_tile = lambda i: jnp.minimum(i, n_tiles - 1)
    dense_tile = lambda i: jnp.maximum(i - 1, 0)
    w_in, w_out, w_gate, w_up, w_down = [ws[m] for m in MATRIX_SLOTS]
    assert w_gate.shape == w_up.shape and w_gate.shape[1] == w_down.shape[0]
    d_ff = w_down.shape[0]
    mats = [w_in.shape, w_out.shape, (d, 2 * d_ff), w_down.shape]
    weight_specs = [pl.BlockSpec(memory_space=pl.ANY) if m in MATRIX_SLOTS else _resident(wt.shape)
                    for m, wt in enumerate(ws)]
    state_block = (nseq, RET_HEADS, RET_HD, RET_HD)
    outs = pl.pallas_call(
        functools.partial(_layer_kernel, n_tiles=n_tiles, tiles_per_seq=tiles_per_seq),
        grid=(n_tiles + 1,),
        in_specs=[pl.BlockSpec((tile, d), lambda i: (mixer_tile(i), 0)),
                  pl.BlockSpec((tile, RET_HD), lambda i: (mixer_tile(i) % tiles_per_seq, 0)),
                  pl.BlockSpec((tile, RET_HD), lambda i: (mixer_tile(i) % tiles_per_seq, 0)),
                  pl.BlockSpec((drows, d), lambda i: (mixer_tile(i), 0)),
                  _resident((drows, RET_HD)), _resident((drows, RET_HD)),
                  pl.BlockSpec((nseq, CONV_K - 1, cw), lambda i: (mixer_tile(i), 0, 0)),
                  pl.BlockSpec(state_block, lambda i: (mixer_tile(i), 0, 0, 0))] + weight_specs,
        out_specs=[pl.BlockSpec((tile, d), lambda i: (dense_tile(i), 0)),
                   pl.BlockSpec((1, CONV_K - 1, cw), lambda i: (mixer_tile(i) // tiles_per_seq, 0, 0)),
                   pl.BlockSpec((1, RET_HEADS, RET_HD, RET_HD),
                                lambda i: (mixer_tile(i) // tiles_per_seq, 0, 0, 0)),
                   pl.BlockSpec((drows, d), lambda i: (dense_tile(i), 0)),
                   pl.BlockSpec((nseq, CONV_K - 1, cw), lambda i: (mixer_tile(i), 0, 0)),
                   pl.BlockSpec(state_block, lambda i: (mixer_tile(i), 0, 0, 0))],
        out_shape=[jax.ShapeDtypeStruct((batch * seq, d), F32),
                   jax.ShapeDtypeStruct((batch, CONV_K - 1, cw), F32),
                   jax.ShapeDtypeStruct((batch, RET_HEADS, RET_HD, RET_HD), F32),
                   jax.ShapeDtypeStruct((nseq_all * seqlen, d), F32),
                   jax.ShapeDtypeStruct((nseq_all, CONV_K - 1, cw), F32),
                   jax.ShapeDtypeStruct((nseq_all, RET_HEADS, RET_HD, RET_HD), F32)],
        scratch_shapes=[pltpu.VMEM(s, BF16) for s in mats] + [
                        pltpu.VMEM((STAGE_SLOTS * STAGE_ROWS, max(in_cols, d_ff, d)), F32),
                        pltpu.SemaphoreType.DMA((STAGE_SLOTS,)),
                        pltpu.VMEM((rows, in_cols), F32),
                        pltpu.VMEM((tile + SUBLANES, cw), F32),
                        pltpu.VMEM((rows, cw), F32),
                        pltpu.VMEM((rows, d), BF16),
                        pltpu.VMEM((rows, d), F32),
                        pltpu.VMEM((rows, d), F32),
                        pltpu.VMEM((rows, d), BF16),
                        pltpu.VMEM((rows, d), BF16),
                        pltpu.VMEM((rows, d_ff), BF16),
                        pltpu.VMEM((RET_HEADS, RET_HD, RET_HD), F32),
                        pltpu.VMEM((RET_HEADS, PROMPT_CHUNK, PROMPT_CHUNK), F32),
                        pltpu.VMEM((RET_HEADS, PROMPT_CHUNK, RET_HD), F32),
                        pltpu.VMEM((RET_HEADS, PROMPT_CHUNK, RET_HD), F32),
                        pltpu.VMEM((3, DECODE_PAD, RET_HD), F32),
                        pltpu.VMEM((RET_HEADS, DECODE_PAD, DECODE_PAD), F32),
                        pltpu.VMEM((RET_HEADS, DECODE_PAD, RET_HD), F32),
                        pltpu.VMEM((RET_HEADS, DECODE_PAD, RET_HD), F32)],
        compiler_params=pltpu.CompilerParams(dimension_semantics=("arbitrary",),
                                             vmem_limit_bytes=VMEM_LIMIT_BYTES),
        name="layer",
    )(x.reshape(batch * seq, d), cs, sn, xd.reshape(nseq_all * seqlen, d), csd, snd, state_conv, state_ret,
      *ws)
    y, cnew, ret, yd, cnewd, retd = outs
    return y.reshape(batch, seq, d), yd.reshape(nseq_all, seqlen, d), cnew, ret, cnewd, retd


def kernel(x_prompt, x_sample, state_conv, state_ret, norm1_g, w_in, conv_w, ret_gn_g, w_out, norm2_g,
           w_gate, w_up, w_down, norm_f_g):
    depth = w_in.shape[0]
    assert depth == 1, "the fused layer kernel takes a single layer"
    row = lambda g: g.reshape(1, -1)
    ws = (row(norm1_g[0]), w_in[0], row(conv_w[0]), row(ret_gn_g[0]), w_out[0],
          row(norm2_g[0]), w_gate[0], w_up[0], w_down[0], row(norm_f_g))
    y_p, y_s, cnew_p, ret_p, cnew_s, ret_s = _layer_call(x_prompt, x_sample, state_conv[0], state_ret[0], ws)
    return (y_p, y_s, cnew_p[None], ret_p[None], cnew_s[None], ret_s[None])
```

```python
import functools
import math

import numpy as np
import jax
import jax.numpy as jnp
from jax import lax
from jax.experimental import pallas as pl
from jax.experimental.pallas import tpu as pltpu

F32 = jnp.float32
BF16 = jnp.bfloat16

RET_HEADS = 4
RET_HD = 128
RET_WIDTH = RET_HEADS * RET_HD
CONV_K = 3
ROPE_BASE = 10000.0
NORM_EPS = 1e-6
GN_EPS = 1e-5
PAST_LEN = 16384

LOG_G = tuple(math.log1p(-(2.0 ** (-5.0 - h))) for h in range(RET_HEADS))

SUBLANES = 8
MXU_N = 256
MATRIX_SLOTS = (1, 4, 6, 7, 8)
STAGE_ROWS = 128
STAGE_SLOTS = 8
PROMPT_TILE = 256
PROMPT_CHUNK = 128
DECODE_PAD = 128
VMEM_LIMIT_BYTES = 63 * 1024 * 1024


def _rope_tables(pos):
    half = RET_HD // 2
    inv = ROPE_BASE ** (-np.arange(half, dtype=np.float64) / half)
    ang = np.asarray(pos, np.float64)[:, None] * inv[None, :]
    cos, sin = np.cos(ang), np.sin(ang)
    cs = np.concatenate([cos, cos], axis=1).astype(np.float32)
    sn = np.concatenate([-sin, sin], axis=1).astype(np.float32)
    return jnp.asarray(cs), jnp.asarray(sn)


def _rmsnorm(x, g):
    ms = jnp.mean(x * x, axis=-1, keepdims=True)
    return x * lax.rsqrt(ms + NORM_EPS) * g


def _silu(x):
    return x * (1.0 / (1.0 + jnp.exp(-x)))


def _rotary(t, cs, sn):
    return t * cs + pltpu.roll(t, RET_HD // 2, axis=1) * sn


def _dot(a, b):
    return jnp.dot(a, b, preferred_element_type=F32)


def _dot_nt(a, b):
    return lax.dot_general(a, b, (((1,), (1,)), ((), ())), preferred_element_type=F32)


def _dot_tn(a, b):
    return lax.dot_general(a, b, (((0,), (0,)), ((), ())), preferred_element_type=F32)


def _groupnorm_gate(o, g, gn_g):
    mu = jnp.mean(o, axis=-1, keepdims=True)
    d = o - mu
    var = jnp.mean(d * d, axis=-1, keepdims=True)
    return _silu(g) * (d * lax.rsqrt(var + GN_EPS) * gn_g)


def _swiglu_piece(h_scr, wgu_ref, act_scr, c0):
    r = _dot(h_scr[...], wgu_ref[:, 2 * c0:2 * c0 + 2 * MXU_N])
    act_scr[:, c0:c0 + MXU_N] = (_silu(r[:, 0:MXU_N]) * r[:, MXU_N:2 * MXU_N]).astype(BF16)


def _load_weights_bf16(pairs, stage, sem):
    rb = STAGE_ROWS
    slots, width = stage.shape[0] // rb, stage.shape[1]
    assert slots >= 2 and slots <= sem.shape[0]
    chunks = []
    for w_hbm, w_bf, lane in pairs:
        rows, cols = w_hbm.shape
        assert rows % rb == 0 and cols <= width and (lane is None or cols % MXU_N == 0)
        chunks += [(w_hbm, w_bf, r0, cols, lane) for r0 in range(0, rows, rb)]

    def slot_view(n):
        s0 = (n % slots) * rb
        return stage.at[s0:s0 + rb, 0:chunks[n][3]]

    def chunk_copy(n):
        w_hbm, r0 = chunks[n][0], chunks[n][2]
        return pltpu.make_async_copy(w_hbm.at[r0:r0 + rb, :], slot_view(n), sem.at[n % slots])

    for n in range(min(slots - 1, len(chunks))):
        chunk_copy(n).start()
    for n, (_, w_bf, r0, cols, lane) in enumerate(chunks):
        chunk_copy(n).wait()
        if n + slots - 1 < len(chunks):
            chunk_copy(n + slots - 1).start()
        if lane is None:
            w_bf[r0:r0 + rb, :] = slot_view(n)[...].astype(BF16)
        else:
            for c0 in range(0, cols, MXU_N):
                d0 = 2 * c0 + lane * MXU_N
                w_bf[r0:r0 + rb, d0:d0 + MXU_N] = slot_view(n)[:, c0:c0 + MXU_N].astype(BF16)


def _layer_kernel(x_ref, cs_ref, sn_ref, xd_ref, csd_ref, snd_ref, cst_ref, sret_ref,
                  g1_ref, win_hbm, convw_ref, gng_ref, wout_hbm, g2_ref, wg_hbm, wu_hbm, wd_hbm, gf_ref,
                  y_ref, cnew_ref, ret_ref, yd_ref, cnewd_ref, retd_ref,
                  win_ref, wout_ref, wgu_ref, wd_ref, stage_scr, load_sem,
                  proj_scr, uext_scr, convy_scr, mix_scr, xkeep_scr, x1_scr, h_scr, h2_scr, act_scr, r_scr,
                  dec_scr, xi_scr, zeta_scr, pad_scr, maskd_scr, xid_scr, zetad_scr,
                  *, n_tiles, tiles_per_seq):
    tile = x_ref.shape[0]
    drows = xd_ref.shape[0]
    nseq = sret_ref.shape[0]
    seqlen = drows // nseq
    rows = tile + drows
    pad = DECODE_PAD
    assert seqlen == SUBLANES, "decode sequences must fill exactly one f32 sublane tile"
    assert nseq == 2 and drows <= pad and tile % 16 == 0 and drows % 16 == 0
    i = pl.program_id(0)
    chunk = PROMPT_CHUNK
    assert tile % chunk == 0
    t = jnp.minimum(i, n_tiles - 1) % tiles_per_seq
    seq_start = t == 0
    seq_end = t == tiles_per_seq - 1
    w = RET_WIDTH
    cw = w
    d_ff = wd_ref.shape[0]

    band_rows = stage_scr.shape[0]
    d_model = wd_ref.shape[1]
    wd_bands = [(r0, min(r0 + band_rows, d_ff)) for r0 in range(0, d_ff, band_rows)]
    assert len(wd_bands) * d_model <= stage_scr.shape[1] and len(wd_bands) <= load_sem.shape[0]

    def wd_band(n):
        r0, r1 = wd_bands[n]
        view = stage_scr.at[0:r1 - r0, n * d_model:(n + 1) * d_model]
        return pltpu.make_async_copy(wd_hbm.at[r0:r1, :], view, load_sem.at[n]), view

    def setup():
        _load_weights_bf16(((win_hbm, win_ref, None), (wout_hbm, wout_ref, None), (wg_hbm, wgu_ref, 0),
                            (wu_hbm, wgu_ref, 1)), stage_scr, load_sem)
        for n in range(len(wd_bands)):
            wd_band(n)[0].start()
        ii = lax.broadcasted_iota(jnp.int32, (chunk, chunk), 0)
        jj = lax.broadcasted_iota(jnp.int32, (chunk, chunk), 1)
        causal = ii >= jj
        diff = jnp.where(causal, ii - jj, 0).astype(F32)
        i_f = lax.broadcasted_iota(jnp.int32, (chunk, RET_HD), 0).astype(F32)
        pi = lax.broadcasted_iota(jnp.int32, (pad, pad), 0)
        pj = lax.broadcasted_iota(jnp.int32, (pad, pad), 1)
        keep = (pi >= pj) & ((pi // seqlen) == (pj // seqlen))
        pdiff = jnp.where(keep, pi - pj, 0).astype(F32)
        pos = (lax.broadcasted_iota(jnp.int32, (pad, RET_HD), 0) % seqlen).astype(F32)
        for hh in range(RET_HEADS):
            dec_scr[hh] = jnp.where(causal, jnp.exp(diff * LOG_G[hh]), 0.0)
            xi_scr[hh] = jnp.exp((i_f + 1.0) * LOG_G[hh])
            zeta_scr[hh] = jnp.exp((chunk - 1.0 - i_f) * LOG_G[hh])
            maskd_scr[hh] = jnp.where(keep, jnp.exp(pdiff * LOG_G[hh]), 0.0)
            xid_scr[hh] = jnp.exp((pos + 1.0) * LOG_G[hh])
            zetad_scr[hh] = jnp.exp((seqlen - 1.0 - pos) * LOG_G[hh])
        r_scr[...] = jnp.zeros_like(r_scr)
        uext_scr[0:SUBLANES, :] = jnp.zeros((SUBLANES, uext_scr.shape[1]), F32)
        pad_scr[...] = jnp.zeros_like(pad_scr)

    def ffn_piece(c0):
        return functools.partial(_swiglu_piece, h2_scr, wgu_ref, act_scr, c0)

    def proj_piece(c0):
        def emit():
            proj_scr[:, c0:c0 + cw] = _dot(h_scr[...], win_ref[:, c0:c0 + cw])
        return emit

    def step(mixer, dense, late=()):
        fillers = []

        def fill(n):
            for _ in range(min(n, len(fillers))):
                fillers.pop(0)()

        if dense:
            x1_scr[...] = xkeep_scr[...] + _dot(mix_scr[...], wout_ref[...])
        if mixer:
            x = x_ref[...]
            xd = xd_ref[...]
            h_scr[0:tile, :] = _rmsnorm(x, g1_ref[...]).astype(BF16)
            h_scr[tile:rows, :] = _rmsnorm(xd, g1_ref[...]).astype(BF16)
            proj_scr[:, 0:2 * w] = _dot(h_scr[...], win_ref[:, 0:2 * w])
        if dense:
            h2_scr[...] = _rmsnorm(x1_scr[...], g2_ref[...]).astype(BF16)
            fillers += [ffn_piece(c0) for c0 in range(0, d_ff, MXU_N)]
            fillers += list(late)
        if not mixer:
            fill(len(fillers))
        else:
            proj_scr[:, 2 * w:4 * w] = _dot(h_scr[...], win_ref[:, 2 * w:4 * w])
            fillers[0:0] = [proj_piece(5 * w), proj_piece(6 * w)]

            n_chunks = tile // chunk
            units = [(c, hh) for c in range(n_chunks) for hh in range(RET_HEADS)]
            qbs, vbs, scores, updates = {}, {}, {}, {}
            for c, hh in units:
                r0, c0 = c * chunk, hh * RET_HD
                cs = cs_ref[r0:r0 + chunk, :]
                sn = sn_ref[r0:r0 + chunk, :]
                q = _rotary(proj_scr[r0:r0 + chunk, c0:c0 + RET_HD], cs, sn)
                k = _rotary(proj_scr[r0:r0 + chunk, w + c0:w + c0 + RET_HD], cs, sn) * (RET_HD ** -0.5)
                qb, kb = q.astype(BF16), k.astype(BF16)
                vb = proj_scr[r0:r0 + chunk, 2 * w + c0:2 * w + c0 + RET_HD].astype(BF16)
                kz = (k * zeta_scr[hh]).astype(BF16)
                qbs[c, hh], vbs[c, hh] = qb, vb
                scores[c, hh] = _dot_nt(qb, kb)
                updates[c, hh] = _dot_tn(kz, vb)

            csd = csd_ref[...]
            snd = snd_ref[...]
            rowblk = lax.broadcasted_iota(jnp.int32, (pad, 2 * RET_HD), 0) // seqlen
            colhalf = lax.broadcasted_iota(jnp.int32, (pad, 2 * RET_HD), 1) // RET_HD
            d_scores, d_vb, d_cross = {}, {}, {}
            for hh in range(RET_HEADS):
                c0 = hh * RET_HD
                pad_scr[0, 0:drows, :] = _rotary(proj_scr[tile:rows, c0:c0 + RET_HD], csd, snd)
                pad_scr[1, 0:drows, :] = (_rotary(proj_scr[tile:rows, w + c0:w + c0 + RET_HD], csd, snd)
                                          * (RET_HD ** -0.5))
                pad_scr[2, 0:drows, :] = proj_scr[tile:rows, 2 * w + c0:2 * w + c0 + RET_HD]
                q, k, v = pad_scr[0], pad_scr[1], pad_scr[2]
                qb, kb, vb = q.astype(BF16), k.astype(BF16), v.astype(BF16)
                d_scores[hh] = _dot_nt(qb[0:drows, :], kb)
                d_vb[hh] = vb
                kzt = (k * zetad_scr[hh]).T.astype(BF16)
                ra = sret_ref[0, hh]
                rb = sret_ref[1, hh]
                rcat = jnp.concatenate([ra, rb], axis=1).astype(BF16)
                pr = _dot(qb[0:drows, :], rcat)
                d_cross[hh] = jnp.concatenate([pr[0:seqlen, 0:RET_HD], pr[seqlen:drows, RET_HD:2 * RET_HD]],
                                              axis=0)
                vpair = jnp.where(rowblk == colhalf, jnp.concatenate([v, v], axis=1), 0.0).astype(BF16)
                upd = _dot(kzt, vpair)
                g_chunk = math.exp(seqlen * LOG_G[hh])
                retd_ref[0, hh] = ra * g_chunk + upd[:, 0:RET_HD]
                retd_ref[1, hh] = rb * g_chunk + upd[:, RET_HD:2 * RET_HD]
            fill(2)

            u = proj_scr[0:tile, 5 * w:5 * w + cw] * proj_scr[0:tile, 5 * w + cw:5 * w + 2 * cw]
            uext_scr[SUBLANES:SUBLANES + tile, :] = u
            um1 = uext_scr[SUBLANES - 1:SUBLANES - 1 + tile, :]
            um2 = uext_scr[SUBLANES - 2:SUBLANES - 2 + tile, :]
            taps = [convw_ref[:, j * cw:(j + 1) * cw] for j in range(CONV_K)]
            convy_scr[0:tile, :] = taps[0] * um2 + taps[1] * um1 + taps[2] * u
            tail = uext_scr[tile + SUBLANES - (CONV_K - 1):tile + SUBLANES, :]
            cnew_ref[0] = tail
            uext_scr[SUBLANES - (CONV_K - 1):SUBLANES, :] = jnp.where(seq_end, 0.0, tail)

            ud = proj_scr[tile:rows, 5 * w:5 * w + cw] * proj_scr[tile:rows, 5 * w + cw:5 * w + 2 * cw]
            cst = cst_ref[...]
            older = jnp.broadcast_to(cst[:, 0:1, :], (nseq, seqlen, cw)).reshape(drows, cw)
            newer = jnp.broadcast_to(cst[:, 1:2, :], (nseq, seqlen, cw)).reshape(drows, cw)
            r8 = lax.broadcasted_iota(jnp.int32, (drows, cw), 0) % seqlen
            ud1 = jnp.where(r8 == 0, newer, pltpu.roll(ud, 1, axis=0))
            ud2 = jnp.where(r8 == 0, older, jnp.where(r8 == 1, newer, pltpu.roll(ud, 2, axis=0)))
            convy_scr[tile:rows, :] = taps[0] * ud2 + taps[1] * ud1 + taps[2] * ud
            cnewd_ref[...] = ud.reshape(nseq, seqlen, cw)[:, seqlen - (CONV_K - 1):, :]
            fill(len(fillers) - len(units) + 1)

            outs = {}
            for hh in range(RET_HEADS):
                state = jnp.where(seq_start, 0.0, r_scr[hh])
                for c in range(n_chunks):
                    s = (scores[c, hh] * dec_scr[hh]).astype(BF16)
                    outs[c, hh] = _dot(s, vbs[c, hh]) + _dot(qbs[c, hh], state.astype(BF16)) * xi_scr[hh]
                    state = state * math.exp(chunk * LOG_G[hh]) + updates[c, hh]
                r_scr[hh] = state
                ret_ref[0, hh] = state
            d_outs = {}
            for hh in range(RET_HEADS):
                s = (d_scores[hh] * maskd_scr[hh, 0:drows, :]).astype(BF16)
                d_outs[hh] = _dot(s, d_vb[hh]) + d_cross[hh] * xid_scr[hh, 0:drows, :]
            for c, hh in units:
                fill(1)
                r0, c0 = c * chunk, hh * RET_HD
                g = proj_scr[r0:r0 + chunk, 3 * w + c0:3 * w + c0 + RET_HD]
                mix_scr[r0:r0 + chunk, c0:c0 + RET_HD] = _groupnorm_gate(
                    outs[c, hh], g, gng_ref[:, c0:c0 + RET_HD]).astype(BF16)
            for hh in range(RET_HEADS):
                c0 = hh * RET_HD
                g = proj_scr[tile:rows, 3 * w + c0:3 * w + c0 + RET_HD]
                mix_scr[tile:rows, c0:c0 + RET_HD] = _groupnorm_gate(
                    d_outs[hh], g, gng_ref[:, c0:c0 + RET_HD]).astype(BF16)
            fill(len(fillers))

        if dense:
            down = _dot(act_scr[...], wd_ref[...])
        if mixer:
            proj_piece(4 * w)()
        if dense:
            y = _rmsnorm(x1_scr[...] + down, gf_ref[...])
            y_ref[...] = y[0:tile, :]
            yd_ref[...] = y[tile:rows, :]
        if mixer:
            mix_scr[:, w:w + cw] = (proj_scr[:, 4 * w:4 * w + cw] * convy_scr[...]).astype(BF16)
            xkeep_scr[0:tile, :] = x
            xkeep_scr[tile:rows, :] = xd

    assert n_tiles >= 2
    @pl.when(i == 0)
    def _():
        setup()

    @pl.when(i < 1)
    def _():
        step(mixer=True, dense=False)

    @pl.when(i == 1)
    def _():
        def cast_band(n):
            def emit():
                r0, r1 = wd_bands[n]
                copy, view = wd_band(n)
                copy.wait()
                wd_ref[r0:r1, :] = view[...].astype(BF16)
            return emit
        step(mixer=True, dense=True, late=[cast_band(n) for n in range(len(wd_bands))])

    @pl.when((i > 1) & (i < n_tiles))
    def _():
        step(mixer=True, dense=True)

    @pl.when(i == n_tiles)
    def _():
        step(mixer=False, dense=True)


def _resident(shape):
    nd = len(shape)
    return pl.BlockSpec(shape, lambda *_: (0,) * nd, pipeline_mode=pl.Buffered(1))


def _layer_call(x, xd, state_conv, state_ret, ws):
    batch, seq, d = x.shape
    nseq_all, seqlen, _ = xd.shape
    tile = PROMPT_TILE
    tiles_per_seq = seq // tile
    n_tiles = batch * tiles_per_seq
    assert nseq_all % n_tiles == 0, "every prompt tile carries the same number of decode sequences"
    nseq = nseq_all // n_tiles
    drows = nseq * seqlen
    rows = tile + drows
    cs, sn = _rope_tables(np.arange(seq))
    csd, snd = _rope_tables(np.tile(PAST_LEN + np.arange(seqlen), nseq))
    in_cols = ws[1].shape[1]
    cw = ws[2].shape[-1] // CONV_K
    mixer_tile = lambda i: jnp.minimum(i, n_tiles - 1)
    dense_tile = lambda i: jnp.maximum(i - 1, 0)
    w_in, w_out, w_gate, w_up, w_down = [ws[m] for m in MATRIX_SLOTS]
    assert w_gate.shape == w_up.shape and w_gate.shape[1] == w_down.shape[0]
    d_ff = w_down.shape[0]
    mats = [w_in.shape, w_out.shape, (d, 2 * d_ff), w_down.shape]
    weight_specs = [pl.BlockSpec(memory_space=pl.ANY) if m in MATRIX_SLOTS else _resident(wt.shape)
                    for m, wt in enumerate(ws)]
    state_block = (nseq, RET_HEADS, RET_HD, RET_HD)
    outs = pl.pallas_call(
        functools.partial(_layer_kernel, n_tiles=n_tiles, tiles_per_seq=tiles_per_seq),
        grid=(n_tiles + 1,),
        in_specs=[pl.BlockSpec((tile, d), lambda i: (mixer_tile(i), 0)),
                  pl.BlockSpec((tile, RET_HD), lambda i: (mixer_tile(i) % tiles_per_seq, 0)),
                  pl.BlockSpec((tile, RET_HD), lambda i: (mixer_tile(i) % tiles_per_seq, 0)),
                  pl.BlockSpec((drows, d), lambda i: (mixer_tile(i), 0)),
                  _resident((drows, RET_HD)), _resident((drows, RET_HD)),
                  pl.BlockSpec((nseq, CONV_K - 1, cw), lambda i: (mixer_tile(i), 0, 0)),
                  pl.BlockSpec(state_block, lambda i: (mixer_tile(i), 0, 0, 0))] + weight_specs,
        out_specs=[pl.BlockSpec((tile, d), lambda i: (dense_tile(i), 0)),
                   pl.BlockSpec((1, CONV_K - 1, cw), lambda i: (mixer_tile(i) // tiles_per_seq, 0, 0)),
                   pl.BlockSpec((1, RET_HEADS, RET_HD, RET_HD),
                                lambda i: (mixer_tile(i) // tiles_per_seq, 0, 0, 0)),
                   pl.BlockSpec((drows, d), lambda i: (dense_tile(i), 0)),
                   pl.BlockSpec((nseq, CONV_K - 1, cw), lambda i: (mixer_tile(i), 0, 0)),
                   pl.BlockSpec(state_block, lambda i: (mixer_tile(i), 0, 0, 0))],
        out_shape=[jax.ShapeDtypeStruct((batch * seq, d), F32),
                   jax.ShapeDtypeStruct((batch, CONV_K - 1, cw), F32),
                   jax.ShapeDtypeStruct((batch, RET_HEADS, RET_HD, RET_HD), F32),
                   jax.ShapeDtypeStruct((nseq_all * seqlen, d), F32),
                   jax.ShapeDtypeStruct((nseq_all, CONV_K - 1, cw), F32),
                   jax.ShapeDtypeStruct((nseq_all, RET_HEADS, RET_HD, RET_HD), F32)],
        scratch_shapes=[pltpu.VMEM(s, BF16) for s in mats] + [
                        pltpu.VMEM((STAGE_SLOTS * STAGE_ROWS, max(in_cols, d_ff, d)), F32),
                        pltpu.SemaphoreType.DMA((STAGE_SLOTS,)),
                        pltpu.VMEM((rows, in_cols), F32),
                        pltpu.VMEM((tile + SUBLANES, cw), F32),
                        pltpu.VMEM((rows, cw), F32),
                        pltpu.VMEM((rows, d), BF16),
                        pltpu.VMEM((rows, d), F32),
                        pltpu.VMEM((rows, d), F32),
                        pltpu.VMEM((rows, d), BF16),
                        pltpu.VMEM((rows, d), BF16),
                        pltpu.VMEM((rows, d_ff), BF16),
                        pltpu.VMEM((RET_HEADS, RET_HD, RET_HD), F32),
                        pltpu.VMEM((RET_HEADS, PROMPT_CHUNK, PROMPT_CHUNK), F32),
                        pltpu.VMEM((RET_HEADS, PROMPT_CHUNK, RET_HD), F32),
                        pltpu.VMEM((RET_HEADS, PROMPT_CHUNK, RET_HD), F32),
                        pltpu.VMEM((3, DECODE_PAD, RET_HD), F32),
                        pltpu.VMEM((RET_HEADS, DECODE_PAD, DECODE_PAD), F32),
                        pltpu.VMEM((RET_HEADS, DECODE_PAD, RET_HD), F32),
                        pltpu.VMEM((RET_HEADS, DECODE_PAD, RET_HD), F32)],
        compiler_params=pltpu.CompilerParams(dimension_semantics=("arbitrary",),
                                             vmem_limit_bytes=VMEM_LIMIT_BYTES),
        name="layer",
    )(x.reshape(batch * seq, d), cs, sn, xd.reshape(nseq_all * seqlen, d), csd, snd, state_conv, state_ret,
      *ws)
    y, cnew, ret, yd, cnewd, retd = outs
    return y.reshape(batch, seq, d), yd.reshape(nseq_all, seqlen, d), cnew, ret, cnewd, retd


def kernel(x_prompt, x_sample, state_conv, state_ret, norm1_g, w_in, conv_w, ret_gn_g, w_out, norm2_g,
           w_gate, w_up, w_down, norm_f_g):
    depth = w_in.shape[0]
    assert depth == 1, "the fused layer kernel takes a single layer"
    row = lambda g: g.reshape(1, -1)
    ws = (row(norm1_g[0]), w_in[0], row(conv_w[0]), row(ret_gn_g[0]), w_out[0],
          row(norm2_g[0]), w_gate[0], w_up[0], w_down[0], row(norm_f_g))
    y_p, y_s, cnew_p, ret_p, cnew_s, ret_s = _layer_call(x_prompt, x_sample, state_conv[0], state_ret[0], ws)
    return (y_p, y_s, cnew_p[None], ret_p[None], cnew_s[None], ret_s[None])
```

```python
import functools
import math

import numpy as np
import jax
import jax.numpy as jnp
from jax import lax
from jax.experimental import pallas as pl
from jax.experimental.pallas import tpu as pltpu

F32 = jnp.float32
BF16 = jnp.bfloat16

RET_HEADS = 4
RET_HD = 128
RET_WIDTH = RET_HEADS * RET_HD
CONV_K = 3
ROPE_BASE = 10000.0
NORM_EPS = 1e-6
GN_EPS = 1e-5
PAST_LEN = 16384

LOG_G = tuple(math.log1p(-(2.0 ** (-5.0 - h))) for h in range(RET_HEADS))

SUBLANES = 8
MXU_N = 256
MATRIX_SLOTS = (1, 4, 6, 7, 8)
STAGE_ROWS = 128
STAGE_SLOTS = 8
PROMPT_TILE = 256
PROMPT_CHUNK = 128
DECODE_PAD = 128
VMEM_LIMIT_BYTES = 63 * 1024 * 1024


def _rope_tables(pos):
    half = RET_HD // 2
    inv = ROPE_BASE ** (-np.arange(half, dtype=np.float64) / half)
    ang = np.asarray(pos, np.float64)[:, None] * inv[None, :]
    cos, sin = np.cos(ang), np.sin(ang)
    cs = np.concatenate([cos, cos], axis=1).astype(np.float32)
    sn = np.concatenate([-sin, sin], axis=1).astype(np.float32)
    return jnp.asarray(cs), jnp.asarray(sn)


def _rmsnorm(x, g):
    ms = jnp.mean(x * x, axis=-1, keepdims=True)
    return x * lax.rsqrt(ms + NORM_EPS) * g


def _silu(x):
    return x * (1.0 / (1.0 + jnp.exp(-x)))


def _rotary(t, cs, sn):
    return t * cs + pltpu.roll(t, RET_HD // 2, axis=1) * sn


def _dot(a, b):
    return jnp.dot(a, b, preferred_element_type=F32)


def _dot_nt(a, b):
    return lax.dot_general(a, b, (((1,), (1,)), ((), ())), preferred_element_type=F32)


def _dot_tn(a, b):
    return lax.dot_general(a, b, (((0,), (0,)), ((), ())), preferred_element_type=F32)


def _groupnorm_gate(o, g, gn_g):
    mu = jnp.mean(o, axis=-1, keepdims=True)
    d = o - mu
    var = jnp.mean(d * d, axis=-1, keepdims=True)
    return _silu(g) * (d * lax.rsqrt(var + GN_EPS) * gn_g)


def _swiglu_piece(h_scr, wgu_ref, act_scr, c0):
    r = _dot(h_scr[...], wgu_ref[:, 2 * c0:2 * c0 + 2 * MXU_N])
    act_scr[:, c0:c0 + MXU_N] = (_silu(r[:, 0:MXU_N]) * r[:, MXU_N:2 * MXU_N]).astype(BF16)


def _load_weights_bf16(pairs, stage, sem, lo=0, hi=None):
    rb = STAGE_ROWS
    slots, width = stage.shape[0] // rb, stage.shape[1]
    assert slots >= 2 and slots <= sem.shape[0]
    chunks = []
    for w_hbm, w_bf, lane in pairs:
        rows, cols = w_hbm.shape
        assert rows % rb == 0 and cols <= width and (lane is None or cols % MXU_N == 0)
        chunks += [(w_hbm, w_bf, r0, cols, lane) for r0 in range(0, rows, rb)]

    def slot_view(n):
        s0 = (n % slots) * rb
        return stage.at[s0:s0 + rb, 0:chunks[n][3]]

    def chunk_copy(n):
        w_hbm, r0 = chunks[n][0], chunks[n][2]
        return pltpu.make_async_copy(w_hbm.at[r0:r0 + rb, :], slot_view(n), sem.at[n % slots])

    if lo == 0:
        for n in range(min(slots - 1, len(chunks))):
            chunk_copy(n).start()
    for n in range(lo, len(chunks) if hi is None else hi):
        _, w_bf, r0, cols, lane = chunks[n]
        chunk_copy(n).wait()
        if n + slots - 1 < len(chunks):
            chunk_copy(n + slots - 1).start()
        if lane is None:
            w_bf[r0:r0 + rb, :] = slot_view(n)[...].astype(BF16)
        else:
            for c0 in range(0, cols, MXU_N):
                d0 = 2 * c0 + lane * MXU_N
                w_bf[r0:r0 + rb, d0:d0 + MXU_N] = slot_view(n)[:, c0:c0 + MXU_N].astype(BF16)


def _layer_kernel(x_ref, cs_ref, sn_ref, xd_ref, csd_ref, snd_ref, cst_ref, sret_ref,
                  g1_ref, win_hbm, convw_ref, gng_ref, wout_hbm, g2_ref, wg_hbm, wu_hbm, wd_hbm, gf_ref,
                  y_ref, cnew_ref, ret_ref, yd_ref, cnewd_ref, retd_ref,
                  win_ref, wout_ref, wgu_ref, wd_ref, stage_scr, load_sem,
                  proj_scr, uext_scr, convy_scr, mix_scr, xkeep_scr, x1_scr, h_scr, h2_scr, act_scr, r_scr,
                  dec_scr, xi_scr, zeta_scr, pad_scr, maskd_scr, xid_scr, zetad_scr,
                  *, n_tiles, tiles_per_seq):
    tile = x_ref.shape[0]
    drows = xd_ref.shape[0]
    nseq = sret_ref.shape[0]
    seqlen = drows // nseq
    rows = tile + drows
    pad = DECODE_PAD
    assert seqlen == SUBLANES, "decode sequences must fill exactly one f32 sublane tile"
    assert nseq == 2 and drows <= pad and tile % 16 == 0 and drows % 16 == 0
    i = pl.program_id(0)
    chunk = PROMPT_CHUNK
    assert tile % chunk == 0
    t = jnp.minimum(i, n_tiles - 1) % tiles_per_seq
    seq_start = t == 0
    seq_end = t == tiles_per_seq - 1
    w = RET_WIDTH
    cw = w
    d_ff = wd_ref.shape[0]

    band_rows = stage_scr.shape[0]
    d_model = wd_ref.shape[1]
    wd_bands = [(r0, min(r0 + band_rows, d_ff)) for r0 in range(0, d_ff, band_rows)]
    assert len(wd_bands) * d_model <= stage_scr.shape[1] and len(wd_bands) <= load_sem.shape[0]

    def wd_band(n):
        r0, r1 = wd_bands[n]
        view = stage_scr.at[0:r1 - r0, n * d_model:(n + 1) * d_model]
        return pltpu.make_async_copy(wd_hbm.at[r0:r1, :], view, load_sem.at[n]), view

    ring = ((win_hbm, win_ref, None), (wout_hbm, wout_ref, None), (wg_hbm, wgu_ref, 0),
            (wu_hbm, wgu_ref, 1))
    n_win = win_hbm.shape[0] // STAGE_ROWS

    def setup_rest():
        _load_weights_bf16(ring, stage_scr, load_sem, lo=n_win)
        for n in range(len(wd_bands)):
            wd_band(n)[0].start()

    def setup():
        _load_weights_bf16(ring, stage_scr, load_sem, hi=n_win)
        ii =lax.broadcasted_iota(jnp.int32, (chunk, chunk), 0)
        jj = lax.broadcasted_iota(jnp.int32, (chunk, chunk), 1)
        causal = ii >= jj
        diff = jnp.where(causal, ii - jj, 0).astype(F32)
        i_f = lax.broadcasted_iota(jnp.int32, (chunk, RET_HD), 0).astype(F32)
        pi = lax.broadcasted_iota(jnp.int32, (pad, pad), 0)
        pj = lax.broadcasted_iota(jnp.int32, (pad, pad), 1)
        keep = (pi >= pj) & ((pi // seqlen) == (pj // seqlen))
        pdiff = jnp.where(keep, pi - pj, 0).astype(F32)
        pos = (lax.broadcasted_iota(jnp.int32, (pad, RET_HD), 0) % seqlen).astype(F32)
        for hh in range(RET_HEADS):
            dec_scr[hh] = jnp.where(causal, jnp.exp(diff * LOG_G[hh]), 0.0)
            xi_scr[hh] = jnp.exp((i_f + 1.0) * LOG_G[hh])
            zeta_scr[hh] = jnp.exp((chunk - 1.0 - i_f) * LOG_G[hh])
            maskd_scr[hh] = jnp.where(keep, jnp.exp(pdiff * LOG_G[hh]), 0.0)
            xid_scr[hh] = jnp.exp((pos + 1.0) * LOG_G[hh])
            zetad_scr[hh] = jnp.exp((seqlen - 1.0 - pos) * LOG_G[hh])
        r_scr[...] = jnp.zeros_like(r_scr)
        uext_scr[0:SUBLANES, :] = jnp.zeros((SUBLANES, uext_scr.shape[1]), F32)
        pad_scr[...] = jnp.zeros_like(pad_scr)

    def ffn_piece(c0):
        return functools.partial(_swiglu_piece, h2_scr, wgu_ref, act_scr, c0)

    def proj_piece(c0):
        def emit():
            proj_scr[:, c0:c0 + cw] = _dot(h_scr[...], win_ref[:, c0:c0 + cw])
        return emit

    def step(mixer, dense, late=()):
        fillers = []

        def fill(n):
            for _ in range(min(n, len(fillers))):
                fillers.pop(0)()

        if dense:
            x1_scr[...] = xkeep_scr[...] + _dot(mix_scr[...], wout_ref[...])
        if mixer:
            x = x_ref[...]
            xd = xd_ref[...]
            h_scr[0:tile, :] = _rmsnorm(x, g1_ref[...]).astype(BF16)
            h_scr[tile:rows, :] = _rmsnorm(xd, g1_ref[...]).astype(BF16)
            proj_scr[:, 0:2 * w] = _dot(h_scr[...], win_ref[:, 0:2 * w])
        if dense:
            h2_scr[...] = _rmsnorm(x1_scr[...], g2_ref[...]).astype(BF16)
            fillers += [ffn_piece(c0) for c0 in range(0, d_ff, MXU_N)]
            fillers += list(late)
        if not mixer:
            fill(len(fillers))
        else:
            proj_scr[:, 2 * w:4 * w] = _dot(h_scr[...], win_ref[:, 2 * w:4 * w])
            fillers[0:0] = [proj_piece(5 * w), proj_piece(6 * w)]

            n_chunks = tile // chunk
            units = [(c, hh) for c in range(n_chunks) for hh in range(RET_HEADS)]
            qbs, vbs, scores, updates = {}, {}, {}, {}
            for c, hh in units:
                r0, c0 = c * chunk, hh * RET_HD
                cs = cs_ref[r0:r0 + chunk, :]
                sn = sn_ref[r0:r0 + chunk, :]
                q = _rotary(proj_scr[r0:r0 + chunk, c0:c0 + RET_HD], cs, sn)
                k = _rotary(proj_scr[r0:r0 + chunk, w + c0:w + c0 + RET_HD], cs, sn) * (RET_HD ** -0.5)
                qb, kb = q.astype(BF16), k.astype(BF16)
                vb = proj_scr[r0:r0 + chunk, 2 * w + c0:2 * w + c0 + RET_HD].astype(BF16)
                kz = (k * zeta_scr[hh]).astype(BF16)
                qbs[c, hh], vbs[c, hh] = qb, vb
                scores[c, hh] = _dot_nt(qb, kb)
                updates[c, hh] = _dot_tn(kz, vb)

            csd = csd_ref[...]
            snd = snd_ref[...]
            rowblk = lax.broadcasted_iota(jnp.int32, (pad, 2 * RET_HD), 0) // seqlen
            colhalf = lax.broadcasted_iota(jnp.int32, (pad, 2 * RET_HD), 1) // RET_HD
            d_scores, d_vb, d_cross = {}, {}, {}
            for hh in range(RET_HEADS):
                c0 = hh * RET_HD
                pad_scr[0, 0:drows, :] = _rotary(proj_scr[tile:rows, c0:c0 + RET_HD], csd, snd)
                pad_scr[1, 0:drows, :] = (_rotary(proj_scr[tile:rows, w + c0:w + c0 + RET_HD], csd, snd)
                                          * (RET_HD ** -0.5))
                pad_scr[2, 0:drows, :] = proj_scr[tile:rows, 2 * w + c0:2 * w + c0 + RET_HD]
                q, k, v = pad_scr[0], pad_scr[1], pad_scr[2]
                qb, kb, vb = q.astype(BF16), k.astype(BF16), v.astype(BF16)
                d_scores[hh] = _dot_nt(qb[0:drows, :], kb)
                d_vb[hh] = vb
                kzt = (k * zetad_scr[hh]).T.astype(BF16)
                ra = sret_ref[0, hh]
                rb = sret_ref[1, hh]
                rcat = jnp.concatenate([ra, rb], axis=1).astype(BF16)
                pr = _dot(qb[0:drows, :], rcat)
                d_cross[hh] = jnp.concatenate([pr[0:seqlen, 0:RET_HD], pr[seqlen:drows, RET_HD:2 * RET_HD]],
                                              axis=0)
                vpair = jnp.where(rowblk == colhalf, jnp.concatenate([v, v], axis=1), 0.0).astype(BF16)
                upd = _dot(kzt, vpair)
                g_chunk = math.exp(seqlen * LOG_G[hh])
                retd_ref[0, hh] = ra * g_chunk + upd[:, 0:RET_HD]
                retd_ref[1, hh] = rb * g_chunk + upd[:, RET_HD:2 * RET_HD]
            fill(2)

            u = proj_scr[0:tile, 5 * w:5 * w + cw] * proj_scr[0:tile, 5 * w + cw:5 * w + 2 * cw]
            uext_scr[SUBLANES:SUBLANES + tile, :] = u
            um1 = uext_scr[SUBLANES - 1:SUBLANES - 1 + tile, :]
            um2 = uext_scr[SUBLANES - 2:SUBLANES - 2 + tile, :]
            taps = [convw_ref[:, j * cw:(j + 1) * cw] for j in range(CONV_K)]
            convy_scr[0:tile, :] = taps[0] * um2 + taps[1] * um1 + taps[2] * u
            tail = uext_scr[tile + SUBLANES - (CONV_K - 1):tile + SUBLANES, :]
            cnew_ref[0] = tail
            uext_scr[SUBLANES - (CONV_K - 1):SUBLANES, :] = jnp.where(seq_end, 0.0, tail)

            ud = proj_scr[tile:rows, 5 * w:5 * w + cw] * proj_scr[tile:rows, 5 * w + cw:5 * w + 2 * cw]
            cst = cst_ref[...]
            older = jnp.broadcast_to(cst[:, 0:1, :], (nseq, seqlen, cw)).reshape(drows, cw)
            newer = jnp.broadcast_to(cst[:, 1:2, :], (nseq, seqlen, cw)).reshape(drows, cw)
            r8 = lax.broadcasted_iota(jnp.int32, (drows, cw), 0) % seqlen
            ud1 = jnp.where(r8 == 0, newer, pltpu.roll(ud, 1, axis=0))
            ud2 = jnp.where(r8 == 0, older, jnp.where(r8 == 1, newer, pltpu.roll(ud, 2, axis=0)))
            convy_scr[tile:rows, :] = taps[0] * ud2 + taps[1] * ud1 + taps[2] * ud
            cnewd_ref[...] = ud.reshape(nseq, seqlen, cw)[:, seqlen - (CONV_K - 1):, :]
            fill(len(fillers) - len(units) + 1)

            outs = {}
            for hh in range(RET_HEADS):
                state = jnp.where(seq_start, 0.0, r_scr[hh])
                for c in range(n_chunks):
                    s = (scores[c, hh] * dec_scr[hh]).astype(BF16)
                    outs[c, hh] = _dot(s, vbs[c, hh]) + _dot(qbs[c, hh], state.astype(BF16)) * xi_scr[hh]
                    state = state * math.exp(chunk * LOG_G[hh]) + updates[c, hh]
                r_scr[hh] = state
                ret_ref[0, hh] = state
            d_outs = {}
            for hh in range(RET_HEADS):
                s = (d_scores[hh] * maskd_scr[hh, 0:drows, :]).astype(BF16)
                d_outs[hh] = _dot(s, d_vb[hh]) + d_cross[hh] * xid_scr[hh, 0:drows, :]
            for c, hh in units:
                fill(1)
                r0, c0 = c * chunk, hh * RET_HD
                g = proj_scr[r0:r0 + chunk, 3 * w + c0:3 * w + c0 + RET_HD]
                mix_scr[r0:r0 + chunk, c0:c0 + RET_HD] = _groupnorm_gate(
                    outs[c, hh], g, gng_ref[:, c0:c0 + RET_HD]).astype(BF16)
            for hh in range(RET_HEADS):
                c0 = hh * RET_HD
                g = proj_scr[tile:rows, 3 * w + c0:3 * w + c0 + RET_HD]
                mix_scr[tile:rows, c0:c0 + RET_HD] = _groupnorm_gate(
                    d_outs[hh], g, gng_ref[:, c0:c0 + RET_HD]).astype(BF16)
            fill(len(fillers))

        if dense:
            down = _dot(act_scr[...], wd_ref[...])
        if mixer:
            proj_piece(4 * w)()
        if dense:
            y = _rmsnorm(x1_scr[...] + down, gf_ref[...])
            y_ref[...] = y[0:tile, :]
            yd_ref[...] = y[tile:rows, :]
        if mixer:
            mix_scr[:, w:w + cw] = (proj_scr[:, 4 * w:4 * w + cw] * convy_scr[...]).astype(BF16)
            xkeep_scr[0:tile, :] = x
            xkeep_scr[tile:rows, :] = xd

    assert n_tiles >= 2
    @pl.when(i == 0)
    def _():
        setup()

    @pl.when(i < 1)
    def _():
        step(mixer=True, dense=False)

    @pl.when(i <= 0)
    def _():
        setup_rest()

    @pl.when(i == 1)
    def _():
        def cast_band(n):
            def emit():
                r0, r1 = wd_bands[n]
                copy, view = wd_band(n)
                copy.wait()
                wd_ref[r0:r1, :] = view[...].astype(BF16)
            return emit
        step(mixer=True, dense=True, late=[cast_band(n) for n in range(len(wd_bands))])

    @pl.when((i > 1) & (i < n_tiles))
    def _():
        step(mixer=True, dense=True)

    @pl.when(i == n_tiles)
    def _():
        step(mixer=False, dense=True)


def _resident(shape):
    nd = len(shape)
    return pl.BlockSpec(shape, lambda *_: (0,) * nd, pipeline_mode=pl.Buffered(1))


def _layer_call(x, xd, state_conv, state_ret, ws):
    batch, seq, d = x.shape
    nseq_all, seqlen, _ = xd.shape
    tile = PROMPT_TILE
    tiles_per_seq = seq // tile
    n_tiles = batch * tiles_per_seq
    assert nseq_all % n_tiles == 0, "every prompt tile carries the same number of decode sequences"
    nseq = nseq_all // n_tiles
    drows = nseq * seqlen
    rows = tile + drows
    cs, sn = _rope_tables(np.arange(seq))
    csd, snd = _rope_tables(np.tile(PAST_LEN + np.arange(seqlen), nseq))
    in_cols = ws[1].shape[1]
    cw = ws[2].shape[-1] // CONV_K
    mixer_tile = lambda i: jnp.minimum(i, n_tiles - 1)
    dense_tile = lambda i: jnp.maximum(i - 1, 0)
    w_in, w_out, w_gate, w_up, w_down = [ws[m] for m in MATRIX_SLOTS]
    assert w_gate.shape == w_up.shape and w_gate.shape[1] == w_down.shape[0]
    d_ff = w_down.shape[0]
    mats = [w_in.shape, w_out.shape, (d, 2 * d_ff), w_down.shape]
    weight_specs = [pl.BlockSpec(memory_space=pl.ANY) if m in MATRIX_SLOTS else _resident(wt.shape)
                    for m, wt in enumerate(ws)]
    state_block = (nseq, RET_HEADS, RET_HD, RET_HD)
    outs = pl.pallas_call(
        functools.partial(_layer_kernel, n_tiles=n_tiles, tiles_per_seq=tiles_per_seq),
        grid=(n_tiles + 1,),
        in_specs=[pl.BlockSpec((tile, d), lambda i: (mixer_tile(i), 0)),
                  pl.BlockSpec((tile, RET_HD), lambda i: (mixer_tile(i) % tiles_per_seq, 0)),
                  pl.BlockSpec((tile, RET_HD), lambda i: (mixer_tile(i) % tiles_per_seq, 0)),
                  pl.BlockSpec((drows, d), lambda i: (mixer_tile(i), 0)),
                  _resident((drows, RET_HD)), _resident((drows, RET_HD)),
                  pl.BlockSpec((nseq, CONV_K - 1, cw), lambda i: (mixer_tile(i), 0, 0)),
                  pl.BlockSpec(state_block, lambda i: (mixer_tile(i), 0, 0, 0))] + weight_specs,
        out_specs=[pl.BlockSpec((tile, d), lambda i: (dense_tile(i), 0)),
                   pl.BlockSpec((1, CONV_K - 1, cw), lambda i: (mixer_tile(i) // tiles_per_seq, 0, 0)),
                   pl.BlockSpec((1, RET_HEADS, RET_HD, RET_HD),
                                lambda i: (mixer_tile(i) // tiles_per_seq, 0, 0, 0)),
                   pl.BlockSpec((drows, d), lambda i: (dense_tile(i), 0)),
                   pl.BlockSpec((nseq, CONV_K - 1, cw), lambda i: (mixer_tile(i), 0, 0)),
                   pl.BlockSpec(state_block, lambda i: (mixer_tile(i), 0, 0, 0))],
        out_shape=[jax.ShapeDtypeStruct((batch * seq, d), F32),
                   jax.ShapeDtypeStruct((batch, CONV_K - 1, cw), F32),
                   jax.ShapeDtypeStruct((batch, RET_HEADS, RET_HD, RET_HD), F32),
                   jax.ShapeDtypeStruct((nseq_all * seqlen, d), F32),
                   jax.ShapeDtypeStruct((nseq_all, CONV_K - 1, cw), F32),
                   jax.ShapeDtypeStruct((nseq_all, RET_HEADS, RET_HD, RET_HD), F32)],
        scratch_shapes=[pltpu.VMEM(s, BF16) for s in mats] + [
                        pltpu.VMEM((STAGE_SLOTS * STAGE_ROWS, max(in_cols, d_ff, d)), F32),
                        pltpu.SemaphoreType.DMA((STAGE_SLOTS,)),
                        pltpu.VMEM((rows, in_cols), F32),
                        pltpu.VMEM((tile + SUBLANES, cw), F32),
                        pltpu.VMEM((rows, cw), F32),
                        pltpu.VMEM((rows, d), BF16),
                        pltpu.VMEM((rows, d), F32),
                        pltpu.VMEM((rows, d), F32),
                        pltpu.VMEM((rows, d), BF16),
                        pltpu.VMEM((rows, d), BF16),
                        pltpu.VMEM((rows, d_ff), BF16),
                        pltpu.VMEM((RET_HEADS, RET_HD, RET_HD), F32),
                        pltpu.VMEM((RET_HEADS, PROMPT_CHUNK, PROMPT_CHUNK), F32),
                        pltpu.VMEM((RET_HEADS, PROMPT_CHUNK, RET_HD), F32),
                        pltpu.VMEM((RET_HEADS, PROMPT_CHUNK, RET_HD), F32),
                        pltpu.VMEM((3, DECODE_PAD, RET_HD), F32),
                        pltpu.VMEM((RET_HEADS, DECODE_PAD, DECODE_PAD), F32),
                        pltpu.VMEM((RET_HEADS, DECODE_PAD, RET_HD), F32),
                        pltpu.VMEM((RET_HEADS, DECODE_PAD, RET_HD), F32)],
        compiler_params=pltpu.CompilerParams(dimension_semantics=("arbitrary",),
                                             vmem_limit_bytes=VMEM_LIMIT_BYTES),
        name="layer",
    )(x.reshape(batch * seq, d), cs, sn, xd.reshape(nseq_all * seqlen, d), csd, snd, state_conv, state_ret,
      *ws)
    y, cnew, ret, yd, cnewd, retd = outs
    return y.reshape(batch, seq, d), yd.reshape(nseq_all, seqlen, d), cnew, ret, cnewd, retd


def kernel(x_prompt, x_sample, state_conv, state_ret, norm1_g, w_in, conv_w, ret_gn_g, w_out, norm2_g,
           w_gate, w_up, w_down, norm_f_g):
    depth = w_in.shape[0]
    assert depth == 1, "the fused layer kernel takes a single layer"
    row = lambda g: g.reshape(1, -1)
    ws = (row(norm1_g[0]), w_in[0], row(conv_w[0]), row(ret_gn_g[0]), w_out[0],
          row(norm2_g[0]), w_gate[0], w_up[0], w_down[0], row(norm_f_g))
    y_p, y_s, cnew_p, ret_p, cnew_s, ret_s = _layer_call(x_prompt, x_sample, state_conv[0], state_ret[0], ws)
    return (y_p, y_s, cnew_p[None], ret_p[None], cnew_s[None], ret_s[None])
```

```python
import functools
import math

import numpy as np
import jax
import jax.numpy as jnp
from jax import lax
from jax.experimental import pallas as pl
from jax.experimental.pallas import tpu as pltpu

F32 = jnp.float32
BF16 = jnp.bfloat16

RET_HEADS = 4
RET_HD = 128
RET_WIDTH = RET_HEADS * RET_HD
CONV_K = 3
ROPE_BASE = 10000.0
NORM_EPS = 1e-6
GN_EPS = 1e-5
PAST_LEN = 16384

LOG_G = tuple(math.log1p(-(2.0 ** (-5.0 - h))) for h in range(RET_HEADS))

SUBLANES = 8
MXU_N = 256
MATRIX_SLOTS = (1, 4, 6, 7, 8)
STAGE_ROWS = 128
STAGE_SLOTS = 8
PROMPT_TILE = 256
PROMPT_CHUNK = 128
DECODE_PAD = 128
VMEM_LIMIT_BYTES = 62 * 1024 * 1024


def _rope_tables(pos):
    half = RET_HD // 2
    inv = ROPE_BASE ** (-np.arange(half, dtype=np.float64) / half)
    ang = np.asarray(pos, np.float64)[:, None] * inv[None, :]
    cos, sin = np.cos(ang), np.sin(ang)
    cs = np.concatenate([cos, cos], axis=1).astype(np.float32)
    sn = np.concatenate([-sin, sin], axis=1).astype(np.float32)
    return jnp.asarray(cs), jnp.asarray(sn)


def _rmsnorm(x, g):
    ms = jnp.mean(x * x, axis=-1, keepdims=True)
    return x * lax.rsqrt(ms + NORM_EPS) * g


def _silu(x):
    return x * (1.0 / (1.0 + jnp.exp(-x)))


def _rotary(t, cs, sn):
    return t * cs + pltpu.roll(t, RET_HD // 2, axis=1) * sn


def _dot(a, b):
    return jnp.dot(a, b, preferred_element_type=F32)


def _dot_nt(a, b):
    return lax.dot_general(a, b, (((1,), (1,)), ((), ())), preferred_element_type=F32)


def _dot_tn(a, b):
    return lax.dot_general(a, b, (((0,), (0,)), ((), ())), preferred_element_type=F32)


def _groupnorm_gate(o, g, gn_g):
    mu = jnp.mean(o, axis=-1, keepdims=True)
    d = o - mu
    var = jnp.mean(d * d, axis=-1, keepdims=True)
    return _silu(g) * (d * lax.rsqrt(var + GN_EPS) * gn_g)


def _swiglu_piece(h_scr, wgu_ref, act_scr, c0):
    r = _dot(h_scr[...], wgu_ref[:, 2 * c0:2 * c0 + 2 * MXU_N])
    act_scr[:, c0:c0 + MXU_N] = (_silu(r[:, 0:MXU_N]) * r[:, MXU_N:2 * MXU_N]).astype(BF16)


def _load_weights_bf16(pairs, stage, sem):
    rb = STAGE_ROWS
    slots, width = stage.shape[0] // rb, stage.shape[1]
    assert slots >= 2 and slots <= sem.shape[0]
    chunks = []
    for w_hbm, w_bf, lane in pairs:
        rows, cols = w_hbm.shape
        assert rows % rb == 0 and cols <= width and (lane is None or cols % MXU_N == 0)
        chunks += [(w_hbm, w_bf, r0, cols, lane) for r0 in range(0, rows, rb)]

    def slot_view(n):
        s0 = (n % slots) * rb
        return stage.at[s0:s0 + rb, 0:chunks[n][3]]

    def chunk_copy(n):
        w_hbm, r0 = chunks[n][0], chunks[n][2]
        return pltpu.make_async_copy(w_hbm.at[r0:r0 + rb, :], slot_view(n), sem.at[n % slots])

    for n in range(min(slots - 1, len(chunks))):
        chunk_copy(n).start()
    for n, (_, w_bf, r0, cols, lane) in enumerate(chunks):
        chunk_copy(n).wait()
        if n + slots - 1 < len(chunks):
            chunk_copy(n + slots - 1).start()
        if lane is None:
            w_bf[r0:r0 + rb, :] = slot_view(n)[...].astype(BF16)
        else:
            for c0 in range(0, cols, MXU_N):
                d0 = 2 * c0 + lane * MXU_N
                w_bf[r0:r0 + rb, d0:d0 + MXU_N] = slot_view(n)[:, c0:c0 + MXU_N].astype(BF16)


def _layer_kernel(x_ref, cs_ref, sn_ref, xd_ref, csd_ref, snd_ref, cst_ref, sret_ref,
                  g1_ref, win_hbm, convw_ref, gng_ref, wout_hbm, g2_ref, wg_hbm, wu_hbm, wd_hbm, gf_ref,
                  y_ref, cnew_ref, ret_ref, yd_ref, cnewd_ref, retd_ref,
                  win_ref, wout_ref, wgu_ref, wd_ref, stage_scr, load_sem,
                  proj_scr, uext_scr, convy_scr, mix_scr, xkeep_scr, x1_scr, h_scr, h2_scr, act_scr, r_scr,
                  dec_scr, xi_scr, zeta_scr, pad_scr, maskd_scr, xid_scr, zetad_scr,
                  *, n_tiles, tiles_per_seq):
    tile = x_ref.shape[0]
    drows = xd_ref.shape[0]
    nseq = sret_ref.shape[0]
    seqlen = drows // nseq
    rows = tile + drows
    pad = DECODE_PAD
    assert seqlen == SUBLANES, "decode sequences must fill exactly one f32 sublane tile"
    assert nseq == 2 and drows <= pad and tile % 16 == 0 and drows % 16 == 0
    i = pl.program_id(0)
    chunk = PROMPT_CHUNK
    assert tile % chunk == 0
    t = jnp.minimum(i, n_tiles - 1) % tiles_per_seq
    seq_start = t == 0
    seq_end = t == tiles_per_seq - 1
    w = RET_WIDTH
    cw = w
    d_ff = wd_ref.shape[0]

    band_rows = stage_scr.shape[0]
    d_model = wd_ref.shape[1]
    wd_bands = [(r0, min(r0 + band_rows, d_ff)) for r0 in range(0, d_ff, band_rows)]
    assert len(wd_bands) * d_model <= stage_scr.shape[1] and len(wd_bands) <= load_sem.shape[0]

    def wd_band(n):
        r0, r1 = wd_bands[n]
        view = stage_scr.at[0:r1 - r0, n * d_model:(n + 1) * d_model]
        return pltpu.make_async_copy(wd_hbm.at[r0:r1, :], view, load_sem.at[n]), view

    def setup():
        _load_weights_bf16(((win_hbm, win_ref, None), (wout_hbm, wout_ref, None), (wg_hbm, wgu_ref, 0),
                            (wu_hbm, wgu_ref, 1)), stage_scr, load_sem)
        for n in range(len(wd_bands)):
            wd_band(n)[0].start()

    def tables():
        ii = lax.broadcasted_iota(jnp.int32, (chunk, chunk), 0)
        jj = lax.broadcasted_iota(jnp.int32, (chunk, chunk), 1)
        causal = ii >= jj
        diff = jnp.where(causal, ii - jj, 0).astype(F32)
        i_f = lax.broadcasted_iota(jnp.int32, (chunk, RET_HD), 0).astype(F32)
        pi = lax.broadcasted_iota(jnp.int32, (pad, pad), 0)
        pj = lax.broadcasted_iota(jnp.int32, (pad, pad), 1)
        keep = (pi >= pj) & ((pi // seqlen) == (pj // seqlen))
        pdiff = jnp.where(keep, pi - pj, 0).astype(F32)
        pos = (lax.broadcasted_iota(jnp.int32, (pad, RET_HD), 0) % seqlen).astype(F32)
        for hh in range(RET_HEADS):
            dec_scr[hh] = jnp.where(causal, jnp.exp(diff * LOG_G[hh]), 0.0)
            xi_scr[hh] = jnp.exp((i_f + 1.0) * LOG_G[hh])
            zeta_scr[hh] = jnp.exp((chunk - 1.0 - i_f) * LOG_G[hh])
            maskd_scr[hh] = jnp.where(keep, jnp.exp(pdiff * LOG_G[hh]), 0.0)
            xid_scr[hh] = jnp.exp((pos + 1.0) * LOG_G[hh])
            zetad_scr[hh] = jnp.exp((seqlen - 1.0 - pos) * LOG_G[hh])
        r_scr[...] = jnp.zeros_like(r_scr)
        uext_scr[0:SUBLANES, :] = jnp.zeros((SUBLANES, uext_scr.shape[1]), F32)
        pad_scr[...] = jnp.zeros_like(pad_scr)

    def ffn_piece(c0):
        return functools.partial(_swiglu_piece, h2_scr, wgu_ref, act_scr, c0)

    def proj_piece(c0):
        def emit():
            proj_scr[:, c0:c0 + cw] = _dot(h_scr[...], win_ref[:, c0:c0 + cw])
        return emit

    def step(mixer, dense):
        fillers = []

        def fill(n):
            for _ in range(min(n, len(fillers))):
                fillers.pop(0)()

        if dense:
            x1_scr[...] = xkeep_scr[...] + _dot(mix_scr[...], wout_ref[...])
        if mixer:
            x = x_ref[...]
            xd = xd_ref[...]
            h_scr[0:tile, :] = _rmsnorm(x, g1_ref[...]).astype(BF16)
            h_scr[tile:rows, :] = _rmsnorm(xd, g1_ref[...]).astype(BF16)
            proj_scr[:, 0:2 * w] = _dot(h_scr[...], win_ref[:, 0:2 * w])
        if dense:
            h2_scr[...] = _rmsnorm(x1_scr[...], g2_ref[...]).astype(BF16)
            fillers += [ffn_piece(c0) for c0 in range(0, d_ff, MXU_N)]
        if not mixer:
            fill(len(fillers))
        else:
            proj_scr[:, 2 * w:4 * w] = _dot(h_scr[...], win_ref[:, 2 * w:4 * w])
            fillers[0:0] = [proj_piece(5 * w), proj_piece(6 * w)]

            n_chunks = tile // chunk
            units = [(c, hh) for c in range(n_chunks) for hh in range(RET_HEADS)]
            qbs, vbs, scores, updates = {}, {}, {}, {}
            for c, hh in units:
                r0, c0 = c * chunk, hh * RET_HD
                cs = cs_ref[r0:r0 + chunk, :]
                sn = sn_ref[r0:r0 + chunk, :]
                q = _rotary(proj_scr[r0:r0 + chunk, c0:c0 + RET_HD], cs, sn)
                k = _rotary(proj_scr[r0:r0 + chunk, w + c0:w + c0 + RET_HD], cs, sn) * (RET_HD ** -0.5)
                qb, kb = q.astype(BF16), k.astype(BF16)
                vb = proj_scr[r0:r0 + chunk, 2 * w + c0:2 * w + c0 + RET_HD].astype(BF16)
                kz = (k * zeta_scr[hh]).astype(BF16)
                qbs[c, hh], vbs[c, hh] = qb, vb
                scores[c, hh] = _dot_nt(qb, kb)
                updates[c, hh] = _dot_tn(kz, vb)

            csd = csd_ref[...]
            snd = snd_ref[...]
            rowblk = lax.broadcasted_iota(jnp.int32, (pad, 2 * RET_HD), 0) // seqlen
            colhalf = lax.broadcasted_iota(jnp.int32, (pad, 2 * RET_HD), 1) // RET_HD
            d_scores, d_vb, d_cross = {}, {}, {}
            for hh in range(RET_HEADS):
                c0 = hh * RET_HD
                pad_scr[0, 0:drows, :] = _rotary(proj_scr[tile:rows, c0:c0 + RET_HD], csd, snd)
                pad_scr[1, 0:drows, :] = (_rotary(proj_scr[tile:rows, w + c0:w + c0 + RET_HD], csd, snd)
                                          * (RET_HD ** -0.5))
                pad_scr[2, 0:drows, :] = proj_scr[tile:rows, 2 * w + c0:2 * w + c0 + RET_HD]
                q, k, v = pad_scr[0], pad_scr[1], pad_scr[2]
                qb, kb, vb = q.astype(BF16), k.astype(BF16), v.astype(BF16)
                d_scores[hh] = _dot_nt(qb[0:drows, :], kb)
                d_vb[hh] = vb
                kzt = (k * zetad_scr[hh]).T.astype(BF16)
                ra = sret_ref[0, hh]
                rb = sret_ref[1, hh]
                rcat = jnp.concatenate([ra, rb], axis=1).astype(BF16)
                pr = _dot(qb[0:drows, :], rcat)
                d_cross[hh] = jnp.concatenate([pr[0:seqlen, 0:RET_HD], pr[seqlen:drows, RET_HD:2 * RET_HD]],
                                              axis=0)
                vpair = jnp.where(rowblk == colhalf, jnp.concatenate([v, v], axis=1), 0.0).astype(BF16)
                upd = _dot(kzt, vpair)
                g_chunk = math.exp(seqlen * LOG_G[hh])
                retd_ref[0, hh] = ra * g_chunk + upd[:, 0:RET_HD]
                retd_ref[1, hh] = rb * g_chunk + upd[:, RET_HD:2 * RET_HD]
            fill(2)

            u = proj_scr[0:tile, 5 * w:5 * w + cw] * proj_scr[0:tile, 5 * w + cw:5 * w + 2 * cw]
            uext_scr[SUBLANES:SUBLANES + tile, :] = u
            um1 = uext_scr[SUBLANES - 1:SUBLANES - 1 + tile, :]
            um2 = uext_scr[SUBLANES - 2:SUBLANES - 2 + tile, :]
            taps = [convw_ref[:, j * cw:(j + 1) * cw] for j in range(CONV_K)]
            convy_scr[0:tile, :] = taps[0] * um2 + taps[1] * um1 + taps[2] * u
            tail = uext_scr[tile + SUBLANES - (CONV_K - 1):tile + SUBLANES, :]
            cnew_ref[0] = tail
            uext_scr[SUBLANES - (CONV_K - 1):SUBLANES, :] = jnp.where(seq_end, 0.0, tail)

            ud = proj_scr[tile:rows, 5 * w:5 * w + cw] * proj_scr[tile:rows, 5 * w + cw:5 * w + 2 * cw]
            cst = cst_ref[...]
            older = jnp.broadcast_to(cst[:, 0:1, :], (nseq, seqlen, cw)).reshape(drows, cw)
            newer = jnp.broadcast_to(cst[:, 1:2, :], (nseq, seqlen, cw)).reshape(drows, cw)
            r8 = lax.broadcasted_iota(jnp.int32, (drows, cw), 0) % seqlen
            ud1 = jnp.where(r8 == 0, newer, pltpu.roll(ud, 1, axis=0))
            ud2 = jnp.where(r8 == 0, older, jnp.where(r8 == 1, newer, pltpu.roll(ud, 2, axis=0)))
            convy_scr[tile:rows, :] = taps[0] * ud2 + taps[1] * ud1 + taps[2] * ud
            cnewd_ref[...] = ud.reshape(nseq, seqlen, cw)[:, seqlen - (CONV_K - 1):, :]
            fill(len(fillers) - len(units) + 1)

            outs = {}
            for hh in range(RET_HEADS):
                state = jnp.where(seq_start, 0.0, r_scr[hh])
                for c in range(n_chunks):
                    s = (scores[c, hh] * dec_scr[hh]).astype(BF16)
                    outs[c, hh] = _dot(s, vbs[c, hh]) + _dot(qbs[c, hh], state.astype(BF16)) * xi_scr[hh]
                    state = state * math.exp(chunk * LOG_G[hh]) + updates[c, hh]
                r_scr[hh] = state
                ret_ref[0, hh] = state
            d_outs = {}
            for hh in range(RET_HEADS):
                s = (d_scores[hh] * maskd_scr[hh, 0:drows, :]).astype(BF16)
                d_outs[hh] = _dot(s, d_vb[hh]) + d_cross[hh] * xid_scr[hh, 0:drows, :]
            for c, hh in units:
                fill(1)
                r0, c0 = c * chunk, hh * RET_HD
                g = proj_scr[r0:r0 + chunk, 3 * w + c0:3 * w + c0 + RET_HD]
                mix_scr[r0:r0 + chunk, c0:c0 + RET_HD] = _groupnorm_gate(
                    outs[c, hh], g, gng_ref[:, c0:c0 + RET_HD]).astype(BF16)
            for hh in range(RET_HEADS):
                c0 = hh * RET_HD
                g = proj_scr[tile:rows, 3 * w + c0:3 * w + c0 + RET_HD]
                mix_scr[tile:rows, c0:c0 + RET_HD] = _groupnorm_gate(
                    d_outs[hh], g, gng_ref[:, c0:c0 + RET_HD]).astype(BF16)
            fill(len(fillers))

        if dense:
            down = _dot(act_scr[...], wd_ref[...])
        if mixer:
            proj_piece(4 * w)()
        if dense:
            y = _rmsnorm(x1_scr[...] + down, gf_ref[...])
            y_ref[...] = y[0:tile, :]
            yd_ref[...] = y[tile:rows, :]
        if mixer:
            mix_scr[:, w:w + cw] = (proj_scr[:, 4 * w:4 * w + cw] * convy_scr[...]).astype(BF16)
            xkeep_scr[0:tile, :] = x
            xkeep_scr[tile:rows, :] = xd

    assert n_tiles >= 2
    @pl.when(i == 0)
    def _():
        setup()

    @pl.when(i < 1)
    def _():
        tables()
        step(mixer=True, dense=False)

    @pl.when(i == 1)
    def _():
        for n, (r0, r1) in enumerate(wd_bands):
            copy, view = wd_band(n)
            copy.wait()
            wd_ref[r0:r1, :] = view[...].astype(BF16)
        step(mixer=True, dense=True)

    @pl.when((i > 1) & (i < n_tiles))
    def _():
        step(mixer=True, dense=True)

    @pl.when(i == n_tiles)
    def _():
        step(mixer=False, dense=True)


def _resident(shape):
    nd = len(shape)
    return pl.BlockSpec(shape, lambda *_: (0,) * nd, pipeline_mode=pl.Buffered(1))


def _layer_call(x, xd, state_conv, state_ret, ws):
    batch, seq, d = x.shape
    nseq_all, seqlen, _ = xd.shape
    tile = PROMPT_TILE
    tiles_per_seq = seq // tile
    n_tiles = batch * tiles_per_seq
    assert nseq_all % n_tiles == 0, "every prompt tile carries the same number of decode sequences"
    nseq = nseq_all // n_tiles
    drows = nseq * seqlen
    rows = tile + drows
    cs, sn = _rope_tables(np.arange(seq))
    csd, snd = _rope_tables(np.tile(PAST_LEN + np.arange(seqlen), nseq))
    in_cols = ws[1].shape[1]
    cw = ws[2].shape[-1] // CONV_K
    mixer_tile = lambda i: jnp.minimum(i, n_tiles - 1)
    dense_tile = lambda i: jnp.maximum(i - 1, 0)
    w_in, w_out, w_gate, w_up, w_down = [ws[m] for m in MATRIX_SLOTS]
    assert w_gate.shape == w_up.shape and w_gate.shape[1] == w_down.shape[0]
    d_ff = w_down.shape[0]
    mats = [w_in.shape, w_out.shape, (d, 2 * d_ff), w_down.shape]
    weight_specs = [pl.BlockSpec(memory_space=pl.ANY) if m in MATRIX_SLOTS else _resident(wt.shape)
                    for m, wt in enumerate(ws)]
    state_block = (nseq, RET_HEADS, RET_HD, RET_HD)
    outs = pl.pallas_call(
        functools.partial(_layer_kernel, n_tiles=n_tiles, tiles_per_seq=tiles_per_seq),
        grid=(n_tiles + 1,),
        in_specs=[pl.BlockSpec((tile, d), lambda i: (mixer_tile(i), 0)),
                  pl.BlockSpec((tile, RET_HD), lambda i: (mixer_tile(i) % tiles_per_seq, 0)),
                  pl.BlockSpec((tile, RET_HD), lambda i: (mixer_tile(i) % tiles_per_seq, 0)),
                  pl.BlockSpec((drows, d), lambda i: (mixer_tile(i), 0)),
                  _resident((drows, RET_HD)), _resident((drows, RET_HD)),
                  pl.BlockSpec((nseq, CONV_K - 1, cw), lambda i: (mixer_tile(i), 0, 0)),
                  pl.BlockSpec(state_block, lambda i: (mixer_tile(i), 0, 0, 0))] + weight_specs,
        out_specs=[pl.BlockSpec((tile, d), lambda i: (dense_tile(i), 0)),
                   pl.BlockSpec((1, CONV_K - 1, cw), lambda i: (mixer_tile(i) // tiles_per_seq, 0, 0)),
                   pl.BlockSpec((1, RET_HEADS, RET_HD, RET_HD),
                                lambda i: (mixer_tile(i) // tiles_per_seq, 0, 0, 0)),
                   pl.BlockSpec((drows, d), lambda i: (dense_tile(i), 0)),
                   pl.BlockSpec((nseq, CONV_K - 1, cw), lambda i: (mixer_tile(i), 0, 0)),
                   pl.BlockSpec(state_block, lambda i: (mixer_tile(i), 0, 0, 0))],
        out_shape=[jax.ShapeDtypeStruct((batch * seq, d), F32),
                   jax.ShapeDtypeStruct((batch, CONV_K - 1, cw), F32),
                   jax.ShapeDtypeStruct((batch, RET_HEADS, RET_HD, RET_HD), F32),
                   jax.ShapeDtypeStruct((nseq_all * seqlen, d), F32),
                   jax.ShapeDtypeStruct((nseq_all, CONV_K - 1, cw), F32),
                   jax.ShapeDtypeStruct((nseq_all, RET_HEADS, RET_HD, RET_HD), F32)],
        scratch_shapes=[pltpu.VMEM(s, BF16) for s in mats] + [
                        pltpu.VMEM((STAGE_SLOTS * STAGE_ROWS, max(in_cols, d_ff, d)), F32),
                        pltpu.SemaphoreType.DMA((STAGE_SLOTS,)),
                        pltpu.VMEM((rows, in_cols), F32),
                        pltpu.VMEM((tile + SUBLANES, cw), F32),
                        pltpu.VMEM((rows, cw), F32),
                        pltpu.VMEM((rows, d), BF16),
                        pltpu.VMEM((rows, d), F32),
                        pltpu.VMEM((rows, d), F32),
                        pltpu.VMEM((rows, d), BF16),
                        pltpu.VMEM((rows, d), BF16),
                        pltpu.VMEM((rows, d_ff), BF16),
                        pltpu.VMEM((RET_HEADS, RET_HD, RET_HD), F32),
                        pltpu.VMEM((RET_HEADS, PROMPT_CHUNK, PROMPT_CHUNK), F32),
                        pltpu.VMEM((RET_HEADS, PROMPT_CHUNK, RET_HD), F32),
                        pltpu.VMEM((RET_HEADS, PROMPT_CHUNK, RET_HD), F32),
                        pltpu.VMEM((3, DECODE_PAD, RET_HD), F32),
                        pltpu.VMEM((RET_HEADS, DECODE_PAD, DECODE_PAD), F32),
                        pltpu.VMEM((RET_HEADS, DECODE_PAD, RET_HD), F32),
                        pltpu.VMEM((RET_HEADS, DECODE_PAD, RET_HD), F32)],
        compiler_params=pltpu.CompilerParams(dimension_semantics=("arbitrary",),
                                             vmem_limit_bytes=VMEM_LIMIT_BYTES),
        name="layer",
    )(x.reshape(batch * seq, d), cs, sn, xd.reshape(nseq_all * seqlen, d), csd, snd, state_conv, state_ret,
      *ws)
    y, cnew, ret, yd, cnewd, retd = outs
    return y.reshape(batch, seq, d), yd.reshape(nseq_all, seqlen, d), cnew, ret, cnewd, retd


def kernel(x_prompt, x_sample, state_conv, state_ret, norm1_g, w_in, conv_w, ret_gn_g, w_out, norm2_g,
           w_gate, w_up, w_down, norm_f_g):
    depth = w_in.shape[0]
    assert depth == 1, "the fused layer kernel takes a single layer"
    row = lambda g: g.reshape(1, -1)
    ws = (row(norm1_g[0]), w_in[0], row(conv_w[0]), row(ret_gn_g[0]), w_out[0],
          row(norm2_g[0]), w_gate[0], w_up[0], w_down[0], row(norm_f_g))
    y_p, y_s, cnew_p, ret_p, cnew_s, ret_s = _layer_call(x_prompt, x_sample, state_conv[0], state_ret[0], ws)
    return (y_p, y_s, cnew_p[None], ret_p[None], cnew_s[None], ret_s[None])
```

```python
import functools
import math

import numpy as np
import jax
import jax.numpy as jnp
from jax import lax
from jax.experimental import pallas as pl
from jax.experimental.pallas import tpu as pltpu

F32 = jnp.float32
BF16 = jnp.bfloat16

RET_HEADS = 4
RET_HD = 128
RET_WIDTH = RET_HEADS * RET_HD
CONV_K = 3
ROPE_BASE = 10000.0
NORM_EPS = 1e-6
GN_EPS = 1e-5
PAST_LEN = 16384

LOG_G = tuple(math.log1p(-(2.0 ** (-5.0 - h))) for h in range(RET_HEADS))

SUBLANES = 8
MXU_N = 256
MATRIX_SLOTS = (1, 4, 6, 7, 8)
STAGE_ROWS = 128
STAGE_SLOTS = 8
PROMPT_TILE = 256
PROMPT_CHUNK = 128
DECODE_PAD = 128
VMEM_LIMIT_BYTES = 62 * 1024 * 1024


def _rope_tables(pos):
    half = RET_HD // 2
    inv = ROPE_BASE ** (-np.arange(half, dtype=np.float64) / half)
    ang = np.asarray(pos, np.float64)[:, None] * inv[None, :]
    cos, sin = np.cos(ang), np.sin(ang)
    cs = np.concatenate([cos, cos], axis=1).astype(np.float32)
    sn = np.concatenate([-sin, sin], axis=1).astype(np.float32)
    return jnp.asarray(cs), jnp.asarray(sn)


def _rmsnorm(x, g):
    ms = jnp.mean(x * x, axis=-1, keepdims=True)
    return x * lax.rsqrt(ms + NORM_EPS) * g


def _silu(x):
    return x * (1.0 / (1.0 + jnp.exp(-x)))


def _rotary(t, cs, sn):
    return t * cs + pltpu.roll(t, RET_HD // 2, axis=1) * sn


def _dot(a, b):
    return jnp.dot(a, b, preferred_element_type=F32)


def _dot_nt(a, b):
    return lax.dot_general(a, b, (((1,), (1,)), ((), ())), preferred_element_type=F32)


def _dot_tn(a, b):
    return lax.dot_general(a, b, (((0,), (0,)), ((), ())), preferred_element_type=F32)


def _groupnorm_gate(o, g, gn_g):
    mu = jnp.mean(o, axis=-1, keepdims=True)
    d = o - mu
    var = jnp.mean(d * d, axis=-1, keepdims=True)
    return _silu(g) * (d * lax.rsqrt(var + GN_EPS) * gn_g)


def _swiglu_piece(h_scr, wgu_ref, act_scr, c0):
    r = _dot(h_scr[...], wgu_ref[:, 2 * c0:2 * c0 + 2 * MXU_N])
    act_scr[:, c0:c0 + MXU_N] = (_silu(r[:, 0:MXU_N]) * r[:, MXU_N:2 * MXU_N]).astype(BF16)


def _load_weights_bf16(pairs, stage, sem):
    rb = STAGE_ROWS
    slots, width = stage.shape[0] // rb, stage.shape[1]
    assert slots >= 2 and slots <= sem.shape[0]
    chunks = []
    for w_hbm, w_bf, lane in pairs:
        rows, cols = w_hbm.shape
        assert rows % rb == 0 and cols <= width and (lane is None or cols % MXU_N == 0)
        chunks += [(w_hbm, w_bf, r0, cols, lane) for r0 in range(0, rows, rb)]

    def slot_view(n):
        s0 = (n % slots) * rb
        return stage.at[s0:s0 + rb, 0:chunks[n][3]]

    def chunk_copy(n):
        w_hbm, r0 = chunks[n][0], chunks[n][2]
        return pltpu.make_async_copy(w_hbm.at[r0:r0 + rb, :], slot_view(n), sem.at[n % slots])

    for n in range(min(slots - 1, len(chunks))):
        chunk_copy(n).start()
    for n, (_, w_bf, r0, cols, lane) in enumerate(chunks):
        chunk_copy(n).wait()
        if n + slots - 1 < len(chunks):
            chunk_copy(n + slots - 1).start()
        if lane is None:
            w_bf[r0:r0 + rb, :] = slot_view(n)[...].astype(BF16)
        else:
            for c0 in range(0, cols, MXU_N):
                d0 = 2 * c0 + lane * MXU_N
                w_bf[r0:r0 + rb, d0:d0 + MXU_N] = slot_view(n)[:, c0:c0 + MXU_N].astype(BF16)


def _layer_kernel(x_ref, cs_ref, sn_ref, xd_ref, csd_ref, snd_ref, cst_ref, sret_ref,
                  g1_ref, win_hbm, convw_ref, gng_ref, wout_hbm, g2_ref, wg_hbm, wu_hbm, wd_hbm, gf_ref,
                  y_ref, cnew_ref, ret_ref, yd_ref, cnewd_ref, retd_ref,
                  win_ref, wout_ref, wgu_ref, wd_ref, stage_scr, load_sem,
                  proj_scr, uext_scr, convy_scr, mix_scr, xkeep_scr, x1_scr, h_scr, h2_scr, act_scr, r_scr,
                  dec_scr, xi_scr, zeta_scr, pad_scr, maskd_scr, xid_scr, zetad_scr,
                  *, n_tiles, tiles_per_seq):
    tile = x_ref.shape[0]
    drows = xd_ref.shape[0]
    nseq = sret_ref.shape[0]
    seqlen = drows // nseq
    rows = tile + drows
    pad = DECODE_PAD
    assert seqlen == SUBLANES, "decode sequences must fill exactly one f32 sublane tile"
    assert nseq == 2 and drows <= pad and tile % 16 == 0 and drows % 16 == 0
    i = pl.program_id(0)
    chunk = PROMPT_CHUNK
    assert tile % chunk == 0
    t = jnp.minimum(i, n_tiles - 1) % tiles_per_seq
    seq_start = t == 0
    seq_end = t == tiles_per_seq - 1
    w = RET_WIDTH
    cw = w
    d_ff = wd_ref.shape[0]

    band_rows = stage_scr.shape[0]
    d_model = wd_ref.shape[1]
    wd_bands = [(r0, min(r0 + band_rows, d_ff)) for r0 in range(0, d_ff, band_rows)]
    assert len(wd_bands) * d_model <= stage_scr.shape[1] and len(wd_bands) <= load_sem.shape[0]

    def wd_view(n):
        r0, r1 = wd_bands[n]
        return stage_scr.at[0:r1 - r0, n * d_model:(n + 1) * d_model]

    def wd_band(n):
        r0, r1 = wd_bands[n]
        return pltpu.make_async_copy(wd_hbm.at[r0:r1, :], wd_view(n), load_sem.at[n])

    def setup():
        _load_weights_bf16(((win_hbm, win_ref, None), (wout_hbm, wout_ref, None), (wg_hbm, wgu_ref, 0),
                            (wu_hbm, wgu_ref, 1)), stage_scr, load_sem)
        for n in range(len(wd_bands)):
            wd_band(n).start()

    def tables():
        ii = lax.broadcasted_iota(jnp.int32, (chunk, chunk), 0)
        jj = lax.broadcasted_iota(jnp.int32, (chunk, chunk), 1)
        causal = ii >= jj
        diff = jnp.where(causal, ii - jj, 0).astype(F32)
        i_f = lax.broadcasted_iota(jnp.int32, (chunk, RET_HD), 0).astype(F32)
        pi = lax.broadcasted_iota(jnp.int32, (pad, pad), 0)
        pj = lax.broadcasted_iota(jnp.int32, (pad, pad), 1)
        keep = (pi >= pj) & ((pi // seqlen) == (pj // seqlen))
        pdiff = jnp.where(keep, pi - pj, 0).astype(F32)
        pos = (lax.broadcasted_iota(jnp.int32, (pad, RET_HD), 0) % seqlen).astype(F32)
        for hh in range(RET_HEADS):
            dec_scr[hh] = jnp.where(causal, jnp.exp(diff * LOG_G[hh]), 0.0)
            xi_scr[hh] = jnp.exp((i_f + 1.0) * LOG_G[hh])
            zeta_scr[hh] = jnp.exp((chunk - 1.0 - i_f) * LOG_G[hh])
            maskd_scr[hh] = jnp.where(keep, jnp.exp(pdiff * LOG_G[hh]), 0.0)
            xid_scr[hh] = jnp.exp((pos + 1.0) * LOG_G[hh])
            zetad_scr[hh] = jnp.exp((seqlen - 1.0 - pos) * LOG_G[hh])
        r_scr[...] = jnp.zeros_like(r_scr)
        uext_scr[0:SUBLANES, :] = jnp.zeros((SUBLANES, uext_scr.shape[1]), F32)
        pad_scr[...] = jnp.zeros_like(pad_scr)

    def ffn_piece(c0):
        return functools.partial(_swiglu_piece, h2_scr, wgu_ref, act_scr, c0)

    def proj_piece(c0):
        def emit():
            proj_scr[:, c0:c0 + cw] = _dot(h_scr[...], win_ref[:, c0:c0 + cw])
        return emit

    def step(mixer, dense):
        fillers = []

        def fill(n):
            for _ in range(min(n, len(fillers))):
                fillers.pop(0)()

        if dense:
            x1_scr[...] = xkeep_scr[...] + _dot(mix_scr[...], wout_ref[...])
        if mixer:
            x = x_ref[...]
            xd = xd_ref[...]
            h_scr[0:tile, :] = _rmsnorm(x, g1_ref[...]).astype(BF16)
            h_scr[tile:rows, :] = _rmsnorm(xd, g1_ref[...]).astype(BF16)
            proj_scr[:, 0:2 * w] = _dot(h_scr[...], win_ref[:, 0:2 * w])
        if dense:
            h2_scr[...] = _rmsnorm(x1_scr[...], g2_ref[...]).astype(BF16)
            fillers += [ffn_piece(c0) for c0 in range(0, d_ff, MXU_N)]
        if not mixer:
            fill(len(fillers))
        else:
            proj_scr[:, 2 * w:4 * w] = _dot(h_scr[...], win_ref[:, 2 * w:4 * w])
            fillers[0:0] = [proj_piece(5 * w), proj_piece(6 * w)]

            n_chunks = tile // chunk
            units = [(c, hh) for c in range(n_chunks) for hh in range(RET_HEADS)]
            qbs, vbs, scores, updates = {}, {}, {}, {}
            for c, hh in units:
                r0, c0 = c * chunk, hh * RET_HD
                cs = cs_ref[r0:r0 + chunk, :]
                sn = sn_ref[r0:r0 + chunk, :]
                q = _rotary(proj_scr[r0:r0 + chunk, c0:c0 + RET_HD], cs, sn)
                k = _rotary(proj_scr[r0:r0 + chunk, w + c0:w + c0 + RET_HD], cs, sn) * (RET_HD ** -0.5)
                qb, kb = q.astype(BF16), k.astype(BF16)
                vb = proj_scr[r0:r0 + chunk, 2 * w + c0:2 * w + c0 + RET_HD].astype(BF16)
                kz = (k * zeta_scr[hh]).astype(BF16)
                qbs[c, hh], vbs[c, hh] = qb, vb
                scores[c, hh] = _dot_nt(qb, kb)
                updates[c, hh] = _dot_tn(kz, vb)

            csd = csd_ref[...]
            snd = snd_ref[...]
            rowblk = lax.broadcasted_iota(jnp.int32, (pad, 2 * RET_HD), 0) // seqlen
            colhalf = lax.broadcasted_iota(jnp.int32, (pad, 2 * RET_HD), 1) // RET_HD
            d_scores, d_vb, d_cross = {}, {}, {}
            for hh in range(RET_HEADS):
                c0 = hh * RET_HD
                pad_scr[0, 0:drows, :] = _rotary(proj_scr[tile:rows, c0:c0 + RET_HD], csd, snd)
                pad_scr[1, 0:drows, :] = (_rotary(proj_scr[tile:rows, w + c0:w + c0 + RET_HD], csd, snd)
                                          * (RET_HD ** -0.5))
                pad_scr[2, 0:drows, :] = proj_scr[tile:rows, 2 * w + c0:2 * w + c0 + RET_HD]
                q, k, v = pad_scr[0], pad_scr[1], pad_scr[2]
                qb, kb, vb = q.astype(BF16), k.astype(BF16), v.astype(BF16)
                d_scores[hh] = _dot_nt(qb[0:drows, :], kb)
                d_vb[hh] = vb
                kzt = (k * zetad_scr[hh]).T.astype(BF16)
                ra = sret_ref[0, hh]
                rb = sret_ref[1, hh]
                rcat = jnp.concatenate([ra, rb], axis=1).astype(BF16)
                pr = _dot(qb[0:drows, :], rcat)
                d_cross[hh] = jnp.concatenate([pr[0:seqlen, 0:RET_HD], pr[seqlen:drows, RET_HD:2 * RET_HD]],
                                              axis=0)
                vpair = jnp.where(rowblk == colhalf, jnp.concatenate([v, v], axis=1), 0.0).astype(BF16)
                upd = _dot(kzt, vpair)
                g_chunk = math.exp(seqlen * LOG_G[hh])
                retd_ref[0, hh] = ra * g_chunk + upd[:, 0:RET_HD]
                retd_ref[1, hh] = rb * g_chunk + upd[:, RET_HD:2 * RET_HD]
            fill(2)

            u = proj_scr[0:tile, 5 * w:5 * w + cw] * proj_scr[0:tile, 5 * w + cw:5 * w + 2 * cw]
            uext_scr[SUBLANES:SUBLANES + tile, :] = u
            um1 = uext_scr[SUBLANES - 1:SUBLANES - 1 + tile, :]
            um2 = uext_scr[SUBLANES - 2:SUBLANES - 2 + tile, :]
            taps = [convw_ref[:, j * cw:(j + 1) * cw] for j in range(CONV_K)]
            convy_scr[0:tile, :] = taps[0] * um2 + taps[1] * um1 + taps[2] * u
            tail = uext_scr[tile + SUBLANES - (CONV_K - 1):tile + SUBLANES, :]
            cnew_ref[0] = tail
            uext_scr[SUBLANES - (CONV_K - 1):SUBLANES, :] = jnp.where(seq_end, 0.0, tail)

            ud = proj_scr[tile:rows, 5 * w:5 * w + cw] * proj_scr[tile:rows, 5 * w + cw:5 * w + 2 * cw]
            cst = cst_ref[...]
            older = jnp.broadcast_to(cst[:, 0:1, :], (nseq, seqlen, cw)).reshape(drows, cw)
            newer = jnp.broadcast_to(cst[:, 1:2, :], (nseq, seqlen, cw)).reshape(drows, cw)
            r8 = lax.broadcasted_iota(jnp.int32, (drows, cw), 0) % seqlen
            ud1 = jnp.where(r8 == 0, newer, pltpu.roll(ud, 1, axis=0))
            ud2 = jnp.where(r8 == 0, older, jnp.where(r8 == 1, newer, pltpu.roll(ud, 2, axis=0)))
            convy_scr[tile:rows, :] = taps[0] * ud2 + taps[1] * ud1 + taps[2] * ud
            cnewd_ref[...] = ud.reshape(nseq, seqlen, cw)[:, seqlen - (CONV_K - 1):, :]
            fill(len(fillers) - len(units) + 1)

            outs = {}
            for hh in range(RET_HEADS):
                state = jnp.where(seq_start, 0.0, r_scr[hh])
                for c in range(n_chunks):
                    s = (scores[c, hh] * dec_scr[hh]).astype(BF16)
                    outs[c, hh] = _dot(s, vbs[c, hh]) + _dot(qbs[c, hh], state.astype(BF16)) * xi_scr[hh]
                    state = state * math.exp(chunk * LOG_G[hh]) + updates[c, hh]
                r_scr[hh] = state
                ret_ref[0, hh] = state
            d_outs = {}
            for hh in range(RET_HEADS):
                s = (d_scores[hh] * maskd_scr[hh, 0:drows, :]).astype(BF16)
                d_outs[hh] = _dot(s, d_vb[hh]) + d_cross[hh] * xid_scr[hh, 0:drows, :]
            for c, hh in units:
                fill(1)
                r0, c0 = c * chunk, hh * RET_HD
                g = proj_scr[r0:r0 + chunk, 3 * w + c0:3 * w + c0 + RET_HD]
                mix_scr[r0:r0 + chunk, c0:c0 + RET_HD] = _groupnorm_gate(
                    outs[c, hh], g, gng_ref[:, c0:c0 + RET_HD]).astype(BF16)
            for hh in range(RET_HEADS):
                c0 = hh * RET_HD
                g = proj_scr[tile:rows, 3 * w + c0:3 * w + c0 + RET_HD]
                mix_scr[tile:rows, c0:c0 + RET_HD] = _groupnorm_gate(
                    d_outs[hh], g, gng_ref[:, c0:c0 + RET_HD]).astype(BF16)
            fill(len(fillers))

        if dense:
            down = _dot(act_scr[...], wd_ref[...])
        if mixer:
            proj_piece(4 * w)()
        if dense:
            y = _rmsnorm(x1_scr[...] + down, gf_ref[...])
            y_ref[...] = y[0:tile, :]
            yd_ref[...] = y[tile:rows, :]
        if mixer:
            mix_scr[:, w:w + cw] = (proj_scr[:, 4 * w:4 * w + cw] * convy_scr[...]).astype(BF16)
            xkeep_scr[0:tile, :] = x
            xkeep_scr[tile:rows, :] = xd

    assert n_tiles >= 2
    @pl.when(i == 0)
    def _():
        setup()

    @pl.when(i < 1)
    def _():
        tables()
        step(mixer=True, dense=False)

    @pl.when(i == 1)
    def _():
        for n in range(len(wd_bands)):
            wd_band(n).wait()
        for n, (r0, r1) in enumerate(wd_bands):
            wd_ref[r0:r1, :] = wd_view(n)[...].astype(BF16)
        step(mixer=True, dense=True)

    @pl.when((i > 1) & (i < n_tiles))
    def _():
        step(mixer=True, dense=True)

    @pl.when(i == n_tiles)
    def _():
        step(mixer=False, dense=True)


def _resident(shape):
    nd = len(shape)
    return pl.BlockSpec(shape, lambda *_: (0,) * nd, pipeline_mode=pl.Buffered(1))


def _layer_call(x, xd, state_conv, state_ret, ws):
    batch, seq, d = x.shape
    nseq_all, seqlen, _ = xd.shape
    tile = PROMPT_TILE
    tiles_per_seq = seq // tile
    n_tiles = batch * tiles_per_seq
    assert nseq_all % n_tiles == 0, "every prompt tile carries the same number of decode sequences"
    nseq = nseq_all // n_tiles
    drows = nseq * seqlen
    rows = tile + drows
    cs, sn = _rope_tables(np.arange(seq))
    csd, snd = _rope_tables(np.tile(PAST_LEN + np.arange(seqlen), nseq))
    in_cols = ws[1].shape[1]
    cw = ws[2].shape[-1] // CONV_K
    mixer_tile = lambda i: jnp.minimum(i, n_tiles - 1)
    dense_tile = lambda i: jnp.maximum(i - 1, 0)
    w_in, w_out, w_gate, w_up, w_down = [ws[m] for m in MATRIX_SLOTS]
    assert w_gate.shape == w_up.shape and w_gate.shape[1] == w_down.shape[0]
    d_ff = w_down.shape[0]
    mats = [w_in.shape, w_out.shape, (d, 2 * d_ff), w_down.shape]
    weight_specs = [pl.BlockSpec(memory_space=pl.ANY) if m in MATRIX_SLOTS else _resident(wt.shape)
                    for m, wt in enumerate(ws)]
    state_block = (nseq, RET_HEADS, RET_HD, RET_HD)
    outs = pl.pallas_call(
        functools.partial(_layer_kernel, n_tiles=n_tiles, tiles_per_seq=tiles_per_seq),
        grid=(n_tiles + 1,),
        in_specs=[pl.BlockSpec((tile, d), lambda i: (mixer_tile(i), 0)),
                  pl.BlockSpec((tile, RET_HD), lambda i: (mixer_tile(i) % tiles_per_seq, 0)),
                  pl.BlockSpec((tile, RET_HD), lambda i: (mixer_tile(i) % tiles_per_seq, 0)),
                  pl.BlockSpec((drows, d), lambda i: (mixer_tile(i), 0)),
                  _resident((drows, RET_HD)), _resident((drows, RET_HD)),
                  pl.BlockSpec((nseq, CONV_K - 1, cw), lambda i: (mixer_tile(i), 0, 0)),
                  pl.BlockSpec(state_block, lambda i: (mixer_tile(i), 0, 0, 0))] + weight_specs,
        out_specs=[pl.BlockSpec((tile, d), lambda i: (dense_tile(i), 0)),
                   pl.BlockSpec((1, CONV_K - 1, cw), lambda i: (mixer_tile(i) // tiles_per_seq, 0, 0)),
                   pl.BlockSpec((1, RET_HEADS, RET_HD, RET_HD),
                                lambda i: (mixer_tile(i) // tiles_per_seq, 0, 0, 0)),
                   pl.BlockSpec((drows, d), lambda i: (dense_tile(i), 0)),
                   pl.BlockSpec((nseq, CONV_K - 1, cw), lambda i: (mixer_tile(i), 0, 0)),
                   pl.BlockSpec(state_block, lambda i: (mixer_tile(i), 0, 0, 0))],
        out_shape=[jax.ShapeDtypeStruct((batch * seq, d), F32),
                   jax.ShapeDtypeStruct((batch, CONV_K - 1, cw), F32),
                   jax.ShapeDtypeStruct((batch, RET_HEADS, RET_HD, RET_HD), F32),
                   jax.ShapeDtypeStruct((nseq_all * seqlen, d), F32),
                   jax.ShapeDtypeStruct((nseq_all, CONV_K - 1, cw), F32),
                   jax.ShapeDtypeStruct((nseq_all, RET_HEADS, RET_HD, RET_HD), F32)],
        scratch_shapes=[pltpu.VMEM(s, BF16) for s in mats] + [
                        pltpu.VMEM((STAGE_SLOTS * STAGE_ROWS, max(in_cols, d_ff, d)), F32),
                        pltpu.SemaphoreType.DMA((STAGE_SLOTS,)),
                        pltpu.VMEM((rows, in_cols), F32),
                        pltpu.VMEM((tile + SUBLANES, cw), F32),
                        pltpu.VMEM((rows, cw), F32),
                        pltpu.VMEM((rows, d), BF16),
                        pltpu.VMEM((rows, d), F32),
                        pltpu.VMEM((rows, d), F32),
                        pltpu.VMEM((rows, d), BF16),
                        pltpu.VMEM((rows, d), BF16),
                        pltpu.VMEM((rows, d_ff), BF16),
                        pltpu.VMEM((RET_HEADS, RET_HD, RET_HD), F32),
                        pltpu.VMEM((RET_HEADS, PROMPT_CHUNK, PROMPT_CHUNK), F32),
                        pltpu.VMEM((RET_HEADS, PROMPT_CHUNK, RET_HD), F32),
                        pltpu.VMEM((RET_HEADS, PROMPT_CHUNK, RET_HD), F32),
                        pltpu.VMEM((3, DECODE_PAD, RET_HD), F32),
                        pltpu.VMEM((RET_HEADS, DECODE_PAD, DECODE_PAD), F32),
                        pltpu.VMEM((RET_HEADS, DECODE_PAD, RET_HD), F32),
                        pltpu.VMEM((RET_HEADS, DECODE_PAD, RET_HD), F32)],
        compiler_params=pltpu.CompilerParams(dimension_semantics=("arbitrary",),
                                             vmem_limit_bytes=VMEM_LIMIT_BYTES),
        name="layer",
    )(x.reshape(batch * seq, d), cs, sn, xd.reshape(nseq_all * seqlen, d), csd, snd, state_conv, state_ret,
      *ws)
    y, cnew, ret, yd, cnewd, retd = outs
    return y.reshape(batch, seq, d), yd.reshape(nseq_all, seqlen, d), cnew, ret, cnewd, retd


def kernel(x_prompt, x_sample, state_conv, state_ret, norm1_g, w_in, conv_w, ret_gn_g, w_out, norm2_g,
           w_gate, w_up, w_down, norm_f_g):
    depth = w_in.shape[0]
    assert depth == 1, "the fused layer kernel takes a single layer"
    row = lambda g: g.reshape(1, -1)
    ws = (row(norm1_g[0]), w_in[0], row(conv_w[0]), row(ret_gn_g[0]), w_out[0],
          row(norm2_g[0]), w_gate[0], w_up[0], w_down[0], row(norm_f_g))
    y_p, y_s, cnew_p, ret_p, cnew_s, ret_s = _layer_call(x_prompt, x_sample, state_conv[0], state_ret[0], ws)
    return (y_p, y_s, cnew_p[None], ret_p[None], cnew_s[None], ret_s[None])
```

```python
import functools
import math

import numpy as np
import jax
import jax.numpy as jnp
from jax import lax
from jax.experimental import pallas as pl
from jax.experimental.pallas import tpu as pltpu

F32 = jnp.float32
BF16 = jnp.bfloat16

RET_HEADS = 4
RET_HD = 128
RET_WIDTH = RET_HEADS * RET_HD
CONV_K = 3
ROPE_BASE = 10000.0
NORM_EPS = 1e-6
GN_EPS = 1e-5
PAST_LEN = 16384

LOG_G = tuple(math.log1p(-(2.0 ** (-5.0 - h))) for h in range(RET_HEADS))

SUBLANES = 8
MXU_N = 256
MATRIX_SLOTS = (1, 4, 6, 7, 8)
STAGE_ROWS = 128
STAGE_SLOTS = 8
PROMPT_TILE = 256
PROMPT_CHUNK = 128
DECODE_PAD = 128
VMEM_LIMIT_BYTES = 63 * 1024 * 1024


def _rope_tables(pos):
    half = RET_HD // 2
    inv = ROPE_BASE ** (-np.arange(half, dtype=np.float64) / half)
    ang = np.asarray(pos, np.float64)[:, None] * inv[None, :]
    cos, sin = np.cos(ang), np.sin(ang)
    cs = np.concatenate([cos, cos], axis=1).astype(np.float32)
    sn = np.concatenate([-sin, sin], axis=1).astype(np.float32)
    return jnp.asarray(cs), jnp.asarray(sn)


def _rmsnorm(x, g):
    ms = jnp.mean(x * x, axis=-1, keepdims=True)
    return x * lax.rsqrt(ms + NORM_EPS) * g


def _silu(x):
    return x * (1.0 / (1.0 + jnp.exp(-x)))


def _rotary(t, cs, sn):
    return t * cs + pltpu.roll(t, RET_HD // 2, axis=1) * sn


def _dot(a, b):
    return jnp.dot(a, b, preferred_element_type=F32)


def _dot_nt(a, b):
    return lax.dot_general(a, b, (((1,), (1,)), ((), ())), preferred_element_type=F32)


def _dot_tn(a, b):
    return lax.dot_general(a, b, (((0,), (0,)), ((), ())), preferred_element_type=F32)


def _groupnorm_gate(o, g, gn_g):
    mu = jnp.mean(o, axis=-1, keepdims=True)
    d = o - mu
    var = jnp.mean(d * d, axis=-1, keepdims=True)
    return _silu(g) * (d * lax.rsqrt(var + GN_EPS) * gn_g)


def _swiglu_piece(h_scr, wgu_ref, act_scr, c0):
    r = _dot(h_scr[...], wgu_ref[:, 2 * c0:2 * c0 + 2 * MXU_N])
    act_scr[:, c0:c0 + MXU_N] = (_silu(r[:, 0:MXU_N]) * r[:, MXU_N:2 * MXU_N]).astype(BF16)


def _load_weights_bf16(pairs, stage, sem):
    rb = STAGE_ROWS
    slots, width = stage.shape[0] // rb, stage.shape[1]
    assert slots >= 2 and slots <= sem.shape[0]
    chunks = []
    for w_hbm, w_bf, lane in pairs:
        rows, cols = w_hbm.shape
        assert rows % rb == 0 and cols <= width and (lane is None or cols % MXU_N == 0)
        chunks += [(w_hbm, w_bf, r0, cols, lane) for r0 in range(0, rows, rb)]

    def slot_view(n):
        s0 = (n % slots) * rb
        return stage.at[s0:s0 + rb, 0:chunks[n][3]]

    def chunk_copy(n):
        w_hbm, r0 = chunks[n][0], chunks[n][2]
        return pltpu.make_async_copy(w_hbm.at[r0:r0 + rb, :], slot_view(n), sem.at[n % slots])

    for n in range(min(slots - 1, len(chunks))):
        chunk_copy(n).start()
    for n, (_, w_bf, r0, cols, lane) in enumerate(chunks):
        chunk_copy(n).wait()
        if n + slots - 1 < len(chunks):
            chunk_copy(n + slots - 1).start()
        if lane is None:
            w_bf[r0:r0 + rb, :] = slot_view(n)[...].astype(BF16)
        else:
            for c0 in range(0, cols, MXU_N):
                d0 = 2 * c0 + lane * MXU_N
                w_bf[r0:r0 + rb, d0:d0 + MXU_N] = slot_view(n)[:, c0:c0 + MXU_N].astype(BF16)


def _layer_kernel(x_ref, cs_ref, sn_ref, xd_ref, csd_ref, snd_ref, cst_ref, sret_ref,
                  g1_ref, win_hbm, convw_ref, gng_ref, wout_hbm, g2_ref, wg_hbm, wu_hbm, wd_hbm, gf_ref,
                  y_ref, cnew_ref, ret_ref, yd_ref, cnewd_ref, retd_ref,
                  win_ref, wout_ref, wgu_ref, wd_ref, stage_scr, load_sem,
                  proj_scr, uext_scr, convy_scr, mix_scr, xkeep_scr, x1_scr, h_scr, h2_scr, act_scr, r_scr,
                  dec_scr, xi_scr, zeta_scr, pad_scr, maskd_scr, xid_scr, zetad_scr,
                  *, n_tiles, tiles_per_seq):
    tile = x_ref.shape[0]
    drows = xd_ref.shape[0]
    nseq = sret_ref.shape[0]
    seqlen = drows // nseq
    rows = tile + drows
    pad = DECODE_PAD
    assert seqlen == SUBLANES, "decode sequences must fill exactly one f32 sublane tile"
    assert nseq == 2 and drows <= pad and tile % 16 == 0 and drows % 16 == 0
    i = pl.program_id(0)
    chunk = PROMPT_CHUNK
    assert tile % chunk == 0
    t = jnp.minimum(i, n_tiles - 1) % tiles_per_seq
    seq_start = t == 0
    seq_end = t == tiles_per_seq - 1
    w = RET_WIDTH
    cw = w
    d_ff = wd_ref.shape[0]

    band_rows = stage_scr.shape[0]
    d_model = wd_ref.shape[1]
    wd_bands = [(r0, min(r0 + band_rows, d_ff)) for r0 in range(0, d_ff, band_rows)]
    assert len(wd_bands) * d_model <= stage_scr.shape[1] and len(wd_bands) <= load_sem.shape[0]

    def wd_band(n):
        r0, r1 = wd_bands[n]
        view = stage_scr.at[0:r1 - r0, n * d_model:(n + 1) * d_model]
        return pltpu.make_async_copy(wd_hbm.at[r0:r1, :], view, load_sem.at[n]), view

    def setup():
        _load_weights_bf16(((win_hbm, win_ref, None), (wout_hbm, wout_ref, None), (wg_hbm, wgu_ref, 0),
                            (wu_hbm, wgu_ref, 1)), stage_scr, load_sem)
        for n in range(len(wd_bands)):
            wd_band(n)[0].start()

    def tables():
        ii = lax.broadcasted_iota(jnp.int32, (chunk, chunk), 0)
        jj = lax.broadcasted_iota(jnp.int32, (chunk, chunk), 1)
        causal = ii >= jj
        diff = jnp.where(causal, ii - jj, 0).astype(F32)
        i_f = lax.broadcasted_iota(jnp.int32, (chunk, RET_HD), 0).astype(F32)
        pi = lax.broadcasted_iota(jnp.int32, (pad, pad), 0)
        pj = lax.broadcasted_iota(jnp.int32, (pad, pad), 1)
        keep = (pi >= pj) & ((pi // seqlen) == (pj // seqlen))
        pdiff = jnp.where(keep, pi - pj, 0).astype(F32)
        pos = (lax.broadcasted_iota(jnp.int32, (pad, RET_HD), 0) % seqlen).astype(F32)
        for hh in range(RET_HEADS):
            dec_scr[hh] = jnp.where(causal, jnp.exp(diff * LOG_G[hh]), 0.0)
            xi_scr[hh] = jnp.exp((i_f + 1.0) * LOG_G[hh])
            zeta_scr[hh] = jnp.exp((chunk - 1.0 - i_f) * LOG_G[hh])
            maskd_scr[hh] = jnp.where(keep, jnp.exp(pdiff * LOG_G[hh]), 0.0)
            xid_scr[hh] = jnp.exp((pos + 1.0) * LOG_G[hh])
            zetad_scr[hh] = jnp.exp((seqlen - 1.0 - pos) * LOG_G[hh])
        r_scr[...] = jnp.zeros_like(r_scr)
        uext_scr[0:SUBLANES, :] = jnp.zeros((SUBLANES, uext_scr.shape[1]), F32)
        pad_scr[...] = jnp.zeros_like(pad_scr)

    def ffn_piece(c0):
        return functools.partial(_swiglu_piece, h2_scr, wgu_ref, act_scr, c0)

    def proj_piece(c0):
        def emit():
            proj_scr[:, c0:c0 + cw] = _dot(h_scr[...], win_ref[:, c0:c0 + cw])
        return emit

    def step(mixer, dense, late=()):
        fillers = []

        def fill(n):
            for _ in range(min(n, len(fillers))):
                fillers.pop(0)()

        if dense:
            x1_scr[...] = xkeep_scr[...] + _dot(mix_scr[...], wout_ref[...])
        if mixer:
            x = x_ref[...]
            xd = xd_ref[...]
            h_scr[0:tile, :] = _rmsnorm(x, g1_ref[...]).astype(BF16)
            h_scr[tile:rows, :] = _rmsnorm(xd, g1_ref[...]).astype(BF16)
            proj_scr[:, 0:2 * w] = _dot(h_scr[...], win_ref[:, 0:2 * w])
        if dense:
            h2_scr[...] = _rmsnorm(x1_scr[...], g2_ref[...]).astype(BF16)
            fillers += [ffn_piece(c0) for c0 in range(0, d_ff, MXU_N)]
            fillers += list(late)
        if not mixer:
            fill(len(fillers))
        else:
            proj_scr[:, 2 * w:4 * w] = _dot(h_scr[...], win_ref[:, 2 * w:4 * w])
            fillers[0:0] = [proj_piece(5 * w), proj_piece(6 * w)]

            n_chunks = tile // chunk
            units = [(c, hh) for c in range(n_chunks) for hh in range(RET_HEADS)]
            qbs, vbs, scores, updates = {}, {}, {}, {}
            for c, hh in units:
                r0, c0 = c * chunk, hh * RET_HD
                cs = cs_ref[r0:r0 + chunk, :]
                sn = sn_ref[r0:r0 + chunk, :]
                q = _rotary(proj_scr[r0:r0 + chunk, c0:c0 + RET_HD], cs, sn)
                k = _rotary(proj_scr[r0:r0 + chunk, w + c0:w + c0 + RET_HD], cs, sn) * (RET_HD ** -0.5)
                qb, kb = q.astype(BF16), k.astype(BF16)
                vb = proj_scr[r0:r0 + chunk, 2 * w + c0:2 * w + c0 + RET_HD].astype(BF16)
                kz = (k * zeta_scr[hh]).astype(BF16)
                qbs[c, hh], vbs[c, hh] = qb, vb
                scores[c, hh] = _dot_nt(qb, kb)
                updates[c, hh] = _dot_tn(kz, vb)

            csd = csd_ref[...]
            snd = snd_ref[...]
            rowblk = lax.broadcasted_iota(jnp.int32, (pad, 2 * RET_HD), 0) // seqlen
            colhalf = lax.broadcasted_iota(jnp.int32, (pad, 2 * RET_HD), 1) // RET_HD
            d_scores, d_vb, d_cross = {}, {}, {}
            for hh in range(RET_HEADS):
                c0 = hh * RET_HD
                pad_scr[0, 0:drows, :] = _rotary(proj_scr[tile:rows, c0:c0 + RET_HD], csd, snd)
                pad_scr[1, 0:drows, :] = (_rotary(proj_scr[tile:rows, w + c0:w + c0 + RET_HD], csd, snd)
                                          * (RET_HD ** -0.5))
                pad_scr[2, 0:drows, :] = proj_scr[tile:rows, 2 * w + c0:2 * w + c0 + RET_HD]
                q, k, v = pad_scr[0], pad_scr[1], pad_scr[2]
                qb, kb, vb = q.astype(BF16), k.astype(BF16), v.astype(BF16)
                d_scores[hh] = _dot_nt(qb[0:drows, :], kb)
                d_vb[hh] = vb
                kzt = (k * zetad_scr[hh]).T.astype(BF16)
                ra = sret_ref[0, hh]
                rb = sret_ref[1, hh]
                rcat = jnp.concatenate([ra, rb], axis=1).astype(BF16)
                pr = _dot(qb[0:drows, :], rcat)
                d_cross[hh] = jnp.concatenate([pr[0:seqlen, 0:RET_HD], pr[seqlen:drows, RET_HD:2 * RET_HD]],
                                              axis=0)
                vpair = jnp.where(rowblk == colhalf, jnp.concatenate([v, v], axis=1), 0.0).astype(BF16)
                upd = _dot(kzt, vpair)
                g_chunk = math.exp(seqlen * LOG_G[hh])
                retd_ref[0, hh] = ra * g_chunk + upd[:, 0:RET_HD]
                retd_ref[1, hh] = rb * g_chunk + upd[:, RET_HD:2 * RET_HD]
            fill(2)

            u = proj_scr[0:tile, 5 * w:5 * w + cw] * proj_scr[0:tile, 5 * w + cw:5 * w + 2 * cw]
            uext_scr[SUBLANES:SUBLANES + tile, :] = u
            um1 = uext_scr[SUBLANES - 1:SUBLANES - 1 + tile, :]
            um2 = uext_scr[SUBLANES - 2:SUBLANES - 2 + tile, :]
            taps = [convw_ref[:, j * cw:(j + 1) * cw] for j in range(CONV_K)]
            convy_scr[0:tile, :] = taps[0] * um2 + taps[1] * um1 + taps[2] * u
            tail = uext_scr[tile + SUBLANES - (CONV_K - 1):tile + SUBLANES, :]
            cnew_ref[0] = tail
            uext_scr[SUBLANES - (CONV_K - 1):SUBLANES, :] = jnp.where(seq_end, 0.0, tail)

            ud = proj_scr[tile:rows, 5 * w:5 * w + cw] * proj_scr[tile:rows, 5 * w + cw:5 * w + 2 * cw]
            cst = cst_ref[...]
            older = jnp.broadcast_to(cst[:, 0:1, :], (nseq, seqlen, cw)).reshape(drows, cw)
            newer = jnp.broadcast_to(cst[:, 1:2, :], (nseq, seqlen, cw)).reshape(drows, cw)
            r8 = lax.broadcasted_iota(jnp.int32, (drows, cw), 0) % seqlen
            ud1 = jnp.where(r8 == 0, newer, pltpu.roll(ud, 1, axis=0))
            ud2 = jnp.where(r8 == 0, older, jnp.where(r8 == 1, newer, pltpu.roll(ud, 2, axis=0)))
            convy_scr[tile:rows, :] = taps[0] * ud2 + taps[1] * ud1 + taps[2] * ud
            cnewd_ref[...] = ud.reshape(nseq, seqlen, cw)[:, seqlen - (CONV_K - 1):, :]
            fill(len(fillers) - len(units) + 1)

            outs = {}
            for hh in range(RET_HEADS):
                state = jnp.where(seq_start, 0.0, r_scr[hh])
                for c in range(n_chunks):
                    s = (scores[c, hh] * dec_scr[hh]).astype(BF16)
                    outs[c, hh] = _dot(s, vbs[c, hh]) + _dot(qbs[c, hh], state.astype(BF16)) * xi_scr[hh]
                    state = state * math.exp(chunk * LOG_G[hh]) + updates[c, hh]
                r_scr[hh] = state
                ret_ref[0, hh] = state
            d_outs = {}
            for hh in range(RET_HEADS):
                s = (d_scores[hh] * maskd_scr[hh, 0:drows, :]).astype(BF16)
                d_outs[hh] = _dot(s, d_vb[hh]) + d_cross[hh] * xid_scr[hh, 0:drows, :]
            for c, hh in units:
                fill(1)
                r0, c0 = c * chunk, hh * RET_HD
                g = proj_scr[r0:r0 + chunk, 3 * w + c0:3 * w + c0 + RET_HD]
                mix_scr[r0:r0 + chunk, c0:c0 + RET_HD] = _groupnorm_gate(
                    outs[c, hh], g, gng_ref[:, c0:c0 + RET_HD]).astype(BF16)
            for hh in range(RET_HEADS):
                c0 = hh * RET_HD
                g = proj_scr[tile:rows, 3 * w + c0:3 * w + c0 + RET_HD]
                mix_scr[tile:rows, c0:c0 + RET_HD] = _groupnorm_gate(
                    d_outs[hh], g, gng_ref[:, c0:c0 + RET_HD]).astype(BF16)
            fill(len(fillers))

        if dense:
            down = _dot(act_scr[...], wd_ref[...])
        if mixer:
            proj_piece(4 * w)()
        if dense:
            y = _rmsnorm(x1_scr[...] + down, gf_ref[...])
            y_ref[...] = y[0:tile, :]
            yd_ref[...] = y[tile:rows, :]
        if mixer:
            mix_scr[:, w:w + cw] = (proj_scr[:, 4 * w:4 * w + cw] * convy_scr[...]).astype(BF16)
            xkeep_scr[0:tile, :] = x
            xkeep_scr[tile:rows, :] = xd

    assert n_tiles >= 2
    @pl.when(i == 0)
    def _():
        setup()

    @pl.when(i < 1)
    def _():
        tables()
        step(mixer=True, dense=False)

    @pl.when(i == 1)
    def _():
        def cast_band(n):
            r0, r1 = wd_bands[n]
            copy, view = wd_band(n)
            copy.wait()
            wd_ref[r0:r1, :] = view[...].astype(BF16)
        for n in range(len(wd_bands) - 1):
            cast_band(n)
        step(mixer=True, dense=True, late=[functools.partial(cast_band, len(wd_bands) - 1)])

    @pl.when((i > 1) & (i < n_tiles))
    def _():
        step(mixer=True, dense=True)

    @pl.when(i == n_tiles)
    def _():
        step(mixer=False, dense=True)


def _resident(shape):
    nd = len(shape)
    return pl.BlockSpec(shape, lambda *_: (0,) * nd, pipeline_mode=pl.Buffered(1))


def _layer_call(x, xd, state_conv, state_ret, ws):
    batch, seq, d = x.shape
    nseq_all, seqlen, _ = xd.shape
    tile = PROMPT_TILE
    tiles_per_seq = seq // tile
    n_tiles = batch * tiles_per_seq
    assert nseq_all % n_tiles == 0, "every prompt tile carries the same number of decode sequences"
    nseq = nseq_all // n_tiles
    drows = nseq * seqlen
    rows = tile + drows
    cs, sn = _rope_tables(np.arange(seq))
    csd, snd = _rope_tables(np.tile(PAST_LEN + np.arange(seqlen), nseq))
    in_cols = ws[1].shape[1]
    cw = ws[2].shape[-1] // CONV_K
    mixer_tile = lambda i: jnp.minimum(i, n_tiles - 1)
    dense_tile = lambda i: jnp.maximum(i - 1, 0)
    w_in, w_out, w_gate, w_up, w_down = [ws[m] for m in MATRIX_SLOTS]
    assert w_gate.shape == w_up.shape and w_gate.shape[1] == w_down.shape[0]
    d_ff = w_down.shape[0]
    mats = [w_in.shape, w_out.shape, (d, 2 * d_ff), w_down.shape]
    weight_specs = [pl.BlockSpec(memory_space=pl.ANY) if m in MATRIX_SLOTS else _resident(wt.shape)
                    for m, wt in enumerate(ws)]
    state_block = (nseq, RET_HEADS, RET_HD, RET_HD)
    outs = pl.pallas_call(
        functools.partial(_layer_kernel, n_tiles=n_tiles, tiles_per_seq=tiles_per_seq),
        grid=(n_tiles + 1,),
        in_specs=[pl.BlockSpec((tile, d), lambda i: (mixer_tile(i), 0)),
                  pl.BlockSpec((tile, RET_HD), lambda i: (mixer_tile(i) % tiles_per_seq, 0)),
                  pl.BlockSpec((tile, RET_HD), lambda i: (mixer_tile(i) % tiles_per_seq, 0)),
                  pl.BlockSpec((drows, d), lambda i: (mixer_tile(i), 0)),
                  _resident((drows, RET_HD)), _resident((drows, RET_HD)),
                  pl.BlockSpec((nseq, CONV_K - 1, cw), lambda i: (mixer_tile(i), 0, 0)),
                  pl.BlockSpec(state_block, lambda i: (mixer_tile(i), 0, 0, 0))] + weight_specs,
        out_specs=[pl.BlockSpec((tile, d), lambda i: (dense_tile(i), 0)),
                   pl.BlockSpec((1, CONV_K - 1, cw), lambda i: (mixer_tile(i) // tiles_per_seq, 0, 0)),
                   pl.BlockSpec((1, RET_HEADS, RET_HD, RET_HD),
                                lambda i: (mixer_tile(i) // tiles_per_seq, 0, 0, 0)),
                   pl.BlockSpec((drows, d), lambda i: (dense_tile(i), 0)),
                   pl.BlockSpec((nseq, CONV_K - 1, cw), lambda i: (mixer_tile(i), 0, 0)),
                   pl.BlockSpec(state_block, lambda i: (mixer_tile(i), 0, 0, 0))],
        out_shape=[jax.ShapeDtypeStruct((batch * seq, d), F32),
                   jax.ShapeDtypeStruct((batch, CONV_K - 1, cw), F32),
                   jax.ShapeDtypeStruct((batch, RET_HEADS, RET_HD, RET_HD), F32),
                   jax.ShapeDtypeStruct((nseq_all * seqlen, d), F32),
                   jax.ShapeDtypeStruct((nseq_all, CONV_K - 1, cw), F32),
                   jax.ShapeDtypeStruct((nseq_all, RET_HEADS, RET_HD, RET_HD), F32)],
        scratch_shapes=[pltpu.VMEM(s, BF16) for s in mats] + [
                        pltpu.VMEM((STAGE_SLOTS * STAGE_ROWS, max(in_cols, d_ff, d)), F32),
                        pltpu.SemaphoreType.DMA((STAGE_SLOTS,)),
                        pltpu.VMEM((rows, in_cols), F32),
                        pltpu.VMEM((tile + SUBLANES, cw), F32),
                        pltpu.VMEM((rows, cw), F32),
                        pltpu.VMEM((rows, d), BF16),
                        pltpu.VMEM((rows, d), F32),
                        pltpu.VMEM((rows, d), F32),
                        pltpu.VMEM((rows, d), BF16),
                        pltpu.VMEM((rows, d), BF16),
                        pltpu.VMEM((rows, d_ff), BF16),
                        pltpu.VMEM((RET_HEADS, RET_HD, RET_HD), F32),
                        pltpu.VMEM((RET_HEADS, PROMPT_CHUNK, PROMPT_CHUNK), F32),
                        pltpu.VMEM((RET_HEADS, PROMPT_CHUNK, RET_HD), F32),
                        pltpu.VMEM((RET_HEADS, PROMPT_CHUNK, RET_HD), F32),
                        pltpu.VMEM((3, DECODE_PAD, RET_HD), F32),
                        pltpu.VMEM((RET_HEADS, DECODE_PAD, DECODE_PAD), F32),
                        pltpu.VMEM((RET_HEADS, DECODE_PAD, RET_HD), F32),
                        pltpu.VMEM((RET_HEADS, DECODE_PAD, RET_HD), F32)],
        compiler_params=pltpu.CompilerParams(dimension_semantics=("arbitrary",),
                                             vmem_limit_bytes=VMEM_LIMIT_BYTES),
        name="layer",
    )(x.reshape(batch * seq, d), cs, sn, xd.reshape(nseq_all * seqlen, d), csd, snd, state_conv, state_ret,
      *ws)
    y, cnew, ret, yd, cnewd, retd = outs
    return y.reshape(batch, seq, d), yd.reshape(nseq_all, seqlen, d), cnew, ret, cnewd, retd


def kernel(x_prompt, x_sample, state_conv, state_ret, norm1_g, w_in, conv_w, ret_gn_g, w_out, norm2_g,
           w_gate, w_up, w_down, norm_f_g):
    depth = w_in.shape[0]
    assert depth == 1, "the fused layer kernel takes a single layer"
    row = lambda g: g.reshape(1, -1)
    ws = (row(norm1_g[0]), w_in[0], row(conv_w[0]), row(ret_gn_g[0]), w_out[0],
          row(norm2_g[0]), w_gate[0], w_up[0], w_down[0], row(norm_f_g))
    y_p, y_s, cnew_p, ret_p, cnew_s, ret_s = _layer_call(x_prompt, x_sample, state_conv[0], state_ret[0], ws)
    return (y_p, y_s, cnew_p[None], ret_p[None], cnew_s[None], ret_s[None])
```
